```python
import math
import jax, jax.numpy as jnp
from jax import lax
import numpy as np

D_MODEL = 1024
BATCH = 8
SEQ = 4096
DEPTH = 4

N_HYB = (DEPTH + 1) // 2
N_REC = DEPTH // 2
NORM_EPS = 1e-6
CONV_K = 4
D_FF = 2816
D_SSM = D_MODEL
SSM_HEAD_DIM = 64
SSM_HEADS = D_SSM // SSM_HEAD_DIM
SSM_GROUPS = 2
SSM_STATE = 128
SSM_CONV_DIM = D_SSM + 2 * SSM_GROUPS * SSM_STATE
SSD_CHUNK = 128
FOX_HEAD_DIM = 128
FOX_HEADS = 8
D_FOX = FOX_HEADS * FOX_HEAD_DIM
Q_BLOCK = 128
HYB_IN = D_SSM + SSM_CONV_DIM + SSM_HEADS + 3 * D_FOX + FOX_HEADS
D_RNN = D_MODEL
RNN_BLOCKS = 8
RNN_BLOCK = D_RNN // RNN_BLOCKS
RG_LRU_C = 8.0

kernel_name = "hybrid_ssd_fox_rglru_macaron_sandwich"


def rms_norm(x, g):
    xf = x.astype(jnp.float32)
    y = xf * lax.rsqrt(jnp.mean(xf * xf, axis=-1, keepdims=True) + NORM_EPS)
    return (y * g.astype(jnp.float32)).astype(x.dtype)


def swiglu(x, w_in, w_out):
    gate, up = jnp.split(x @ w_in, 2, axis=-1)
    return (jax.nn.silu(gate) * up) @ w_out


def causal_conv(x, w, bias):
    s = x.shape[1]
    kw = w.shape[0]
    xp = jnp.pad(x, ((0, 0), (kw - 1, 0), (0, 0)))
    y = xp[:, 0:s] * w[0]
    for k in range(1, kw):
        y = y + xp[:, k:k + s] * w[k]
    return y + bias


def ssd_chunked(xh, dt, a, bm, cm):
    b, s, nh, p = xh.shape
    g, n = bm.shape[-2:]
    kh = nh // g
    c = s // SSD_CHUNK
    X = (xh.astype(jnp.float32) * dt[..., None]).reshape(b, c, SSD_CHUNK, g, kh, p)
    Bc = bm.astype(jnp.float32).reshape(b, c, SSD_CHUNK, g, n)
    Cc = cm.astype(jnp.float32).reshape(b, c, SSD_CHUNK, g, n)
    dA = (dt * a).reshape(b, c, SSD_CHUNK, g, kh).transpose(0, 3, 4, 1, 2)
    a_cs = jnp.cumsum(dA, axis=-1)
    causal = jnp.tril(jnp.ones((SSD_CHUNK, SSD_CHUNK), dtype=bool))
    seg = a_cs[..., :, None] - a_cs[..., None, :]
    L = jnp.exp(jnp.where(causal, seg, -jnp.inf))
    cb = jnp.einsum("bclgn,bcsgn->bgcls", Cc, Bc)
    y_diag = jnp.einsum("bgkcls,bcsgkp->bclgkp", L * cb[:, :, None], X)
    decay_states = jnp.exp(a_cs[..., -1:] - a_cs)
    states = jnp.einsum("bclgn,bgkcl,bclgkp->bcgkpn", Bc, decay_states, X)
    chunk_tot = a_cs[..., -1]
    chunk_cs = jnp.cumsum(chunk_tot, axis=-1)
    cs_prev = chunk_cs - chunk_tot
    seg_c = cs_prev[..., :, None] - chunk_cs[..., None, :]
    strict = jnp.tril(jnp.ones((c, c), dtype=bool), k=-1)
    decay_chunk = jnp.exp(jnp.where(strict, seg_c, -jnp.inf))
    init_states = jnp.einsum("bgkzc,bcgkpn->bzgkpn", decay_chunk, states)
    state_decay_out = jnp.exp(a_cs)
    y_off = jnp.einsum("bclgn,bcgkpn,bgkcl->bclgkp", Cc, init_states, state_decay_out)
    return (y_diag + y_off).reshape(b, s, nh, p)


def forgetting_attention(q, k, v, log_f):
    b, s, h, d = q.shape
    nblk = s // Q_BLOCK
    cum = jnp.cumsum(log_f, axis=1).transpose(0, 2, 1)
    qb = q.reshape(b, nblk, Q_BLOCK, h, d).transpose(1, 0, 2, 3, 4)
    cqb = cum.reshape(b, h, nblk, Q_BLOCK).transpose(2, 0, 1, 3)
    kpos = jnp.arange(s)
    scale = d ** -0.5

    def block(args):
        qi, ci, i = args
        logits = jnp.einsum("bqhd,bkhd->bhqk", qi, k).astype(jnp.float32) * scale
        logits = logits + ci[..., :, None] - cum[:, :, None, :]
        qpos = i * Q_BLOCK + jnp.arange(Q_BLOCK)
        mask = kpos[None, :] <= qpos[:, None]
        probs = jax.nn.softmax(jnp.where(mask, logits, -jnp.inf), axis=-1)
        return jnp.einsum("bhqk,bkhd->bqhd", probs.astype(v.dtype), v)

    out = lax.map(block, (qb, cqb, jnp.arange(nblk)))
    return out.transpose(1, 0, 2, 3, 4).reshape(b, s, h, d)


def hybrid_mixer(h, w_in, conv_w, conv_b, dt_bias, a_log, d_skip, ssm_norm_g, fox_b_f, w_out):
    b, s, _ = h.shape
    sizes = (D_SSM, SSM_CONV_DIM, SSM_HEADS, D_FOX, D_FOX, D_FOX)
    offs = np.cumsum(sizes).tolist()
    z, xbc, dt_raw, q, k, v, f_raw = jnp.split(h @ w_in, offs, axis=-1)
    xbc = jax.nn.silu(causal_conv(xbc, conv_w, conv_b))
    xs, bm, cm = jnp.split(xbc, [D_SSM, D_SSM + SSM_GROUPS * SSM_STATE], axis=-1)
    xh = xs.reshape(b, s, SSM_HEADS, SSM_HEAD_DIM)
    dt = jax.nn.softplus(dt_raw.astype(jnp.float32) + dt_bias.astype(jnp.float32))
    a = -jnp.exp(a_log.astype(jnp.float32))
    y = ssd_chunked(xh, dt, a,
                    bm.reshape(b, s, SSM_GROUPS, SSM_STATE),
                    cm.reshape(b, s, SSM_GROUPS, SSM_STATE))
    y = y + d_skip.astype(jnp.float32)[:, None] * xh.astype(jnp.float32)
    y = y.reshape(b, s, D_SSM) * jax.nn.silu(z.astype(jnp.float32))
    yg = y.reshape(b, s, SSM_GROUPS, D_SSM // SSM_GROUPS)
    yg = yg * lax.rsqrt(jnp.mean(yg * yg, axis=-1, keepdims=True) + NORM_EPS)
    y_ssm = (yg.reshape(b, s, D_SSM) * ssm_norm_g.astype(jnp.float32)).astype(h.dtype)
    log_f = jax.nn.log_sigmoid(f_raw.astype(jnp.float32) + fox_b_f.astype(jnp.float32))
    o = forgetting_attention(q.reshape(b, s, FOX_HEADS, FOX_HEAD_DIM),
                             k.reshape(b, s, FOX_HEADS, FOX_HEAD_DIM),
                             v.reshape(b, s, FOX_HEADS, FOX_HEAD_DIM), log_f)
    o = o.reshape(b, s, D_FOX).astype(h.dtype)
    return jnp.concatenate([y_ssm, o], axis=-1) @ w_out


def rg_lru(x, w_a, b_a, w_x, b_x, lam):
    b, s, d = x.shape
    xb = x.reshape(b, s, RNN_BLOCKS, RNN_BLOCK)
    r = jax.nn.sigmoid((jnp.einsum("bsnj,njk->bsnk", xb, w_a).reshape(b, s, d) + b_a).astype(jnp.float32))
    i = jax.nn.sigmoid((jnp.einsum("bsnj,njk->bsnk", xb, w_x).reshape(b, s, d) + b_x).astype(jnp.float32))
    log_a = RG_LRU_C * r * jax.nn.log_sigmoid(lam.astype(jnp.float32))
    a = jnp.exp(log_a)
    u = jnp.sqrt(-jnp.expm1(2.0 * log_a)) * (i * x.astype(jnp.float32))

    def combine(left, right):
        a1, b1 = left
        a2, b2 = right
        return a1 * a2, a2 * b1 + b2

    _, hs = lax.associative_scan(combine, (a, u), axis=1)
    return hs.astype(x.dtype)


def recurrent_mixer(h, w_in, conv_w, conv_b, w_a, b_a, w_x, b_x, lam, w_out):
    gate, xr = jnp.split(h @ w_in, 2, axis=-1)
    gate = jax.nn.gelu(gate, approximate=True)
    xr = causal_conv(xr, conv_w, conv_b)
    return (rg_lru(xr, w_a, b_a, w_x, b_x, lam) * gate) @ w_out


def _fwd_setup_inputs(seed: int = 0) -> dict:
    key = jax.random.key(seed)
    ks = jax.random.split(key, 24)
    f32 = jnp.float32

    def nrm(k, shape, scale):
        return jax.random.normal(k, shape, f32) * scale

    def unif(k, shape, lo, hi):
        return jax.random.uniform(k, shape, f32, lo, hi)

    x = nrm(ks[0], (BATCH, SEQ, D_MODEL), 1.0)
    norm_g = 1.0 + nrm(ks[1], (DEPTH, 6, D_MODEL), 0.05)
    ffn_w_in = nrm(ks[2], (DEPTH, 2, D_MODEL, 2 * D_FF), D_MODEL ** -0.5)
    ffn_w_out = nrm(ks[3], (DEPTH, 2, D_FF, D_MODEL), D_FF ** -0.5)
    hyb_w_in = nrm(ks[4], (N_HYB, D_MODEL, HYB_IN), D_MODEL ** -0.5)
    ssm_conv_w = nrm(ks[5], (N_HYB, CONV_K, SSM_CONV_DIM), CONV_K ** -0.5)
    ssm_conv_b = nrm(ks[6], (N_HYB, SSM_CONV_DIM), 0.02)
    dt0 = jnp.exp(unif(ks[7], (N_HYB, SSM_HEADS), math.log(1e-3), math.log(1e-1)))
    ssm_dt_bias = dt0 + jnp.log(-jnp.expm1(-dt0))
    ssm_a_log = jnp.log(unif(ks[8], (N_HYB, SSM_HEADS), 1.0, 16.0))
    ssm_d = 1.0 + nrm(ks[9], (N_HYB, SSM_HEADS), 0.1)
    ssm_norm_g = 1.0 + nrm(ks[10], (N_HYB, D_SSM), 0.05)
    fox_b_f = unif(ks[11], (N_HYB, FOX_HEADS), 1.0, 4.0)
    hyb_w_out = nrm(ks[12], (N_HYB, D_SSM + D_FOX, D_MODEL), (D_SSM + D_FOX) ** -0.5)
    rec_w_in = nrm(ks[13], (N_REC, D_MODEL, 2 * D_RNN), D_MODEL ** -0.5)
    rec_conv_w = nrm(ks[14], (N_REC, CONV_K, D_RNN), CONV_K ** -0.5)
    rec_conv_b = nrm(ks[15], (N_REC, D_RNN), 0.02)
    rec_w_a = nrm(ks[16], (N_REC, RNN_BLOCKS, RNN_BLOCK, RNN_BLOCK), RNN_BLOCK ** -0.5)
    rec_b_a = nrm(ks[17], (N_REC, D_RNN), 0.02)
    rec_w_x = nrm(ks[18], (N_REC, RNN_BLOCKS, RNN_BLOCK, RNN_BLOCK), RNN_BLOCK ** -0.5)
    rec_b_x = nrm(ks[19], (N_REC, D_RNN), 0.02)
    a0 = unif(ks[20], (N_REC, D_RNN), 0.9, 0.999)
    s0 = a0 ** (1.0 / RG_LRU_C)
    rec_lambda = jnp.log(s0) - jnp.log1p(-s0)
    rec_w_out = nrm(ks[21], (N_REC, D_RNN, D_MODEL), D_RNN ** -0.5)
    return {"x": x, "norm_g": norm_g, "ffn_w_in": ffn_w_in, "ffn_w_out": ffn_w_out,
            "hyb_w_in": hyb_w_in, "ssm_conv_w": ssm_conv_w, "ssm_conv_b": ssm_conv_b,
            "ssm_dt_bias": ssm_dt_bias, "ssm_a_log": ssm_a_log, "ssm_d": ssm_d,
            "ssm_norm_g": ssm_norm_g, "fox_b_f": fox_b_f, "hyb_w_out": hyb_w_out,
            "rec_w_in": rec_w_in, "rec_conv_w": rec_conv_w, "rec_conv_b": rec_conv_b,
            "rec_w_a": rec_w_a, "rec_b_a": rec_b_a, "rec_w_x": rec_w_x, "rec_b_x": rec_b_x,
            "rec_lambda": rec_lambda, "rec_w_out": rec_w_out}


def _fwd_reference(x, norm_g, ffn_w_in, ffn_w_out, hyb_w_in, ssm_conv_w, ssm_conv_b, ssm_dt_bias,
              ssm_a_log, ssm_d, ssm_norm_g, fox_b_f, hyb_w_out, rec_w_in, rec_conv_w, rec_conv_b,
              rec_w_a, rec_b_a, rec_w_x, rec_b_x, rec_lambda, rec_w_out):
    for layer in range(DEPTH):
        g = norm_g[layer]
        h = swiglu(rms_norm(x, g[0]), ffn_w_in[layer, 0], ffn_w_out[layer, 0])
        x = x + 0.5 * rms_norm(h, g[1])
        h = rms_norm(x, g[2])
        if layer % 2 == 0:
            i = layer // 2
            h = hybrid_mixer(h, hyb_w_in[i], ssm_conv_w[i], ssm_conv_b[i], ssm_dt_bias[i],
                             ssm_a_log[i], ssm_d[i], ssm_norm_g[i], fox_b_f[i], hyb_w_out[i])
        else:
            j = layer // 2
            h = recurrent_mixer(h, rec_w_in[j], rec_conv_w[j], rec_conv_b[j], rec_w_a[j], rec_b_a[j],
                                rec_w_x[j], rec_b_x[j], rec_lambda[j], rec_w_out[j])
        x = x + rms_norm(h, g[3])
        h = swiglu(rms_norm(x, g[4]), ffn_w_in[layer, 1], ffn_w_out[layer, 1])
        x = x + 0.5 * rms_norm(h, g[5])
    return x


import jax as _jax
import jax.numpy as _jnp

TWIN_FORMAT = 'train_step'
FWD_PARAMS = ['x', 'norm_g', 'ffn_w_in', 'ffn_w_out', 'hyb_w_in', 'ssm_conv_w', 'ssm_conv_b', 'ssm_dt_bias', 'ssm_a_log', 'ssm_d', 'ssm_norm_g', 'fox_b_f', 'hyb_w_out', 'rec_w_in', 'rec_conv_w', 'rec_conv_b', 'rec_w_a', 'rec_b_a', 'rec_w_x', 'rec_b_x', 'rec_lambda', 'rec_w_out']
TWIN_WEIGHTS = ['norm_g', 'ffn_w_in', 'ffn_w_out', 'hyb_w_in', 'ssm_conv_w', 'ssm_conv_b', 'ssm_dt_bias', 'ssm_a_log', 'ssm_d', 'ssm_norm_g', 'fox_b_f', 'hyb_w_out', 'rec_w_in', 'rec_conv_w', 'rec_conv_b', 'rec_w_a', 'rec_b_a', 'rec_w_x', 'rec_b_x', 'rec_lambda', 'rec_w_out']
TWIN_DIFF_INPUT = 'x'
TWIN_INPUTS = ['x', 'norm_g', 'ffn_w_in', 'ffn_w_out', 'hyb_w_in', 'ssm_conv_w', 'ssm_conv_b', 'ssm_dt_bias', 'ssm_a_log', 'ssm_d', 'ssm_norm_g', 'fox_b_f', 'hyb_w_out', 'rec_w_in', 'rec_conv_w', 'rec_conv_b', 'rec_w_a', 'rec_b_a', 'rec_w_x', 'rec_b_x', 'rec_lambda', 'rec_w_out', 'loss_target', 'm_norm_g', 'm_ffn_w_in', 'm_ffn_w_out', 'm_hyb_w_in', 'm_ssm_conv_w', 'm_ssm_conv_b', 'm_ssm_dt_bias', 'm_ssm_a_log', 'm_ssm_d', 'm_ssm_norm_g', 'm_fox_b_f', 'm_hyb_w_out', 'm_rec_w_in', 'm_rec_conv_w', 'm_rec_conv_b', 'm_rec_w_a', 'm_rec_b_a', 'm_rec_w_x', 'm_rec_b_x', 'm_rec_lambda', 'm_rec_w_out', 'v_norm_g', 'v_ffn_w_in', 'v_ffn_w_out', 'v_hyb_w_in', 'v_ssm_conv_w', 'v_ssm_conv_b', 'v_ssm_dt_bias', 'v_ssm_a_log', 'v_ssm_d', 'v_ssm_norm_g', 'v_fox_b_f', 'v_hyb_w_out', 'v_rec_w_in', 'v_rec_conv_w', 'v_rec_conv_b', 'v_rec_w_a', 'v_rec_b_a', 'v_rec_w_x', 'v_rec_b_x', 'v_rec_lambda', 'v_rec_w_out']
TWIN_OUTPUTS = ['loss', 'grad_x', 'grad_norm_g', 'grad_ffn_w_in', 'grad_ffn_w_out', 'grad_hyb_w_in', 'grad_ssm_conv_w', 'grad_ssm_conv_b', 'grad_ssm_dt_bias', 'grad_ssm_a_log', 'grad_ssm_d', 'grad_ssm_norm_g', 'grad_fox_b_f', 'grad_hyb_w_out', 'grad_rec_w_in', 'grad_rec_conv_w', 'grad_rec_conv_b', 'grad_rec_w_a', 'grad_rec_b_a', 'grad_rec_w_x', 'grad_rec_b_x', 'grad_rec_lambda', 'grad_rec_w_out', 'delta_norm_g', 'delta_ffn_w_in', 'delta_ffn_w_out', 'delta_hyb_w_in', 'delta_ssm_conv_w', 'delta_ssm_conv_b', 'delta_ssm_dt_bias', 'delta_ssm_a_log', 'delta_ssm_d', 'delta_ssm_norm_g', 'delta_fox_b_f', 'delta_hyb_w_out', 'delta_rec_w_in', 'delta_rec_conv_w', 'delta_rec_conv_b', 'delta_rec_w_a', 'delta_rec_b_a', 'delta_rec_w_x', 'delta_rec_b_x', 'delta_rec_lambda', 'delta_rec_w_out', 'new_m_norm_g', 'new_m_ffn_w_in', 'new_m_ffn_w_out', 'new_m_hyb_w_in', 'new_m_ssm_conv_w', 'new_m_ssm_conv_b', 'new_m_ssm_dt_bias', 'new_m_ssm_a_log', 'new_m_ssm_d', 'new_m_ssm_norm_g', 'new_m_fox_b_f', 'new_m_hyb_w_out', 'new_m_rec_w_in', 'new_m_rec_conv_w', 'new_m_rec_conv_b', 'new_m_rec_w_a', 'new_m_rec_b_a', 'new_m_rec_w_x', 'new_m_rec_b_x', 'new_m_rec_lambda', 'new_m_rec_w_out', 'new_v_norm_g', 'new_v_ffn_w_in', 'new_v_ffn_w_out', 'new_v_hyb_w_in', 'new_v_ssm_conv_w', 'new_v_ssm_conv_b', 'new_v_ssm_dt_bias', 'new_v_ssm_a_log', 'new_v_ssm_d', 'new_v_ssm_norm_g', 'new_v_fox_b_f', 'new_v_hyb_w_out', 'new_v_rec_w_in', 'new_v_rec_conv_w', 'new_v_rec_conv_b', 'new_v_rec_w_a', 'new_v_rec_b_a', 'new_v_rec_w_x', 'new_v_rec_b_x', 'new_v_rec_lambda', 'new_v_rec_w_out']
TWIN_LEAF_KINDS = {'loss': 'loss', 'grad_x': 'grad_x', 'grad_norm_g': 'grad_w', 'grad_ffn_w_in': 'grad_w', 'grad_ffn_w_out': 'grad_w', 'grad_hyb_w_in': 'grad_w', 'grad_ssm_conv_w': 'grad_w', 'grad_ssm_conv_b': 'grad_w', 'grad_ssm_dt_bias': 'grad_w', 'grad_ssm_a_log': 'grad_w', 'grad_ssm_d': 'grad_w', 'grad_ssm_norm_g': 'grad_w', 'grad_fox_b_f': 'grad_w', 'grad_hyb_w_out': 'grad_w', 'grad_rec_w_in': 'grad_w', 'grad_rec_conv_w': 'grad_w', 'grad_rec_conv_b': 'grad_w', 'grad_rec_w_a': 'grad_w', 'grad_rec_b_a': 'grad_w', 'grad_rec_w_x': 'grad_w', 'grad_rec_b_x': 'grad_w', 'grad_rec_lambda': 'grad_w', 'grad_rec_w_out': 'grad_w', 'delta_norm_g': 'delta_w', 'delta_ffn_w_in': 'delta_w', 'delta_ffn_w_out': 'delta_w', 'delta_hyb_w_in': 'delta_w', 'delta_ssm_conv_w': 'delta_w', 'delta_ssm_conv_b': 'delta_w', 'delta_ssm_dt_bias': 'delta_w', 'delta_ssm_a_log': 'delta_w', 'delta_ssm_d': 'delta_w', 'delta_ssm_norm_g': 'delta_w', 'delta_fox_b_f': 'delta_w', 'delta_hyb_w_out': 'delta_w', 'delta_rec_w_in': 'delta_w', 'delta_rec_conv_w': 'delta_w', 'delta_rec_conv_b': 'delta_w', 'delta_rec_w_a': 'delta_w', 'delta_rec_b_a': 'delta_w', 'delta_rec_w_x': 'delta_w', 'delta_rec_b_x': 'delta_w', 'delta_rec_lambda': 'delta_w', 'delta_rec_w_out': 'delta_w', 'new_m_norm_g': 'new_m', 'new_m_ffn_w_in': 'new_m', 'new_m_ffn_w_out': 'new_m', 'new_m_hyb_w_in': 'new_m', 'new_m_ssm_conv_w': 'new_m', 'new_m_ssm_conv_b': 'new_m', 'new_m_ssm_dt_bias': 'new_m', 'new_m_ssm_a_log': 'new_m', 'new_m_ssm_d': 'new_m', 'new_m_ssm_norm_g': 'new_m', 'new_m_fox_b_f': 'new_m', 'new_m_hyb_w_out': 'new_m', 'new_m_rec_w_in': 'new_m', 'new_m_rec_conv_w': 'new_m', 'new_m_rec_conv_b': 'new_m', 'new_m_rec_w_a': 'new_m', 'new_m_rec_b_a': 'new_m', 'new_m_rec_w_x': 'new_m', 'new_m_rec_b_x': 'new_m', 'new_m_rec_lambda': 'new_m', 'new_m_rec_w_out': 'new_m', 'new_v_norm_g': 'new_v', 'new_v_ffn_w_in': 'new_v', 'new_v_ffn_w_out': 'new_v', 'new_v_hyb_w_in': 'new_v', 'new_v_ssm_conv_w': 'new_v', 'new_v_ssm_conv_b': 'new_v', 'new_v_ssm_dt_bias': 'new_v', 'new_v_ssm_a_log': 'new_v', 'new_v_ssm_d': 'new_v', 'new_v_ssm_norm_g': 'new_v', 'new_v_fox_b_f': 'new_v', 'new_v_hyb_w_out': 'new_v', 'new_v_rec_w_in': 'new_v', 'new_v_rec_conv_w': 'new_v', 'new_v_rec_conv_b': 'new_v', 'new_v_rec_w_a': 'new_v', 'new_v_rec_b_a': 'new_v', 'new_v_rec_w_x': 'new_v', 'new_v_rec_b_x': 'new_v', 'new_v_rec_lambda': 'new_v', 'new_v_rec_w_out': 'new_v'}


def _forward(args):
    return _fwd_reference(*[args[k] for k in FWD_PARAMS])


def _output_shape():
    out = _jax.eval_shape(lambda: _forward(_fwd_setup_inputs(0)))
    return out.shape, out.dtype

N_MICROBATCH = 1
ADAM_LR = 0.001
ADAM_B1 = 0.9
ADAM_B2 = 0.999
ADAM_EPS = 1e-08
ADAM_WD = 0.01
ADAM_STEP = 10
PER_EXAMPLE_BATCH_AXIS = {'x': 0, 'loss_target': 0}
SHARED_INPUTS = []
_WEIGHT_DTYPES = {'norm_g': _jnp.float32, 'ffn_w_in': _jnp.float32, 'ffn_w_out': _jnp.float32, 'hyb_w_in': _jnp.float32, 'ssm_conv_w': _jnp.float32, 'ssm_conv_b': _jnp.float32, 'ssm_dt_bias': _jnp.float32, 'ssm_a_log': _jnp.float32, 'ssm_d': _jnp.float32, 'ssm_norm_g': _jnp.float32, 'fox_b_f': _jnp.float32, 'hyb_w_out': _jnp.float32, 'rec_w_in': _jnp.float32, 'rec_conv_w': _jnp.float32, 'rec_conv_b': _jnp.float32, 'rec_w_a': _jnp.float32, 'rec_b_a': _jnp.float32, 'rec_w_x': _jnp.float32, 'rec_b_x': _jnp.float32, 'rec_lambda': _jnp.float32, 'rec_w_out': _jnp.float32}
MOMENT_SCALE = {'norm_g': 1.431428e+01, 'ffn_w_in': 5.003939e-01, 'ffn_w_out': 9.562342e-01, 'hyb_w_in': 1.182683e+00, 'ssm_conv_w': 3.780729e+00, 'ssm_conv_b': 1.216765e+01, 'ssm_dt_bias': 3.054498e+00, 'ssm_a_log': 1.747847e+01, 'ssm_d': 1.727885e+01, 'ssm_norm_g': 7.193494e+00, 'fox_b_f': 2.703849e+00, 'hyb_w_out': 8.078231e+00, 'rec_w_in': 5.237681e+00, 'rec_conv_w': 6.742742e+00, 'rec_conv_b': 4.069591e+01, 'rec_w_a': 1.163730e+00, 'rec_b_a': 9.552074e-01, 'rec_w_x': 2.142783e+00, 'rec_b_x': 2.529856e+00, 'rec_lambda': 2.080558e+00, 'rec_w_out': 7.267162e+00}


def _to_microbatches(a, axis):
    t = _jnp.moveaxis(a, axis, 0)
    t = t.reshape((N_MICROBATCH, t.shape[0] // N_MICROBATCH) + t.shape[1:])
    return _jnp.moveaxis(t, 1, axis + 1)


def setup_inputs(seed: int = 0) -> dict:
    inp = _fwd_setup_inputs(seed)
    key = _jax.random.fold_in(_jax.random.key(seed), 7919)
    shape, _ = _output_shape()
    out = dict(inp)
    out["loss_target"] = _jax.random.normal(_jax.random.fold_in(key, 0), shape, _jnp.float32)
    for i, name in enumerate(TWIN_WEIGHTS):
        w = inp[name].astype(_jnp.float32)
        if MOMENT_SCALE is None:
            s = _jnp.sqrt(_jnp.mean(_jnp.square(w)) + 1e-30)
        else:
            s = MOMENT_SCALE[name]
        km, kv = _jax.random.split(_jax.random.fold_in(key, i + 1))
        out[name] = w
        out["m_" + name] = s * _jax.random.normal(km, w.shape, _jnp.float32)
        out["v_" + name] = (s * s) * _jax.random.uniform(kv, w.shape, _jnp.float32, 0.5, 1.5)
    if N_MICROBATCH > 1:
        for name, axis in PER_EXAMPLE_BATCH_AXIS.items():
            out[name] = _to_microbatches(out[name], axis)
    return {'x': out['x'], 'norm_g': out['norm_g'], 'ffn_w_in': out['ffn_w_in'], 'ffn_w_out': out['ffn_w_out'], 'hyb_w_in': out['hyb_w_in'], 'ssm_conv_w': out['ssm_conv_w'], 'ssm_conv_b': out['ssm_conv_b'], 'ssm_dt_bias': out['ssm_dt_bias'], 'ssm_a_log': out['ssm_a_log'], 'ssm_d': out['ssm_d'], 'ssm_norm_g': out['ssm_norm_g'], 'fox_b_f': out['fox_b_f'], 'hyb_w_out': out['hyb_w_out'], 'rec_w_in': out['rec_w_in'], 'rec_conv_w': out['rec_conv_w'], 'rec_conv_b': out['rec_conv_b'], 'rec_w_a': out['rec_w_a'], 'rec_b_a': out['rec_b_a'], 'rec_w_x': out['rec_w_x'], 'rec_b_x': out['rec_b_x'], 'rec_lambda': out['rec_lambda'], 'rec_w_out': out['rec_w_out'], 'loss_target': out['loss_target'], 'm_norm_g': out['m_norm_g'], 'm_ffn_w_in': out['m_ffn_w_in'], 'm_ffn_w_out': out['m_ffn_w_out'], 'm_hyb_w_in': out['m_hyb_w_in'], 'm_ssm_conv_w': out['m_ssm_conv_w'], 'm_ssm_conv_b': out['m_ssm_conv_b'], 'm_ssm_dt_bias': out['m_ssm_dt_bias'], 'm_ssm_a_log': out['m_ssm_a_log'], 'm_ssm_d': out['m_ssm_d'], 'm_ssm_norm_g': out['m_ssm_norm_g'], 'm_fox_b_f': out['m_fox_b_f'], 'm_hyb_w_out': out['m_hyb_w_out'], 'm_rec_w_in': out['m_rec_w_in'], 'm_rec_conv_w': out['m_rec_conv_w'], 'm_rec_conv_b': out['m_rec_conv_b'], 'm_rec_w_a': out['m_rec_w_a'], 'm_rec_b_a': out['m_rec_b_a'], 'm_rec_w_x': out['m_rec_w_x'], 'm_rec_b_x': out['m_rec_b_x'], 'm_rec_lambda': out['m_rec_lambda'], 'm_rec_w_out': out['m_rec_w_out'], 'v_norm_g': out['v_norm_g'], 'v_ffn_w_in': out['v_ffn_w_in'], 'v_ffn_w_out': out['v_ffn_w_out'], 'v_hyb_w_in': out['v_hyb_w_in'], 'v_ssm_conv_w': out['v_ssm_conv_w'], 'v_ssm_conv_b': out['v_ssm_conv_b'], 'v_ssm_dt_bias': out['v_ssm_dt_bias'], 'v_ssm_a_log': out['v_ssm_a_log'], 'v_ssm_d': out['v_ssm_d'], 'v_ssm_norm_g': out['v_ssm_norm_g'], 'v_fox_b_f': out['v_fox_b_f'], 'v_hyb_w_out': out['v_hyb_w_out'], 'v_rec_w_in': out['v_rec_w_in'], 'v_rec_conv_w': out['v_rec_conv_w'], 'v_rec_conv_b': out['v_rec_conv_b'], 'v_rec_w_a': out['v_rec_w_a'], 'v_rec_b_a': out['v_rec_b_a'], 'v_rec_w_x': out['v_rec_w_x'], 'v_rec_b_x': out['v_rec_b_x'], 'v_rec_lambda': out['v_rec_lambda'], 'v_rec_w_out': out['v_rec_w_out']}


def _loss(weights, diff, rest, loss_target):
    with _jax.named_scope("forward"):
        args = {**rest, TWIN_DIFF_INPUT: diff, **{k: w.astype(_WEIGHT_DTYPES[k]) for k, w in weights.items()}}
        y = _forward(args)
    with _jax.named_scope("loss_head"):
        err = _jnp.square(y.astype(_jnp.float32) - loss_target)
        return 0.5 * _jnp.sum(_jnp.mean(err, axis=-1)) if err.ndim else 0.5 * err


def _adamw(w, g, m, v):
    m = ADAM_B1 * m + (1.0 - ADAM_B1) * g
    v = ADAM_B2 * v + (1.0 - ADAM_B2) * _jnp.square(g)
    m_hat = m / (1.0 - ADAM_B1 ** ADAM_STEP)
    v_hat = v / (1.0 - ADAM_B2 ** ADAM_STEP)
    delta = -ADAM_LR * (m_hat / (_jnp.sqrt(v_hat) + ADAM_EPS) + ADAM_WD * w)
    return delta, m, v


def reference(x, norm_g, ffn_w_in, ffn_w_out, hyb_w_in, ssm_conv_w, ssm_conv_b, ssm_dt_bias, ssm_a_log, ssm_d, ssm_norm_g, fox_b_f, hyb_w_out, rec_w_in, rec_conv_w, rec_conv_b, rec_w_a, rec_b_a, rec_w_x, rec_b_x, rec_lambda, rec_w_out, loss_target, m_norm_g, m_ffn_w_in, m_ffn_w_out, m_hyb_w_in, m_ssm_conv_w, m_ssm_conv_b, m_ssm_dt_bias, m_ssm_a_log, m_ssm_d, m_ssm_norm_g, m_fox_b_f, m_hyb_w_out, m_rec_w_in, m_rec_conv_w, m_rec_conv_b, m_rec_w_a, m_rec_b_a, m_rec_w_x, m_rec_b_x, m_rec_lambda, m_rec_w_out, v_norm_g, v_ffn_w_in, v_ffn_w_out, v_hyb_w_in, v_ssm_conv_w, v_ssm_conv_b, v_ssm_dt_bias, v_ssm_a_log, v_ssm_d, v_ssm_norm_g, v_fox_b_f, v_hyb_w_out, v_rec_w_in, v_rec_conv_w, v_rec_conv_b, v_rec_w_a, v_rec_b_a, v_rec_w_x, v_rec_b_x, v_rec_lambda, v_rec_w_out):
    given = dict(x=x, norm_g=norm_g, ffn_w_in=ffn_w_in, ffn_w_out=ffn_w_out, hyb_w_in=hyb_w_in, ssm_conv_w=ssm_conv_w, ssm_conv_b=ssm_conv_b, ssm_dt_bias=ssm_dt_bias, ssm_a_log=ssm_a_log, ssm_d=ssm_d, ssm_norm_g=ssm_norm_g, fox_b_f=fox_b_f, hyb_w_out=hyb_w_out, rec_w_in=rec_w_in, rec_conv_w=rec_conv_w, rec_conv_b=rec_conv_b, rec_w_a=rec_w_a, rec_b_a=rec_b_a, rec_w_x=rec_w_x, rec_b_x=rec_b_x, rec_lambda=rec_lambda, rec_w_out=rec_w_out, loss_target=loss_target, m_norm_g=m_norm_g, m_ffn_w_in=m_ffn_w_in, m_ffn_w_out=m_ffn_w_out, m_hyb_w_in=m_hyb_w_in, m_ssm_conv_w=m_ssm_conv_w, m_ssm_conv_b=m_ssm_conv_b, m_ssm_dt_bias=m_ssm_dt_bias, m_ssm_a_log=m_ssm_a_log, m_ssm_d=m_ssm_d, m_ssm_norm_g=m_ssm_norm_g, m_fox_b_f=m_fox_b_f, m_hyb_w_out=m_hyb_w_out, m_rec_w_in=m_rec_w_in, m_rec_conv_w=m_rec_conv_w, m_rec_conv_b=m_rec_conv_b, m_rec_w_a=m_rec_w_a, m_rec_b_a=m_rec_b_a, m_rec_w_x=m_rec_w_x, m_rec_b_x=m_rec_b_x, m_rec_lambda=m_rec_lambda, m_rec_w_out=m_rec_w_out, v_norm_g=v_norm_g, v_ffn_w_in=v_ffn_w_in, v_ffn_w_out=v_ffn_w_out, v_hyb_w_in=v_hyb_w_in, v_ssm_conv_w=v_ssm_conv_w, v_ssm_conv_b=v_ssm_conv_b, v_ssm_dt_bias=v_ssm_dt_bias, v_ssm_a_log=v_ssm_a_log, v_ssm_d=v_ssm_d, v_ssm_norm_g=v_ssm_norm_g, v_fox_b_f=v_fox_b_f, v_hyb_w_out=v_hyb_w_out, v_rec_w_in=v_rec_w_in, v_rec_conv_w=v_rec_conv_w, v_rec_conv_b=v_rec_conv_b, v_rec_w_a=v_rec_w_a, v_rec_b_a=v_rec_b_a, v_rec_w_x=v_rec_w_x, v_rec_b_x=v_rec_b_x, v_rec_lambda=v_rec_lambda, v_rec_w_out=v_rec_w_out)
    weights = {n: given[n] for n in TWIN_WEIGHTS}
    shared = {n: given[n] for n in SHARED_INPUTS}
    per_example = {n: given[n] for n in ['x']}
    grad_fn = _jax.value_and_grad(_loss, argnums=(0, 1))

    def one_microbatch(ex, loss_target):
        ex = dict(ex)
        diff = ex.pop(TWIN_DIFF_INPUT)
        return grad_fn(weights, diff, {**shared, **ex}, loss_target)

    if N_MICROBATCH == 1:
        loss, (grad_w, grad_x) = one_microbatch(per_example, given["loss_target"])
    else:
        def body(carry, xs):
            loss_sum, grad_sum = carry
            l_k, (gw_k, gx_k) = one_microbatch(xs[0], xs[1])
            with _jax.named_scope("update"):
                return (loss_sum + l_k, _jax.tree.map(_jnp.add, grad_sum, gw_k)), gx_k

        init = (_jnp.zeros((), _jnp.float32), _jax.tree.map(_jnp.zeros_like, weights))
        (loss, grad_w), grad_x = _jax.lax.scan(body, init, (per_example, given["loss_target"]))
    with _jax.named_scope("update"):
        delta_w, new_m, new_v = {}, {}, {}
        for n in TWIN_WEIGHTS:
            delta_w[n], new_m[n], new_v[n] = _adamw(weights[n], grad_w[n], given["m_" + n], given["v_" + n])
    return (loss, grad_x, *[grad_w[n] for n in TWIN_WEIGHTS], *[delta_w[n] for n in TWIN_WEIGHTS],
            *[new_m[n] for n in TWIN_WEIGHTS], *[new_v[n] for n in TWIN_WEIGHTS])
```

```python
import functools

import jax
import jax.numpy as jnp
from jax import lax
from jax.experimental import pallas as pl
from jax.experimental.pallas import tpu as pltpu

F32, BF16 = jnp.float32, jnp.bfloat16

NORM_EPS = 1e-6
CONV_K = 4
SSM_HEAD_DIM = 64
SSM_STATE = 128
SSM_GROUPS = 2
FOX_HEAD_DIM = 128
RNN_BLOCK = 128
RG_LRU_C = 8.0
ADAM_LR, ADAM_B1, ADAM_B2, ADAM_EPS, ADAM_WD, ADAM_STEP = 0.001, 0.9, 0.999, 1e-08, 0.01, 10

LANES = 128
SUBLANES = 8
VMEM_LIMIT = 48 * 1024 * 1024
ATT_TILE = 256
NEG = -1e30

NT = (((1,), (1,)), ((), ()))
NN = (((1,), (0,)), ((), ()))
TN = (((0,), (0,)), ((), ()))

BIG = ("ffn_w_in", "ffn_w_out", "hyb_w_in", "hyb_w_out", "rec_w_in", "rec_w_out")
SMALL_SHARDED = ("norm_g", "ssm_conv_w", "rec_conv_w", "rec_conv_b", "rec_b_a", "rec_b_x", "rec_lambda")
SMALL_REPL = ("ssm_conv_b", "ssm_dt_bias", "ssm_a_log", "ssm_d", "ssm_norm_g", "fox_b_f", "rec_w_a", "rec_w_x")
WEIGHTS = ("norm_g", "ffn_w_in", "ffn_w_out", "hyb_w_in", "ssm_conv_w", "ssm_conv_b", "ssm_dt_bias", "ssm_a_log",
           "ssm_d", "ssm_norm_g", "fox_b_f", "hyb_w_out", "rec_w_in", "rec_conv_w", "rec_conv_b", "rec_w_a",
           "rec_b_a", "rec_w_x", "rec_b_x", "rec_lambda", "rec_w_out")
SHARD_AXIS = {"norm_g": 2, "ffn_w_in": 3, "ffn_w_out": 2, "hyb_w_in": 2, "ssm_conv_w": 2, "hyb_w_out": 1,
              "rec_w_in": 2, "rec_conv_w": 2, "rec_conv_b": 1, "rec_b_a": 1, "rec_b_x": 1, "rec_lambda": 1,
              "rec_w_out": 1}


def _params(*sem):
    return pltpu.CompilerParams(dimension_semantics=sem if sem else None, vmem_limit_bytes=VMEM_LIMIT)


def _tile(dim, pref):
    if dim <= pref:
        return dim
    for align in (LANES, SUBLANES):
        t = (pref // align) * align
        while t >= align:
            if dim % t == 0:
                return t
            t -= align
    return dim


def _sigmoid(x):
    return 1.0 / (1.0 + jnp.exp(-x))


def _softplus(x):
    return jnp.maximum(x, 0.0) + jnp.log(1.0 + jnp.exp(-jnp.abs(x)))


def _log_sigmoid(x):
    return -_softplus(-x)


def _silu(x):
    return x * _sigmoid(x)


def _gelu_tanh(x):
    return 0.5 * x * (1.0 + jnp.tanh(0.7978845608028654 * (x + 0.044715 * x * x * x)))


def _neg_expm1(x):
    series = -x * (1.0 + x * (0.5 + x * (1.0 / 6.0)))
    return jnp.where(x > -1e-2, series, 1.0 - jnp.exp(x))


def _rms(x, g):
    xf = x.astype(F32)
    return xf * lax.rsqrt(jnp.mean(xf * xf, axis=-1, keepdims=True) + NORM_EPS) * g


def _mm(a, b, mode, out_dtype, name, tm=512, tn=512, tk=512):
    if mode == "nn":
        (M, K), (K2, N) = a.shape, b.shape
    elif mode == "nt":
        (M, K), (N, K2) = a.shape, b.shape
    else:
        (K, M), (K2, N) = a.shape, b.shape
    assert K == K2, (a.shape, b.shape, mode)
    tm, tn, tk = _tile(M, tm), _tile(N, tn), _tile(K, tk)
    nk = K // tk
    dims = {"nn": NN, "nt": NT, "tn": TN}[mode]
    if mode == "tn":
        a_spec = pl.BlockSpec((tk, tm), lambda n, m, k: (k, m))
    else:
        a_spec = pl.BlockSpec((tm, tk), lambda n, m, k: (m, k))
    if mode == "nt":
        b_spec = pl.BlockSpec((tn, tk), lambda n, m, k: (n, k))
    else:
        b_spec = pl.BlockSpec((tk, tn), lambda n, m, k: (k, n))

    def body(a_ref, b_ref, o_ref, *acc):
        p = lax.dot_general(a_ref[...].astype(BF16), b_ref[...].astype(BF16), dims, preferred_element_type=F32)
        if nk == 1:
            o_ref[...] = p.astype(o_ref.dtype)
        else:
            acc_ref, = acc
            k = pl.program_id(2)

            @pl.when(k == 0)
            def _():
                acc_ref[...] = p

            @pl.when(k > 0)
            def _():
                acc_ref[...] += p

            @pl.when(k == nk - 1)
            def _():
                o_ref[...] = acc_ref[...].astype(o_ref.dtype)

    return pl.pallas_call(
        body, name=name, grid=(N // tn, M // tm, nk),
        in_specs=[a_spec, b_spec], out_specs=pl.BlockSpec((tm, tn), lambda n, m, k: (m, n)),
        out_shape=jax.ShapeDtypeStruct((M, N), out_dtype),
        scratch_shapes=[pltpu.VMEM((tm, tn), F32)] if nk > 1 else [],
        compiler_params=_params("parallel", "parallel", "arbitrary"),
    )(a, b)


def _ew(fn, tiled, params, outs, reds, *, name, tm=256, cb=None, ncb=1):
    T = tiled[0][0].shape[0]
    cb = tiled[0][0].shape[1] if cb is None else cb
    tm = _tile(T, tm)
    in_specs, args = [], []
    for arr, off in tiled:
        in_specs.append(pl.BlockSpec((tm, cb), functools.partial(lambda n, i, o: (i, n + o), o=off)))
        args.append(arr)
    for arr, off in params:
        if arr.ndim == 2:
            in_specs.append(pl.BlockSpec((arr.shape[0], cb), functools.partial(lambda n, i, o: (0, n + o), o=off)))
        else:
            in_specs.append(pl.BlockSpec((None,) + arr.shape[1:], lambda n, i: (n, 0, 0)))
        args.append(arr)
    out_shape = [jax.ShapeDtypeStruct((T, cb * ncb), dt) for dt in outs]
    out_specs = [pl.BlockSpec((tm, cb), lambda n, i: (i, n)) for _ in outs]
    for shape in reds:
        out_shape.append(jax.ShapeDtypeStruct(shape, F32))
        if len(shape) == 2:
            out_specs.append(pl.BlockSpec((shape[0], cb), lambda n, i: (0, n)))
        else:
            out_specs.append(pl.BlockSpec((None,) + tuple(shape[1:]), lambda n, i: (n, 0, 0)))
    n_in, n_out = len(args), len(outs)

    def body(*refs):
        i = pl.program_id(1)
        res = fn(*[r[...] for r in refs[:n_in]])
        res = res if isinstance(res, (tuple, list)) else (res,)
        for r, v in zip(refs[n_in:n_in + n_out], res[:n_out]):
            r[...] = v.astype(r.dtype)
        for r, v in zip(refs[n_in + n_out:], res[n_out:]):
            @pl.when(i == 0)
            def _():
                r[...] = jnp.zeros(r.shape, r.dtype)

            r[...] += v.astype(r.dtype).reshape(r.shape)

    res = pl.pallas_call(
        body, name=name, grid=(ncb, T // tm), in_specs=in_specs, out_specs=out_specs, out_shape=out_shape,
        compiler_params=_params("arbitrary", "arbitrary"),
    )(*args)
    return res[0] if len(res) == 1 else tuple(res)


def _rms_fwd(x, g, name):
    return _ew(lambda xv, gv: _rms(xv, gv), [(x, 0)], [(g, 0)], [BF16], [], name=name)


def _rms_bwd_add(x, g, dn, dres, name):
    def fn(xv, dnv, drv, gv):
        _, vjp = jax.vjp(_rms, xv, gv)
        dx, dg = vjp(dnv.astype(F32))
        return drv + dx, dg
    return _ew(fn, [(x, 0), (dn, 0), (dres, 0)], [(g, 0)], [F32], [g.shape], name=name)


def _post_fwd(x, h, g, w, name):
    return _ew(lambda xv, hv, gv: xv + w * _rms(hv, gv), [(x, 0), (h, 0)], [(g, 0)], [F32], [], name=name)


def _post_bwd(dy, h, g, w, name):
    def fn(dyv, hv, gv):
        _, vjp = jax.vjp(lambda a, b: w * _rms(a, b), hv, gv)
        return vjp(dyv)
    return _ew(fn, [(dy, 0), (h, 0)], [(g, 0)], [BF16], [g.shape], name=name)


def _swiglu_fwd(gu, name):
    F = gu.shape[1] // 2
    cb = F // 2
    return _ew(lambda gv, uv: _silu(gv.astype(F32)) * uv.astype(F32), [(gu, 0), (gu, 2)], [], [BF16], [],
               name=name, cb=cb, ncb=2)


def _swiglu_bwd(gu, da, name):
    T, F2 = gu.shape
    cb = F2 // 4
    tm = _tile(T, 256)

    def body(g_ref, u_ref, da_ref, o_ref):
        n = pl.program_id(0)
        g, u, d = g_ref[...].astype(F32), u_ref[...].astype(F32), da_ref[...].astype(F32)
        s = _sigmoid(g)

        @pl.when(n < 2)
        def _():
            o_ref[...] = (d * u * (s * (1.0 + g * (1.0 - s)))).astype(o_ref.dtype)

        @pl.when(n >= 2)
        def _():
            o_ref[...] = (d * g * s).astype(o_ref.dtype)

    return pl.pallas_call(
        body, name=name, grid=(4, T // tm),
        in_specs=[pl.BlockSpec((tm, cb), lambda n, i: (i, n % 2)), pl.BlockSpec((tm, cb), lambda n, i: (i, n % 2 + 2)),
                  pl.BlockSpec((tm, cb), lambda n, i: (i, n % 2))],
        out_specs=pl.BlockSpec((tm, cb), lambda n, i: (i, n)),
        out_shape=jax.ShapeDtypeStruct((T, F2), BF16), compiler_params=_params("arbitrary", "arbitrary"),
    )(gu, gu, da)


def _ffn_fwd(x, g_pre, g_post, w_in, w_out, tag):
    n = _rms_fwd(x, g_pre, f"{tag}_rms")
    gu = _mm(n, w_in, "nn", BF16, f"{tag}_mm_in", tn=1408, tk=1024)
    a = _swiglu_fwd(gu, f"{tag}_swiglu")
    h = _mm(a, w_out, "nn", F32, f"{tag}_mm_out", tn=1024, tk=1408)
    return _post_fwd(x, h, g_post, 0.5, f"{tag}_post"), (x, n, gu, a, h)


def _ffn_bwd(dy, saved, g_pre, g_post, w_in, w_out, tag):
    x, n, gu, a, h = saved
    dh, dg_post = _post_bwd(dy, h, g_post, 0.5, f"{tag}_post_b")
    da = _mm(dh, w_out, "nt", BF16, f"{tag}_mm_da", tn=1408, tk=1024)
    dw_out = _mm(a, dh, "tn", BF16, f"{tag}_mm_dwout", tm=1408, tn=1024, tk=512)
    dgu = _swiglu_bwd(gu, da, f"{tag}_swiglu_b")
    dn = _mm(dgu, w_in, "nt", F32, f"{tag}_mm_dn", tn=1024, tk=1408)
    dw_in = _mm(n, dgu, "tn", BF16, f"{tag}_mm_dwin", tm=1024, tn=704, tk=512)
    dy2, dg_pre = _rms_bwd_add(x, g_pre, dn, dy, f"{tag}_rms_b")
    return dy2, dg_pre, dg_post, dw_in, dw_out


def _shift_down(x, s):
    if s == 0:
        return x
    rows = lax.broadcasted_iota(jnp.int32, x.shape, 0)
    return jnp.where(rows >= s, pltpu.roll(x, s, 0), 0.0)


def _shift_up(x, s):
    if s == 0:
        return x
    T = x.shape[0]
    rows = lax.broadcasted_iota(jnp.int32, x.shape, 0)
    return jnp.where(rows < T - s, pltpu.roll(x, T - s, 0), 0.0)


def _conv_pre(x, w, b):
    y = b
    for k in range(CONV_K):
        y = y + w[k:k + 1, :] * _shift_down(x, CONV_K - 1 - k)
    return y


def _conv_fwd(x, col0, C, w, b, act, name, ct=256):
    T = x.shape[0]
    off = col0 // ct

    def body(x_ref, w_ref, b_ref, o_ref):
        y = _conv_pre(x_ref[...], w_ref[...], b_ref[...])
        o_ref[...] = _silu(y) if act else y

    return pl.pallas_call(
        body, name=name, grid=(C // ct,),
        in_specs=[pl.BlockSpec((T, ct), lambda j: (0, j + off)), pl.BlockSpec((CONV_K, ct), lambda j: (0, j)),
                  pl.BlockSpec((1, ct), lambda j: (0, j))],
        out_specs=pl.BlockSpec((T, ct), lambda j: (0, j)), out_shape=jax.ShapeDtypeStruct((T, C), F32),
        compiler_params=_params("parallel"),
    )(x, w, b)


def _conv_bwd(x, col0, C, w, b, dyact, act, name, ct=256):
    T = x.shape[0]
    off = col0 // ct

    def body(x_ref, w_ref, b_ref, dy_ref, dx_ref, dw_ref, db_ref):
        xv, wv = x_ref[...], w_ref[...]
        dy = dy_ref[...].astype(F32)
        if act:
            pre = _conv_pre(xv, wv, b_ref[...])
            s = _sigmoid(pre)
            dy = dy * (s * (1.0 + pre * (1.0 - s)))
        dx = jnp.zeros_like(dy)
        dws = []
        for k in range(CONV_K):
            dx = dx + wv[k:k + 1, :] * _shift_up(dy, CONV_K - 1 - k)
            dws.append(jnp.sum(dy * _shift_down(xv, CONV_K - 1 - k), axis=0, keepdims=True))
        dx_ref[...] = dx.astype(dx_ref.dtype)
        dw_ref[...] = jnp.concatenate(dws, axis=0)
        db_ref[...] = jnp.sum(dy, axis=0, keepdims=True)

    return pl.pallas_call(
        body, name=name, grid=(C // ct,),
        in_specs=[pl.BlockSpec((T, ct), lambda j: (0, j + off)), pl.BlockSpec((CONV_K, ct), lambda j: (0, j)),
                  pl.BlockSpec((1, ct), lambda j: (0, j)), pl.BlockSpec((T, ct), lambda j: (0, j))],
        out_specs=[pl.BlockSpec((T, ct), lambda j: (0, j)), pl.BlockSpec((CONV_K, ct), lambda j: (0, j)),
                   pl.BlockSpec((1, ct), lambda j: (0, j))],
        out_shape=[jax.ShapeDtypeStruct((T, C), BF16), jax.ShapeDtypeStruct((CONV_K, C), F32),
                   jax.ShapeDtypeStruct((1, C), F32)],
        compiler_params=_params("parallel"),
    )(x, w, b, dyact)


def _gate_act(v, n_ssm):
    lane = lax.broadcasted_iota(jnp.int32, v.shape, 1)
    return jnp.where(lane < n_ssm, _softplus(v), _log_sigmoid(v))


def _gates_fwd(proj, col0, bias, mult, n_ssm, name, tb=512):
    T = proj.shape[0]
    tb = _tile(T, tb)
    off = col0 // LANES

    def body(s_ref, bias_ref, mult_ref, act_ref, cs_ref, carry_ref):
        i = pl.program_id(0)

        @pl.when(i == 0)
        def _():
            carry_ref[...] = jnp.zeros_like(carry_ref)

        act = _gate_act(s_ref[...] + bias_ref[...], n_ssm)
        inc = act * mult_ref[...]
        r = lax.broadcasted_iota(jnp.int32, (tb, tb), 0)
        c = lax.broadcasted_iota(jnp.int32, (tb, tb), 1)
        tri = jnp.where(r >= c, 1.0, 0.0).astype(F32)
        cs = jnp.dot(tri, inc, precision=lax.Precision.HIGHEST, preferred_element_type=F32) + carry_ref[...]
        act_ref[...] = act
        cs_ref[...] = cs
        carry_ref[...] = cs[tb - 1:tb, :]

    return pl.pallas_call(
        body, name=name, grid=(T // tb,),
        in_specs=[pl.BlockSpec((tb, LANES), lambda i: (i, off)), pl.BlockSpec((1, LANES), lambda i: (0, 0)),
                  pl.BlockSpec((1, LANES), lambda i: (0, 0))],
        out_specs=[pl.BlockSpec((tb, LANES), lambda i: (i, 0))] * 2,
        out_shape=[jax.ShapeDtypeStruct((T, LANES), F32)] * 2,
        scratch_shapes=[pltpu.VMEM((1, LANES), F32)], compiler_params=_params("arbitrary"),
    )(proj, bias, mult)


def _gates_bwd(proj, col0, bias, mult, n_ssm, dact, dcs, name, tb=512):
    T = proj.shape[0]
    tb = _tile(T, tb)
    nb = T // tb
    off = col0 // LANES

    def body(s_ref, bias_ref, mult_ref, dact_ref, dcs_ref, ds_ref, dmult_ref, dbias_ref, carry_ref):
        i = pl.program_id(0)

        @pl.when(i == 0)
        def _():
            carry_ref[...] = jnp.zeros_like(carry_ref)
            dmult_ref[...] = jnp.zeros_like(dmult_ref)
            dbias_ref[...] = jnp.zeros_like(dbias_ref)

        v = s_ref[...] + bias_ref[...]
        act = _gate_act(v, n_ssm)
        r = lax.broadcasted_iota(jnp.int32, (tb, tb), 0)
        c = lax.broadcasted_iota(jnp.int32, (tb, tb), 1)
        tri = jnp.where(r <= c, 1.0, 0.0).astype(F32)
        dinc = jnp.dot(tri, dcs_ref[...], precision=lax.Precision.HIGHEST, preferred_element_type=F32) + carry_ref[...]
        carry_ref[...] = dinc[0:1, :]
        da = dact_ref[...] + dinc * mult_ref[...]
        sg = _sigmoid(v)
        lane = lax.broadcasted_iota(jnp.int32, v.shape, 1)
        dv = da * jnp.where(lane < n_ssm, sg, 1.0 - sg)
        ds_ref[...] = dv.astype(ds_ref.dtype)
        dmult_ref[...] += jnp.sum(dinc * act, axis=0, keepdims=True)
        dbias_ref[...] += jnp.sum(dv, axis=0, keepdims=True)

    rev = lambda i: (nb - 1 - i, 0)
    return pl.pallas_call(
        body, name=name, grid=(nb,),
        in_specs=[pl.BlockSpec((tb, LANES), lambda i: (nb - 1 - i, off)), pl.BlockSpec((1, LANES), lambda i: (0, 0)),
                  pl.BlockSpec((1, LANES), lambda i: (0, 0)), pl.BlockSpec((tb, LANES), rev),
                  pl.BlockSpec((tb, LANES), rev)],
        out_specs=[pl.BlockSpec((tb, LANES), rev), pl.BlockSpec((1, LANES), lambda i: (0, 0)),
                   pl.BlockSpec((1, LANES), lambda i: (0, 0))],
        out_shape=[jax.ShapeDtypeStruct((T, LANES), BF16), jax.ShapeDtypeStruct((1, LANES), F32),
                   jax.ShapeDtypeStruct((1, LANES), F32)],
        scratch_shapes=[pltpu.VMEM((1, LANES), F32)], compiler_params=_params("arbitrary"),
    )(proj, bias, mult, dact, dcs)


def _rep_layout(v):
    return jnp.repeat(v, LANES, axis=1)


def _row_layout(v, tk):
    T, H = v.shape
    return v.T.reshape(H, T // tk, 1, tk)


def _causal(tq):
    r = lax.broadcasted_iota(jnp.int32, (tq, tq), 0)
    c = lax.broadcasted_iota(jnp.int32, (tq, tq), 1)
    return r >= c


def _ssd_fwd(xbc, X, cs_rep, cs_row, name, tq=ATT_TILE):
    T = X.shape[0]
    tq = _tile(T, tq)
    nq = T // tq
    d_ssm = X.shape[1]
    gw = d_ssm // SSM_GROUPS
    hpg = gw // SSM_HEAD_DIM
    b_off = d_ssm // SSM_STATE
    c_off = b_off + SSM_GROUPS

    def body(c_ref, b_ref, x_ref, csq_ref, csk_ref, y_ref):
        i = pl.program_id(1)
        c = c_ref[...].astype(BF16)
        half = lax.broadcasted_iota(jnp.int32, (tq, LANES), 1) // SSM_HEAD_DIM
        mask = _causal(tq)

        def step(j, acc, masked):
            r0 = pl.multiple_of(j * tq, tq)
            s = lax.dot_general(c, b_ref[pl.ds(r0, tq), :].astype(BF16), NT, preferred_element_type=F32)
            out = []
            for p in range(hpg // 2):
                xp = x_ref[pl.ds(r0, tq), p * LANES:(p + 1) * LANES]
                a = acc[p]
                for e in range(2):
                    h = 2 * p + e
                    diff = jnp.tile(csq_ref[:, h * LANES:(h + 1) * LANES], (1, tq // LANES)) - csk_ref[h, j]
                    if masked:
                        diff = jnp.where(mask, diff, NEG)
                    pm = (s * jnp.exp(diff)).astype(BF16)
                    xm = jnp.where(half == e, xp, jnp.zeros_like(xp))
                    a = a + jnp.dot(pm, xm, preferred_element_type=F32)
                out.append(a)
            return tuple(out)

        acc = tuple(jnp.zeros((tq, LANES), F32) for _ in range(hpg // 2))
        acc = lax.fori_loop(0, i, lambda j, a: step(j, a, False), acc)
        acc = step(i, acc, True)
        y_ref[...] = jnp.concatenate(acc, axis=1)

    return pl.pallas_call(
        body, name=name, grid=(SSM_GROUPS, nq),
        in_specs=[pl.BlockSpec((tq, SSM_STATE), lambda g, i: (i, c_off + g)),
                  pl.BlockSpec((T, SSM_STATE), lambda g, i: (0, b_off + g)),
                  pl.BlockSpec((T, gw), lambda g, i: (0, g)),
                  pl.BlockSpec((tq, hpg * LANES), lambda g, i: (i, g)),
                  pl.BlockSpec((hpg, nq, 1, tq), lambda g, i: (g, 0, 0, 0))],
        out_specs=pl.BlockSpec((tq, gw), lambda g, i: (i, g)),
        out_shape=jax.ShapeDtypeStruct((T, d_ssm), F32), compiler_params=_params("parallel", "arbitrary"),
    )(xbc, xbc, X, cs_rep, cs_row)


def _ssd_bwd(xbc, X, cs_rep, cs_row, dY, name, tq=ATT_TILE):
    T = X.shape[0]
    tq = _tile(T, tq)
    nq = T // tq
    d_ssm = X.shape[1]
    gw = d_ssm // SSM_GROUPS
    hpg = gw // SSM_HEAD_DIM
    nheads = d_ssm // SSM_HEAD_DIM
    b_off = d_ssm // SSM_STATE
    c_off = b_off + SSM_GROUPS

    def body(c_ref, b_ref, x_ref, dy_ref, csq_ref, csk_ref, dc_ref, db_ref, dx_ref, dcsq_ref, dcsk_ref):
        i = pl.program_id(1)

        @pl.when(i == 0)
        def _():
            db_ref[...] = jnp.zeros_like(db_ref)
            dx_ref[...] = jnp.zeros_like(dx_ref)
            dcsk_ref[...] = jnp.zeros_like(dcsk_ref)

        c = c_ref[...].astype(BF16)
        half = lax.broadcasted_iota(jnp.int32, (tq, LANES), 1) // SSM_HEAD_DIM
        mask = _causal(tq)

        def step(j, carry, masked):
            dc_acc, rows = carry
            rows = list(rows)
            r0 = pl.multiple_of(j * tq, tq)
            b = b_ref[pl.ds(r0, tq), :].astype(BF16)
            s = lax.dot_general(c, b, NT, preferred_element_type=F32)
            ds_tot = jnp.zeros((tq, tq), F32)
            for p in range(hpg // 2):
                cols = slice(p * LANES, (p + 1) * LANES)
                xp = x_ref[pl.ds(r0, tq), cols]
                dyp = dy_ref[:, cols]
                dx_p = jnp.zeros((tq, LANES), F32)
                for e in range(2):
                    h = 2 * p + e
                    diff = jnp.tile(csq_ref[:, h * LANES:(h + 1) * LANES], (1, tq // LANES)) - csk_ref[h, j]
                    if masked:
                        diff = jnp.where(mask, diff, NEG)
                    decay = jnp.exp(diff)
                    dym = jnp.where(half == e, dyp, jnp.zeros_like(dyp))
                    g = lax.dot_general(dym, xp, NT, preferred_element_type=F32) * decay
                    ds_tot = ds_tot + g
                    m = g * s
                    rows[h] = rows[h] + jnp.sum(m, axis=1, keepdims=True)
                    dcsk_ref[h, j] -= jnp.sum(m, axis=0, keepdims=True)
                    pm = (s * decay).astype(BF16)
                    dx_p = dx_p + lax.dot_general(pm, dym, TN, preferred_element_type=F32)
                dx_ref[pl.ds(r0, tq), cols] += dx_p
            dsb = ds_tot.astype(BF16)
            dc_acc = dc_acc + jnp.dot(dsb, b, preferred_element_type=F32)
            db_ref[pl.ds(r0, tq), :] += lax.dot_general(dsb, c, TN, preferred_element_type=F32)
            return dc_acc, tuple(rows)

        carry = (jnp.zeros((tq, SSM_STATE), F32), tuple(jnp.zeros((tq, 1), F32) for _ in range(hpg)))
        carry = lax.fori_loop(0, i, lambda j, cr: step(j, cr, False), carry)
        dc_acc, rows = step(i, carry, True)
        dc_ref[...] = dc_acc
        dcsq_ref[...] = jnp.concatenate([jnp.broadcast_to(r, (tq, LANES)) for r in rows], axis=1)

    return pl.pallas_call(
        body, name=name, grid=(SSM_GROUPS, nq),
        in_specs=[pl.BlockSpec((tq, SSM_STATE), lambda g, i: (i, c_off + g)),
                  pl.BlockSpec((T, SSM_STATE), lambda g, i: (0, b_off + g)),
                  pl.BlockSpec((T, gw), lambda g, i: (0, g)),
                  pl.BlockSpec((tq, gw), lambda g, i: (i, g)),
                  pl.BlockSpec((tq, hpg * LANES), lambda g, i: (i, g)),
                  pl.BlockSpec((hpg, nq, 1, tq), lambda g, i: (g, 0, 0, 0))],
        out_specs=[pl.BlockSpec((tq, SSM_STATE), lambda g, i: (i, g)),
                   pl.BlockSpec((T, SSM_STATE), lambda g, i: (0, g)),
                   pl.BlockSpec((T, gw), lambda g, i: (0, g)),
                   pl.BlockSpec((tq, hpg * LANES), lambda g, i: (i, g)),
                   pl.BlockSpec((hpg, nq, 1, tq), lambda g, i: (g, 0, 0, 0))],
        out_shape=[jax.ShapeDtypeStruct((T, SSM_GROUPS * SSM_STATE), F32),
                   jax.ShapeDtypeStruct((T, SSM_GROUPS * SSM_STATE), F32),
                   jax.ShapeDtypeStruct((T, d_ssm), F32),
                   jax.ShapeDtypeStruct((T, nheads * LANES), F32),
                   jax.ShapeDtypeStruct((nheads, nq, 1, tq), F32)],
        compiler_params=_params("arbitrary", "arbitrary"),
    )(xbc, xbc, X, dY, cs_rep, cs_row)


def _fox_fwd(proj, q0, k0, v0, nh, cum_rep, cum_row, name, tq=ATT_TILE):
    T = proj.shape[0]
    tq = _tile(T, tq)
    nq = T // tq
    hd = FOX_HEAD_DIM
    scale = hd ** -0.5
    qo, ko, vo = q0 // hd, k0 // hd, v0 // hd

    def body(q_ref, k_ref, v_ref, cq_ref, ck_ref, o_ref, lse_ref):
        i = pl.program_id(1)
        q = q_ref[...].astype(BF16)
        cq = jnp.tile(cq_ref[...], (1, tq // LANES))
        mask = _causal(tq)

        def step(j, carry, masked):
            m, l, acc = carry
            r0 = pl.multiple_of(j * tq, tq)
            k = k_ref[pl.ds(r0, tq), :].astype(BF16)
            v = v_ref[pl.ds(r0, tq), :].astype(BF16)
            s = lax.dot_general(q, k, NT, preferred_element_type=F32) * scale + cq - ck_ref[j]
            if masked:
                s = jnp.where(mask, s, NEG)
            m_new = jnp.maximum(m, jnp.max(s, axis=1, keepdims=True))
            alpha = jnp.exp(m - m_new)
            p = jnp.exp(s - m_new)
            l = alpha * l + jnp.sum(p, axis=1, keepdims=True)
            acc = alpha * acc + jnp.dot(p.astype(BF16), v, preferred_element_type=F32)
            return m_new, l, acc

        carry = (jnp.full((tq, 1), NEG, F32), jnp.zeros((tq, 1), F32), jnp.zeros((tq, hd), F32))
        carry = lax.fori_loop(0, i, lambda j, cr: step(j, cr, False), carry)
        m, l, acc = step(i, carry, True)
        o_ref[...] = (acc / l).astype(o_ref.dtype)
        lse_ref[...] = jnp.broadcast_to(m + jnp.log(l), (tq, LANES))

    return pl.pallas_call(
        body, name=name, grid=(nh, nq),
        in_specs=[pl.BlockSpec((tq, hd), lambda h, i: (i, qo + h)), pl.BlockSpec((T, hd), lambda h, i: (0, ko + h)),
                  pl.BlockSpec((T, hd), lambda h, i: (0, vo + h)), pl.BlockSpec((tq, LANES), lambda h, i: (i, h)),
                  pl.BlockSpec((None, nq, 1, tq), lambda h, i: (h, 0, 0, 0))],
        out_specs=[pl.BlockSpec((tq, hd), lambda h, i: (i, h)), pl.BlockSpec((tq, LANES), lambda h, i: (i, h))],
        out_shape=[jax.ShapeDtypeStruct((T, nh * hd), BF16), jax.ShapeDtypeStruct((T, nh * LANES), F32)],
        compiler_params=_params("parallel", "arbitrary"),
    )(proj, proj, proj, cum_rep, cum_row)


def _fox_bwd(proj, q0, k0, v0, nh, cum_rep, cum_row, o, lse, dcat, do0, name, tq=ATT_TILE):
    T = proj.shape[0]
    tq = _tile(T, tq)
    nq = T // tq
    hd = FOX_HEAD_DIM
    scale = hd ** -0.5
    qo, ko, vo, doo = q0 // hd, k0 // hd, v0 // hd, do0 // hd

    def body(q_ref, k_ref, v_ref, do_ref, o_ref, lse_ref, cq_ref, ck_ref, dq_ref, dk_ref, dv_ref, dck_ref, dcq_ref):
        i = pl.program_id(1)

        @pl.when(i == 0)
        def _():
            dk_ref[...] = jnp.zeros_like(dk_ref)
            dv_ref[...] = jnp.zeros_like(dv_ref)
            dck_ref[...] = jnp.zeros_like(dck_ref)

        q = q_ref[...].astype(BF16)
        do = do_ref[...].astype(F32)
        dob = do.astype(BF16)
        delta = jnp.sum(do * o_ref[...].astype(F32), axis=1, keepdims=True)
        bias = jnp.tile(cq_ref[...] - lse_ref[...], (1, tq // LANES))
        mask = _causal(tq)

        def step(j, carry, masked):
            dq, rows = carry
            r0 = pl.multiple_of(j * tq, tq)
            k = k_ref[pl.ds(r0, tq), :].astype(BF16)
            v = v_ref[pl.ds(r0, tq), :].astype(BF16)
            s = lax.dot_general(q, k, NT, preferred_element_type=F32) * scale + bias - ck_ref[j]
            if masked:
                s = jnp.where(mask, s, NEG)
            p = jnp.exp(s)
            dp = lax.dot_general(dob, v, NT, preferred_element_type=F32)
            ds = p * (dp - delta)
            dsb = ds.astype(BF16)
            dq = dq + jnp.dot(dsb, k, preferred_element_type=F32) * scale
            dk_ref[pl.ds(r0, tq), :] += lax.dot_general(dsb, q, TN, preferred_element_type=F32) * scale
            dv_ref[pl.ds(r0, tq), :] += lax.dot_general(p.astype(BF16), dob, TN, preferred_element_type=F32)
            dck_ref[j] -= jnp.sum(ds, axis=0, keepdims=True)
            return dq, rows + jnp.sum(ds, axis=1, keepdims=True)

        carry = (jnp.zeros((tq, hd), F32), jnp.zeros((tq, 1), F32))
        carry = lax.fori_loop(0, i, lambda j, cr: step(j, cr, False), carry)
        dq, rows = step(i, carry, True)
        dq_ref[...] = dq.astype(dq_ref.dtype)
        dcq_ref[...] = jnp.broadcast_to(rows, (tq, LANES))

    return pl.pallas_call(
        body, name=name, grid=(nh, nq),
        in_specs=[pl.BlockSpec((tq, hd), lambda h, i: (i, qo + h)), pl.BlockSpec((T, hd), lambda h, i: (0, ko + h)),
                  pl.BlockSpec((T, hd), lambda h, i: (0, vo + h)), pl.BlockSpec((tq, hd), lambda h, i: (i, doo + h)),
                  pl.BlockSpec((tq, hd), lambda h, i: (i, h)), pl.BlockSpec((tq, LANES), lambda h, i: (i, h)),
                  pl.BlockSpec((tq, LANES), lambda h, i: (i, h)),
                  pl.BlockSpec((None, nq, 1, tq), lambda h, i: (h, 0, 0, 0))],
        out_specs=[pl.BlockSpec((tq, hd), lambda h, i: (i, h)), pl.BlockSpec((T, hd), lambda h, i: (0, h)),
                   pl.BlockSpec((T, hd), lambda h, i: (0, h)), pl.BlockSpec((None, nq, 1, tq), lambda h, i: (h, 0, 0, 0)),
                   pl.BlockSpec((tq, LANES), lambda h, i: (i, h))],
        out_shape=[jax.ShapeDtypeStruct((T, nh * hd), BF16), jax.ShapeDtypeStruct((T, nh * hd), F32),
                   jax.ShapeDtypeStruct((T, nh * hd), F32), jax.ShapeDtypeStruct((nh, nq, 1, tq), F32),
                   jax.ShapeDtypeStruct((T, nh * LANES), F32)],
        compiler_params=_params("arbitrary", "arbitrary"),
    )(proj, proj, proj, dcat, o, lse, cum_rep, cum_row)


def _scan_fwd(a, u, name, ct=256):
    T, C = a.shape

    def body(a_ref, u_ref, h_ref):
        def blk(tb, h):
            r0 = pl.multiple_of(tb * SUBLANES, SUBLANES)
            ab, ub = a_ref[pl.ds(r0, SUBLANES), :], u_ref[pl.ds(r0, SUBLANES), :]
            rows = []
            for r in range(SUBLANES):
                h = ab[r:r + 1, :] * h + ub[r:r + 1, :]
                rows.append(h)
            h_ref[pl.ds(r0, SUBLANES), :] = jnp.concatenate(rows, axis=0)
            return h

        lax.fori_loop(0, T // SUBLANES, blk, jnp.zeros((1, ct), F32))

    spec = pl.BlockSpec((T, ct), lambda j: (0, j))
    return pl.pallas_call(body, name=name, grid=(C // ct,), in_specs=[spec, spec], out_specs=spec,
                          out_shape=jax.ShapeDtypeStruct((T, C), F32), compiler_params=_params("parallel"))(a, u)


def _scan_bwd(a, dh, h, name, ct=128):
    T, C = a.shape
    nb = T // SUBLANES

    def body(a_ref, dh_ref, h_ref, g_ref, da_ref):
        def blk(t, carry):
            r0 = pl.multiple_of((nb - 1 - t) * SUBLANES, SUBLANES)
            ab, db = a_ref[pl.ds(r0, SUBLANES), :], dh_ref[pl.ds(r0, SUBLANES), :]
            rows = [None] * SUBLANES
            for r in range(SUBLANES - 1, -1, -1):
                g = db[r:r + 1, :] + carry
                carry = ab[r:r + 1, :] * g
                rows[r] = g
            g_ref[pl.ds(r0, SUBLANES), :] = jnp.concatenate(rows, axis=0)
            return carry

        lax.fori_loop(0, nb, blk, jnp.zeros((1, ct), F32))
        da_ref[...] = g_ref[...] * _shift_down(h_ref[...], 1)

    spec = pl.BlockSpec((T, ct), lambda j: (0, j))
    return pl.pallas_call(body, name=name, grid=(C // ct,), in_specs=[spec] * 3, out_specs=[spec] * 2,
                          out_shape=[jax.ShapeDtypeStruct((T, C), F32)] * 2,
                          compiler_params=_params("parallel"))(a, dh, h)


def _lru_elem(xc, ra, ia, lam):
    r, i = _sigmoid(ra), _sigmoid(ia)
    log_a = RG_LRU_C * r * _log_sigmoid(lam)
    return jnp.exp(log_a), jnp.sqrt(_neg_expm1(2.0 * log_a)) * (i * xc)


def _lru_gates_fwd(xc, w_a, b_a, w_x, b_x, lam, name):
    def fn(xv, ba, bx, lm, wa, wx):
        xb = xv.astype(BF16)
        ra = jnp.dot(xb, wa.astype(BF16), preferred_element_type=F32) + ba
        ia = jnp.dot(xb, wx.astype(BF16), preferred_element_type=F32) + bx
        return _lru_elem(xv, ra, ia, lm)
    nb = xc.shape[1] // RNN_BLOCK
    return _ew(fn, [(xc, 0)], [(b_a, 0), (b_x, 0), (lam, 0), (w_a, 0), (w_x, 0)], [F32, F32], [],
               name=name, tm=512, cb=RNN_BLOCK, ncb=nb)


def _lru_gates_bwd(xc, w_a, b_a, w_x, b_x, lam, da, du, name):
    def fn(xv, dav, duv, ba, bx, lm, wa, wx):
        xb, wab, wxb = xv.astype(BF16), wa.astype(BF16), wx.astype(BF16)
        ra = jnp.dot(xb, wab, preferred_element_type=F32) + ba
        ia = jnp.dot(xb, wxb, preferred_element_type=F32) + bx
        _, vjp = jax.vjp(_lru_elem, xv, ra, ia, lm)
        dx, dra, dia, dlm = vjp((dav, duv))
        drb, dib = dra.astype(BF16), dia.astype(BF16)
        dx = dx + lax.dot_general(drb, wab, NT, preferred_element_type=F32)
        dx = dx + lax.dot_general(dib, wxb, NT, preferred_element_type=F32)
        dwa = lax.dot_general(xb, drb, TN, preferred_element_type=F32)
        dwx = lax.dot_general(xb, dib, TN, preferred_element_type=F32)
        return (dx, jnp.sum(dra, axis=0, keepdims=True), jnp.sum(dia, axis=0, keepdims=True), dlm, dwa, dwx)
    nb = xc.shape[1] // RNN_BLOCK
    return _ew(fn, [(xc, 0), (da, 0), (du, 0)], [(b_a, 0), (b_x, 0), (lam, 0), (w_a, 0), (w_x, 0)], [F32],
               [b_a.shape, b_x.shape, lam.shape, w_a.shape, w_x.shape], name=name, tm=512, cb=RNN_BLOCK, ncb=nb)


def _hyb_cols(D):
    conv = D + 2 * SSM_GROUPS * SSM_STATE
    z0, x0, q0 = 0, D, D + conv
    return dict(z=z0, xbc=x0, q=q0, k=q0 + D, v=q0 + 2 * D, small=q0 + 3 * D, total=q0 + 3 * D + LANES, conv=conv)


def _hyb_w_in_reorder(w, D):
    cols = _hyb_cols(D)
    nh_s, nh_f = D // SSM_HEAD_DIM, D // FOX_HEAD_DIM
    a = D + cols["conv"]
    pad = jnp.zeros((w.shape[0], LANES - nh_s - nh_f), w.dtype)
    return jnp.concatenate([w[:, :a], w[:, a + nh_s:a + nh_s + 3 * D], w[:, a:a + nh_s], w[:, a + nh_s + 3 * D:], pad], axis=1)


def _hyb_w_in_restore(dw, D):
    cols = _hyb_cols(D)
    nh_s, nh_f = D // SSM_HEAD_DIM, D // FOX_HEAD_DIM
    a = D + cols["conv"]
    s = cols["small"]
    return jnp.concatenate([dw[:, :a], dw[:, s:s + nh_s], dw[:, a:s], dw[:, s + nh_s:s + nh_s + nh_f]], axis=1)


def _ssm_out(Y, xs, z, dfull, ng):
    y = (Y + dfull * xs) * _silu(z)
    return y * lax.rsqrt(jnp.mean(y * y, axis=-1, keepdims=True) + NORM_EPS) * ng


def _hyb_fwd(x, g_pre, g_post, p, tag):
    T, D = x.shape
    cols = _hyb_cols(D)
    nh_s, nh_f = D // SSM_HEAD_DIM, D // FOX_HEAD_DIM
    tq = _tile(T, ATT_TILE)
    n = _rms_fwd(x, g_pre, f"{tag}_rms")
    proj = _mm(n, p["w_in"], "nn", F32, f"{tag}_mm_in", tn=1152, tk=1024)
    a_neg = -jnp.exp(p["a_log"])
    bias = jnp.concatenate([p["dt_bias"], p["b_f"], jnp.zeros((LANES - nh_s - nh_f,), F32)])[None]
    mult = jnp.concatenate([a_neg, jnp.ones((nh_f,), F32), jnp.zeros((LANES - nh_s - nh_f,), F32)])[None]
    act, cs = _gates_fwd(proj, cols["small"], bias, mult, nh_s, f"{tag}_gates")
    dt, cs_s, cum = act[:, :nh_s], cs[:, :nh_s], cs[:, nh_s:nh_s + nh_f]
    dtf = jnp.repeat(dt, SSM_HEAD_DIM, axis=1)
    cs_rep, cs_row = _rep_layout(cs_s), _row_layout(cs_s, tq)
    cum_rep, cum_row = _rep_layout(cum), _row_layout(cum, tq)
    xbc = _conv_fwd(proj, cols["xbc"], cols["conv"], p["conv_w"], p["conv_b"], True, f"{tag}_conv")
    X = _ew(lambda xv, dv: xv * dv, [(xbc, 0), (dtf, 0)], [], [BF16], [], name=f"{tag}_xdt", cb=512, ncb=D // 512)
    Y = _ssd_fwd(xbc, X, cs_rep, cs_row, f"{tag}_ssd")
    dfull = jnp.repeat(p["d"], SSM_HEAD_DIM)[None]
    gw = D // SSM_GROUPS
    y_ssm = _ew(_ssm_out, [(Y, 0), (xbc, 0), (proj, cols["z"] // gw)], [(dfull, 0), (p["norm_g"], 0)], [BF16], [],
                name=f"{tag}_ssm_out", cb=gw, ncb=SSM_GROUPS)
    o, lse = _fox_fwd(proj, cols["q"], cols["k"], cols["v"], nh_f, cum_rep, cum_row, f"{tag}_fox")
    cat = jnp.concatenate([y_ssm, o], axis=1)
    mix = _mm(cat, p["w_out"], "nn", F32, f"{tag}_mm_out", tn=1024, tk=1024)
    x2 = _post_fwd(x, mix, g_post, 1.0, f"{tag}_post")
    return x2, (x, n, proj, bias, mult, dtf, cs_rep, cs_row, cum_rep, cum_row, xbc, X, Y, dfull, o, lse, cat, mix)


def _hyb_bwd(dy, saved, g_pre, g_post, p, tag):
    x, n, proj, bias, mult, dtf, cs_rep, cs_row, cum_rep, cum_row, xbc, X, Y, dfull, o, lse, cat, mix = saved
    T, D = x.shape
    cols = _hyb_cols(D)
    nh_s, nh_f = D // SSM_HEAD_DIM, D // FOX_HEAD_DIM
    gw = D // SSM_GROUPS
    dmix, dg_post = _post_bwd(dy, mix, g_post, 1.0, f"{tag}_post_b")
    dcat = _mm(dmix, p["w_out"], "nt", BF16, f"{tag}_mm_dcat", tn=1024, tk=1024)
    dw_out = _mm(cat, dmix, "tn", BF16, f"{tag}_mm_dwout", tm=1024, tn=1024, tk=512)

    def ssm_out_b(Yv, xv, zv, dv, dfv, ngv):
        _, vjp = jax.vjp(_ssm_out, Yv, xv, zv, dfv, ngv)
        return vjp(dv.astype(F32))
    dY, dxs_skip, dz, ddfull, dng = _ew(
        ssm_out_b, [(Y, 0), (xbc, 0), (proj, cols["z"] // gw), (dcat, 0)], [(dfull, 0), (p["norm_g"], 0)],
        [BF16, F32, BF16], [dfull.shape, p["norm_g"].shape], name=f"{tag}_ssm_out_b", cb=gw, ncb=SSM_GROUPS)
    dC, dB, dX, dcs_q, dcs_k = _ssd_bwd(xbc, X, cs_rep, cs_row, dY, f"{tag}_ssd_b")
    dxs, ddtf = _ew(lambda dXv, skv, xv, dv: (dXv * dv + skv, dXv * xv), [(dX, 0), (dxs_skip, 0), (xbc, 0), (dtf, 0)],
                    [], [F32, F32], [], name=f"{tag}_xdt_b", cb=512, ncb=D // 512)
    ddt = ddtf.reshape(T, nh_s, SSM_HEAD_DIM).sum(-1)
    dcs_s = dcs_q[:, ::LANES] + dcs_k.reshape(nh_s, T).T
    dq, dk, dv, dcum_k, dcum_q = _fox_bwd(proj, cols["q"], cols["k"], cols["v"], nh_f, cum_rep, cum_row, o, lse, dcat, D,
                                  f"{tag}_fox_b")
    dcum = dcum_q[:, ::LANES] + dcum_k.reshape(nh_f, T).T
    zpad = jnp.zeros((T, LANES - nh_s - nh_f), F32)
    dact = jnp.concatenate([ddt, jnp.zeros((T, nh_f), F32), zpad], axis=1)
    dcs = jnp.concatenate([dcs_s, dcum, zpad], axis=1)
    dsmall, dmult, dbias = _gates_bwd(proj, cols["small"], bias, mult, nh_s, dact, dcs, f"{tag}_gates_b")
    dxbc_act = jnp.concatenate([dxs, dB, dC], axis=1)
    dxbc, dconv_w, dconv_b = _conv_bwd(proj, cols["xbc"], cols["conv"], p["conv_w"], p["conv_b"], dxbc_act, True,
                                       f"{tag}_conv_b")
    dproj = jnp.concatenate([dz, dxbc, dq, dk.astype(BF16), dv.astype(BF16), dsmall], axis=1)
    dn = _mm(dproj, p["w_in"], "nt", F32, f"{tag}_mm_dn", tn=1024, tk=1152)
    dw_in = _mm(n, dproj, "tn", BF16, f"{tag}_mm_dwin", tm=1024, tn=1152, tk=512)
    dy2, dg_pre = _rms_bwd_add(x, g_pre, dn, dy, f"{tag}_rms_b")
    grads = dict(w_in=dw_in, w_out=dw_out, conv_w=dconv_w, conv_b=dconv_b[0], dt_bias=dbias[0, :nh_s],
                 a_log=dmult[0, :nh_s] * mult[0, :nh_s], d=ddfull.reshape(nh_s, SSM_HEAD_DIM).sum(-1),
                 norm_g=dng[0], b_f=dbias[0, nh_s:nh_s + nh_f])
    return dy2, dg_pre, dg_post, grads


def _rec_fwd(x, g_pre, g_post, p, tag):
    T, D = x.shape
    n = _rms_fwd(x, g_pre, f"{tag}_rms")
    pr = _mm(n, p["w_in"], "nn", F32, f"{tag}_mm_in", tn=1024, tk=1024)
    xc = _conv_fwd(pr, D, D, p["conv_w"], p["conv_b"], False, f"{tag}_conv")
    a, u = _lru_gates_fwd(xc, p["w_a"], p["b_a"], p["w_x"], p["b_x"], p["lam"], f"{tag}_lru")
    hs = _scan_fwd(a, u, f"{tag}_scan")
    og = _ew(lambda hv, gv: hv * _gelu_tanh(gv), [(hs, 0), (pr, 0)], [], [BF16], [], name=f"{tag}_gate", cb=D)
    mix = _mm(og, p["w_out"], "nn", F32, f"{tag}_mm_out", tn=1024, tk=1024)
    x2 = _post_fwd(x, mix, g_post, 1.0, f"{tag}_post")
    return x2, (x, n, pr, xc, a, hs, og, mix)


def _rec_bwd(dy, saved, g_pre, g_post, p, tag):
    x, n, pr, xc, a, hs, og, mix = saved
    T, D = x.shape
    dmix, dg_post = _post_bwd(dy, mix, g_post, 1.0, f"{tag}_post_b")
    dog = _mm(dmix, p["w_out"], "nt", F32, f"{tag}_mm_dog", tn=1024, tk=1024)
    dw_out = _mm(og, dmix, "tn", BF16, f"{tag}_mm_dwout", tm=1024, tn=1024, tk=512)

    def gate_b(hv, gv, dv):
        _, vjp = jax.vjp(lambda h_, g_: h_ * _gelu_tanh(g_), hv, gv)
        return vjp(dv)
    dhs, dgate = _ew(gate_b, [(hs, 0), (pr, 0), (dog, 0)], [], [F32, BF16], [], name=f"{tag}_gate_b", cb=D)
    du, da = _scan_bwd(a, dhs, hs, f"{tag}_scan_b")
    dxc, db_a, db_x, dlam, dw_a, dw_x = _lru_gates_bwd(xc, p["w_a"], p["b_a"], p["w_x"], p["b_x"], p["lam"], da, du,
                                                       f"{tag}_lru_b")
    dxr, dconv_w, dconv_b = _conv_bwd(pr, D, D, p["conv_w"], p["conv_b"], dxc, False, f"{tag}_conv_b")
    dpr = jnp.concatenate([dgate, dxr], axis=1)
    dn = _mm(dpr, p["w_in"], "nt", F32, f"{tag}_mm_dn", tn=1024, tk=1024)
    dw_in = _mm(n, dpr, "tn", BF16, f"{tag}_mm_dwin", tm=1024, tn=1024, tk=512)
    dy2, dg_pre = _rms_bwd_add(x, g_pre, dn, dy, f"{tag}_rms_b")
    grads = dict(w_in=dw_in, w_out=dw_out, conv_w=dconv_w, conv_b=dconv_b[0], w_a=dw_a, b_a=db_a[0], w_x=dw_x,
                 b_x=db_x[0], lam=dlam[0])
    return dy2, dg_pre, dg_post, grads


def _hyb_params(W, i, D):
    return dict(w_in=_hyb_w_in_reorder(W["hyb_w_in"][i], D), w_out=W["hyb_w_out"][i], conv_w=W["ssm_conv_w"][i],
                conv_b=W["ssm_conv_b"][i][None], dt_bias=W["ssm_dt_bias"][i], a_log=W["ssm_a_log"][i],
                d=W["ssm_d"][i], norm_g=W["ssm_norm_g"][i][None], b_f=W["fox_b_f"][i])


def _rec_params(W, j):
    return dict(w_in=W["rec_w_in"][j], w_out=W["rec_w_out"][j], conv_w=W["rec_conv_w"][j],
                conv_b=W["rec_conv_b"][j][None], w_a=W["rec_w_a"][j], b_a=W["rec_b_a"][j][None],
                w_x=W["rec_w_x"][j], b_x=W["rec_b_x"][j][None], lam=W["rec_lambda"][j][None])


def _local_step(x, target, W):
    T, D = x.shape
    depth = W["norm_g"].shape[0]
    g = lambda l, k: W["norm_g"][l, k][None]
    saved = []
    for l in range(depth):
        x, s0 = _ffn_fwd(x, g(l, 0), g(l, 1), W["ffn_w_in"][l, 0], W["ffn_w_out"][l, 0], f"l{l}_ffn0")
        if l % 2 == 0:
            pm = _hyb_params(W, l // 2, D)
            x, s1 = _hyb_fwd(x, g(l, 2), g(l, 3), pm, f"l{l}_hyb")
        else:
            pm = _rec_params(W, l // 2)
            x, s1 = _rec_fwd(x, g(l, 2), g(l, 3), pm, f"l{l}_rec")
        x, s2 = _ffn_fwd(x, g(l, 4), g(l, 5), W["ffn_w_in"][l, 1], W["ffn_w_out"][l, 1], f"l{l}_ffn1")
        saved.append((s0, s1, s2, pm))

    def loss_fn(yv, tv):
        err = yv - tv
        part = 0.5 * jnp.sum(jnp.sum(err * err, axis=1, keepdims=True), axis=0, keepdims=True) / D
        return err * (1.0 / D), jnp.broadcast_to(part, (1, D))
    dy, loss_row = _ew(loss_fn, [(x, 0), (target, 0)], [], [F32], [(1, D)], name="loss")
    loss = loss_row[0, 0]

    gn = [[None] * 6 for _ in range(depth)]
    g_ffn_in = [[None, None] for _ in range(depth)]
    g_ffn_out = [[None, None] for _ in range(depth)]
    g_hyb, g_rec = [], []
    for l in reversed(range(depth)):
        s0, s1, s2, pm = saved[l]
        dy, gn[l][4], gn[l][5], g_ffn_in[l][1], g_ffn_out[l][1] = _ffn_bwd(
            dy, s2, g(l, 4), g(l, 5), W["ffn_w_in"][l, 1], W["ffn_w_out"][l, 1], f"l{l}_ffn1")
        if l % 2 == 0:
            dy, gn[l][2], gn[l][3], gm = _hyb_bwd(dy, s1, g(l, 2), g(l, 3), pm, f"l{l}_hyb")
            g_hyb.insert(0, gm)
        else:
            dy, gn[l][2], gn[l][3], gm = _rec_bwd(dy, s1, g(l, 2), g(l, 3), pm, f"l{l}_rec")
            g_rec.insert(0, gm)
        dy, gn[l][0], gn[l][1], g_ffn_in[l][0], g_ffn_out[l][0] = _ffn_bwd(
            dy, s0, g(l, 0), g(l, 1), W["ffn_w_in"][l, 0], W["ffn_w_out"][l, 0], f"l{l}_ffn0")

    st = lambda items: jnp.stack(items)
    grads = {
        "norm_g": st([st([r[0] for r in row]) for row in gn]),
        "ffn_w_in": st([st(row) for row in g_ffn_in]),
        "ffn_w_out": st([st(row) for row in g_ffn_out]),
        "hyb_w_in": st([_hyb_w_in_restore(m["w_in"], D) for m in g_hyb]),
        "ssm_conv_w": st([m["conv_w"] for m in g_hyb]), "ssm_conv_b": st([m["conv_b"] for m in g_hyb]),
        "ssm_dt_bias": st([m["dt_bias"] for m in g_hyb]), "ssm_a_log": st([m["a_log"] for m in g_hyb]),
        "ssm_d": st([m["d"] for m in g_hyb]), "ssm_norm_g": st([m["norm_g"] for m in g_hyb]),
        "fox_b_f": st([m["b_f"] for m in g_hyb]), "hyb_w_out": st([m["w_out"] for m in g_hyb]),
        "rec_w_in": st([m["w_in"] for m in g_rec]), "rec_conv_w": st([m["conv_w"] for m in g_rec]),
        "rec_conv_b": st([m["conv_b"] for m in g_rec]), "rec_w_a": st([m["w_a"] for m in g_rec]),
        "rec_b_a": st([m["b_a"] for m in g_rec]), "rec_w_x": st([m["w_x"] for m in g_rec]),
        "rec_b_x": st([m["b_x"] for m in g_rec]), "rec_lambda": st([m["lam"] for m in g_rec]),
        "rec_w_out": st([m["w_out"] for m in g_rec]),
    }
    return loss, dy, grads


MESH_AXES = ("x", "y", "c")
N_CHIPS = 4
N_DEV = 8
HBM = pl.BlockSpec(memory_space=pltpu.HBM)


def _mesh_pos():
    return tuple(lax.axis_index(n) for n in MESH_AXES)


def _other_chips(x, y):
    chips = [(1 - x, y), (x, 1 - y), (1 - x, 1 - y)]
    return chips, [2 * cx + cy for cx, cy in chips]


def _rcopy(src, dst, send_sem, recv_sem, dev):
    return pltpu.make_async_remote_copy(src_ref=src, dst_ref=dst, send_sem=send_sem, recv_sem=recv_sem,
                                        device_id=dev, device_id_type=pl.DeviceIdType.MESH)


def _comm_params():
    return pltpu.CompilerParams()


def _all_gather(arrs, name):
    n = len(arrs)

    def body(*refs):
        ins, outs = refs[:n], refs[n:2 * n]
        send, recv, fsend, frecv, lsem = refs[2 * n:]
        x, y, c = _mesh_pos()
        k = 2 * x + y
        sibling = (x, y, 1 - c)
        chips, chip_k = _other_chips(x, y)
        halves = [r.shape[0] // 2 for r in ins]
        local, sent = [], []
        for a in range(n):
            h = halves[a]
            lc = pltpu.make_async_copy(ins[a], outs[a].at[k], lsem.at[a])
            lc.start()
            local.append(lc)
            for j, chip in enumerate(chips):
                cp = _rcopy(ins[a].at[pl.ds(c * h, h)], outs[a].at[k, pl.ds(c * h, h)], send.at[a, j], recv.at[a, j],
                            (*chip, c))
                cp.start()
                sent.append(cp)
        for a in range(n):
            h = halves[a]
            for j, chip in enumerate(chips):
                blk = outs[a].at[chip_k[j], pl.ds(c * h, h)]
                _rcopy(blk, blk, send.at[a, j], recv.at[a, j], (*chip, c)).wait_recv()
                fw = _rcopy(blk, blk, fsend.at[a, j], frecv.at[a, j], sibling)
                fw.start()
                sent.append(fw)
        for a in range(n):
            h = halves[a]
            for j in range(3):
                blk = outs[a].at[chip_k[j], pl.ds((1 - c) * h, h)]
                _rcopy(blk, blk, fsend.at[a, j], frecv.at[a, j], sibling).wait_recv()
        for cp in sent:
            cp.wait_send()
        for lc in local:
            lc.wait()

    return pl.pallas_call(
        body, name=name, in_specs=[HBM] * n, out_specs=[HBM] * n,
        out_shape=[jax.ShapeDtypeStruct((N_CHIPS,) + a.shape, a.dtype) for a in arrs],
        scratch_shapes=[pltpu.SemaphoreType.DMA((n, 3))] * 4 + [pltpu.SemaphoreType.DMA((n,))],
        compiler_params=_comm_params(),
    )(*arrs)


def _pair_exchange(gs, name):
    n = len(gs)

    def body(*refs):
        ins, mine, theirs = refs[:n], refs[n:2 * n], refs[2 * n:3 * n]
        send, recv, lsem = refs[3 * n:]
        x, y, c = _mesh_pos()
        sibling = (x, y, 1 - c)
        copies = []
        for a in range(n):
            h = ins[a].shape[1] // 2
            lc = pltpu.make_async_copy(ins[a].at[:, pl.ds(c * h, h)], mine[a], lsem.at[a])
            lc.start()
            cp = _rcopy(ins[a].at[:, pl.ds((1 - c) * h, h)], theirs[a], send.at[a], recv.at[a], sibling)
            cp.start()
            copies.append((lc, cp))
        for lc, cp in copies:
            cp.wait()
            lc.wait()

    half = [jax.ShapeDtypeStruct((a.shape[0], a.shape[1] // 2) + a.shape[2:], a.dtype) for a in gs]
    res = pl.pallas_call(
        body, name=name, in_specs=[HBM] * n, out_specs=[HBM] * (2 * n), out_shape=half + half,
        scratch_shapes=[pltpu.SemaphoreType.DMA((n,))] * 3, compiler_params=_comm_params(),
    )(*gs)
    return res[:n], res[n:]


def _chip_exchange(ss, name):
    n = len(ss)

    def body(*refs):
        ins, own = refs[:n], refs[n:2 * n]
        got = [refs[2 * n + 3 * a:2 * n + 3 * a + 3] for a in range(n)]
        send, recv, lsem = refs[5 * n:]
        x, y, c = _mesh_pos()
        k = 2 * x + y
        chips, chip_k = _other_chips(x, y)
        copies = []
        for a in range(n):
            lc = pltpu.make_async_copy(ins[a].at[k], own[a], lsem.at[a])
            lc.start()
            copies.append(lc)
            for j, chip in enumerate(chips):
                cp = _rcopy(ins[a].at[chip_k[j]], got[a][j], send.at[a, j], recv.at[a, j], (*chip, c))
                cp.start()
                copies.append(cp)
        for cp in copies:
            cp.wait()

    own = [jax.ShapeDtypeStruct(a.shape[1:], a.dtype) for a in ss]
    got = [jax.ShapeDtypeStruct(a.shape[1:], a.dtype) for a in ss for _ in range(3)]
    res = pl.pallas_call(
        body, name=name, in_specs=[HBM] * n, out_specs=[HBM] * (4 * n), out_shape=own + got,
        scratch_shapes=[pltpu.SemaphoreType.DMA((n, 3))] * 2 + [pltpu.SemaphoreType.DMA((n,))],
        compiler_params=_comm_params(),
    )(*ss)
    return res[:n], [res[n + 3 * a:n + 3 * a + 3] for a in range(n)]


def _pair_share(rs, name):
    n = len(rs)

    def body(*refs):
        ins, outs = refs[:n], refs[n:2 * n]
        send, recv, lsem = refs[2 * n:]
        x, y, c = _mesh_pos()
        sibling = (x, y, 1 - c)
        copies = []
        for a in range(n):
            h = ins[a].shape[0]
            lc = pltpu.make_async_copy(ins[a], outs[a].at[pl.ds(c * h, h)], lsem.at[a])
            lc.start()
            cp = _rcopy(ins[a], outs[a].at[pl.ds(c * h, h)], send.at[a], recv.at[a], sibling)
            cp.start()
            copies.append((lc, cp, h))
        for a, (lc, cp, h) in enumerate(copies):
            cp.wait_send()
            blk = outs[a].at[pl.ds((1 - c) * h, h)]
            _rcopy(blk, blk, send.at[a], recv.at[a], sibling).wait_recv()
            lc.wait()

    return pl.pallas_call(
        body, name=name, in_specs=[HBM] * n, out_specs=[HBM] * n,
        out_shape=[jax.ShapeDtypeStruct((2 * a.shape[0],) + a.shape[1:], a.dtype) for a in rs],
        scratch_shapes=[pltpu.SemaphoreType.DMA((n,))] * 3, compiler_params=_comm_params(),
    )(*rs)


def _exchange_all(vec, name):
    def body(v_ref, out_ref, send, recv, lsem):
        x, y, c = _mesh_pos()
        flip = lambda p, f: 1 - p if f else p
        lc = pltpu.make_async_copy(v_ref, out_ref.at[4 * x + 2 * y + c], lsem)
        lc.start()
        copies = []
        for j in range(1, N_DEV):
            fx, fy, fc = (j >> 2) & 1, (j >> 1) & 1, j & 1
            cp = _rcopy(v_ref, out_ref.at[4 * x + 2 * y + c], send.at[j - 1], recv.at[j - 1],
                        (flip(x, fx), flip(y, fy), flip(c, fc)))
            cp.start()
            copies.append(cp)
        for j in range(1, N_DEV):
            fx, fy, fc = (j >> 2) & 1, (j >> 1) & 1, j & 1
            slot = out_ref.at[4 * flip(x, fx) + 2 * flip(y, fy) + flip(c, fc)]
            _rcopy(slot, slot, send.at[j - 1], recv.at[j - 1], (x, y, c)).wait_recv()
        for cp in copies:
            cp.wait_send()
        lc.wait()

    return pl.pallas_call(
        body, name=name, in_specs=[HBM], out_specs=HBM,
        out_shape=jax.ShapeDtypeStruct((N_DEV,) + vec.shape, vec.dtype),
        scratch_shapes=[pltpu.SemaphoreType.DMA((N_DEV - 1,))] * 2 + [pltpu.SemaphoreType.DMA(())],
        compiler_params=_comm_params(),
    )(vec)


def _rows(a):
    return a.reshape(-1, a.shape[-1])


def _sum_kernel(parts, out_dtype, name):
    def fn(*vals):
        acc = vals[0].astype(F32)
        for v in vals[1:]:
            acc = acc + v.astype(F32)
        return acc
    out = _ew(fn, [(_rows(p), 0) for p in parts], [], [out_dtype], [], name=name)
    return out.reshape(parts[0].shape)


def _reduce_scatter(gs, tag):
    mine, theirs = _pair_exchange(gs, f"{tag}_pair")
    pair = [_sum_kernel([m, t], BF16, f"{tag}_add_pair{a}") for a, (m, t) in enumerate(zip(mine, theirs))]
    own, got = _chip_exchange(pair, f"{tag}_chips")
    red = [_sum_kernel([o, g[0], g[1], g[2]], F32, f"{tag}_add_chips{a}") for a, (o, g) in enumerate(zip(own, got))]
    return _pair_share(red, f"{tag}_share")


def _pack(arrs, row_mult):
    flat = jnp.concatenate([a.reshape(-1).astype(F32) for a in arrs])
    unit = row_mult * LANES
    pad = (-flat.size) % unit
    return jnp.pad(flat, (0, pad)).reshape(-1, LANES)


def _unpack(mat, shapes):
    flat, out, pos = mat.reshape(-1), [], 0
    for s in shapes:
        size = 1
        for d in s:
            size *= d
        out.append(flat[pos:pos + size].reshape(s))
        pos += size
    return out


def _to_shards(a, axis):
    sh = a.shape
    a = a.reshape(sh[:axis] + (N_CHIPS, sh[axis] // N_CHIPS) + sh[axis + 1:])
    return jnp.moveaxis(a, axis, 0)


def _from_shards(g, axis):
    g = jnp.moveaxis(g, 0, axis)
    sh = g.shape
    return g.reshape(sh[:axis] + (sh[axis] * sh[axis + 1],) + sh[axis + 2:])


def _adamw_fn(w, g, m, v):
    m2 = ADAM_B1 * m + (1.0 - ADAM_B1) * g
    v2 = ADAM_B2 * v + (1.0 - ADAM_B2) * (g * g)
    m_hat = m2 / (1.0 - ADAM_B1 ** ADAM_STEP)
    v_hat = v2 / (1.0 - ADAM_B2 ** ADAM_STEP)
    return -ADAM_LR * (m_hat / (jnp.sqrt(v_hat) + ADAM_EPS) + ADAM_WD * w), m2, v2


def _adamw(w, g, m, v, name):
    res = _ew(_adamw_fn, [(_rows(a), 0) for a in (w, g, m, v)], [], [F32, F32, F32], [], name=name)
    return tuple(r.reshape(w.shape) for r in res)


def kernel(x, norm_g, ffn_w_in, ffn_w_out, hyb_w_in, ssm_conv_w, ssm_conv_b, ssm_dt_bias, ssm_a_log, ssm_d, ssm_norm_g, fox_b_f, hyb_w_out, rec_w_in, rec_conv_w, rec_conv_b, rec_w_a, rec_b_a, rec_w_x, rec_b_x, rec_lambda, rec_w_out, loss_target, m_norm_g, m_ffn_w_in, m_ffn_w_out, m_hyb_w_in, m_ssm_conv_w, m_ssm_conv_b, m_ssm_dt_bias, m_ssm_a_log, m_ssm_d, m_ssm_norm_g, m_fox_b_f, m_hyb_w_out, m_rec_w_in, m_rec_conv_w, m_rec_conv_b, m_rec_w_a, m_rec_b_a, m_rec_w_x, m_rec_b_x, m_rec_lambda, m_rec_w_out, v_norm_g, v_ffn_w_in, v_ffn_w_out, v_hyb_w_in, v_ssm_conv_w, v_ssm_conv_b, v_ssm_dt_bias, v_ssm_a_log, v_ssm_d, v_ssm_norm_g, v_fox_b_f, v_hyb_w_out, v_rec_w_in, v_rec_conv_w, v_rec_conv_b, v_rec_w_a, v_rec_b_a, v_rec_w_x, v_rec_b_x, v_rec_lambda, v_rec_w_out):
    given = dict(locals())
    w = {n: given[n] for n in WEIGHTS}
    m = {n: given["m_" + n] for n in WEIGHTS}
    v = {n: given["v_" + n] for n in WEIGHTS}
    k = 2 * lax.axis_index("x") + lax.axis_index("y")

    big_bf16 = [_ew(lambda t: t, [(_rows(w[n]), 0)], [], [BF16], [], name=f"cast_{n}").reshape(w[n].shape) for n in BIG]
    small_shapes = [w[n].shape for n in SMALL_SHARDED]
    small_pack = _pack([w[n] for n in SMALL_SHARDED], 2 * SUBLANES)
    gathered = _all_gather(big_bf16 + [small_pack], "gather_weights")
    W = {n: w[n] for n in SMALL_REPL}
    for n, g in zip(BIG, gathered[:-1]):
        W[n] = _from_shards(g, SHARD_AXIS[n])
    per_chip = [_unpack(gathered[-1][kk], small_shapes) for kk in range(N_CHIPS)]
    for idx, n in enumerate(SMALL_SHARDED):
        W[n] = jnp.concatenate([per_chip[kk][idx] for kk in range(N_CHIPS)], axis=SHARD_AXIS[n])

    loss_part, dy, grads = _local_step(x[0], loss_target[0], W)
    loss = lax.psum(loss_part, MESH_AXES)

    red_big = _reduce_scatter([_to_shards(grads[n], SHARD_AXIS[n]) for n in BIG], "rs")
    g_out = dict(zip(BIG, red_big))
    small_names = SMALL_SHARDED + SMALL_REPL
    slots = _exchange_all(_pack([grads[n] for n in small_names], SUBLANES), "gather_small_grads")
    small_sum = _sum_kernel([slots[d] for d in range(N_DEV)], F32, "add_small_grads")
    for n, g in zip(small_names, _unpack(small_sum, [grads[n].shape for n in small_names])):
        if n in SHARD_AXIS:
            loc = g.shape[SHARD_AXIS[n]] // N_CHIPS
            g = lax.dynamic_slice_in_dim(g, k * loc, loc, axis=SHARD_AXIS[n])
        g_out[n] = g

    delta, new_m, new_v = {}, {}, {}
    for n in BIG:
        delta[n], new_m[n], new_v[n] = _adamw(w[n], g_out[n], m[n], v[n], f"adamw_{n}")
    shapes = [w[n].shape for n in small_names]
    packed = [_pack([d[n] for n in small_names], SUBLANES) for d in (w, g_out, m, v)]
    for d, mat in zip((delta, new_m, new_v), _adamw(*packed, "adamw_small")):
        d.update(zip(small_names, _unpack(mat, shapes)))

    return (loss, dy[None], *[g_out[n] for n in WEIGHTS], *[delta[n] for n in WEIGHTS],
            *[new_m[n] for n in WEIGHTS], *[new_v[n] for n in WEIGHTS])
```

```python
import functools

import jax
import jax.numpy as jnp
from jax import lax
from jax.experimental import pallas as pl
from jax.experimental.pallas import tpu as pltpu

F32, BF16 = jnp.float32, jnp.bfloat16

NORM_EPS = 1e-6
CONV_K = 4
SSM_HEAD_DIM = 64
SSM_STATE = 128
SSM_GROUPS = 2
FOX_HEAD_DIM = 128
RNN_BLOCK = 128
RG_LRU_C = 8.0
ADAM_LR, ADAM_B1, ADAM_B2, ADAM_EPS, ADAM_WD, ADAM_STEP = 0.001, 0.9, 0.999, 1e-08, 0.01, 10

LANES = 128
SUBLANES = 8
VMEM_LIMIT = 48 * 1024 * 1024
SSD_TILE = 256
FOX_TILE = 512
NEG = -1e30

NT = (((1,), (1,)), ((), ()))
NN = (((1,), (0,)), ((), ()))
TN = (((0,), (0,)), ((), ()))

BIG = ("ffn_w_in", "ffn_w_out", "hyb_w_in", "hyb_w_out", "rec_w_in", "rec_w_out")
SMALL_SHARDED = ("norm_g", "ssm_conv_w", "rec_conv_w", "rec_conv_b", "rec_b_a", "rec_b_x", "rec_lambda")
SMALL_REPL = ("ssm_conv_b", "ssm_dt_bias", "ssm_a_log", "ssm_d", "ssm_norm_g", "fox_b_f", "rec_w_a", "rec_w_x")
WEIGHTS = ("norm_g", "ffn_w_in", "ffn_w_out", "hyb_w_in", "ssm_conv_w", "ssm_conv_b", "ssm_dt_bias", "ssm_a_log",
           "ssm_d", "ssm_norm_g", "fox_b_f", "hyb_w_out", "rec_w_in", "rec_conv_w", "rec_conv_b", "rec_w_a",
           "rec_b_a", "rec_w_x", "rec_b_x", "rec_lambda", "rec_w_out")
SHARD_AXIS = {"norm_g": 2, "ffn_w_in": 3, "ffn_w_out": 2, "hyb_w_in": 2, "ssm_conv_w": 2, "hyb_w_out": 1,
              "rec_w_in": 2, "rec_conv_w": 2, "rec_conv_b": 1, "rec_b_a": 1, "rec_b_x": 1, "rec_lambda": 1,
              "rec_w_out": 1}


def _params(*sem):
    return pltpu.CompilerParams(dimension_semantics=sem if sem else None, vmem_limit_bytes=VMEM_LIMIT)


def _tile(dim, pref):
    if dim <= pref:
        return dim
    for align in (LANES, SUBLANES):
        t = (pref // align) * align
        while t >= align:
            if dim % t == 0:
                return t
            t -= align
    return dim


def _sigmoid(x):
    return 1.0 / (1.0 + jnp.exp(-x))


def _softplus(x):
    return jnp.maximum(x, 0.0) + jnp.log(1.0 + jnp.exp(-jnp.abs(x)))


def _log_sigmoid(x):
    return -_softplus(-x)


def _silu(x):
    return x * _sigmoid(x)


def _gelu_tanh(x):
    return 0.5 * x * (1.0 + jnp.tanh(0.7978845608028654 * (x + 0.044715 * x * x * x)))


def _neg_expm1(x):
    series = -x * (1.0 + x * (0.5 + x * (1.0 / 6.0)))
    return jnp.where(x > -1e-2, series, 1.0 - jnp.exp(x))


def _rms(x, g):
    xf = x.astype(F32)
    return xf * lax.rsqrt(jnp.mean(xf * xf, axis=-1, keepdims=True) + NORM_EPS) * g


def _mm(a, b, mode, out_dtype, name, tm=512, tn=512, tk=512, b_sel=()):
    bshape = b.shape[len(b_sel):]
    if mode == "nn":
        (M, K), (K2, N) = a.shape, bshape
    elif mode == "nt":
        (M, K), (N, K2) = a.shape, bshape
    else:
        (K, M), (K2, N) = a.shape, bshape
    assert K == K2, (a.shape, b.shape, mode)
    tm, tn, tk = _tile(M, tm), _tile(N, tn), _tile(K, tk)
    nk = K // tk
    dims = {"nn": NN, "nt": NT, "tn": TN}[mode]
    lead = (None,) * len(b_sel)
    if mode == "tn":
        a_spec = pl.BlockSpec((tk, tm), lambda n, m, k: (k, m))
    else:
        a_spec = pl.BlockSpec((tm, tk), lambda n, m, k: (m, k))
    if mode == "nt":
        b_spec = pl.BlockSpec(lead + (tn, tk), lambda n, m, k: (*b_sel, n, k))
    else:
        b_spec = pl.BlockSpec(lead + (tk, tn), lambda n, m, k: (*b_sel, k, n))

    def body(a_ref, b_ref, o_ref, *acc):
        p = lax.dot_general(a_ref[...].astype(BF16), b_ref[...].astype(BF16), dims, preferred_element_type=F32)
        if nk == 1:
            o_ref[...] = p.astype(o_ref.dtype)
        else:
            acc_ref, = acc
            k = pl.program_id(2)

            @pl.when(k == 0)
            def _():
                acc_ref[...] = p

            @pl.when(k > 0)
            def _():
                acc_ref[...] += p

            @pl.when(k == nk - 1)
            def _():
                o_ref[...] = acc_ref[...].astype(o_ref.dtype)

    return pl.pallas_call(
        body, name=name, grid=(N // tn, M // tm, nk),
        in_specs=[a_spec, b_spec], out_specs=pl.BlockSpec((tm, tn), lambda n, m, k: (m, n)),
        out_shape=jax.ShapeDtypeStruct((M, N), out_dtype),
        scratch_shapes=[pltpu.VMEM((tm, tn), F32)] if nk > 1 else [],
        compiler_params=_params("parallel", "parallel", "arbitrary"),
    )(a, b)


def _mm_parts(parts, mode, out_dtype, name, tm=512, tn=512):
    M = parts[0][0].shape[0]
    N = parts[0][1].shape[1] if mode == "nn" else parts[0][1].shape[0]
    tm, tn = _tile(M, tm), _tile(N, tn)
    dims = NN if mode == "nn" else NT
    in_specs, args = [], []
    for a, b in parts:
        K = a.shape[1]
        in_specs.append(pl.BlockSpec((tm, K), lambda n, m: (m, 0)))
        if mode == "nn":
            in_specs.append(pl.BlockSpec((K, tn), lambda n, m: (0, n)))
        else:
            in_specs.append(pl.BlockSpec((tn, K), lambda n, m: (n, 0)))
        args += [a, b]

    def body(*refs):
        o_ref = refs[-1]
        acc = None
        for p in range(len(parts)):
            d = lax.dot_general(refs[2 * p][...].astype(BF16), refs[2 * p + 1][...].astype(BF16), dims,
                                preferred_element_type=F32)
            acc = d if acc is None else acc + d
        o_ref[...] = acc.astype(o_ref.dtype)

    return pl.pallas_call(
        body, name=name, grid=(N // tn, M // tm), in_specs=in_specs,
        out_specs=pl.BlockSpec((tm, tn), lambda n, m: (m, n)), out_shape=jax.ShapeDtypeStruct((M, N), out_dtype),
        compiler_params=_params("parallel", "parallel"),
    )(*args)


def _ew(fn, tiled, params, outs, reds, *, name, tm=256, cb=None, ncb=1):
    T = tiled[0][0].shape[0]
    cb = tiled[0][0].shape[1] if cb is None else cb
    tm = _tile(T, tm)
    in_specs, args = [], []
    for arr, off in tiled:
        in_specs.append(pl.BlockSpec((tm, cb), functools.partial(lambda n, i, o: (i, n + o), o=off)))
        args.append(arr)
    for arr, off in params:
        if arr.ndim == 2:
            in_specs.append(pl.BlockSpec((arr.shape[0], cb), functools.partial(lambda n, i, o: (0, n + o), o=off)))
        else:
            in_specs.append(pl.BlockSpec((None,) + arr.shape[1:], lambda n, i: (n, 0, 0)))
        args.append(arr)
    out_shape = [jax.ShapeDtypeStruct((T, cb * ncb), dt) for dt in outs]
    out_specs = [pl.BlockSpec((tm, cb), lambda n, i: (i, n)) for _ in outs]
    for shape in reds:
        out_shape.append(jax.ShapeDtypeStruct(shape, F32))
        if len(shape) == 2:
            out_specs.append(pl.BlockSpec((shape[0], cb), lambda n, i: (0, n)))
        else:
            out_specs.append(pl.BlockSpec((None,) + tuple(shape[1:]), lambda n, i: (n, 0, 0)))
    n_in, n_out = len(args), len(outs)

    def body(*refs):
        i = pl.program_id(1)
        res = fn(*[r[...] for r in refs[:n_in]])
        res = res if isinstance(res, (tuple, list)) else (res,)
        for r, v in zip(refs[n_in:n_in + n_out], res[:n_out]):
            r[...] = v.astype(r.dtype)
        for r, v in zip(refs[n_in + n_out:], res[n_out:]):
            @pl.when(i == 0)
            def _():
                r[...] = jnp.zeros(r.shape, r.dtype)

            r[...] += v.astype(r.dtype).reshape(r.shape)

    res = pl.pallas_call(
        body, name=name, grid=(ncb, T // tm), in_specs=in_specs, out_specs=out_specs, out_shape=out_shape,
        compiler_params=_params("arbitrary", "arbitrary"),
    )(*args)
    return res[0] if len(res) == 1 else tuple(res)


def _rms_fwd(x, g, name):
    return _ew(lambda xv, gv: _rms(xv, gv), [(x, 0)], [(g, 0)], [BF16], [], name=name)


def _rms_bwd_add(x, g, dn, dres, name):
    def fn(xv, dnv, drv, gv):
        _, vjp = jax.vjp(_rms, xv, gv)
        dx, dg = vjp(dnv.astype(F32))
        return drv + dx, dg
    return _ew(fn, [(x, 0), (dn, 0), (dres, 0)], [(g, 0)], [F32], [g.shape], name=name)


def _post_fwd(x, h, g, w, name):
    return _ew(lambda xv, hv, gv: xv + w * _rms(hv, gv), [(x, 0), (h, 0)], [(g, 0)], [F32], [], name=name)


def _post_bwd(dy, h, g, w, name):
    def fn(dyv, hv, gv):
        _, vjp = jax.vjp(lambda a, b: w * _rms(a, b), hv, gv)
        return vjp(dyv)
    return _ew(fn, [(dy, 0), (h, 0)], [(g, 0)], [BF16], [g.shape], name=name)


def _swiglu_fwd(gu, name):
    F = gu.shape[1] // 2
    cb = F // 2
    return _ew(lambda gv, uv: _silu(gv.astype(F32)) * uv.astype(F32), [(gu, 0), (gu, 2)], [], [BF16], [],
               name=name, cb=cb, ncb=2)


def _swiglu_bwd(gu, da, name):
    T, F2 = gu.shape
    F = F2 // 2
    tm = _tile(T, 256)

    def body(gu_ref, da_ref, o_ref):
        g, u, d = gu_ref[:, :F].astype(F32), gu_ref[:, F:].astype(F32), da_ref[...].astype(F32)
        s = _sigmoid(g)
        o_ref[:, :F] = (d * u * (s * (1.0 + g * (1.0 - s)))).astype(o_ref.dtype)
        o_ref[:, F:] = (d * g * s).astype(o_ref.dtype)

    return pl.pallas_call(
        body, name=name, grid=(T // tm,),
        in_specs=[pl.BlockSpec((tm, F2), lambda i: (i, 0)), pl.BlockSpec((tm, F), lambda i: (i, 0))],
        out_specs=pl.BlockSpec((tm, F2), lambda i: (i, 0)),
        out_shape=jax.ShapeDtypeStruct((T, F2), BF16), compiler_params=_params("parallel"),
    )(gu, da)


def _ffn_fwd(x, g_pre, g_post, w_in, w_out, sel, tag):
    n = _rms_fwd(x, g_pre, f"{tag}_rms")
    gu = _mm(n, w_in, "nn", BF16, f"{tag}_mm_in", tn=1408, tk=1024, b_sel=sel)
    a = _swiglu_fwd(gu, f"{tag}_swiglu")
    h = _mm(a, w_out, "nn", F32, f"{tag}_mm_out", tn=1024, tk=1408, b_sel=sel)
    return _post_fwd(x, h, g_post, 0.5, f"{tag}_post"), (x, n, gu, a, h)


def _ffn_bwd(dy, saved, g_pre, g_post, w_in, w_out, sel, tag):
    x, n, gu, a, h = saved
    dh, dg_post = _post_bwd(dy, h, g_post, 0.5, f"{tag}_post_b")
    da = _mm(dh, w_out, "nt", BF16, f"{tag}_mm_da", tn=1408, tk=1024, b_sel=sel)
    dw_out = _mm(a, dh, "tn", BF16, f"{tag}_mm_dwout", tm=1408, tn=1024, tk=1024)
    dgu = _swiglu_bwd(gu, da, f"{tag}_swiglu_b")
    dn = _mm(dgu, w_in, "nt", F32, f"{tag}_mm_dn", tn=1024, tk=1408, b_sel=sel)
    dw_in = _mm(n, dgu, "tn", BF16, f"{tag}_mm_dwin", tm=1024, tn=1408, tk=1024)
    dy2, dg_pre = _rms_bwd_add(x, g_pre, dn, dy, f"{tag}_rms_b")
    return dy2, dg_pre, dg_post, dw_in, dw_out


def _shift_down(x, s):
    if s == 0:
        return x
    rows = lax.broadcasted_iota(jnp.int32, x.shape, 0)
    return jnp.where(rows >= s, pltpu.roll(x, s, 0), 0.0)


def _shift_up(x, s):
    if s == 0:
        return x
    T = x.shape[0]
    rows = lax.broadcasted_iota(jnp.int32, x.shape, 0)
    return jnp.where(rows < T - s, pltpu.roll(x, T - s, 0), 0.0)


def _conv_pre(x, w, b):
    y = b
    for k in range(CONV_K):
        y = y + w[k:k + 1, :] * _shift_down(x, CONV_K - 1 - k)
    return y


def _conv_fwd(x, col0, C, w, b, act, name, ct=256):
    T = x.shape[0]
    off = col0 // ct

    def body(x_ref, w_ref, b_ref, o_ref):
        y = _conv_pre(x_ref[...], w_ref[...], b_ref[...])
        o_ref[...] = _silu(y) if act else y

    return pl.pallas_call(
        body, name=name, grid=(C // ct,),
        in_specs=[pl.BlockSpec((T, ct), lambda j: (0, j + off)), pl.BlockSpec((CONV_K, ct), lambda j: (0, j)),
                  pl.BlockSpec((1, ct), lambda j: (0, j))],
        out_specs=pl.BlockSpec((T, ct), lambda j: (0, j)), out_shape=jax.ShapeDtypeStruct((T, C), F32),
        compiler_params=_params("parallel"),
    )(x, w, b)


def _conv_bwd(x, col0, C, w, b, dyact, act, name, ct=256):
    T = x.shape[0]
    off = col0 // ct

    def body(x_ref, w_ref, b_ref, dy_ref, dx_ref, dw_ref, db_ref):
        xv, wv = x_ref[...], w_ref[...]
        dy = dy_ref[...].astype(F32)
        if act:
            pre = _conv_pre(xv, wv, b_ref[...])
            s = _sigmoid(pre)
            dy = dy * (s * (1.0 + pre * (1.0 - s)))
        dx = jnp.zeros_like(dy)
        dws = []
        for k in range(CONV_K):
            dx = dx + wv[k:k + 1, :] * _shift_up(dy, CONV_K - 1 - k)
            dws.append(jnp.sum(dy * _shift_down(xv, CONV_K - 1 - k), axis=0, keepdims=True))
        dx_ref[...] = dx.astype(dx_ref.dtype)
        dw_ref[...] = jnp.concatenate(dws, axis=0)
        db_ref[...] = jnp.sum(dy, axis=0, keepdims=True)

    return pl.pallas_call(
        body, name=name, grid=(C // ct,),
        in_specs=[pl.BlockSpec((T, ct), lambda j: (0, j + off)), pl.BlockSpec((CONV_K, ct), lambda j: (0, j)),
                  pl.BlockSpec((1, ct), lambda j: (0, j)), pl.BlockSpec((T, ct), lambda j: (0, j))],
        out_specs=[pl.BlockSpec((T, ct), lambda j: (0, j)), pl.BlockSpec((CONV_K, ct), lambda j: (0, j)),
                   pl.BlockSpec((1, ct), lambda j: (0, j))],
        out_shape=[jax.ShapeDtypeStruct((T, C), BF16), jax.ShapeDtypeStruct((CONV_K, C), F32),
                   jax.ShapeDtypeStruct((1, C), F32)],
        compiler_params=_params("parallel"),
    )(x, w, b, dyact)


def _gate_act(v, n_ssm):
    lane = lax.broadcasted_iota(jnp.int32, v.shape, 1)
    return jnp.where(lane < n_ssm, _softplus(v), _log_sigmoid(v))


def _gates_fwd(proj, col0, bias, mult, n_ssm, name, tb=512):
    T = proj.shape[0]
    tb = _tile(T, tb)
    off = col0 // LANES

    def body(s_ref, bias_ref, mult_ref, act_ref, cs_ref, carry_ref):
        i = pl.program_id(0)

        @pl.when(i == 0)
        def _():
            carry_ref[...] = jnp.zeros_like(carry_ref)

        act = _gate_act(s_ref[...] + bias_ref[...], n_ssm)
        inc = act * mult_ref[...]
        r = lax.broadcasted_iota(jnp.int32, (tb, tb), 0)
        c = lax.broadcasted_iota(jnp.int32, (tb, tb), 1)
        tri = jnp.where(r >= c, 1.0, 0.0).astype(F32)
        cs = jnp.dot(tri, inc, precision=lax.Precision.HIGHEST, preferred_element_type=F32) + carry_ref[...]
        act_ref[...] = act
        cs_ref[...] = cs
        carry_ref[...] = cs[tb - 1:tb, :]

    return pl.pallas_call(
        body, name=name, grid=(T // tb,),
        in_specs=[pl.BlockSpec((tb, LANES), lambda i: (i, off)), pl.BlockSpec((1, LANES), lambda i: (0, 0)),
                  pl.BlockSpec((1, LANES), lambda i: (0, 0))],
        out_specs=[pl.BlockSpec((tb, LANES), lambda i: (i, 0))] * 2,
        out_shape=[jax.ShapeDtypeStruct((T, LANES), F32)] * 2,
        scratch_shapes=[pltpu.VMEM((1, LANES), F32)], compiler_params=_params("arbitrary"),
    )(proj, bias, mult)


def _gates_bwd(proj, col0, bias, mult, n_ssm, dact, dcs, name, tb=512):
    T = proj.shape[0]
    tb = _tile(T, tb)
    nb = T // tb
    off = col0 // LANES

    def body(s_ref, bias_ref, mult_ref, dact_ref, dcs_ref, ds_ref, dmult_ref, dbias_ref, carry_ref):
        i = pl.program_id(0)

        @pl.when(i == 0)
        def _():
            carry_ref[...] = jnp.zeros_like(carry_ref)
            dmult_ref[...] = jnp.zeros_like(dmult_ref)
            dbias_ref[...] = jnp.zeros_like(dbias_ref)

        v = s_ref[...] + bias_ref[...]
        act = _gate_act(v, n_ssm)
        r = lax.broadcasted_iota(jnp.int32, (tb, tb), 0)
        c = lax.broadcasted_iota(jnp.int32, (tb, tb), 1)
        tri = jnp.where(r <= c, 1.0, 0.0).astype(F32)
        dinc = jnp.dot(tri, dcs_ref[...], precision=lax.Precision.HIGHEST, preferred_element_type=F32) + carry_ref[...]
        carry_ref[...] = dinc[0:1, :]
        da = dact_ref[...] + dinc * mult_ref[...]
        sg = _sigmoid(v)
        lane = lax.broadcasted_iota(jnp.int32, v.shape, 1)
        dv = da * jnp.where(lane < n_ssm, sg, 1.0 - sg)
        ds_ref[...] = dv.astype(ds_ref.dtype)
        dmult_ref[...] += jnp.sum(dinc * act, axis=0, keepdims=True)
        dbias_ref[...] += jnp.sum(dv, axis=0, keepdims=True)

    rev = lambda i: (nb - 1 - i, 0)
    return pl.pallas_call(
        body, name=name, grid=(nb,),
        in_specs=[pl.BlockSpec((tb, LANES), lambda i: (nb - 1 - i, off)), pl.BlockSpec((1, LANES), lambda i: (0, 0)),
                  pl.BlockSpec((1, LANES), lambda i: (0, 0)), pl.BlockSpec((tb, LANES), rev),
                  pl.BlockSpec((tb, LANES), rev)],
        out_specs=[pl.BlockSpec((tb, LANES), rev), pl.BlockSpec((1, LANES), lambda i: (0, 0)),
                   pl.BlockSpec((1, LANES), lambda i: (0, 0))],
        out_shape=[jax.ShapeDtypeStruct((T, LANES), BF16), jax.ShapeDtypeStruct((1, LANES), F32),
                   jax.ShapeDtypeStruct((1, LANES), F32)],
        scratch_shapes=[pltpu.VMEM((1, LANES), F32)], compiler_params=_params("arbitrary"),
    )(proj, bias, mult, dact, dcs)


def _rep_layout(v):
    return jnp.repeat(v, LANES, axis=1)


def _row_layout(v, tk):
    T, H = v.shape
    return v.T.reshape(H, T // tk, 1, tk)


def _causal(tq):
    r = lax.broadcasted_iota(jnp.int32, (tq, tq), 0)
    c = lax.broadcasted_iota(jnp.int32, (tq, tq), 1)
    return r >= c


def _ssd_fwd(xbc, X, cs_rep, cs_row, name, tq=SSD_TILE):
    T = X.shape[0]
    tq = _tile(T, tq)
    nq = T // tq
    d_ssm = X.shape[1]
    gw = d_ssm // SSM_GROUPS
    hpg = gw // SSM_HEAD_DIM
    b_off = d_ssm // SSM_STATE
    c_off = b_off + SSM_GROUPS

    def body(c_ref, b_ref, x_ref, csq_ref, csk_ref, y_ref):
        i = pl.program_id(1)
        c = c_ref[...].astype(BF16)
        half = lax.broadcasted_iota(jnp.int32, (tq, LANES), 1) // SSM_HEAD_DIM
        mask = _causal(tq)

        def step(j, acc, masked):
            r0 = pl.multiple_of(j * tq, tq)
            s = lax.dot_general(c, b_ref[pl.ds(r0, tq), :].astype(BF16), NT, preferred_element_type=F32)
            out = []
            for p in range(hpg // 2):
                xp = x_ref[pl.ds(r0, tq), p * LANES:(p + 1) * LANES]
                a = acc[p]
                for e in range(2):
                    h = 2 * p + e
                    diff = jnp.tile(csq_ref[:, h * LANES:(h + 1) * LANES], (1, tq // LANES)) - csk_ref[h, j]
                    if masked:
                        diff = jnp.where(mask, diff, NEG)
                    pm = (s * jnp.exp(diff)).astype(BF16)
                    xm = jnp.where(half == e, xp, jnp.zeros_like(xp))
                    a = a + jnp.dot(pm, xm, preferred_element_type=F32)
                out.append(a)
            return tuple(out)

        acc = tuple(jnp.zeros((tq, LANES), F32) for _ in range(hpg // 2))
        acc = lax.fori_loop(0, i, lambda j, a: step(j, a, False), acc)
        acc = step(i, acc, True)
        y_ref[...] = jnp.concatenate(acc, axis=1)

    return pl.pallas_call(
        body, name=name, grid=(SSM_GROUPS, nq),
        in_specs=[pl.BlockSpec((tq, SSM_STATE), lambda g, i: (i, c_off + g)),
                  pl.BlockSpec((T, SSM_STATE), lambda g, i: (0, b_off + g)),
                  pl.BlockSpec((T, gw), lambda g, i: (0, g)),
                  pl.BlockSpec((tq, hpg * LANES), lambda g, i: (i, g)),
                  pl.BlockSpec((hpg, nq, 1, tq), lambda g, i: (g, 0, 0, 0))],
        out_specs=pl.BlockSpec((tq, gw), lambda g, i: (i, g)),
        out_shape=jax.ShapeDtypeStruct((T, d_ssm), F32), compiler_params=_params("parallel", "arbitrary"),
    )(xbc, xbc, X, cs_rep, cs_row)


def _ssd_bwd(xbc, X, cs_rep, cs_row, dY, name, tq=SSD_TILE):
    T = X.shape[0]
    tq = _tile(T, tq)
    nq = T // tq
    d_ssm = X.shape[1]
    gw = d_ssm // SSM_GROUPS
    hpg = gw // SSM_HEAD_DIM
    nheads = d_ssm // SSM_HEAD_DIM
    b_off = d_ssm // SSM_STATE
    c_off = b_off + SSM_GROUPS

    def body(c_ref, b_ref, x_ref, dy_ref, csq_ref, csk_ref, dc_ref, db_ref, dx_ref, dcsq_ref, dcsk_ref):
        i = pl.program_id(1)

        @pl.when(i == 0)
        def _():
            db_ref[...] = jnp.zeros_like(db_ref)
            dx_ref[...] = jnp.zeros_like(dx_ref)
            dcsk_ref[...] = jnp.zeros_like(dcsk_ref)

        c = c_ref[...].astype(BF16)
        half = lax.broadcasted_iota(jnp.int32, (tq, LANES), 1) // SSM_HEAD_DIM
        mask = _causal(tq)

        def step(j, carry, masked):
            dc_acc, rows = carry
            rows = list(rows)
            r0 = pl.multiple_of(j * tq, tq)
            b = b_ref[pl.ds(r0, tq), :].astype(BF16)
            s = lax.dot_general(c, b, NT, preferred_element_type=F32)
            ds_tot = jnp.zeros((tq, tq), F32)
            for p in range(hpg // 2):
                cols = slice(p * LANES, (p + 1) * LANES)
                xp = x_ref[pl.ds(r0, tq), cols]
                dyp = dy_ref[:, cols]
                dx_p = jnp.zeros((tq, LANES), F32)
                for e in range(2):
                    h = 2 * p + e
                    diff = jnp.tile(csq_ref[:, h * LANES:(h + 1) * LANES], (1, tq // LANES)) - csk_ref[h, j]
                    if masked:
                        diff = jnp.where(mask, diff, NEG)
                    decay = jnp.exp(diff)
                    dym = jnp.where(half == e, dyp, jnp.zeros_like(dyp))
                    g = lax.dot_general(dym, xp, NT, preferred_element_type=F32) * decay
                    ds_tot = ds_tot + g
                    m = g * s
                    rows[h] = rows[h] + jnp.sum(m, axis=1, keepdims=True)
                    dcsk_ref[h, j] -= jnp.sum(m, axis=0, keepdims=True)
                    pm = (s * decay).astype(BF16)
                    dx_p = dx_p + lax.dot_general(pm, dym, TN, preferred_element_type=F32)
                dx_ref[pl.ds(r0, tq), cols] += dx_p
            dsb = ds_tot.astype(BF16)
            dc_acc = dc_acc + jnp.dot(dsb, b, preferred_element_type=F32)
            db_ref[pl.ds(r0, tq), :] += lax.dot_general(dsb, c, TN, preferred_element_type=F32)
            return dc_acc, tuple(rows)

        carry = (jnp.zeros((tq, SSM_STATE), F32), tuple(jnp.zeros((tq, 1), F32) for _ in range(hpg)))
        carry = lax.fori_loop(0, i, lambda j, cr: step(j, cr, False), carry)
        dc_acc, rows = step(i, carry, True)
        dc_ref[...] = dc_acc
        dcsq_ref[...] = jnp.concatenate([jnp.broadcast_to(r, (tq, LANES)) for r in rows], axis=1)

    return pl.pallas_call(
        body, name=name, grid=(SSM_GROUPS, nq),
        in_specs=[pl.BlockSpec((tq, SSM_STATE), lambda g, i: (i, c_off + g)),
                  pl.BlockSpec((T, SSM_STATE), lambda g, i: (0, b_off + g)),
                  pl.BlockSpec((T, gw), lambda g, i: (0, g)),
                  pl.BlockSpec((tq, gw), lambda g, i: (i, g)),
                  pl.BlockSpec((tq, hpg * LANES), lambda g, i: (i, g)),
                  pl.BlockSpec((hpg, nq, 1, tq), lambda g, i: (g, 0, 0, 0))],
        out_specs=[pl.BlockSpec((tq, SSM_STATE), lambda g, i: (i, g)),
                   pl.BlockSpec((T, SSM_STATE), lambda g, i: (0, g)),
                   pl.BlockSpec((T, gw), lambda g, i: (0, g)),
                   pl.BlockSpec((tq, hpg * LANES), lambda g, i: (i, g)),
                   pl.BlockSpec((hpg, nq, 1, tq), lambda g, i: (g, 0, 0, 0))],
        out_shape=[jax.ShapeDtypeStruct((T, SSM_GROUPS * SSM_STATE), F32),
                   jax.ShapeDtypeStruct((T, SSM_GROUPS * SSM_STATE), F32),
                   jax.ShapeDtypeStruct((T, d_ssm), F32),
                   jax.ShapeDtypeStruct((T, nheads * LANES), F32),
                   jax.ShapeDtypeStruct((nheads, nq, 1, tq), F32)],
        compiler_params=_params("arbitrary", "arbitrary"),
    )(xbc, xbc, X, dY, cs_rep, cs_row)


def _fox_fwd(proj, q0, k0, v0, nh, cum_rep, cum_row, name, tq=FOX_TILE):
    T = proj.shape[0]
    tq = _tile(T, tq)
    nq = T // tq
    hd = FOX_HEAD_DIM
    scale = hd ** -0.5
    qo, ko, vo = q0 // hd, k0 // hd, v0 // hd

    def body(q_ref, k_ref, v_ref, cq_ref, ck_ref, o_ref, lse_ref):
        i = pl.program_id(1)
        q = q_ref[...].astype(BF16)
        cq = jnp.tile(cq_ref[...], (1, tq // LANES))
        mask = _causal(tq)

        def step(j, carry, masked):
            m, l, acc = carry
            r0 = pl.multiple_of(j * tq, tq)
            k = k_ref[pl.ds(r0, tq), :].astype(BF16)
            v = v_ref[pl.ds(r0, tq), :].astype(BF16)
            s = lax.dot_general(q, k, NT, preferred_element_type=F32) * scale + cq - ck_ref[j]
            if masked:
                s = jnp.where(mask, s, NEG)
            m_new = jnp.maximum(m, jnp.max(s, axis=1, keepdims=True))
            alpha = jnp.exp(m - m_new)
            p = jnp.exp(s - m_new)
            l = alpha * l + jnp.sum(p, axis=1, keepdims=True)
            acc = alpha * acc + jnp.dot(p.astype(BF16), v, preferred_element_type=F32)
            return m_new, l, acc

        carry = (jnp.full((tq, 1), NEG, F32), jnp.zeros((tq, 1), F32), jnp.zeros((tq, hd), F32))
        carry = lax.fori_loop(0, i, lambda j, cr: step(j, cr, False), carry)
        m, l, acc = step(i, carry, True)
        o_ref[...] = (acc / l).astype(o_ref.dtype)
        lse_ref[...] = jnp.broadcast_to(m + jnp.log(l), (tq, LANES))

    return pl.pallas_call(
        body, name=name, grid=(nh, nq),
        in_specs=[pl.BlockSpec((tq, hd), lambda h, i: (i, qo + h)), pl.BlockSpec((T, hd), lambda h, i: (0, ko + h)),
                  pl.BlockSpec((T, hd), lambda h, i: (0, vo + h)), pl.BlockSpec((tq, LANES), lambda h, i: (i, h)),
                  pl.BlockSpec((None, nq, 1, tq), lambda h, i: (h, 0, 0, 0))],
        out_specs=[pl.BlockSpec((tq, hd), lambda h, i: (i, h)), pl.BlockSpec((tq, LANES), lambda h, i: (i, h))],
        out_shape=[jax.ShapeDtypeStruct((T, nh * hd), BF16), jax.ShapeDtypeStruct((T, nh * LANES), F32)],
        compiler_params=_params("parallel", "arbitrary"),
    )(proj, proj, proj, cum_rep, cum_row)


def _fox_bwd(proj, q0, k0, v0, nh, cum_rep, cum_row, o, lse, dcat, do0, name, tq=FOX_TILE):
    T = proj.shape[0]
    tq = _tile(T, tq)
    nq = T // tq
    hd = FOX_HEAD_DIM
    scale = hd ** -0.5
    qo, ko, vo, doo = q0 // hd, k0 // hd, v0 // hd, do0 // hd

    def body(q_ref, k_ref, v_ref, do_ref, o_ref, lse_ref, cq_ref, ck_ref, dq_ref, dk_ref, dv_ref, dck_ref, dcq_ref):
        i = pl.program_id(1)

        @pl.when(i == 0)
        def _():
            dk_ref[...] = jnp.zeros_like(dk_ref)
            dv_ref[...] = jnp.zeros_like(dv_ref)
            dck_ref[...] = jnp.zeros_like(dck_ref)

        q = q_ref[...].astype(BF16)
        do = do_ref[...].astype(F32)
        dob = do.astype(BF16)
        delta = jnp.sum(do * o_ref[...].astype(F32), axis=1, keepdims=True)
        bias = jnp.tile(cq_ref[...] - lse_ref[...], (1, tq // LANES))
        mask = _causal(tq)

        def step(j, carry, masked):
            dq, rows = carry
            r0 = pl.multiple_of(j * tq, tq)
            k = k_ref[pl.ds(r0, tq), :].astype(BF16)
            v = v_ref[pl.ds(r0, tq), :].astype(BF16)
            s = lax.dot_general(q, k, NT, preferred_element_type=F32) * scale + bias - ck_ref[j]
            if masked:
                s = jnp.where(mask, s, NEG)
            p = jnp.exp(s)
            dp = lax.dot_general(dob, v, NT, preferred_element_type=F32)
            ds = p * (dp - delta)
            dsb = ds.astype(BF16)
            dq = dq + jnp.dot(dsb, k, preferred_element_type=F32) * scale
            dk_ref[pl.ds(r0, tq), :] += lax.dot_general(dsb, q, TN, preferred_element_type=F32) * scale
            dv_ref[pl.ds(r0, tq), :] += lax.dot_general(p.astype(BF16), dob, TN, preferred_element_type=F32)
            dck_ref[j] -= jnp.sum(ds, axis=0, keepdims=True)
            return dq, rows + jnp.sum(ds, axis=1, keepdims=True)

        carry = (jnp.zeros((tq, hd), F32), jnp.zeros((tq, 1), F32))
        carry = lax.fori_loop(0, i, lambda j, cr: step(j, cr, False), carry)
        dq, rows = step(i, carry, True)
        dq_ref[...] = dq.astype(dq_ref.dtype)
        dcq_ref[...] = jnp.broadcast_to(rows, (tq, LANES))

    return pl.pallas_call(
        body, name=name, grid=(nh, nq),
        in_specs=[pl.BlockSpec((tq, hd), lambda h, i: (i, qo + h)), pl.BlockSpec((T, hd), lambda h, i: (0, ko + h)),
                  pl.BlockSpec((T, hd), lambda h, i: (0, vo + h)), pl.BlockSpec((tq, hd), lambda h, i: (i, doo + h)),
                  pl.BlockSpec((tq, hd), lambda h, i: (i, h)), pl.BlockSpec((tq, LANES), lambda h, i: (i, h)),
                  pl.BlockSpec((tq, LANES), lambda h, i: (i, h)),
                  pl.BlockSpec((None, nq, 1, tq), lambda h, i: (h, 0, 0, 0))],
        out_specs=[pl.BlockSpec((tq, hd), lambda h, i: (i, h)), pl.BlockSpec((T, hd), lambda h, i: (0, h)),
                   pl.BlockSpec((T, hd), lambda h, i: (0, h)), pl.BlockSpec((None, nq, 1, tq), lambda h, i: (h, 0, 0, 0)),
                   pl.BlockSpec((tq, LANES), lambda h, i: (i, h))],
        out_shape=[jax.ShapeDtypeStruct((T, nh * hd), BF16), jax.ShapeDtypeStruct((T, nh * hd), F32),
                   jax.ShapeDtypeStruct((T, nh * hd), F32), jax.ShapeDtypeStruct((nh, nq, 1, tq), F32),
                   jax.ShapeDtypeStruct((T, nh * LANES), F32)],
        compiler_params=_params("arbitrary", "arbitrary"),
    )(proj, proj, proj, dcat, o, lse, cum_rep, cum_row)


def _scan_fwd(a, u, name, ct=256):
    T, C = a.shape

    def body(a_ref, u_ref, h_ref):
        def blk(tb, h):
            r0 = pl.multiple_of(tb * SUBLANES, SUBLANES)
            ab, ub = a_ref[pl.ds(r0, SUBLANES), :], u_ref[pl.ds(r0, SUBLANES), :]
            rows = []
            for r in range(SUBLANES):
                h = ab[r:r + 1, :] * h + ub[r:r + 1, :]
                rows.append(h)
            h_ref[pl.ds(r0, SUBLANES), :] = jnp.concatenate(rows, axis=0)
            return h

        lax.fori_loop(0, T // SUBLANES, blk, jnp.zeros((1, ct), F32))

    spec = pl.BlockSpec((T, ct), lambda j: (0, j))
    return pl.pallas_call(body, name=name, grid=(C // ct,), in_specs=[spec, spec], out_specs=spec,
                          out_shape=jax.ShapeDtypeStruct((T, C), F32), compiler_params=_params("parallel"))(a, u)


def _scan_bwd(a, dh, h, name, ct=128):
    T, C = a.shape
    nb = T // SUBLANES

    def body(a_ref, dh_ref, h_ref, g_ref, da_ref):
        def blk(t, carry):
            r0 = pl.multiple_of((nb - 1 - t) * SUBLANES, SUBLANES)
            ab, db = a_ref[pl.ds(r0, SUBLANES), :], dh_ref[pl.ds(r0, SUBLANES), :]
            rows = [None] * SUBLANES
            for r in range(SUBLANES - 1, -1, -1):
                g = db[r:r + 1, :] + carry
                carry = ab[r:r + 1, :] * g
                rows[r] = g
            g_ref[pl.ds(r0, SUBLANES), :] = jnp.concatenate(rows, axis=0)
            return carry

        lax.fori_loop(0, nb, blk, jnp.zeros((1, ct), F32))
        da_ref[...] = g_ref[...] * _shift_down(h_ref[...], 1)

    spec = pl.BlockSpec((T, ct), lambda j: (0, j))
    return pl.pallas_call(body, name=name, grid=(C // ct,), in_specs=[spec] * 3, out_specs=[spec] * 2,
                          out_shape=[jax.ShapeDtypeStruct((T, C), F32)] * 2,
                          compiler_params=_params("parallel"))(a, dh, h)


def _lru_elem(xc, ra, ia, lam):
    r, i = _sigmoid(ra), _sigmoid(ia)
    log_a = RG_LRU_C * r * _log_sigmoid(lam)
    return jnp.exp(log_a), jnp.sqrt(_neg_expm1(2.0 * log_a)) * (i * xc)


def _lru_gates_fwd(xc, w_a, b_a, w_x, b_x, lam, name):
    def fn(xv, ba, bx, lm, wa, wx):
        xb = xv.astype(BF16)
        ra = jnp.dot(xb, wa.astype(BF16), preferred_element_type=F32) + ba
        ia = jnp.dot(xb, wx.astype(BF16), preferred_element_type=F32) + bx
        return _lru_elem(xv, ra, ia, lm)
    nb = xc.shape[1] // RNN_BLOCK
    return _ew(fn, [(xc, 0)], [(b_a, 0), (b_x, 0), (lam, 0), (w_a, 0), (w_x, 0)], [F32, F32], [],
               name=name, tm=512, cb=RNN_BLOCK, ncb=nb)


def _lru_gates_bwd(xc, w_a, b_a, w_x, b_x, lam, da, du, name):
    def fn(xv, dav, duv, ba, bx, lm, wa, wx):
        xb, wab, wxb = xv.astype(BF16), wa.astype(BF16), wx.astype(BF16)
        ra = jnp.dot(xb, wab, preferred_element_type=F32) + ba
        ia = jnp.dot(xb, wxb, preferred_element_type=F32) + bx
        _, vjp = jax.vjp(_lru_elem, xv, ra, ia, lm)
        dx, dra, dia, dlm = vjp((dav, duv))
        drb, dib = dra.astype(BF16), dia.astype(BF16)
        dx = dx + lax.dot_general(drb, wab, NT, preferred_element_type=F32)
        dx = dx + lax.dot_general(dib, wxb, NT, preferred_element_type=F32)
        dwa = lax.dot_general(xb, drb, TN, preferred_element_type=F32)
        dwx = lax.dot_general(xb, dib, TN, preferred_element_type=F32)
        return (dx, jnp.sum(dra, axis=0, keepdims=True), jnp.sum(dia, axis=0, keepdims=True), dlm, dwa, dwx)
    nb = xc.shape[1] // RNN_BLOCK
    return _ew(fn, [(xc, 0), (da, 0), (du, 0)], [(b_a, 0), (b_x, 0), (lam, 0), (w_a, 0), (w_x, 0)], [F32],
               [b_a.shape, b_x.shape, lam.shape, w_a.shape, w_x.shape], name=name, tm=512, cb=RNN_BLOCK, ncb=nb)


def _hyb_cols(D):
    conv = D + 2 * SSM_GROUPS * SSM_STATE
    z0, x0, q0 = 0, D, D + conv
    return dict(z=z0, xbc=x0, q=q0, k=q0 + D, v=q0 + 2 * D, small=q0 + 3 * D, total=q0 + 3 * D + LANES, conv=conv)


def _hyb_w_in_reorder(w, D):
    cols = _hyb_cols(D)
    nh_s, nh_f = D // SSM_HEAD_DIM, D // FOX_HEAD_DIM
    a = D + cols["conv"]
    pad = jnp.zeros((w.shape[0], LANES - nh_s - nh_f), w.dtype)
    return jnp.concatenate([w[:, :a], w[:, a + nh_s:a + nh_s + 3 * D], w[:, a:a + nh_s], w[:, a + nh_s + 3 * D:], pad], axis=1)


def _hyb_w_in_restore(dw, D):
    cols = _hyb_cols(D)
    nh_s, nh_f = D // SSM_HEAD_DIM, D // FOX_HEAD_DIM
    a = D + cols["conv"]
    s = cols["small"]
    return jnp.concatenate([dw[:, :a], dw[:, s:s + nh_s], dw[:, a:s], dw[:, s + nh_s:s + nh_s + nh_f]], axis=1)


def _ssm_out(Y, xs, z, dfull, ng):
    y = (Y + dfull * xs) * _silu(z)
    return y * lax.rsqrt(jnp.mean(y * y, axis=-1, keepdims=True) + NORM_EPS) * ng


def _hyb_fwd(x, g_pre, g_post, p, tag):
    T, D = x.shape
    cols = _hyb_cols(D)
    nh_s, nh_f = D // SSM_HEAD_DIM, D // FOX_HEAD_DIM
    n = _rms_fwd(x, g_pre, f"{tag}_rms")
    proj = _mm(n, p["w_in"], "nn", F32, f"{tag}_mm_in", tn=1152, tk=1024)
    a_neg = -jnp.exp(p["a_log"])
    bias = jnp.concatenate([p["dt_bias"], p["b_f"], jnp.zeros((LANES - nh_s - nh_f,), F32)])[None]
    mult = jnp.concatenate([a_neg, jnp.ones((nh_f,), F32), jnp.zeros((LANES - nh_s - nh_f,), F32)])[None]
    act, cs = _gates_fwd(proj, cols["small"], bias, mult, nh_s, f"{tag}_gates")
    dt, cs_s, cum = act[:, :nh_s], cs[:, :nh_s], cs[:, nh_s:nh_s + nh_f]
    dtf = jnp.repeat(dt, SSM_HEAD_DIM, axis=1)
    cs_rep, cs_row = _rep_layout(cs_s), _row_layout(cs_s, _tile(T, SSD_TILE))
    cum_rep, cum_row = _rep_layout(cum), _row_layout(cum, _tile(T, FOX_TILE))
    xbc = _conv_fwd(proj, cols["xbc"], cols["conv"], p["conv_w"], p["conv_b"], True, f"{tag}_conv")
    X = _ew(lambda xv, dv: xv * dv, [(xbc, 0), (dtf, 0)], [], [BF16], [], name=f"{tag}_xdt", cb=512, ncb=D // 512)
    Y = _ssd_fwd(xbc, X, cs_rep, cs_row, f"{tag}_ssd")
    dfull = jnp.repeat(p["d"], SSM_HEAD_DIM)[None]
    gw = D // SSM_GROUPS
    y_ssm = _ew(_ssm_out, [(Y, 0), (xbc, 0), (proj, cols["z"] // gw)], [(dfull, 0), (p["norm_g"], 0)], [BF16], [],
                name=f"{tag}_ssm_out", cb=gw, ncb=SSM_GROUPS)
    o, lse = _fox_fwd(proj, cols["q"], cols["k"], cols["v"], nh_f, cum_rep, cum_row, f"{tag}_fox")
    mix = _mm_parts([(y_ssm, p["w_out"][:D]), (o, p["w_out"][D:])], "nn", F32, f"{tag}_mm_out", tn=1024)
    x2 = _post_fwd(x, mix, g_post, 1.0, f"{tag}_post")
    return x2, (x, n, proj, bias, mult, dtf, cs_rep, cs_row, cum_rep, cum_row, xbc, X, Y, dfull, o, lse, y_ssm, mix)


def _hyb_bwd(dy, saved, g_pre, g_post, p, tag):
    x, n, proj, bias, mult, dtf, cs_rep, cs_row, cum_rep, cum_row, xbc, X, Y, dfull, o, lse, y_ssm, mix = saved
    T, D = x.shape
    cols = _hyb_cols(D)
    nh_s, nh_f = D // SSM_HEAD_DIM, D // FOX_HEAD_DIM
    gw = D // SSM_GROUPS
    dmix, dg_post = _post_bwd(dy, mix, g_post, 1.0, f"{tag}_post_b")
    dcat = _mm(dmix, p["w_out"], "nt", BF16, f"{tag}_mm_dcat", tn=1024, tk=1024)
    dw_out = jnp.concatenate([_mm(y_ssm, dmix, "tn", BF16, f"{tag}_mm_dwout_s", tm=1024, tn=1024, tk=1024),
                              _mm(o, dmix, "tn", BF16, f"{tag}_mm_dwout_f", tm=1024, tn=1024, tk=1024)], axis=0)

    def ssm_out_b(Yv, xv, zv, dv, dfv, ngv):
        _, vjp = jax.vjp(_ssm_out, Yv, xv, zv, dfv, ngv)
        return vjp(dv.astype(F32))
    dY, dxs_skip, dz, ddfull, dng = _ew(
        ssm_out_b, [(Y, 0), (xbc, 0), (proj, cols["z"] // gw), (dcat, 0)], [(dfull, 0), (p["norm_g"], 0)],
        [BF16, F32, BF16], [dfull.shape, p["norm_g"].shape], name=f"{tag}_ssm_out_b", cb=gw, ncb=SSM_GROUPS)
    dC, dB, dX, dcs_q, dcs_k = _ssd_bwd(xbc, X, cs_rep, cs_row, dY, f"{tag}_ssd_b")
    dxs, ddtf = _ew(lambda dXv, skv, xv, dv: (dXv * dv + skv, dXv * xv), [(dX, 0), (dxs_skip, 0), (xbc, 0), (dtf, 0)],
                    [], [F32, F32], [], name=f"{tag}_xdt_b", cb=512, ncb=D // 512)
    ddt = ddtf.reshape(T, nh_s, SSM_HEAD_DIM).sum(-1)
    dcs_s = dcs_q[:, ::LANES] + dcs_k.reshape(nh_s, T).T
    dq, dk, dv, dcum_k, dcum_q = _fox_bwd(proj, cols["q"], cols["k"], cols["v"], nh_f, cum_rep, cum_row, o, lse, dcat, D,
                                  f"{tag}_fox_b")
    dcum = dcum_q[:, ::LANES] + dcum_k.reshape(nh_f, T).T
    zpad = jnp.zeros((T, LANES - nh_s - nh_f), F32)
    dact = jnp.concatenate([ddt, jnp.zeros((T, nh_f), F32), zpad], axis=1)
    dcs = jnp.concatenate([dcs_s, dcum, zpad], axis=1)
    dsmall, dmult, dbias = _gates_bwd(proj, cols["small"], bias, mult, nh_s, dact, dcs, f"{tag}_gates_b")
    dxbc_act = jnp.concatenate([dxs, dB, dC], axis=1)
    dxbc, dconv_w, dconv_b = _conv_bwd(proj, cols["xbc"], cols["conv"], p["conv_w"], p["conv_b"], dxbc_act, True,
                                       f"{tag}_conv_b")
    pieces = [(dz, "z"), (dxbc, "xbc"), (dq, "q"), (dk, "k"), (dv, "v"), (dsmall, "small")]
    w_cols = lambda d, key: p["w_in"][:, cols[key]:cols[key] + d.shape[1]]
    dn = _mm_parts([(d, w_cols(d, key)) for d, key in pieces], "nt", F32, f"{tag}_mm_dn", tm=256, tn=512)
    dw_in = jnp.concatenate([_mm(n, d, "tn", BF16, f"{tag}_mm_dwin_{key}", tm=1024, tn=1024, tk=1024)
                             for d, key in pieces], axis=1)
    dy2, dg_pre = _rms_bwd_add(x, g_pre, dn, dy, f"{tag}_rms_b")
    grads = dict(w_in=dw_in, w_out=dw_out, conv_w=dconv_w, conv_b=dconv_b[0], dt_bias=dbias[0, :nh_s],
                 a_log=dmult[0, :nh_s] * mult[0, :nh_s], d=ddfull.reshape(nh_s, SSM_HEAD_DIM).sum(-1),
                 norm_g=dng[0], b_f=dbias[0, nh_s:nh_s + nh_f])
    return dy2, dg_pre, dg_post, grads


def _rec_fwd(x, g_pre, g_post, p, tag):
    T, D = x.shape
    n = _rms_fwd(x, g_pre, f"{tag}_rms")
    pr = _mm(n, p["w_in"], "nn", F32, f"{tag}_mm_in", tn=1024, tk=1024)
    xc = _conv_fwd(pr, D, D, p["conv_w"], p["conv_b"], False, f"{tag}_conv")
    a, u = _lru_gates_fwd(xc, p["w_a"], p["b_a"], p["w_x"], p["b_x"], p["lam"], f"{tag}_lru")
    hs = _scan_fwd(a, u, f"{tag}_scan")
    og = _ew(lambda hv, gv: hv * _gelu_tanh(gv), [(hs, 0), (pr, 0)], [], [BF16], [], name=f"{tag}_gate", cb=D)
    mix = _mm(og, p["w_out"], "nn", F32, f"{tag}_mm_out", tn=1024, tk=1024)
    x2 = _post_fwd(x, mix, g_post, 1.0, f"{tag}_post")
    return x2, (x, n, pr, xc, a, hs, og, mix)


def _rec_bwd(dy, saved, g_pre, g_post, p, tag):
    x, n, pr, xc, a, hs, og, mix = saved
    T, D = x.shape
    dmix, dg_post = _post_bwd(dy, mix, g_post, 1.0, f"{tag}_post_b")
    dog = _mm(dmix, p["w_out"], "nt", F32, f"{tag}_mm_dog", tn=1024, tk=1024)
    dw_out = _mm(og, dmix, "tn", BF16, f"{tag}_mm_dwout", tm=1024, tn=1024, tk=1024)

    def gate_b(hv, gv, dv):
        _, vjp = jax.vjp(lambda h_, g_: h_ * _gelu_tanh(g_), hv, gv)
        return vjp(dv)
    dhs, dgate = _ew(gate_b, [(hs, 0), (pr, 0), (dog, 0)], [], [F32, BF16], [], name=f"{tag}_gate_b", cb=D)
    du, da = _scan_bwd(a, dhs, hs, f"{tag}_scan_b")
    dxc, db_a, db_x, dlam, dw_a, dw_x = _lru_gates_bwd(xc, p["w_a"], p["b_a"], p["w_x"], p["b_x"], p["lam"], da, du,
                                                       f"{tag}_lru_b")
    dxr, dconv_w, dconv_b = _conv_bwd(pr, D, D, p["conv_w"], p["conv_b"], dxc, False, f"{tag}_conv_b")
    dn = _mm_parts([(dgate, p["w_in"][:, :D]), (dxr, p["w_in"][:, D:])], "nt", F32, f"{tag}_mm_dn", tn=1024)
    dw_in = jnp.concatenate([_mm(n, dgate, "tn", BF16, f"{tag}_mm_dwin_g", tm=1024, tn=1024, tk=1024),
                             _mm(n, dxr, "tn", BF16, f"{tag}_mm_dwin_x", tm=1024, tn=1024, tk=1024)], axis=1)
    dy2, dg_pre = _rms_bwd_add(x, g_pre, dn, dy, f"{tag}_rms_b")
    grads = dict(w_in=dw_in, w_out=dw_out, conv_w=dconv_w, conv_b=dconv_b[0], w_a=dw_a, b_a=db_a[0], w_x=dw_x,
                 b_x=db_x[0], lam=dlam[0])
    return dy2, dg_pre, dg_post, grads


def _hyb_params(W, i, D):
    return dict(w_in=_hyb_w_in_reorder(W["hyb_w_in"][i], D), w_out=W["hyb_w_out"][i], conv_w=W["ssm_conv_w"][i],
                conv_b=W["ssm_conv_b"][i][None], dt_bias=W["ssm_dt_bias"][i], a_log=W["ssm_a_log"][i],
                d=W["ssm_d"][i], norm_g=W["ssm_norm_g"][i][None], b_f=W["fox_b_f"][i])


def _rec_params(W, j):
    return dict(w_in=W["rec_w_in"][j], w_out=W["rec_w_out"][j], conv_w=W["rec_conv_w"][j],
                conv_b=W["rec_conv_b"][j][None], w_a=W["rec_w_a"][j], b_a=W["rec_b_a"][j][None],
                w_x=W["rec_w_x"][j], b_x=W["rec_b_x"][j][None], lam=W["rec_lambda"][j][None])


def _local_step(x, target, W):
    T, D = x.shape
    depth = W["norm_g"].shape[0]
    g = lambda l, k: W["norm_g"][l, k][None]
    saved = []
    for l in range(depth):
        x, s0 = _ffn_fwd(x, g(l, 0), g(l, 1), W["ffn_w_in"], W["ffn_w_out"], (l, 0), f"l{l}_ffn0")
        if l % 2 == 0:
            pm = _hyb_params(W, l // 2, D)
            x, s1 = _hyb_fwd(x, g(l, 2), g(l, 3), pm, f"l{l}_hyb")
        else:
            pm = _rec_params(W, l // 2)
            x, s1 = _rec_fwd(x, g(l, 2), g(l, 3), pm, f"l{l}_rec")
        x, s2 = _ffn_fwd(x, g(l, 4), g(l, 5), W["ffn_w_in"], W["ffn_w_out"], (l, 1), f"l{l}_ffn1")
        saved.append((s0, s1, s2, pm))

    def loss_fn(yv, tv):
        err = yv - tv
        part = 0.5 * jnp.sum(jnp.sum(err * err, axis=1, keepdims=True), axis=0, keepdims=True) / D
        return err * (1.0 / D), jnp.broadcast_to(part, (1, D))
    dy, loss_row = _ew(loss_fn, [(x, 0), (target, 0)], [], [F32], [(1, D)], name="loss")
    loss = loss_row[0, 0]

    gn = [[None] * 6 for _ in range(depth)]
    g_ffn_in = [[None, None] for _ in range(depth)]
    g_ffn_out = [[None, None] for _ in range(depth)]
    g_hyb, g_rec = [], []
    for l in reversed(range(depth)):
        s0, s1, s2, pm = saved[l]
        dy, gn[l][4], gn[l][5], g_ffn_in[l][1], g_ffn_out[l][1] = _ffn_bwd(
            dy, s2, g(l, 4), g(l, 5), W["ffn_w_in"], W["ffn_w_out"], (l, 1), f"l{l}_ffn1")
        if l % 2 == 0:
            dy, gn[l][2], gn[l][3], gm = _hyb_bwd(dy, s1, g(l, 2), g(l, 3), pm, f"l{l}_hyb")
            g_hyb.insert(0, gm)
        else:
            dy, gn[l][2], gn[l][3], gm = _rec_bwd(dy, s1, g(l, 2), g(l, 3), pm, f"l{l}_rec")
            g_rec.insert(0, gm)
        dy, gn[l][0], gn[l][1], g_ffn_in[l][0], g_ffn_out[l][0] = _ffn_bwd(
            dy, s0, g(l, 0), g(l, 1), W["ffn_w_in"], W["ffn_w_out"], (l, 0), f"l{l}_ffn0")

    st = lambda items: jnp.stack(items)
    grads = {
        "norm_g": st([st([r[0] for r in row]) for row in gn]),
        "ffn_w_in": st([st(row) for row in g_ffn_in]),
        "ffn_w_out": st([st(row) for row in g_ffn_out]),
        "hyb_w_in": st([_hyb_w_in_restore(m["w_in"], D) for m in g_hyb]),
        "ssm_conv_w": st([m["conv_w"] for m in g_hyb]), "ssm_conv_b": st([m["conv_b"] for m in g_hyb]),
        "ssm_dt_bias": st([m["dt_bias"] for m in g_hyb]), "ssm_a_log": st([m["a_log"] for m in g_hyb]),
        "ssm_d": st([m["d"] for m in g_hyb]), "ssm_norm_g": st([m["norm_g"] for m in g_hyb]),
        "fox_b_f": st([m["b_f"] for m in g_hyb]), "hyb_w_out": st([m["w_out"] for m in g_hyb]),
        "rec_w_in": st([m["w_in"] for m in g_rec]), "rec_conv_w": st([m["conv_w"] for m in g_rec]),
        "rec_conv_b": st([m["conv_b"] for m in g_rec]), "rec_w_a": st([m["w_a"] for m in g_rec]),
        "rec_b_a": st([m["b_a"] for m in g_rec]), "rec_w_x": st([m["w_x"] for m in g_rec]),
        "rec_b_x": st([m["b_x"] for m in g_rec]), "rec_lambda": st([m["lam"] for m in g_rec]),
        "rec_w_out": st([m["w_out"] for m in g_rec]),
    }
    return loss, dy, grads


MESH_AXES = ("x", "y", "c")
N_CHIPS = 4
N_DEV = 8
HBM = pl.BlockSpec(memory_space=pltpu.HBM)


def _mesh_pos():
    return tuple(lax.axis_index(n) for n in MESH_AXES)


def _other_chips(x, y):
    chips = [(1 - x, y), (x, 1 - y), (1 - x, 1 - y)]
    return chips, [2 * cx + cy for cx, cy in chips]


def _rcopy(src, dst, send_sem, recv_sem, dev):
    return pltpu.make_async_remote_copy(src_ref=src, dst_ref=dst, send_sem=send_sem, recv_sem=recv_sem,
                                        device_id=dev, device_id_type=pl.DeviceIdType.MESH)


def _comm_params():
    return pltpu.CompilerParams()


DMA_CHUNK_BYTES = 1 << 20
DMA_ROW_ALIGN = 16


def _nchunks(rows, row_bytes):
    n = max(1, min(rows // DMA_ROW_ALIGN, (rows * row_bytes) // DMA_CHUNK_BYTES))
    while n > 1 and (rows % n or (rows // n) % DMA_ROW_ALIGN):
        n -= 1
    return n


def _row_bytes(ref):
    return ref.shape[-1] * jnp.dtype(ref.dtype).itemsize


def _all_gather(arrs, name):
    n = len(arrs)

    def body(*refs):
        ins, outs = refs[:n], refs[n:2 * n]
        send, recv, fsend, frecv, lsem = refs[2 * n:]
        x, y, c = _mesh_pos()
        k = 2 * x + y
        sibling = (x, y, 1 - c)
        chips, chip_k = _other_chips(x, y)
        halves = [r.shape[0] // 2 for r in ins]
        nchs = [_nchunks(h, _row_bytes(r)) for h, r in zip(halves, ins)]
        for a in range(n):
            h, step = halves[a], halves[a] // nchs[a]
            for q in range(2 * nchs[a]):
                rows = pl.ds(q * step, step)
                pltpu.make_async_copy(ins[a].at[rows], outs[a].at[k, rows], lsem.at[a]).start()
            for j, chip in enumerate(chips):
                for q in range(nchs[a]):
                    rows = pl.ds(c * h + q * step, step)
                    _rcopy(ins[a].at[rows], outs[a].at[k, rows], send.at[a, j], recv.at[a, j], (*chip, c)).start()
        for a in range(n):
            h, step = halves[a], halves[a] // nchs[a]
            for j, chip in enumerate(chips):
                blk = outs[a].at[chip_k[j], pl.ds(c * h, h)]
                _rcopy(blk, blk, send.at[a, j], recv.at[a, j], (*chip, c)).wait_recv()
                for q in range(nchs[a]):
                    part = outs[a].at[chip_k[j], pl.ds(c * h + q * step, step)]
                    _rcopy(part, part, fsend.at[a, j], frecv.at[a, j], sibling).start()
        for a in range(n):
            h = halves[a]
            for j, chip in enumerate(chips):
                blk = outs[a].at[chip_k[j], pl.ds((1 - c) * h, h)]
                _rcopy(blk, blk, fsend.at[a, j], frecv.at[a, j], sibling).wait_recv()
        for a in range(n):
            h = halves[a]
            for j, chip in enumerate(chips):
                _rcopy(ins[a].at[pl.ds(c * h, h)], outs[a].at[k, pl.ds(c * h, h)], send.at[a, j], recv.at[a, j],
                       (*chip, c)).wait_send()
                blk = outs[a].at[chip_k[j], pl.ds(c * h, h)]
                _rcopy(blk, blk, fsend.at[a, j], frecv.at[a, j], sibling).wait_send()
            pltpu.make_async_copy(ins[a], outs[a].at[k], lsem.at[a]).wait()

    return pl.pallas_call(
        body, name=name, in_specs=[HBM] * n, out_specs=[HBM] * n,
        out_shape=[jax.ShapeDtypeStruct((N_CHIPS,) + a.shape, a.dtype) for a in arrs],
        scratch_shapes=[pltpu.SemaphoreType.DMA((n, 3))] * 4 + [pltpu.SemaphoreType.DMA((n,))],
        compiler_params=_comm_params(),
    )(*arrs)


def _pair_exchange(gs, name):
    n = len(gs)

    def body(*refs):
        ins, mine, theirs = refs[:n], refs[n:2 * n], refs[2 * n:3 * n]
        send, recv, lsem = refs[3 * n:]
        x, y, c = _mesh_pos()
        sibling = (x, y, 1 - c)
        for a in range(n):
            h = ins[a].shape[1] // 2
            nch = _nchunks(h, _row_bytes(ins[a]))
            step = h // nch
            for kk in range(N_CHIPS):
                for q in range(nch):
                    dst_rows = pl.ds(q * step, step)
                    pltpu.make_async_copy(ins[a].at[kk, pl.ds(c * h + q * step, step)], mine[a].at[kk, dst_rows],
                                          lsem.at[a]).start()
                    _rcopy(ins[a].at[kk, pl.ds((1 - c) * h + q * step, step)], theirs[a].at[kk, dst_rows],
                           send.at[a], recv.at[a], sibling).start()
        for a in range(n):
            h = ins[a].shape[1] // 2
            _rcopy(ins[a].at[:, pl.ds((1 - c) * h, h)], theirs[a], send.at[a], recv.at[a], sibling).wait()
            pltpu.make_async_copy(ins[a].at[:, pl.ds(c * h, h)], mine[a], lsem.at[a]).wait()

    half = [jax.ShapeDtypeStruct((a.shape[0], a.shape[1] // 2) + a.shape[2:], a.dtype) for a in gs]
    res = pl.pallas_call(
        body, name=name, in_specs=[HBM] * n, out_specs=[HBM] * (2 * n), out_shape=half + half,
        scratch_shapes=[pltpu.SemaphoreType.DMA((n,))] * 3, compiler_params=_comm_params(),
    )(*gs)
    return res[:n], res[n:]


def _chip_exchange(ss, name):
    n = len(ss)

    def body(*refs):
        ins, own = refs[:n], refs[n:2 * n]
        got = [refs[2 * n + 3 * a:2 * n + 3 * a + 3] for a in range(n)]
        send, recv, lsem = refs[5 * n:]
        x, y, c = _mesh_pos()
        k = 2 * x + y
        chips, chip_k = _other_chips(x, y)
        for a in range(n):
            h = ins[a].shape[1]
            nch = _nchunks(h, _row_bytes(ins[a]))
            step = h // nch
            for q in range(nch):
                rows = pl.ds(q * step, step)
                pltpu.make_async_copy(ins[a].at[k, rows], own[a].at[rows], lsem.at[a]).start()
                for j, chip in enumerate(chips):
                    _rcopy(ins[a].at[chip_k[j], rows], got[a][j].at[rows], send.at[a, j], recv.at[a, j],
                           (*chip, c)).start()
        for a in range(n):
            for j, chip in enumerate(chips):
                _rcopy(ins[a].at[chip_k[j]], got[a][j], send.at[a, j], recv.at[a, j], (*chip, c)).wait()
            pltpu.make_async_copy(ins[a].at[k], own[a], lsem.at[a]).wait()

    own = [jax.ShapeDtypeStruct(a.shape[1:], a.dtype) for a in ss]
    got = [jax.ShapeDtypeStruct(a.shape[1:], a.dtype) for a in ss for _ in range(3)]
    res = pl.pallas_call(
        body, name=name, in_specs=[HBM] * n, out_specs=[HBM] * (4 * n), out_shape=own + got,
        scratch_shapes=[pltpu.SemaphoreType.DMA((n, 3))] * 2 + [pltpu.SemaphoreType.DMA((n,))],
        compiler_params=_comm_params(),
    )(*ss)
    return res[:n], [res[n + 3 * a:n + 3 * a + 3] for a in range(n)]


def _pair_share(rs, name):
    n = len(rs)

    def body(*refs):
        ins, outs = refs[:n], refs[n:2 * n]
        send, recv, lsem = refs[2 * n:]
        x, y, c = _mesh_pos()
        sibling = (x, y, 1 - c)
        for a in range(n):
            h = ins[a].shape[0]
            nch = _nchunks(h, _row_bytes(ins[a]))
            step = h // nch
            for q in range(nch):
                src, dst = ins[a].at[pl.ds(q * step, step)], outs[a].at[pl.ds(c * h + q * step, step)]
                pltpu.make_async_copy(src, dst, lsem.at[a]).start()
                _rcopy(src, dst, send.at[a], recv.at[a], sibling).start()
        for a in range(n):
            h = ins[a].shape[0]
            _rcopy(ins[a], outs[a].at[pl.ds(c * h, h)], send.at[a], recv.at[a], sibling).wait_send()
            blk = outs[a].at[pl.ds((1 - c) * h, h)]
            _rcopy(blk, blk, send.at[a], recv.at[a], sibling).wait_recv()
            pltpu.make_async_copy(ins[a], outs[a].at[pl.ds(c * h, h)], lsem.at[a]).wait()

    return pl.pallas_call(
        body, name=name, in_specs=[HBM] * n, out_specs=[HBM] * n,
        out_shape=[jax.ShapeDtypeStruct((2 * a.shape[0],) + a.shape[1:], a.dtype) for a in rs],
        scratch_shapes=[pltpu.SemaphoreType.DMA((n,))] * 3, compiler_params=_comm_params(),
    )(*rs)


def _exchange_all(vec, name):
    def body(v_ref, out_ref, send, recv, lsem):
        x, y, c = _mesh_pos()
        flip = lambda p, f: 1 - p if f else p
        me = 4 * x + 2 * y + c
        rows_all = v_ref.shape[0]
        nch = _nchunks(rows_all, _row_bytes(v_ref))
        step = rows_all // nch
        lc = pltpu.make_async_copy(v_ref, out_ref.at[me], lsem)
        lc.start()
        for j in range(1, N_DEV):
            fx, fy, fc = (j >> 2) & 1, (j >> 1) & 1, j & 1
            for q in range(nch):
                rows = pl.ds(q * step, step)
                _rcopy(v_ref.at[rows], out_ref.at[me, rows], send.at[j - 1], recv.at[j - 1],
                       (flip(x, fx), flip(y, fy), flip(c, fc))).start()
        for j in range(1, N_DEV):
            fx, fy, fc = (j >> 2) & 1, (j >> 1) & 1, j & 1
            slot = out_ref.at[4 * flip(x, fx) + 2 * flip(y, fy) + flip(c, fc)]
            _rcopy(slot, slot, send.at[j - 1], recv.at[j - 1], (x, y, c)).wait_recv()
        for j in range(1, N_DEV):
            _rcopy(v_ref, out_ref.at[me], send.at[j - 1], recv.at[j - 1], (x, y, c)).wait_send()
        lc.wait()

    return pl.pallas_call(
        body, name=name, in_specs=[HBM], out_specs=HBM,
        out_shape=jax.ShapeDtypeStruct((N_DEV,) + vec.shape, vec.dtype),
        scratch_shapes=[pltpu.SemaphoreType.DMA((N_DEV - 1,))] * 2 + [pltpu.SemaphoreType.DMA(())],
        compiler_params=_comm_params(),
    )(vec)


def _rows(a):
    return a.reshape(-1, a.shape[-1])


def _sum_kernel(parts, out_dtype, name):
    def fn(*vals):
        acc = vals[0].astype(F32)
        for v in vals[1:]:
            acc = acc + v.astype(F32)
        return acc
    out = _ew(fn, [(_rows(p), 0) for p in parts], [], [out_dtype], [], name=name)
    return out.reshape(parts[0].shape)


def _reduce_scatter(gs, tag):
    mine, theirs = _pair_exchange(gs, f"{tag}_pair")
    pair = [_sum_kernel([m, t], BF16, f"{tag}_add_pair{a}") for a, (m, t) in enumerate(zip(mine, theirs))]
    own, got = _chip_exchange(pair, f"{tag}_chips")
    red = [_sum_kernel([o, g[0], g[1], g[2]], F32, f"{tag}_add_chips{a}") for a, (o, g) in enumerate(zip(own, got))]
    return _pair_share(red, f"{tag}_share")


def _pack(arrs, row_mult):
    flat = jnp.concatenate([a.reshape(-1).astype(F32) for a in arrs])
    unit = row_mult * LANES
    pad = (-flat.size) % unit
    return jnp.pad(flat, (0, pad)).reshape(-1, LANES)


def _unpack(mat, shapes):
    flat, out, pos = mat.reshape(-1), [], 0
    for s in shapes:
        size = 1
        for d in s:
            size *= d
        out.append(flat[pos:pos + size].reshape(s))
        pos += size
    return out


def _to_shards(a, axis):
    sh = a.shape
    a = a.reshape(sh[:axis] + (N_CHIPS, sh[axis] // N_CHIPS) + sh[axis + 1:])
    return jnp.moveaxis(a, axis, 0)


def _from_shards(g, axis):
    g = jnp.moveaxis(g, 0, axis)
    sh = g.shape
    return g.reshape(sh[:axis] + (sh[axis] * sh[axis + 1],) + sh[axis + 2:])


def _adamw_fn(w, g, m, v):
    m2 = ADAM_B1 * m + (1.0 - ADAM_B1) * g
    v2 = ADAM_B2 * v + (1.0 - ADAM_B2) * (g * g)
    m_hat = m2 / (1.0 - ADAM_B1 ** ADAM_STEP)
    v_hat = v2 / (1.0 - ADAM_B2 ** ADAM_STEP)
    return -ADAM_LR * (m_hat / (jnp.sqrt(v_hat) + ADAM_EPS) + ADAM_WD * w), m2, v2


def _adamw(w, g, m, v, name):
    res = _ew(_adamw_fn, [(_rows(a), 0) for a in (w, g, m, v)], [], [F32, F32, F32], [], name=name)
    return tuple(r.reshape(w.shape) for r in res)


def kernel(x, norm_g, ffn_w_in, ffn_w_out, hyb_w_in, ssm_conv_w, ssm_conv_b, ssm_dt_bias, ssm_a_log, ssm_d, ssm_norm_g, fox_b_f, hyb_w_out, rec_w_in, rec_conv_w, rec_conv_b, rec_w_a, rec_b_a, rec_w_x, rec_b_x, rec_lambda, rec_w_out, loss_target, m_norm_g, m_ffn_w_in, m_ffn_w_out, m_hyb_w_in, m_ssm_conv_w, m_ssm_conv_b, m_ssm_dt_bias, m_ssm_a_log, m_ssm_d, m_ssm_norm_g, m_fox_b_f, m_hyb_w_out, m_rec_w_in, m_rec_conv_w, m_rec_conv_b, m_rec_w_a, m_rec_b_a, m_rec_w_x, m_rec_b_x, m_rec_lambda, m_rec_w_out, v_norm_g, v_ffn_w_in, v_ffn_w_out, v_hyb_w_in, v_ssm_conv_w, v_ssm_conv_b, v_ssm_dt_bias, v_ssm_a_log, v_ssm_d, v_ssm_norm_g, v_fox_b_f, v_hyb_w_out, v_rec_w_in, v_rec_conv_w, v_rec_conv_b, v_rec_w_a, v_rec_b_a, v_rec_w_x, v_rec_b_x, v_rec_lambda, v_rec_w_out):
    given = dict(locals())
    w = {n: given[n] for n in WEIGHTS}
    m = {n: given["m_" + n] for n in WEIGHTS}
    v = {n: given["v_" + n] for n in WEIGHTS}
    k = 2 * lax.axis_index("x") + lax.axis_index("y")

    big_bf16 = [_ew(lambda t: t, [(_rows(w[n]), 0)], [], [BF16], [], name=f"cast_{n}") for n in BIG]
    small_shapes = [w[n].shape for n in SMALL_SHARDED]
    small_pack = _pack([w[n] for n in SMALL_SHARDED], 2 * SUBLANES)
    gathered = _all_gather(big_bf16 + [small_pack], "gather_weights")
    W = {n: w[n] for n in SMALL_REPL}
    for n, g in zip(BIG, gathered[:-1]):
        W[n] = _from_shards(g.reshape((N_CHIPS,) + w[n].shape), SHARD_AXIS[n])
    per_chip = [_unpack(gathered[-1][kk], small_shapes) for kk in range(N_CHIPS)]
    for idx, n in enumerate(SMALL_SHARDED):
        W[n] = jnp.concatenate([per_chip[kk][idx] for kk in range(N_CHIPS)], axis=SHARD_AXIS[n])

    loss_part, dy, grads = _local_step(x[0], loss_target[0], W)
    loss = lax.psum(loss_part, MESH_AXES)

    red_big = _reduce_scatter(
        [_to_shards(grads[n], SHARD_AXIS[n]).reshape(N_CHIPS, -1, w[n].shape[-1]) for n in BIG], "rs")
    g_out = {n: r.reshape(w[n].shape) for n, r in zip(BIG, red_big)}
    small_names = SMALL_SHARDED + SMALL_REPL
    slots = _exchange_all(_pack([grads[n] for n in small_names], SUBLANES), "gather_small_grads")
    small_sum = _sum_kernel([slots[d] for d in range(N_DEV)], F32, "add_small_grads")
    for n, g in zip(small_names, _unpack(small_sum, [grads[n].shape for n in small_names])):
        if n in SHARD_AXIS:
            loc = g.shape[SHARD_AXIS[n]] // N_CHIPS
            g = lax.dynamic_slice_in_dim(g, k * loc, loc, axis=SHARD_AXIS[n])
        g_out[n] = g

    delta, new_m, new_v = {}, {}, {}
    for n in BIG:
        delta[n], new_m[n], new_v[n] = _adamw(w[n], g_out[n], m[n], v[n], f"adamw_{n}")
    shapes = [w[n].shape for n in small_names]
    packed = [_pack([d[n] for n in small_names], SUBLANES) for d in (w, g_out, m, v)]
    for d, mat in zip((delta, new_m, new_v), _adamw(*packed, "adamw_small")):
        d.update(zip(small_names, _unpack(mat, shapes)))

    return (loss, dy[None], *[g_out[n] for n in WEIGHTS], *[delta[n] for n in WEIGHTS],
            *[new_m[n] for n in WEIGHTS], *[new_v[n] for n in WEIGHTS])
```

```python
import functools

import jax
import jax.numpy as jnp
from jax import lax
from jax.experimental import pallas as pl
from jax.experimental.pallas import tpu as pltpu

F32, BF16 = jnp.float32, jnp.bfloat16

NORM_EPS = 1e-6
CONV_K = 4
SSM_HEAD_DIM = 64
SSM_STATE = 128
SSM_GROUPS = 2
FOX_HEAD_DIM = 128
RNN_BLOCK = 128
RG_LRU_C = 8.0
ADAM_LR, ADAM_B1, ADAM_B2, ADAM_EPS, ADAM_WD, ADAM_STEP = 0.001, 0.9, 0.999, 1e-08, 0.01, 10

LANES = 128
SUBLANES = 8
VMEM_LIMIT = 48 * 1024 * 1024
SSD_TILE = 256
FOX_TILE = 512
NEG = -1e30

NT = (((1,), (1,)), ((), ()))
NN = (((1,), (0,)), ((), ()))
TN = (((0,), (0,)), ((), ()))

BIG = ("ffn_w_in", "ffn_w_out", "hyb_w_in", "hyb_w_out", "rec_w_in", "rec_w_out")
SMALL_SHARDED = ("norm_g", "ssm_conv_w", "rec_conv_w", "rec_conv_b", "rec_b_a", "rec_b_x", "rec_lambda")
SMALL_REPL = ("ssm_conv_b", "ssm_dt_bias", "ssm_a_log", "ssm_d", "ssm_norm_g", "fox_b_f", "rec_w_a", "rec_w_x")
WEIGHTS = ("norm_g", "ffn_w_in", "ffn_w_out", "hyb_w_in", "ssm_conv_w", "ssm_conv_b", "ssm_dt_bias", "ssm_a_log",
           "ssm_d", "ssm_norm_g", "fox_b_f", "hyb_w_out", "rec_w_in", "rec_conv_w", "rec_conv_b", "rec_w_a",
           "rec_b_a", "rec_w_x", "rec_b_x", "rec_lambda", "rec_w_out")
SHARD_AXIS = {"norm_g": 2, "ffn_w_in": 3, "ffn_w_out": 2, "hyb_w_in": 2, "ssm_conv_w": 2, "hyb_w_out": 1,
              "rec_w_in": 2, "rec_conv_w": 2, "rec_conv_b": 1, "rec_b_a": 1, "rec_b_x": 1, "rec_lambda": 1,
              "rec_w_out": 1}


def _params(*sem):
    return pltpu.CompilerParams(dimension_semantics=sem if sem else None, vmem_limit_bytes=VMEM_LIMIT)


def _tile(dim, pref):
    if dim <= pref:
        return dim
    for align in (LANES, SUBLANES):
        t = (pref // align) * align
        while t >= align:
            if dim % t == 0:
                return t
            t -= align
    return dim


def _sigmoid(x):
    return 1.0 / (1.0 + jnp.exp(-x))


def _softplus(x):
    return jnp.maximum(x, 0.0) + jnp.log(1.0 + jnp.exp(-jnp.abs(x)))


def _log_sigmoid(x):
    return -_softplus(-x)


def _silu(x):
    return x * _sigmoid(x)


def _gelu_tanh(x):
    return 0.5 * x * (1.0 + jnp.tanh(0.7978845608028654 * (x + 0.044715 * x * x * x)))


def _neg_expm1(x):
    series = -x * (1.0 + x * (0.5 + x * (1.0 / 6.0)))
    return jnp.where(x > -1e-2, series, 1.0 - jnp.exp(x))


def _rms(x, g):
    xf = x.astype(F32)
    return xf * lax.rsqrt(jnp.mean(xf * xf, axis=-1, keepdims=True) + NORM_EPS) * g


def _mm(a, b, mode, out_dtype, name, tm=512, tn=512, tk=512, b_sel=()):
    bshape = b.shape[len(b_sel):]
    if mode == "nn":
        (M, K), (K2, N) = a.shape, bshape
    elif mode == "nt":
        (M, K), (N, K2) = a.shape, bshape
    else:
        (K, M), (K2, N) = a.shape, bshape
    assert K == K2, (a.shape, b.shape, mode)
    tm, tn, tk = _tile(M, tm), _tile(N, tn), _tile(K, tk)
    nk = K // tk
    dims = {"nn": NN, "nt": NT, "tn": TN}[mode]
    lead = (None,) * len(b_sel)
    if mode == "tn":
        a_spec = pl.BlockSpec((tk, tm), lambda n, m, k: (k, m))
    else:
        a_spec = pl.BlockSpec((tm, tk), lambda n, m, k: (m, k))
    if mode == "nt":
        b_spec = pl.BlockSpec(lead + (tn, tk), lambda n, m, k: (*b_sel, n, k))
    else:
        b_spec = pl.BlockSpec(lead + (tk, tn), lambda n, m, k: (*b_sel, k, n))

    def body(a_ref, b_ref, o_ref, *acc):
        p = lax.dot_general(a_ref[...].astype(BF16), b_ref[...].astype(BF16), dims, preferred_element_type=F32)
        if nk == 1:
            o_ref[...] = p.astype(o_ref.dtype)
        else:
            acc_ref, = acc
            k = pl.program_id(2)

            @pl.when(k == 0)
            def _():
                acc_ref[...] = p

            @pl.when(k > 0)
            def _():
                acc_ref[...] += p

            @pl.when(k == nk - 1)
            def _():
                o_ref[...] = acc_ref[...].astype(o_ref.dtype)

    return pl.pallas_call(
        body, name=name, grid=(N // tn, M // tm, nk),
        in_specs=[a_spec, b_spec], out_specs=pl.BlockSpec((tm, tn), lambda n, m, k: (m, n)),
        out_shape=jax.ShapeDtypeStruct((M, N), out_dtype),
        scratch_shapes=[pltpu.VMEM((tm, tn), F32)] if nk > 1 else [],
        compiler_params=_params("parallel", "parallel", "arbitrary"),
    )(a, b)


def _mm_parts(parts, mode, out_dtype, name, tm=512, tn=512):
    M = parts[0][0].shape[0]
    N = parts[0][1].shape[1] if mode == "nn" else parts[0][1].shape[0]
    tm, tn = _tile(M, tm), _tile(N, tn)
    dims = NN if mode == "nn" else NT
    in_specs, args = [], []
    for a, b in parts:
        K = a.shape[1]
        in_specs.append(pl.BlockSpec((tm, K), lambda n, m: (m, 0)))
        if mode == "nn":
            in_specs.append(pl.BlockSpec((K, tn), lambda n, m: (0, n)))
        else:
            in_specs.append(pl.BlockSpec((tn, K), lambda n, m: (n, 0)))
        args += [a, b]

    def body(*refs):
        o_ref = refs[-1]
        acc = None
        for p in range(len(parts)):
            d = lax.dot_general(refs[2 * p][...].astype(BF16), refs[2 * p + 1][...].astype(BF16), dims,
                                preferred_element_type=F32)
            acc = d if acc is None else acc + d
        o_ref[...] = acc.astype(o_ref.dtype)

    return pl.pallas_call(
        body, name=name, grid=(N // tn, M // tm), in_specs=in_specs,
        out_specs=pl.BlockSpec((tm, tn), lambda n, m: (m, n)), out_shape=jax.ShapeDtypeStruct((M, N), out_dtype),
        compiler_params=_params("parallel", "parallel"),
    )(*args)


def _ew(fn, tiled, params, outs, reds, *, name, tm=256, cb=None, ncb=1):
    T = tiled[0][0].shape[0]
    cb = tiled[0][0].shape[1] if cb is None else cb
    tm = _tile(T, tm)
    in_specs, args = [], []
    for arr, off in tiled:
        in_specs.append(pl.BlockSpec((tm, cb), functools.partial(lambda n, i, o: (i, n + o), o=off)))
        args.append(arr)
    for arr, off in params:
        if arr.ndim == 2:
            in_specs.append(pl.BlockSpec((arr.shape[0], cb), functools.partial(lambda n, i, o: (0, n + o), o=off)))
        else:
            in_specs.append(pl.BlockSpec((None,) + arr.shape[1:], lambda n, i: (n, 0, 0)))
        args.append(arr)
    out_shape = [jax.ShapeDtypeStruct((T, cb * ncb), dt) for dt in outs]
    out_specs = [pl.BlockSpec((tm, cb), lambda n, i: (i, n)) for _ in outs]
    for shape in reds:
        out_shape.append(jax.ShapeDtypeStruct(shape, F32))
        if len(shape) == 2:
            out_specs.append(pl.BlockSpec((shape[0], cb), lambda n, i: (0, n)))
        else:
            out_specs.append(pl.BlockSpec((None,) + tuple(shape[1:]), lambda n, i: (n, 0, 0)))
    n_in, n_out = len(args), len(outs)

    def body(*refs):
        i = pl.program_id(1)
        res = fn(*[r[...] for r in refs[:n_in]])
        res = res if isinstance(res, (tuple, list)) else (res,)
        for r, v in zip(refs[n_in:n_in + n_out], res[:n_out]):
            r[...] = v.astype(r.dtype)
        for r, v in zip(refs[n_in + n_out:], res[n_out:]):
            @pl.when(i == 0)
            def _():
                r[...] = jnp.zeros(r.shape, r.dtype)

            r[...] += v.astype(r.dtype).reshape(r.shape)

    res = pl.pallas_call(
        body, name=name, grid=(ncb, T // tm), in_specs=in_specs, out_specs=out_specs, out_shape=out_shape,
        compiler_params=_params("arbitrary", "arbitrary"),
    )(*args)
    return res[0] if len(res) == 1 else tuple(res)


def _rms_fwd(x, g, name):
    return _ew(lambda xv, gv: _rms(xv, gv), [(x, 0)], [(g, 0)], [BF16], [], name=name)


def _rms_bwd_add(x, g, dn, dres, name):
    def fn(xv, dnv, drv, gv):
        _, vjp = jax.vjp(_rms, xv, gv)
        dx, dg = vjp(dnv.astype(F32))
        return drv + dx, dg
    return _ew(fn, [(x, 0), (dn, 0), (dres, 0)], [(g, 0)], [F32], [g.shape], name=name)


def _post_fwd(x, h, g, w, name):
    return _ew(lambda xv, hv, gv: xv + w * _rms(hv, gv), [(x, 0), (h, 0)], [(g, 0)], [F32], [], name=name)


def _post_bwd(dy, h, g, w, name):
    def fn(dyv, hv, gv):
        _, vjp = jax.vjp(lambda a, b: w * _rms(a, b), hv, gv)
        return vjp(dyv)
    return _ew(fn, [(dy, 0), (h, 0)], [(g, 0)], [BF16], [g.shape], name=name)


def _mm_swiglu(x, w_in, sel, name, tm=512):
    T, K = x.shape
    F = w_in.shape[-1] // 2
    tn = F // 2
    tm = _tile(T, tm)
    lead = (None,) * len(sel)

    def body(a_ref, bg_ref, bu_ref, g_ref, u_ref, act_ref):
        a = a_ref[...].astype(BF16)
        g = jnp.dot(a, bg_ref[...].astype(BF16), preferred_element_type=F32)
        u = jnp.dot(a, bu_ref[...].astype(BF16), preferred_element_type=F32)
        g_ref[...] = g.astype(g_ref.dtype)
        u_ref[...] = u.astype(u_ref.dtype)
        act_ref[...] = (_silu(g) * u).astype(act_ref.dtype)

    out = jax.ShapeDtypeStruct((T, F), BF16)
    o_spec = pl.BlockSpec((tm, tn), lambda n, m: (m, n))
    return pl.pallas_call(
        body, name=name, grid=(2, T // tm),
        in_specs=[pl.BlockSpec((tm, K), lambda n, m: (m, 0)),
                  pl.BlockSpec(lead + (K, tn), lambda n, m: (*sel, 0, n)),
                  pl.BlockSpec(lead + (K, tn), lambda n, m: (*sel, 0, n + 2))],
        out_specs=[o_spec] * 3, out_shape=[out] * 3, compiler_params=_params("parallel", "parallel"),
    )(x, w_in, w_in)


def _swiglu_bwd(gate, up, da, name):
    T, F = gate.shape
    tm = _tile(T, 256)

    def body(g_ref, u_ref, da_ref, o_ref):
        g, u, d = g_ref[...].astype(F32), u_ref[...].astype(F32), da_ref[...].astype(F32)
        s = _sigmoid(g)
        o_ref[:, :F] = (d * u * (s * (1.0 + g * (1.0 - s)))).astype(o_ref.dtype)
        o_ref[:, F:] = (d * g * s).astype(o_ref.dtype)

    spec = pl.BlockSpec((tm, F), lambda i: (i, 0))
    return pl.pallas_call(
        body, name=name, grid=(T // tm,), in_specs=[spec] * 3, out_specs=pl.BlockSpec((tm, 2 * F), lambda i: (i, 0)),
        out_shape=jax.ShapeDtypeStruct((T, 2 * F), BF16), compiler_params=_params("parallel"),
    )(gate, up, da)


def _ffn_fwd(x, g_pre, g_post, w_in, w_out, sel, tag):
    n = _rms_fwd(x, g_pre, f"{tag}_rms")
    gate, up, a = _mm_swiglu(n, w_in, sel, f"{tag}_mm_in")
    h = _mm(a, w_out, "nn", F32, f"{tag}_mm_out", tn=1024, tk=1408, b_sel=sel)
    return _post_fwd(x, h, g_post, 0.5, f"{tag}_post"), (x, n, gate, up, a, h)


def _ffn_bwd(dy, saved, g_pre, g_post, w_in, w_out, sel, tag):
    x, n, gate, up, a, h = saved
    dh, dg_post = _post_bwd(dy, h, g_post, 0.5, f"{tag}_post_b")
    da = _mm(dh, w_out, "nt", BF16, f"{tag}_mm_da", tn=1408, tk=1024, b_sel=sel)
    dw_out = _mm(a, dh, "tn", BF16, f"{tag}_mm_dwout", tm=1408, tn=1024, tk=1024)
    dgu = _swiglu_bwd(gate, up, da, f"{tag}_swiglu_b")
    dn = _mm(dgu, w_in, "nt", F32, f"{tag}_mm_dn", tn=1024, tk=2816, b_sel=sel)
    dw_in = _mm(n, dgu, "tn", BF16, f"{tag}_mm_dwin", tm=1024, tn=1408, tk=1024)
    dy2, dg_pre = _rms_bwd_add(x, g_pre, dn, dy, f"{tag}_rms_b")
    return dy2, dg_pre, dg_post, dw_in, dw_out


def _shift_down(x, s):
    if s == 0:
        return x
    rows = lax.broadcasted_iota(jnp.int32, x.shape, 0)
    return jnp.where(rows >= s, pltpu.roll(x, s, 0), 0.0)


def _shift_up(x, s):
    if s == 0:
        return x
    T = x.shape[0]
    rows = lax.broadcasted_iota(jnp.int32, x.shape, 0)
    return jnp.where(rows < T - s, pltpu.roll(x, T - s, 0), 0.0)


def _conv_pre(x, w, b):
    y = b
    for k in range(CONV_K):
        y = y + w[k:k + 1, :] * _shift_down(x, CONV_K - 1 - k)
    return y


def _conv_fwd(x, col0, C, w, b, act, name, ct=256):
    T = x.shape[0]
    off = col0 // ct

    def body(x_ref, w_ref, b_ref, o_ref):
        y = _conv_pre(x_ref[...], w_ref[...], b_ref[...])
        o_ref[...] = _silu(y) if act else y

    return pl.pallas_call(
        body, name=name, grid=(C // ct,),
        in_specs=[pl.BlockSpec((T, ct), lambda j: (0, j + off)), pl.BlockSpec((CONV_K, ct), lambda j: (0, j)),
                  pl.BlockSpec((1, ct), lambda j: (0, j))],
        out_specs=pl.BlockSpec((T, ct), lambda j: (0, j)), out_shape=jax.ShapeDtypeStruct((T, C), F32),
        compiler_params=_params("parallel"),
    )(x, w, b)


def _conv_bwd(x, col0, C, w, b, dyact, act, name, ct=256):
    T = x.shape[0]
    off = col0 // ct

    def body(x_ref, w_ref, b_ref, dy_ref, dx_ref, dw_ref, db_ref):
        xv, wv = x_ref[...], w_ref[...]
        dy = dy_ref[...].astype(F32)
        if act:
            pre = _conv_pre(xv, wv, b_ref[...])
            s = _sigmoid(pre)
            dy = dy * (s * (1.0 + pre * (1.0 - s)))
        dx = jnp.zeros_like(dy)
        dws = []
        for k in range(CONV_K):
            dx = dx + wv[k:k + 1, :] * _shift_up(dy, CONV_K - 1 - k)
            dws.append(jnp.sum(dy * _shift_down(xv, CONV_K - 1 - k), axis=0, keepdims=True))
        dx_ref[...] = dx.astype(dx_ref.dtype)
        dw_ref[...] = jnp.concatenate(dws, axis=0)
        db_ref[...] = jnp.sum(dy, axis=0, keepdims=True)

    return pl.pallas_call(
        body, name=name, grid=(C // ct,),
        in_specs=[pl.BlockSpec((T, ct), lambda j: (0, j + off)), pl.BlockSpec((CONV_K, ct), lambda j: (0, j)),
                  pl.BlockSpec((1, ct), lambda j: (0, j)), pl.BlockSpec((T, ct), lambda j: (0, j))],
        out_specs=[pl.BlockSpec((T, ct), lambda j: (0, j)), pl.BlockSpec((CONV_K, ct), lambda j: (0, j)),
                   pl.BlockSpec((1, ct), lambda j: (0, j))],
        out_shape=[jax.ShapeDtypeStruct((T, C), BF16), jax.ShapeDtypeStruct((CONV_K, C), F32),
                   jax.ShapeDtypeStruct((1, C), F32)],
        compiler_params=_params("parallel"),
    )(x, w, b, dyact)


def _gate_act(v, n_ssm):
    lane = lax.broadcasted_iota(jnp.int32, v.shape, 1)
    return jnp.where(lane < n_ssm, _softplus(v), _log_sigmoid(v))


def _gates_fwd(proj, col0, bias, mult, n_ssm, name, tb=512):
    T = proj.shape[0]
    tb = _tile(T, tb)
    off = col0 // LANES

    def body(s_ref, bias_ref, mult_ref, act_ref, cs_ref, carry_ref):
        i = pl.program_id(0)

        @pl.when(i == 0)
        def _():
            carry_ref[...] = jnp.zeros_like(carry_ref)

        act = _gate_act(s_ref[...] + bias_ref[...], n_ssm)
        inc = act * mult_ref[...]
        r = lax.broadcasted_iota(jnp.int32, (tb, tb), 0)
        c = lax.broadcasted_iota(jnp.int32, (tb, tb), 1)
        tri = jnp.where(r >= c, 1.0, 0.0).astype(F32)
        cs = jnp.dot(tri, inc, precision=lax.Precision.HIGHEST, preferred_element_type=F32) + carry_ref[...]
        act_ref[...] = act
        cs_ref[...] = cs
        carry_ref[...] = cs[tb - 1:tb, :]

    return pl.pallas_call(
        body, name=name, grid=(T // tb,),
        in_specs=[pl.BlockSpec((tb, LANES), lambda i: (i, off)), pl.BlockSpec((1, LANES), lambda i: (0, 0)),
                  pl.BlockSpec((1, LANES), lambda i: (0, 0))],
        out_specs=[pl.BlockSpec((tb, LANES), lambda i: (i, 0))] * 2,
        out_shape=[jax.ShapeDtypeStruct((T, LANES), F32)] * 2,
        scratch_shapes=[pltpu.VMEM((1, LANES), F32)], compiler_params=_params("arbitrary"),
    )(proj, bias, mult)


def _gates_bwd(proj, col0, bias, mult, n_ssm, dact, dcs, name, tb=512):
    T = proj.shape[0]
    tb = _tile(T, tb)
    nb = T // tb
    off = col0 // LANES

    def body(s_ref, bias_ref, mult_ref, dact_ref, dcs_ref, ds_ref, dmult_ref, dbias_ref, carry_ref):
        i = pl.program_id(0)

        @pl.when(i == 0)
        def _():
            carry_ref[...] = jnp.zeros_like(carry_ref)
            dmult_ref[...] = jnp.zeros_like(dmult_ref)
            dbias_ref[...] = jnp.zeros_like(dbias_ref)

        v = s_ref[...] + bias_ref[...]
        act = _gate_act(v, n_ssm)
        r = lax.broadcasted_iota(jnp.int32, (tb, tb), 0)
        c = lax.broadcasted_iota(jnp.int32, (tb, tb), 1)
        tri = jnp.where(r <= c, 1.0, 0.0).astype(F32)
        dinc = jnp.dot(tri, dcs_ref[...], precision=lax.Precision.HIGHEST, preferred_element_type=F32) + carry_ref[...]
        carry_ref[...] = dinc[0:1, :]
        da = dact_ref[...] + dinc * mult_ref[...]
        sg = _sigmoid(v)
        lane = lax.broadcasted_iota(jnp.int32, v.shape, 1)
        dv = da * jnp.where(lane < n_ssm, sg, 1.0 - sg)
        ds_ref[...] = dv.astype(ds_ref.dtype)
        dmult_ref[...] += jnp.sum(dinc * act, axis=0, keepdims=True)
        dbias_ref[...] += jnp.sum(dv, axis=0, keepdims=True)

    rev = lambda i: (nb - 1 - i, 0)
    return pl.pallas_call(
        body, name=name, grid=(nb,),
        in_specs=[pl.BlockSpec((tb, LANES), lambda i: (nb - 1 - i, off)), pl.BlockSpec((1, LANES), lambda i: (0, 0)),
                  pl.BlockSpec((1, LANES), lambda i: (0, 0)), pl.BlockSpec((tb, LANES), rev),
                  pl.BlockSpec((tb, LANES), rev)],
        out_specs=[pl.BlockSpec((tb, LANES), rev), pl.BlockSpec((1, LANES), lambda i: (0, 0)),
                   pl.BlockSpec((1, LANES), lambda i: (0, 0))],
        out_shape=[jax.ShapeDtypeStruct((T, LANES), BF16), jax.ShapeDtypeStruct((1, LANES), F32),
                   jax.ShapeDtypeStruct((1, LANES), F32)],
        scratch_shapes=[pltpu.VMEM((1, LANES), F32)], compiler_params=_params("arbitrary"),
    )(proj, bias, mult, dact, dcs)


def _rep_layout(v):
    return jnp.repeat(v, LANES, axis=1)


def _row_layout(v, tk):
    T, H = v.shape
    return v.T.reshape(H, T // tk, 1, tk)


def _causal(tq):
    r = lax.broadcasted_iota(jnp.int32, (tq, tq), 0)
    c = lax.broadcasted_iota(jnp.int32, (tq, tq), 1)
    return r >= c


def _ssd_fwd(xbc, X, cs_rep, cs_row, name, tq=SSD_TILE):
    T = X.shape[0]
    tq = _tile(T, tq)
    nq = T // tq
    d_ssm = X.shape[1]
    gw = d_ssm // SSM_GROUPS
    hpg = gw // SSM_HEAD_DIM
    b_off = d_ssm // SSM_STATE
    c_off = b_off + SSM_GROUPS

    def body(c_ref, b_ref, x_ref, csq_ref, csk_ref, y_ref):
        i = pl.program_id(1)
        c = c_ref[...].astype(BF16)
        half = lax.broadcasted_iota(jnp.int32, (tq, LANES), 1) // SSM_HEAD_DIM
        mask = _causal(tq)

        def step(j, acc, masked):
            r0 = pl.multiple_of(j * tq, tq)
            s = lax.dot_general(c, b_ref[pl.ds(r0, tq), :].astype(BF16), NT, preferred_element_type=F32)
            out = []
            for p in range(hpg // 2):
                xp = x_ref[pl.ds(r0, tq), p * LANES:(p + 1) * LANES]
                a = acc[p]
                for e in range(2):
                    h = 2 * p + e
                    diff = jnp.tile(csq_ref[:, h * LANES:(h + 1) * LANES], (1, tq // LANES)) - csk_ref[h, j]
                    if masked:
                        diff = jnp.where(mask, diff, NEG)
                    pm = (s * jnp.exp(diff)).astype(BF16)
                    xm = jnp.where(half == e, xp, jnp.zeros_like(xp))
                    a = a + jnp.dot(pm, xm, preferred_element_type=F32)
                out.append(a)
            return tuple(out)

        acc = tuple(jnp.zeros((tq, LANES), F32) for _ in range(hpg // 2))
        acc = lax.fori_loop(0, i, lambda j, a: step(j, a, False), acc)
        acc = step(i, acc, True)
        y_ref[...] = jnp.concatenate(acc, axis=1)

    return pl.pallas_call(
        body, name=name, grid=(SSM_GROUPS, nq),
        in_specs=[pl.BlockSpec((tq, SSM_STATE), lambda g, i: (i, c_off + g)),
                  pl.BlockSpec((T, SSM_STATE), lambda g, i: (0, b_off + g)),
                  pl.BlockSpec((T, gw), lambda g, i: (0, g)),
                  pl.BlockSpec((tq, hpg * LANES), lambda g, i: (i, g)),
                  pl.BlockSpec((hpg, nq, 1, tq), lambda g, i: (g, 0, 0, 0))],
        out_specs=pl.BlockSpec((tq, gw), lambda g, i: (i, g)),
        out_shape=jax.ShapeDtypeStruct((T, d_ssm), F32), compiler_params=_params("parallel", "arbitrary"),
    )(xbc, xbc, X, cs_rep, cs_row)


def _ssd_bwd(xbc, X, cs_rep, cs_row, dY, name, tq=SSD_TILE):
    T = X.shape[0]
    tq = _tile(T, tq)
    nq = T // tq
    d_ssm = X.shape[1]
    gw = d_ssm // SSM_GROUPS
    hpg = gw // SSM_HEAD_DIM
    nheads = d_ssm // SSM_HEAD_DIM
    b_off = d_ssm // SSM_STATE
    c_off = b_off + SSM_GROUPS

    def body(c_ref, b_ref, x_ref, dy_ref, csq_ref, csk_ref, dc_ref, db_ref, dx_ref, dcsq_ref, dcsk_ref):
        i = pl.program_id(1)

        @pl.when(i == 0)
        def _():
            db_ref[...] = jnp.zeros_like(db_ref)
            dx_ref[...] = jnp.zeros_like(dx_ref)
            dcsk_ref[...] = jnp.zeros_like(dcsk_ref)

        c = c_ref[...].astype(BF16)
        half = lax.broadcasted_iota(jnp.int32, (tq, LANES), 1) // SSM_HEAD_DIM
        mask = _causal(tq)

        def step(j, carry, masked):
            dc_acc, rows = carry
            rows = list(rows)
            r0 = pl.multiple_of(j * tq, tq)
            b = b_ref[pl.ds(r0, tq), :].astype(BF16)
            s = lax.dot_general(c, b, NT, preferred_element_type=F32)
            ds_tot = jnp.zeros((tq, tq), F32)
            for p in range(hpg // 2):
                cols = slice(p * LANES, (p + 1) * LANES)
                xp = x_ref[pl.ds(r0, tq), cols]
                dyp = dy_ref[:, cols]
                dx_p = jnp.zeros((tq, LANES), F32)
                for e in range(2):
                    h = 2 * p + e
                    diff = jnp.tile(csq_ref[:, h * LANES:(h + 1) * LANES], (1, tq // LANES)) - csk_ref[h, j]
                    if masked:
                        diff = jnp.where(mask, diff, NEG)
                    decay = jnp.exp(diff)
                    dym = jnp.where(half == e, dyp, jnp.zeros_like(dyp))
                    g = lax.dot_general(dym, xp, NT, preferred_element_type=F32) * decay
                    ds_tot = ds_tot + g
                    m = g * s
                    rows[h] = rows[h] + jnp.sum(m, axis=1, keepdims=True)
                    dcsk_ref[h, j] -= jnp.sum(m, axis=0, keepdims=True)
                    pm = (s * decay).astype(BF16)
                    dx_p = dx_p + lax.dot_general(pm, dym, TN, preferred_element_type=F32)
                dx_ref[pl.ds(r0, tq), cols] += dx_p
            dsb = ds_tot.astype(BF16)
            dc_acc = dc_acc + jnp.dot(dsb, b, preferred_element_type=F32)
            db_ref[pl.ds(r0, tq), :] += lax.dot_general(dsb, c, TN, preferred_element_type=F32)
            return dc_acc, tuple(rows)

        carry = (jnp.zeros((tq, SSM_STATE), F32), tuple(jnp.zeros((tq, 1), F32) for _ in range(hpg)))
        carry = lax.fori_loop(0, i, lambda j, cr: step(j, cr, False), carry)
        dc_acc, rows = step(i, carry, True)
        dc_ref[...] = dc_acc
        dcsq_ref[...] = jnp.concatenate([jnp.broadcast_to(r, (tq, LANES)) for r in rows], axis=1)

    return pl.pallas_call(
        body, name=name, grid=(SSM_GROUPS, nq),
        in_specs=[pl.BlockSpec((tq, SSM_STATE), lambda g, i: (i, c_off + g)),
                  pl.BlockSpec((T, SSM_STATE), lambda g, i: (0, b_off + g)),
                  pl.BlockSpec((T, gw), lambda g, i: (0, g)),
                  pl.BlockSpec((tq, gw), lambda g, i: (i, g)),
                  pl.BlockSpec((tq, hpg * LANES), lambda g, i: (i, g)),
                  pl.BlockSpec((hpg, nq, 1, tq), lambda g, i: (g, 0, 0, 0))],
        out_specs=[pl.BlockSpec((tq, SSM_STATE), lambda g, i: (i, g)),
                   pl.BlockSpec((T, SSM_STATE), lambda g, i: (0, g)),
                   pl.BlockSpec((T, gw), lambda g, i: (0, g)),
                   pl.BlockSpec((tq, hpg * LANES), lambda g, i: (i, g)),
                   pl.BlockSpec((hpg, nq, 1, tq), lambda g, i: (g, 0, 0, 0))],
        out_shape=[jax.ShapeDtypeStruct((T, SSM_GROUPS * SSM_STATE), F32),
                   jax.ShapeDtypeStruct((T, SSM_GROUPS * SSM_STATE), F32),
                   jax.ShapeDtypeStruct((T, d_ssm), F32),
                   jax.ShapeDtypeStruct((T, nheads * LANES), F32),
                   jax.ShapeDtypeStruct((nheads, nq, 1, tq), F32)],
        compiler_params=_params("arbitrary", "arbitrary"),
    )(xbc, xbc, X, dY, cs_rep, cs_row)


def _fox_fwd(proj, q0, k0, v0, nh, cum_rep, cum_row, name, tq=FOX_TILE):
    T = proj.shape[0]
    tq = _tile(T, tq)
    nq = T // tq
    hd = FOX_HEAD_DIM
    scale = hd ** -0.5
    qo, ko, vo = q0 // hd, k0 // hd, v0 // hd

    def body(q_ref, k_ref, v_ref, cq_ref, ck_ref, o_ref, lse_ref):
        i = pl.program_id(1)
        q = q_ref[...].astype(BF16)
        cq = jnp.tile(cq_ref[...], (1, tq // LANES))
        mask = _causal(tq)

        def step(j, carry, masked):
            m, l, acc = carry
            r0 = pl.multiple_of(j * tq, tq)
            k = k_ref[pl.ds(r0, tq), :].astype(BF16)
            v = v_ref[pl.ds(r0, tq), :].astype(BF16)
            s = lax.dot_general(q, k, NT, preferred_element_type=F32) * scale + cq - ck_ref[j]
            if masked:
                s = jnp.where(mask, s, NEG)
            m_new = jnp.maximum(m, jnp.max(s, axis=1, keepdims=True))
            alpha = jnp.exp(m - m_new)
            p = jnp.exp(s - m_new)
            l = alpha * l + jnp.sum(p, axis=1, keepdims=True)
            acc = alpha * acc + jnp.dot(p.astype(BF16), v, preferred_element_type=F32)
            return m_new, l, acc

        carry = (jnp.full((tq, 1), NEG, F32), jnp.zeros((tq, 1), F32), jnp.zeros((tq, hd), F32))
        carry = lax.fori_loop(0, i, lambda j, cr: step(j, cr, False), carry)
        m, l, acc = step(i, carry, True)
        o_ref[...] = (acc / l).astype(o_ref.dtype)
        lse_ref[...] = jnp.broadcast_to(m + jnp.log(l), (tq, LANES))

    return pl.pallas_call(
        body, name=name, grid=(nh, nq),
        in_specs=[pl.BlockSpec((tq, hd), lambda h, i: (i, qo + h)), pl.BlockSpec((T, hd), lambda h, i: (0, ko + h)),
                  pl.BlockSpec((T, hd), lambda h, i: (0, vo + h)), pl.BlockSpec((tq, LANES), lambda h, i: (i, h)),
                  pl.BlockSpec((None, nq, 1, tq), lambda h, i: (h, 0, 0, 0))],
        out_specs=[pl.BlockSpec((tq, hd), lambda h, i: (i, h)), pl.BlockSpec((tq, LANES), lambda h, i: (i, h))],
        out_shape=[jax.ShapeDtypeStruct((T, nh * hd), BF16), jax.ShapeDtypeStruct((T, nh * LANES), F32)],
        compiler_params=_params("parallel", "arbitrary"),
    )(proj, proj, proj, cum_rep, cum_row)


def _fox_bwd(proj, q0, k0, v0, nh, cum_rep, cum_row, o, lse, dcat, do0, name, tq=FOX_TILE):
    T = proj.shape[0]
    tq = _tile(T, tq)
    nq = T // tq
    hd = FOX_HEAD_DIM
    scale = hd ** -0.5
    qo, ko, vo, doo = q0 // hd, k0 // hd, v0 // hd, do0 // hd

    def body(q_ref, k_ref, v_ref, do_ref, o_ref, lse_ref, cq_ref, ck_ref, dq_ref, dk_ref, dv_ref, dck_ref, dcq_ref):
        i = pl.program_id(1)

        @pl.when(i == 0)
        def _():
            dk_ref[...] = jnp.zeros_like(dk_ref)
            dv_ref[...] = jnp.zeros_like(dv_ref)
            dck_ref[...] = jnp.zeros_like(dck_ref)

        q = q_ref[...].astype(BF16)
        do = do_ref[...].astype(F32)
        dob = do.astype(BF16)
        delta = jnp.sum(do * o_ref[...].astype(F32), axis=1, keepdims=True)
        bias = jnp.tile(cq_ref[...] - lse_ref[...], (1, tq // LANES))
        mask = _causal(tq)

        def step(j, carry, masked):
            dq, rows = carry
            r0 = pl.multiple_of(j * tq, tq)
            k = k_ref[pl.ds(r0, tq), :].astype(BF16)
            v = v_ref[pl.ds(r0, tq), :].astype(BF16)
            s = lax.dot_general(q, k, NT, preferred_element_type=F32) * scale + bias - ck_ref[j]
            if masked:
                s = jnp.where(mask, s, NEG)
            p = jnp.exp(s)
            dp = lax.dot_general(dob, v, NT, preferred_element_type=F32)
            ds = p * (dp - delta)
            dsb = ds.astype(BF16)
            dq = dq + jnp.dot(dsb, k, preferred_element_type=F32) * scale
            dk_ref[pl.ds(r0, tq), :] += lax.dot_general(dsb, q, TN, preferred_element_type=F32) * scale
            dv_ref[pl.ds(r0, tq), :] += lax.dot_general(p.astype(BF16), dob, TN, preferred_element_type=F32)
            dck_ref[j] -= jnp.sum(ds, axis=0, keepdims=True)
            return dq, rows + jnp.sum(ds, axis=1, keepdims=True)

        carry = (jnp.zeros((tq, hd), F32), jnp.zeros((tq, 1), F32))
        carry = lax.fori_loop(0, i, lambda j, cr: step(j, cr, False), carry)
        dq, rows = step(i, carry, True)
        dq_ref[...] = dq.astype(dq_ref.dtype)
        dcq_ref[...] = jnp.broadcast_to(rows, (tq, LANES))

    return pl.pallas_call(
        body, name=name, grid=(nh, nq),
        in_specs=[pl.BlockSpec((tq, hd), lambda h, i: (i, qo + h)), pl.BlockSpec((T, hd), lambda h, i: (0, ko + h)),
                  pl.BlockSpec((T, hd), lambda h, i: (0, vo + h)), pl.BlockSpec((tq, hd), lambda h, i: (i, doo + h)),
                  pl.BlockSpec((tq, hd), lambda h, i: (i, h)), pl.BlockSpec((tq, LANES), lambda h, i: (i, h)),
                  pl.BlockSpec((tq, LANES), lambda h, i: (i, h)),
                  pl.BlockSpec((None, nq, 1, tq), lambda h, i: (h, 0, 0, 0))],
        out_specs=[pl.BlockSpec((tq, hd), lambda h, i: (i, h)), pl.BlockSpec((T, hd), lambda h, i: (0, h)),
                   pl.BlockSpec((T, hd), lambda h, i: (0, h)), pl.BlockSpec((None, nq, 1, tq), lambda h, i: (h, 0, 0, 0)),
                   pl.BlockSpec((tq, LANES), lambda h, i: (i, h))],
        out_shape=[jax.ShapeDtypeStruct((T, nh * hd), BF16), jax.ShapeDtypeStruct((T, nh * hd), F32),
                   jax.ShapeDtypeStruct((T, nh * hd), F32), jax.ShapeDtypeStruct((nh, nq, 1, tq), F32),
                   jax.ShapeDtypeStruct((T, nh * LANES), F32)],
        compiler_params=_params("arbitrary", "arbitrary"),
    )(proj, proj, proj, dcat, o, lse, cum_rep, cum_row)


def _scan_fwd(a, u, name, ct=256):
    T, C = a.shape

    def body(a_ref, u_ref, h_ref):
        def blk(tb, h):
            r0 = pl.multiple_of(tb * SUBLANES, SUBLANES)
            ab, ub = a_ref[pl.ds(r0, SUBLANES), :], u_ref[pl.ds(r0, SUBLANES), :]
            rows = []
            for r in range(SUBLANES):
                h = ab[r:r + 1, :] * h + ub[r:r + 1, :]
                rows.append(h)
            h_ref[pl.ds(r0, SUBLANES), :] = jnp.concatenate(rows, axis=0)
            return h

        lax.fori_loop(0, T // SUBLANES, blk, jnp.zeros((1, ct), F32))

    spec = pl.BlockSpec((T, ct), lambda j: (0, j))
    return pl.pallas_call(body, name=name, grid=(C // ct,), in_specs=[spec, spec], out_specs=spec,
                          out_shape=jax.ShapeDtypeStruct((T, C), F32), compiler_params=_params("parallel"))(a, u)


def _scan_bwd(a, dh, h, name, ct=128):
    T, C = a.shape
    nb = T // SUBLANES

    def body(a_ref, dh_ref, h_ref, g_ref, da_ref):
        def blk(t, carry):
            r0 = pl.multiple_of((nb - 1 - t) * SUBLANES, SUBLANES)
            ab, db = a_ref[pl.ds(r0, SUBLANES), :], dh_ref[pl.ds(r0, SUBLANES), :]
            rows = [None] * SUBLANES
            for r in range(SUBLANES - 1, -1, -1):
                g = db[r:r + 1, :] + carry
                carry = ab[r:r + 1, :] * g
                rows[r] = g
            g_ref[pl.ds(r0, SUBLANES), :] = jnp.concatenate(rows, axis=0)
            return carry

        lax.fori_loop(0, nb, blk, jnp.zeros((1, ct), F32))
        da_ref[...] = g_ref[...] * _shift_down(h_ref[...], 1)

    spec = pl.BlockSpec((T, ct), lambda j: (0, j))
    return pl.pallas_call(body, name=name, grid=(C // ct,), in_specs=[spec] * 3, out_specs=[spec] * 2,
                          out_shape=[jax.ShapeDtypeStruct((T, C), F32)] * 2,
                          compiler_params=_params("parallel"))(a, dh, h)


def _lru_elem(xc, ra, ia, lam):
    r, i = _sigmoid(ra), _sigmoid(ia)
    log_a = RG_LRU_C * r * _log_sigmoid(lam)
    return jnp.exp(log_a), jnp.sqrt(_neg_expm1(2.0 * log_a)) * (i * xc)


def _lru_gates_fwd(xc, w_a, b_a, w_x, b_x, lam, name):
    def fn(xv, ba, bx, lm, wa, wx):
        xb = xv.astype(BF16)
        ra = jnp.dot(xb, wa.astype(BF16), preferred_element_type=F32) + ba
        ia = jnp.dot(xb, wx.astype(BF16), preferred_element_type=F32) + bx
        return _lru_elem(xv, ra, ia, lm)
    nb = xc.shape[1] // RNN_BLOCK
    return _ew(fn, [(xc, 0)], [(b_a, 0), (b_x, 0), (lam, 0), (w_a, 0), (w_x, 0)], [F32, F32], [],
               name=name, tm=512, cb=RNN_BLOCK, ncb=nb)


def _lru_gates_bwd(xc, w_a, b_a, w_x, b_x, lam, da, du, name):
    def fn(xv, dav, duv, ba, bx, lm, wa, wx):
        xb, wab, wxb = xv.astype(BF16), wa.astype(BF16), wx.astype(BF16)
        ra = jnp.dot(xb, wab, preferred_element_type=F32) + ba
        ia = jnp.dot(xb, wxb, preferred_element_type=F32) + bx
        _, vjp = jax.vjp(_lru_elem, xv, ra, ia, lm)
        dx, dra, dia, dlm = vjp((dav, duv))
        drb, dib = dra.astype(BF16), dia.astype(BF16)
        dx = dx + lax.dot_general(drb, wab, NT, preferred_element_type=F32)
        dx = dx + lax.dot_general(dib, wxb, NT, preferred_element_type=F32)
        dwa = lax.dot_general(xb, drb, TN, preferred_element_type=F32)
        dwx = lax.dot_general(xb, dib, TN, preferred_element_type=F32)
        return (dx, jnp.sum(dra, axis=0, keepdims=True), jnp.sum(dia, axis=0, keepdims=True), dlm, dwa, dwx)
    nb = xc.shape[1] // RNN_BLOCK
    return _ew(fn, [(xc, 0), (da, 0), (du, 0)], [(b_a, 0), (b_x, 0), (lam, 0), (w_a, 0), (w_x, 0)], [F32],
               [b_a.shape, b_x.shape, lam.shape, w_a.shape, w_x.shape], name=name, tm=512, cb=RNN_BLOCK, ncb=nb)


def _hyb_cols(D):
    conv = D + 2 * SSM_GROUPS * SSM_STATE
    z0, x0, q0 = 0, D, D + conv
    return dict(z=z0, xbc=x0, q=q0, k=q0 + D, v=q0 + 2 * D, small=q0 + 3 * D, total=q0 + 3 * D + LANES, conv=conv)


def _hyb_w_in_reorder(w, D):
    cols = _hyb_cols(D)
    nh_s, nh_f = D // SSM_HEAD_DIM, D // FOX_HEAD_DIM
    a = D + cols["conv"]
    pad = jnp.zeros((w.shape[0], LANES - nh_s - nh_f), w.dtype)
    return jnp.concatenate([w[:, :a], w[:, a + nh_s:a + nh_s + 3 * D], w[:, a:a + nh_s], w[:, a + nh_s + 3 * D:], pad], axis=1)


def _hyb_w_in_restore(dw, D):
    cols = _hyb_cols(D)
    nh_s, nh_f = D // SSM_HEAD_DIM, D // FOX_HEAD_DIM
    a = D + cols["conv"]
    s = cols["small"]
    return jnp.concatenate([dw[:, :a], dw[:, s:s + nh_s], dw[:, a:s], dw[:, s + nh_s:s + nh_s + nh_f]], axis=1)


def _ssm_out(Y, xs, z, dfull, ng):
    y = (Y + dfull * xs) * _silu(z)
    return y * lax.rsqrt(jnp.mean(y * y, axis=-1, keepdims=True) + NORM_EPS) * ng


def _hyb_fwd(x, g_pre, g_post, p, tag):
    T, D = x.shape
    cols = _hyb_cols(D)
    nh_s, nh_f = D // SSM_HEAD_DIM, D // FOX_HEAD_DIM
    n = _rms_fwd(x, g_pre, f"{tag}_rms")
    proj = _mm(n, p["w_in"], "nn", F32, f"{tag}_mm_in", tn=1152, tk=1024)
    a_neg = -jnp.exp(p["a_log"])
    bias = jnp.concatenate([p["dt_bias"], p["b_f"], jnp.zeros((LANES - nh_s - nh_f,), F32)])[None]
    mult = jnp.concatenate([a_neg, jnp.ones((nh_f,), F32), jnp.zeros((LANES - nh_s - nh_f,), F32)])[None]
    act, cs = _gates_fwd(proj, cols["small"], bias, mult, nh_s, f"{tag}_gates")
    dt, cs_s, cum = act[:, :nh_s], cs[:, :nh_s], cs[:, nh_s:nh_s + nh_f]
    dtf = jnp.repeat(dt, SSM_HEAD_DIM, axis=1)
    cs_rep, cs_row = _rep_layout(cs_s), _row_layout(cs_s, _tile(T, SSD_TILE))
    cum_rep, cum_row = _rep_layout(cum), _row_layout(cum, _tile(T, FOX_TILE))
    xbc = _conv_fwd(proj, cols["xbc"], cols["conv"], p["conv_w"], p["conv_b"], True, f"{tag}_conv")
    X = _ew(lambda xv, dv: xv * dv, [(xbc, 0), (dtf, 0)], [], [BF16], [], name=f"{tag}_xdt", cb=512, ncb=D // 512)
    Y = _ssd_fwd(xbc, X, cs_rep, cs_row, f"{tag}_ssd")
    dfull = jnp.repeat(p["d"], SSM_HEAD_DIM)[None]
    gw = D // SSM_GROUPS
    y_ssm = _ew(_ssm_out, [(Y, 0), (xbc, 0), (proj, cols["z"] // gw)], [(dfull, 0), (p["norm_g"], 0)], [BF16], [],
                name=f"{tag}_ssm_out", cb=gw, ncb=SSM_GROUPS)
    o, lse = _fox_fwd(proj, cols["q"], cols["k"], cols["v"], nh_f, cum_rep, cum_row, f"{tag}_fox")
    mix = _mm_parts([(y_ssm, p["w_out"][:D]), (o, p["w_out"][D:])], "nn", F32, f"{tag}_mm_out", tn=1024)
    x2 = _post_fwd(x, mix, g_post, 1.0, f"{tag}_post")
    return x2, (x, n, proj, bias, mult, dtf, cs_rep, cs_row, cum_rep, cum_row, xbc, X, Y, dfull, o, lse, y_ssm, mix)


def _hyb_bwd(dy, saved, g_pre, g_post, p, tag):
    x, n, proj, bias, mult, dtf, cs_rep, cs_row, cum_rep, cum_row, xbc, X, Y, dfull, o, lse, y_ssm, mix = saved
    T, D = x.shape
    cols = _hyb_cols(D)
    nh_s, nh_f = D // SSM_HEAD_DIM, D // FOX_HEAD_DIM
    gw = D // SSM_GROUPS
    dmix, dg_post = _post_bwd(dy, mix, g_post, 1.0, f"{tag}_post_b")
    dcat = _mm(dmix, p["w_out"], "nt", BF16, f"{tag}_mm_dcat", tn=1024, tk=1024)
    dw_out = jnp.concatenate([_mm(y_ssm, dmix, "tn", BF16, f"{tag}_mm_dwout_s", tm=1024, tn=1024, tk=1024),
                              _mm(o, dmix, "tn", BF16, f"{tag}_mm_dwout_f", tm=1024, tn=1024, tk=1024)], axis=0)

    def ssm_out_b(Yv, xv, zv, dv, dfv, ngv):
        _, vjp = jax.vjp(_ssm_out, Yv, xv, zv, dfv, ngv)
        return vjp(dv.astype(F32))
    dY, dxs_skip, dz, ddfull, dng = _ew(
        ssm_out_b, [(Y, 0), (xbc, 0), (proj, cols["z"] // gw), (dcat, 0)], [(dfull, 0), (p["norm_g"], 0)],
        [BF16, F32, BF16], [dfull.shape, p["norm_g"].shape], name=f"{tag}_ssm_out_b", cb=gw, ncb=SSM_GROUPS)
    dC, dB, dX, dcs_q, dcs_k = _ssd_bwd(xbc, X, cs_rep, cs_row, dY, f"{tag}_ssd_b")
    dxs, ddtf = _ew(lambda dXv, skv, xv, dv: (dXv * dv + skv, dXv * xv), [(dX, 0), (dxs_skip, 0), (xbc, 0), (dtf, 0)],
                    [], [F32, F32], [], name=f"{tag}_xdt_b", cb=512, ncb=D // 512)
    ddt = ddtf.reshape(T, nh_s, SSM_HEAD_DIM).sum(-1)
    dcs_s = dcs_q[:, ::LANES] + dcs_k.reshape(nh_s, T).T
    dq, dk, dv, dcum_k, dcum_q = _fox_bwd(proj, cols["q"], cols["k"], cols["v"], nh_f, cum_rep, cum_row, o, lse, dcat, D,
                                  f"{tag}_fox_b")
    dcum = dcum_q[:, ::LANES] + dcum_k.reshape(nh_f, T).T
    zpad = jnp.zeros((T, LANES - nh_s - nh_f), F32)
    dact = jnp.concatenate([ddt, jnp.zeros((T, nh_f), F32), zpad], axis=1)
    dcs = jnp.concatenate([dcs_s, dcum, zpad], axis=1)
    dsmall, dmult, dbias = _gates_bwd(proj, cols["small"], bias, mult, nh_s, dact, dcs, f"{tag}_gates_b")
    dxbc_act = jnp.concatenate([dxs, dB, dC], axis=1)
    dxbc, dconv_w, dconv_b = _conv_bwd(proj, cols["xbc"], cols["conv"], p["conv_w"], p["conv_b"], dxbc_act, True,
                                       f"{tag}_conv_b")
    pieces = [(dz, "z"), (dxbc, "xbc"), (dq, "q"), (dk, "k"), (dv, "v"), (dsmall, "small")]
    w_cols = lambda d, key: p["w_in"][:, cols[key]:cols[key] + d.shape[1]]
    dn = _mm_parts([(d, w_cols(d, key)) for d, key in pieces], "nt", F32, f"{tag}_mm_dn", tm=256, tn=512)
    dw_in = jnp.concatenate([_mm(n, d, "tn", BF16, f"{tag}_mm_dwin_{key}", tm=1024, tn=1024, tk=1024)
                             for d, key in pieces], axis=1)
    dy2, dg_pre = _rms_bwd_add(x, g_pre, dn, dy, f"{tag}_rms_b")
    grads = dict(w_in=dw_in, w_out=dw_out, conv_w=dconv_w, conv_b=dconv_b[0], dt_bias=dbias[0, :nh_s],
                 a_log=dmult[0, :nh_s] * mult[0, :nh_s], d=ddfull.reshape(nh_s, SSM_HEAD_DIM).sum(-1),
                 norm_g=dng[0], b_f=dbias[0, nh_s:nh_s + nh_f])
    return dy2, dg_pre, dg_post, grads


def _rec_fwd(x, g_pre, g_post, p, tag):
    T, D = x.shape
    n = _rms_fwd(x, g_pre, f"{tag}_rms")
    pr = _mm(n, p["w_in"], "nn", F32, f"{tag}_mm_in", tn=1024, tk=1024)
    xc = _conv_fwd(pr, D, D, p["conv_w"], p["conv_b"], False, f"{tag}_conv")
    a, u = _lru_gates_fwd(xc, p["w_a"], p["b_a"], p["w_x"], p["b_x"], p["lam"], f"{tag}_lru")
    hs = _scan_fwd(a, u, f"{tag}_scan")
    og = _ew(lambda hv, gv: hv * _gelu_tanh(gv), [(hs, 0), (pr, 0)], [], [BF16], [], name=f"{tag}_gate", cb=D)
    mix = _mm(og, p["w_out"], "nn", F32, f"{tag}_mm_out", tn=1024, tk=1024)
    x2 = _post_fwd(x, mix, g_post, 1.0, f"{tag}_post")
    return x2, (x, n, pr, xc, a, hs, og, mix)


def _rec_bwd(dy, saved, g_pre, g_post, p, tag):
    x, n, pr, xc, a, hs, og, mix = saved
    T, D = x.shape
    dmix, dg_post = _post_bwd(dy, mix, g_post, 1.0, f"{tag}_post_b")
    dog = _mm(dmix, p["w_out"], "nt", F32, f"{tag}_mm_dog", tn=1024, tk=1024)
    dw_out = _mm(og, dmix, "tn", BF16, f"{tag}_mm_dwout", tm=1024, tn=1024, tk=1024)

    def gate_b(hv, gv, dv):
        _, vjp = jax.vjp(lambda h_, g_: h_ * _gelu_tanh(g_), hv, gv)
        return vjp(dv)
    dhs, dgate = _ew(gate_b, [(hs, 0), (pr, 0), (dog, 0)], [], [F32, BF16], [], name=f"{tag}_gate_b", cb=D)
    du, da = _scan_bwd(a, dhs, hs, f"{tag}_scan_b")
    dxc, db_a, db_x, dlam, dw_a, dw_x = _lru_gates_bwd(xc, p["w_a"], p["b_a"], p["w_x"], p["b_x"], p["lam"], da, du,
                                                       f"{tag}_lru_b")
    dxr, dconv_w, dconv_b = _conv_bwd(pr, D, D, p["conv_w"], p["conv_b"], dxc, False, f"{tag}_conv_b")
    dn = _mm_parts([(dgate, p["w_in"][:, :D]), (dxr, p["w_in"][:, D:])], "nt", F32, f"{tag}_mm_dn", tn=1024)
    dw_in = jnp.concatenate([_mm(n, dgate, "tn", BF16, f"{tag}_mm_dwin_g", tm=1024, tn=1024, tk=1024),
                             _mm(n, dxr, "tn", BF16, f"{tag}_mm_dwin_x", tm=1024, tn=1024, tk=1024)], axis=1)
    dy2, dg_pre = _rms_bwd_add(x, g_pre, dn, dy, f"{tag}_rms_b")
    grads = dict(w_in=dw_in, w_out=dw_out, conv_w=dconv_w, conv_b=dconv_b[0], w_a=dw_a, b_a=db_a[0], w_x=dw_x,
                 b_x=db_x[0], lam=dlam[0])
    return dy2, dg_pre, dg_post, grads


def _hyb_params(W, i, D):
    return dict(w_in=_hyb_w_in_reorder(W["hyb_w_in"][i], D), w_out=W["hyb_w_out"][i], conv_w=W["ssm_conv_w"][i],
                conv_b=W["ssm_conv_b"][i][None], dt_bias=W["ssm_dt_bias"][i], a_log=W["ssm_a_log"][i],
                d=W["ssm_d"][i], norm_g=W["ssm_norm_g"][i][None], b_f=W["fox_b_f"][i])


def _rec_params(W, j):
    return dict(w_in=W["rec_w_in"][j], w_out=W["rec_w_out"][j], conv_w=W["rec_conv_w"][j],
                conv_b=W["rec_conv_b"][j][None], w_a=W["rec_w_a"][j], b_a=W["rec_b_a"][j][None],
                w_x=W["rec_w_x"][j], b_x=W["rec_b_x"][j][None], lam=W["rec_lambda"][j][None])


def _local_step(x, target, W):
    T, D = x.shape
    depth = W["norm_g"].shape[0]
    g = lambda l, k: W["norm_g"][l, k][None]
    saved = []
    for l in range(depth):
        x, s0 = _ffn_fwd(x, g(l, 0), g(l, 1), W["ffn_w_in"], W["ffn_w_out"], (l, 0), f"l{l}_ffn0")
        if l % 2 == 0:
            pm = _hyb_params(W, l // 2, D)
            x, s1 = _hyb_fwd(x, g(l, 2), g(l, 3), pm, f"l{l}_hyb")
        else:
            pm = _rec_params(W, l // 2)
            x, s1 = _rec_fwd(x, g(l, 2), g(l, 3), pm, f"l{l}_rec")
        x, s2 = _ffn_fwd(x, g(l, 4), g(l, 5), W["ffn_w_in"], W["ffn_w_out"], (l, 1), f"l{l}_ffn1")
        saved.append((s0, s1, s2, pm))

    def loss_fn(yv, tv):
        err = yv - tv
        part = 0.5 * jnp.sum(jnp.sum(err * err, axis=1, keepdims=True), axis=0, keepdims=True) / D
        return err * (1.0 / D), jnp.broadcast_to(part, (1, D))
    dy, loss_row = _ew(loss_fn, [(x, 0), (target, 0)], [], [F32], [(1, D)], name="loss")
    loss = loss_row[0, 0]

    gn = [[None] * 6 for _ in range(depth)]
    g_ffn_in = [[None, None] for _ in range(depth)]
    g_ffn_out = [[None, None] for _ in range(depth)]
    g_hyb, g_rec = [], []
    for l in reversed(range(depth)):
        s0, s1, s2, pm = saved[l]
        dy, gn[l][4], gn[l][5], g_ffn_in[l][1], g_ffn_out[l][1] = _ffn_bwd(
            dy, s2, g(l, 4), g(l, 5), W["ffn_w_in"], W["ffn_w_out"], (l, 1), f"l{l}_ffn1")
        if l % 2 == 0:
            dy, gn[l][2], gn[l][3], gm = _hyb_bwd(dy, s1, g(l, 2), g(l, 3), pm, f"l{l}_hyb")
            g_hyb.insert(0, gm)
        else:
            dy, gn[l][2], gn[l][3], gm = _rec_bwd(dy, s1, g(l, 2), g(l, 3), pm, f"l{l}_rec")
            g_rec.insert(0, gm)
        dy, gn[l][0], gn[l][1], g_ffn_in[l][0], g_ffn_out[l][0] = _ffn_bwd(
            dy, s0, g(l, 0), g(l, 1), W["ffn_w_in"], W["ffn_w_out"], (l, 0), f"l{l}_ffn0")

    st = lambda items: jnp.stack(items)
    grads = {
        "norm_g": st([st([r[0] for r in row]) for row in gn]),
        "ffn_w_in": st([st(row) for row in g_ffn_in]),
        "ffn_w_out": st([st(row) for row in g_ffn_out]),
        "hyb_w_in": st([_hyb_w_in_restore(m["w_in"], D) for m in g_hyb]),
        "ssm_conv_w": st([m["conv_w"] for m in g_hyb]), "ssm_conv_b": st([m["conv_b"] for m in g_hyb]),
        "ssm_dt_bias": st([m["dt_bias"] for m in g_hyb]), "ssm_a_log": st([m["a_log"] for m in g_hyb]),
        "ssm_d": st([m["d"] for m in g_hyb]), "ssm_norm_g": st([m["norm_g"] for m in g_hyb]),
        "fox_b_f": st([m["b_f"] for m in g_hyb]), "hyb_w_out": st([m["w_out"] for m in g_hyb]),
        "rec_w_in": st([m["w_in"] for m in g_rec]), "rec_conv_w": st([m["conv_w"] for m in g_rec]),
        "rec_conv_b": st([m["conv_b"] for m in g_rec]), "rec_w_a": st([m["w_a"] for m in g_rec]),
        "rec_b_a": st([m["b_a"] for m in g_rec]), "rec_w_x": st([m["w_x"] for m in g_rec]),
        "rec_b_x": st([m["b_x"] for m in g_rec]), "rec_lambda": st([m["lam"] for m in g_rec]),
        "rec_w_out": st([m["w_out"] for m in g_rec]),
    }
    return loss, dy, grads


MESH_AXES = ("x", "y", "c")
N_CHIPS = 4
N_DEV = 8
HBM = pl.BlockSpec(memory_space=pltpu.HBM)


def _mesh_pos():
    return tuple(lax.axis_index(n) for n in MESH_AXES)


def _other_chips(x, y):
    chips = [(1 - x, y), (x, 1 - y), (1 - x, 1 - y)]
    return chips, [2 * cx + cy for cx, cy in chips]


def _rcopy(src, dst, send_sem, recv_sem, dev):
    return pltpu.make_async_remote_copy(src_ref=src, dst_ref=dst, send_sem=send_sem, recv_sem=recv_sem,
                                        device_id=dev, device_id_type=pl.DeviceIdType.MESH)


def _comm_params():
    return pltpu.CompilerParams()


DMA_CHUNK_BYTES = 1 << 20
DMA_ROW_ALIGN = 16


def _nchunks(rows, row_bytes):
    n = max(1, min(rows // DMA_ROW_ALIGN, (rows * row_bytes) // DMA_CHUNK_BYTES))
    while n > 1 and (rows % n or (rows // n) % DMA_ROW_ALIGN):
        n -= 1
    return n


def _row_bytes(ref):
    return ref.shape[-1] * jnp.dtype(ref.dtype).itemsize


def _all_gather(arrs, name):
    n = len(arrs)

    def body(*refs):
        ins, outs = refs[:n], refs[n:2 * n]
        send, recv, fsend, frecv = refs[2 * n:]
        x, y, c = _mesh_pos()
        k = 2 * x + y
        sibling = (x, y, 1 - c)
        chips, chip_k = _other_chips(x, y)
        halves = [r.shape[0] // 2 for r in ins]
        nchs = [_nchunks(h, _row_bytes(r)) for h, r in zip(halves, ins)]
        for a in range(n):
            h, step = halves[a], halves[a] // nchs[a]
            for j, chip in enumerate(chips):
                for q in range(nchs[a]):
                    rows = pl.ds(c * h + q * step, step)
                    _rcopy(ins[a].at[rows], outs[a].at[k, rows], send.at[a, j], recv.at[a, j], (*chip, c)).start()
        for a in range(n):
            h, step = halves[a], halves[a] // nchs[a]
            for j, chip in enumerate(chips):
                blk = outs[a].at[chip_k[j], pl.ds(c * h, h)]
                _rcopy(blk, blk, send.at[a, j], recv.at[a, j], (*chip, c)).wait_recv()
                for q in range(nchs[a]):
                    part = outs[a].at[chip_k[j], pl.ds(c * h + q * step, step)]
                    _rcopy(part, part, fsend.at[a, j], frecv.at[a, j], sibling).start()
        for a in range(n):
            h = halves[a]
            for j, chip in enumerate(chips):
                blk = outs[a].at[chip_k[j], pl.ds((1 - c) * h, h)]
                _rcopy(blk, blk, fsend.at[a, j], frecv.at[a, j], sibling).wait_recv()
        for a in range(n):
            h = halves[a]
            for j, chip in enumerate(chips):
                _rcopy(ins[a].at[pl.ds(c * h, h)], outs[a].at[k, pl.ds(c * h, h)], send.at[a, j], recv.at[a, j],
                       (*chip, c)).wait_send()
                blk = outs[a].at[chip_k[j], pl.ds(c * h, h)]
                _rcopy(blk, blk, fsend.at[a, j], frecv.at[a, j], sibling).wait_send()

    outs = pl.pallas_call(
        body, name=name, in_specs=[HBM] * n, out_specs=[HBM] * n,
        out_shape=[jax.ShapeDtypeStruct((N_CHIPS,) + a.shape, a.dtype) for a in arrs],
        scratch_shapes=[pltpu.SemaphoreType.DMA((n, 3))] * 4, compiler_params=_comm_params(),
    )(*arrs)
    k = 2 * lax.axis_index("x") + lax.axis_index("y")
    return [lax.dynamic_update_index_in_dim(o, a, k, 0) for o, a in zip(outs, arrs)]


def _pair_exchange(gs, name):
    n = len(gs)

    def body(*refs):
        ins, theirs = refs[:n], refs[n:2 * n]
        send, recv = refs[2 * n:]
        x, y, c = _mesh_pos()
        sibling = (x, y, 1 - c)
        for a in range(n):
            h = ins[a].shape[1] // 2
            nch = _nchunks(h, _row_bytes(ins[a]))
            step = h // nch
            for kk in range(N_CHIPS):
                for q in range(nch):
                    _rcopy(ins[a].at[kk, pl.ds((1 - c) * h + q * step, step)], theirs[a].at[kk, pl.ds(q * step, step)],
                           send.at[a], recv.at[a], sibling).start()
        for a in range(n):
            h = ins[a].shape[1] // 2
            _rcopy(ins[a].at[:, pl.ds((1 - c) * h, h)], theirs[a], send.at[a], recv.at[a], sibling).wait()

    half = [jax.ShapeDtypeStruct((a.shape[0], a.shape[1] // 2) + a.shape[2:], a.dtype) for a in gs]
    theirs = pl.pallas_call(
        body, name=name, in_specs=[HBM] * n, out_specs=[HBM] * n, out_shape=half,
        scratch_shapes=[pltpu.SemaphoreType.DMA((n,))] * 2, compiler_params=_comm_params(),
    )(*gs)
    c = lax.axis_index("c")
    mine = [lax.dynamic_slice_in_dim(g, c * (g.shape[1] // 2), g.shape[1] // 2, axis=1) for g in gs]
    return mine, theirs


def _chip_exchange(ss, name):
    n = len(ss)

    def body(*refs):
        ins = refs[:n]
        got = [refs[n + 3 * a:n + 3 * a + 3] for a in range(n)]
        send, recv = refs[4 * n:]
        x, y, c = _mesh_pos()
        chips, chip_k = _other_chips(x, y)
        for a in range(n):
            h = ins[a].shape[1]
            nch = _nchunks(h, _row_bytes(ins[a]))
            step = h // nch
            for q in range(nch):
                rows = pl.ds(q * step, step)
                for j, chip in enumerate(chips):
                    _rcopy(ins[a].at[chip_k[j], rows], got[a][j].at[rows], send.at[a, j], recv.at[a, j],
                           (*chip, c)).start()
        for a in range(n):
            for j, chip in enumerate(chips):
                _rcopy(ins[a].at[chip_k[j]], got[a][j], send.at[a, j], recv.at[a, j], (*chip, c)).wait()

    got = [jax.ShapeDtypeStruct(a.shape[1:], a.dtype) for a in ss for _ in range(3)]
    res = pl.pallas_call(
        body, name=name, in_specs=[HBM] * n, out_specs=[HBM] * (3 * n), out_shape=got,
        scratch_shapes=[pltpu.SemaphoreType.DMA((n, 3))] * 2, compiler_params=_comm_params(),
    )(*ss)
    k = 2 * lax.axis_index("x") + lax.axis_index("y")
    own = [lax.dynamic_index_in_dim(s, k, 0, keepdims=False) for s in ss]
    return own, [res[3 * a:3 * a + 3] for a in range(n)]


def _pair_share(rs, name):
    n = len(rs)

    def body(*refs):
        ins, outs = refs[:n], refs[n:2 * n]
        send, recv = refs[2 * n:]
        x, y, c = _mesh_pos()
        sibling = (x, y, 1 - c)
        for a in range(n):
            h = ins[a].shape[0]
            nch = _nchunks(h, _row_bytes(ins[a]))
            step = h // nch
            for q in range(nch):
                rows = pl.ds(q * step, step)
                _rcopy(ins[a].at[rows], outs[a].at[rows], send.at[a], recv.at[a], sibling).start()
        for a in range(n):
            _rcopy(ins[a], outs[a], send.at[a], recv.at[a], sibling).wait()

    theirs = pl.pallas_call(
        body, name=name, in_specs=[HBM] * n, out_specs=[HBM] * n,
        out_shape=[jax.ShapeDtypeStruct(a.shape, a.dtype) for a in rs],
        scratch_shapes=[pltpu.SemaphoreType.DMA((n,))] * 2, compiler_params=_comm_params(),
    )(*rs)
    c = lax.axis_index("c")
    return [jnp.concatenate([jnp.where(c == 0, r, t), jnp.where(c == 0, t, r)], axis=0) for r, t in zip(rs, theirs)]


def _exchange_all(vec, name):
    def body(v_ref, out_ref, send, recv):
        x, y, c = _mesh_pos()
        flip = lambda p, f: 1 - p if f else p
        me = 4 * x + 2 * y + c
        rows_all = v_ref.shape[0]
        nch = _nchunks(rows_all, _row_bytes(v_ref))
        step = rows_all // nch
        for j in range(1, N_DEV):
            fx, fy, fc = (j >> 2) & 1, (j >> 1) & 1, j & 1
            for q in range(nch):
                rows = pl.ds(q * step, step)
                _rcopy(v_ref.at[rows], out_ref.at[me, rows], send.at[j - 1], recv.at[j - 1],
                       (flip(x, fx), flip(y, fy), flip(c, fc))).start()
        for j in range(1, N_DEV):
            fx, fy, fc = (j >> 2) & 1, (j >> 1) & 1, j & 1
            slot = out_ref.at[4 * flip(x, fx) + 2 * flip(y, fy) + flip(c, fc)]
            _rcopy(slot, slot, send.at[j - 1], recv.at[j - 1], (x, y, c)).wait_recv()
        for j in range(1, N_DEV):
            _rcopy(v_ref, out_ref.at[me], send.at[j - 1], recv.at[j - 1], (x, y, c)).wait_send()

    out = pl.pallas_call(
        body, name=name, in_specs=[HBM], out_specs=HBM,
        out_shape=jax.ShapeDtypeStruct((N_DEV,) + vec.shape, vec.dtype),
        scratch_shapes=[pltpu.SemaphoreType.DMA((N_DEV - 1,))] * 2, compiler_params=_comm_params(),
    )(vec)
    me = 4 * lax.axis_index("x") + 2 * lax.axis_index("y") + lax.axis_index("c")
    return lax.dynamic_update_index_in_dim(out, vec, me, 0)


def _rows(a):
    return a.reshape(-1, a.shape[-1])


def _sum_kernel(parts, out_dtype, name):
    def fn(*vals):
        acc = vals[0].astype(F32)
        for v in vals[1:]:
            acc = acc + v.astype(F32)
        return acc
    out = _ew(fn, [(_rows(p), 0) for p in parts], [], [out_dtype], [], name=name)
    return out.reshape(parts[0].shape)


def _reduce_scatter(gs, tag):
    mine, theirs = _pair_exchange(gs, f"{tag}_pair")
    pair = [_sum_kernel([m, t], BF16, f"{tag}_add_pair{a}") for a, (m, t) in enumerate(zip(mine, theirs))]
    own, got = _chip_exchange(pair, f"{tag}_chips")
    red = [_sum_kernel([o, g[0], g[1], g[2]], F32, f"{tag}_add_chips{a}") for a, (o, g) in enumerate(zip(own, got))]
    return _pair_share(red, f"{tag}_share")


def _pack(arrs, row_mult):
    flat = jnp.concatenate([a.reshape(-1).astype(F32) for a in arrs])
    unit = row_mult * LANES
    pad = (-flat.size) % unit
    return jnp.pad(flat, (0, pad)).reshape(-1, LANES)


def _unpack(mat, shapes):
    flat, out, pos = mat.reshape(-1), [], 0
    for s in shapes:
        size = 1
        for d in s:
            size *= d
        out.append(flat[pos:pos + size].reshape(s))
        pos += size
    return out


def _to_shards(a, axis):
    sh = a.shape
    a = a.reshape(sh[:axis] + (N_CHIPS, sh[axis] // N_CHIPS) + sh[axis + 1:])
    return jnp.moveaxis(a, axis, 0)


def _from_shards(g, axis):
    g = jnp.moveaxis(g, 0, axis)
    sh = g.shape
    return g.reshape(sh[:axis] + (sh[axis] * sh[axis + 1],) + sh[axis + 2:])


def _adamw_fn(w, g, m, v):
    m2 = ADAM_B1 * m + (1.0 - ADAM_B1) * g
    v2 = ADAM_B2 * v + (1.0 - ADAM_B2) * (g * g)
    m_hat = m2 / (1.0 - ADAM_B1 ** ADAM_STEP)
    v_hat = v2 / (1.0 - ADAM_B2 ** ADAM_STEP)
    return -ADAM_LR * (m_hat / (jnp.sqrt(v_hat) + ADAM_EPS) + ADAM_WD * w), m2, v2


def _adamw(w, g, m, v, name):
    res = _ew(_adamw_fn, [(_rows(a), 0) for a in (w, g, m, v)], [], [F32, F32, F32], [], name=name)
    return tuple(r.reshape(w.shape) for r in res)


def kernel(x, norm_g, ffn_w_in, ffn_w_out, hyb_w_in, ssm_conv_w, ssm_conv_b, ssm_dt_bias, ssm_a_log, ssm_d, ssm_norm_g, fox_b_f, hyb_w_out, rec_w_in, rec_conv_w, rec_conv_b, rec_w_a, rec_b_a, rec_w_x, rec_b_x, rec_lambda, rec_w_out, loss_target, m_norm_g, m_ffn_w_in, m_ffn_w_out, m_hyb_w_in, m_ssm_conv_w, m_ssm_conv_b, m_ssm_dt_bias, m_ssm_a_log, m_ssm_d, m_ssm_norm_g, m_fox_b_f, m_hyb_w_out, m_rec_w_in, m_rec_conv_w, m_rec_conv_b, m_rec_w_a, m_rec_b_a, m_rec_w_x, m_rec_b_x, m_rec_lambda, m_rec_w_out, v_norm_g, v_ffn_w_in, v_ffn_w_out, v_hyb_w_in, v_ssm_conv_w, v_ssm_conv_b, v_ssm_dt_bias, v_ssm_a_log, v_ssm_d, v_ssm_norm_g, v_fox_b_f, v_hyb_w_out, v_rec_w_in, v_rec_conv_w, v_rec_conv_b, v_rec_w_a, v_rec_b_a, v_rec_w_x, v_rec_b_x, v_rec_lambda, v_rec_w_out):
    given = dict(locals())
    w = {n: given[n] for n in WEIGHTS}
    m = {n: given["m_" + n] for n in WEIGHTS}
    v = {n: given["v_" + n] for n in WEIGHTS}
    k = 2 * lax.axis_index("x") + lax.axis_index("y")

    big_bf16 = [_ew(lambda t: t, [(_rows(w[n]), 0)], [], [BF16], [], name=f"cast_{n}") for n in BIG]
    small_shapes = [w[n].shape for n in SMALL_SHARDED]
    small_pack = _pack([w[n] for n in SMALL_SHARDED], 2 * SUBLANES)
    gathered = _all_gather(big_bf16 + [small_pack], "gather_weights")
    W = {n: w[n] for n in SMALL_REPL}
    for n, g in zip(BIG, gathered[:-1]):
        W[n] = _from_shards(g.reshape((N_CHIPS,) + w[n].shape), SHARD_AXIS[n])
    per_chip = [_unpack(gathered[-1][kk], small_shapes) for kk in range(N_CHIPS)]
    for idx, n in enumerate(SMALL_SHARDED):
        W[n] = jnp.concatenate([per_chip[kk][idx] for kk in range(N_CHIPS)], axis=SHARD_AXIS[n])

    loss_part, dy, grads = _local_step(x[0], loss_target[0], W)
    loss = lax.psum(loss_part, MESH_AXES)

    red_big = _reduce_scatter(
        [_to_shards(grads[n], SHARD_AXIS[n]).reshape(N_CHIPS, -1, w[n].shape[-1]) for n in BIG], "rs")
    g_out = {n: r.reshape(w[n].shape) for n, r in zip(BIG, red_big)}
    small_names = SMALL_SHARDED + SMALL_REPL
    slots = _exchange_all(_pack([grads[n] for n in small_names], SUBLANES), "gather_small_grads")
    small_sum = _sum_kernel([slots[d] for d in range(N_DEV)], F32, "add_small_grads")
    for n, g in zip(small_names, _unpack(small_sum, [grads[n].shape for n in small_names])):
        if n in SHARD_AXIS:
            loc = g.shape[SHARD_AXIS[n]] // N_CHIPS
            g = lax.dynamic_slice_in_dim(g, k * loc, loc, axis=SHARD_AXIS[n])
        g_out[n] = g

    delta, new_m, new_v = {}, {}, {}
    for n in BIG:
        delta[n], new_m[n], new_v[n] = _adamw(w[n], g_out[n], m[n], v[n], f"adamw_{n}")
    shapes = [w[n].shape for n in small_names]
    packed = [_pack([d[n] for n in small_names], SUBLANES) for d in (w, g_out, m, v)]
    for d, mat in zip((delta, new_m, new_v), _adamw(*packed, "adamw_small")):
        d.update(zip(small_names, _unpack(mat, shapes)))

    return (loss, dy[None], *[g_out[n] for n in WEIGHTS], *[delta[n] for n in WEIGHTS],
            *[new_m[n] for n in WEIGHTS], *[new_v[n] for n in WEIGHTS])
```

```python
import functools

import jax
import jax.numpy as jnp
from jax import lax
from jax.experimental import pallas as pl
from jax.experimental.pallas import tpu as pltpu

F32, BF16 = jnp.float32, jnp.bfloat16

NORM_EPS = 1e-6
CONV_K = 4
SSM_HEAD_DIM = 64
SSM_STATE = 128
SSM_GROUPS = 2
FOX_HEAD_DIM = 128
RNN_BLOCK = 128
RG_LRU_C = 8.0
ADAM_LR, ADAM_B1, ADAM_B2, ADAM_EPS, ADAM_WD, ADAM_STEP = 0.001, 0.9, 0.999, 1e-08, 0.01, 10

LANES = 128
SUBLANES = 8
VMEM_LIMIT = 48 * 1024 * 1024
SSD_TILE = 256
FOX_TILE = 512
NEG = -1e30

NT = (((1,), (1,)), ((), ()))
NN = (((1,), (0,)), ((), ()))
TN = (((0,), (0,)), ((), ()))

BIG = ("ffn_w_in", "ffn_w_out", "hyb_w_in", "hyb_w_out", "rec_w_in", "rec_w_out")
FFN = ("ffn_w_in", "ffn_w_out")
SMALL_SHARDED = ("norm_g", "ssm_conv_w", "rec_conv_w", "rec_conv_b", "rec_b_a", "rec_b_x", "rec_lambda")
SMALL_REPL = ("ssm_conv_b", "ssm_dt_bias", "ssm_a_log", "ssm_d", "ssm_norm_g", "fox_b_f", "rec_w_a", "rec_w_x")
WEIGHTS = ("norm_g", "ffn_w_in", "ffn_w_out", "hyb_w_in", "ssm_conv_w", "ssm_conv_b", "ssm_dt_bias", "ssm_a_log",
           "ssm_d", "ssm_norm_g", "fox_b_f", "hyb_w_out", "rec_w_in", "rec_conv_w", "rec_conv_b", "rec_w_a",
           "rec_b_a", "rec_w_x", "rec_b_x", "rec_lambda", "rec_w_out")
SHARD_AXIS = {"norm_g": 2, "ffn_w_in": 3, "ffn_w_out": 2, "hyb_w_in": 2, "ssm_conv_w": 2, "hyb_w_out": 1,
              "rec_w_in": 2, "rec_conv_w": 2, "rec_conv_b": 1, "rec_b_a": 1, "rec_b_x": 1, "rec_lambda": 1,
              "rec_w_out": 1}


def _params(*sem):
    return pltpu.CompilerParams(dimension_semantics=sem if sem else None, vmem_limit_bytes=VMEM_LIMIT)


def _tile(dim, pref):
    if dim <= pref:
        return dim
    for align in (LANES, SUBLANES):
        t = (pref // align) * align
        while t >= align:
            if dim % t == 0:
                return t
            t -= align
    return dim


def _sigmoid(x):
    return 1.0 / (1.0 + jnp.exp(-x))


def _softplus(x):
    return jnp.maximum(x, 0.0) + jnp.log(1.0 + jnp.exp(-jnp.abs(x)))


def _log_sigmoid(x):
    return -_softplus(-x)


def _silu(x):
    return x * _sigmoid(x)


def _gelu_tanh(x):
    return 0.5 * x * (1.0 + jnp.tanh(0.7978845608028654 * (x + 0.044715 * x * x * x)))


def _neg_expm1(x):
    series = -x * (1.0 + x * (0.5 + x * (1.0 / 6.0)))
    return jnp.where(x > -1e-2, series, 1.0 - jnp.exp(x))


def _rms(x, g):
    xf = x.astype(F32)
    return xf * lax.rsqrt(jnp.mean(xf * xf, axis=-1, keepdims=True) + NORM_EPS) * g


def _mm(a, b, mode, out_dtype, name, tm=512, tn=512, tk=512, b_sel=()):
    bshape = b.shape[len(b_sel):]
    if mode == "nn":
        (M, K), (K2, N) = a.shape, bshape
    elif mode == "nt":
        (M, K), (N, K2) = a.shape, bshape
    else:
        (K, M), (K2, N) = a.shape, bshape
    assert K == K2, (a.shape, b.shape, mode)
    tm, tn, tk = _tile(M, tm), _tile(N, tn), _tile(K, tk)
    nk = K // tk
    dims = {"nn": NN, "nt": NT, "tn": TN}[mode]
    lead = (None,) * len(b_sel)
    if mode == "tn":
        a_spec = pl.BlockSpec((tk, tm), lambda n, m, k: (k, m))
    else:
        a_spec = pl.BlockSpec((tm, tk), lambda n, m, k: (m, k))
    if mode == "nt":
        b_spec = pl.BlockSpec(lead + (tn, tk), lambda n, m, k: (*b_sel, n, k))
    else:
        b_spec = pl.BlockSpec(lead + (tk, tn), lambda n, m, k: (*b_sel, k, n))

    def body(a_ref, b_ref, o_ref, *acc):
        p = lax.dot_general(a_ref[...].astype(BF16), b_ref[...].astype(BF16), dims, preferred_element_type=F32)
        if nk == 1:
            o_ref[...] = p.astype(o_ref.dtype)
        else:
            acc_ref, = acc
            k = pl.program_id(2)

            @pl.when(k == 0)
            def _():
                acc_ref[...] = p

            @pl.when(k > 0)
            def _():
                acc_ref[...] += p

            @pl.when(k == nk - 1)
            def _():
                o_ref[...] = acc_ref[...].astype(o_ref.dtype)

    return pl.pallas_call(
        body, name=name, grid=(N // tn, M // tm, nk),
        in_specs=[a_spec, b_spec], out_specs=pl.BlockSpec((tm, tn), lambda n, m, k: (m, n)),
        out_shape=jax.ShapeDtypeStruct((M, N), out_dtype),
        scratch_shapes=[pltpu.VMEM((tm, tn), F32)] if nk > 1 else [],
        compiler_params=_params("parallel", "parallel", "arbitrary"),
    )(a, b)


def _mm_parts(parts, mode, out_dtype, name, tm=512, tn=512):
    M = parts[0][0].shape[0]
    N = parts[0][1].shape[1] if mode == "nn" else parts[0][1].shape[0]
    tm, tn = _tile(M, tm), _tile(N, tn)
    dims = NN if mode == "nn" else NT
    in_specs, args = [], []
    for a, b in parts:
        K = a.shape[1]
        in_specs.append(pl.BlockSpec((tm, K), lambda n, m: (m, 0)))
        if mode == "nn":
            in_specs.append(pl.BlockSpec((K, tn), lambda n, m: (0, n)))
        else:
            in_specs.append(pl.BlockSpec((tn, K), lambda n, m: (n, 0)))
        args += [a, b]

    def body(*refs):
        o_ref = refs[-1]
        acc = None
        for p in range(len(parts)):
            d = lax.dot_general(refs[2 * p][...].astype(BF16), refs[2 * p + 1][...].astype(BF16), dims,
                                preferred_element_type=F32)
            acc = d if acc is None else acc + d
        o_ref[...] = acc.astype(o_ref.dtype)

    return pl.pallas_call(
        body, name=name, grid=(N // tn, M // tm), in_specs=in_specs,
        out_specs=pl.BlockSpec((tm, tn), lambda n, m: (m, n)), out_shape=jax.ShapeDtypeStruct((M, N), out_dtype),
        compiler_params=_params("parallel", "parallel"),
    )(*args)


def _ew(fn, tiled, params, outs, reds, *, name, tm=256, cb=None, ncb=1):
    T = tiled[0][0].shape[0]
    cb = tiled[0][0].shape[1] if cb is None else cb
    tm = _tile(T, tm)
    in_specs, args = [], []
    for arr, off in tiled:
        in_specs.append(pl.BlockSpec((tm, cb), functools.partial(lambda n, i, o: (i, n + o), o=off)))
        args.append(arr)
    for arr, off in params:
        if arr.ndim == 2:
            in_specs.append(pl.BlockSpec((arr.shape[0], cb), functools.partial(lambda n, i, o: (0, n + o), o=off)))
        else:
            in_specs.append(pl.BlockSpec((None,) + arr.shape[1:], lambda n, i: (n, 0, 0)))
        args.append(arr)
    out_shape = [jax.ShapeDtypeStruct((T, cb * ncb), dt) for dt in outs]
    out_specs = [pl.BlockSpec((tm, cb), lambda n, i: (i, n)) for _ in outs]
    for shape in reds:
        out_shape.append(jax.ShapeDtypeStruct(shape, F32))
        if len(shape) == 2:
            out_specs.append(pl.BlockSpec((shape[0], cb), lambda n, i: (0, n)))
        else:
            out_specs.append(pl.BlockSpec((None,) + tuple(shape[1:]), lambda n, i: (n, 0, 0)))
    n_in, n_out = len(args), len(outs)

    def body(*refs):
        i = pl.program_id(1)
        res = fn(*[r[...] for r in refs[:n_in]])
        res = res if isinstance(res, (tuple, list)) else (res,)
        for r, v in zip(refs[n_in:n_in + n_out], res[:n_out]):
            r[...] = v.astype(r.dtype)
        for r, v in zip(refs[n_in + n_out:], res[n_out:]):
            @pl.when(i == 0)
            def _():
                r[...] = jnp.zeros(r.shape, r.dtype)

            r[...] += v.astype(r.dtype).reshape(r.shape)

    res = pl.pallas_call(
        body, name=name, grid=(ncb, T // tm), in_specs=in_specs, out_specs=out_specs, out_shape=out_shape,
        compiler_params=_params("arbitrary", "arbitrary"),
    )(*args)
    return res[0] if len(res) == 1 else tuple(res)


def _rms_fwd(x, g, name):
    return _ew(lambda xv, gv: _rms(xv, gv), [(x, 0)], [(g, 0)], [BF16], [], name=name)


def _rms_bwd_add(x, g, dn, dres, name):
    def fn(xv, dnv, drv, gv):
        _, vjp = jax.vjp(_rms, xv, gv)
        dx, dg = vjp(dnv.astype(F32))
        return drv + dx, dg
    return _ew(fn, [(x, 0), (dn, 0), (dres, 0)], [(g, 0)], [F32], [g.shape], name=name)


def _post_fwd(x, h, g, w, name):
    return _ew(lambda xv, hv, gv: xv + w * _rms(hv, gv), [(x, 0), (h, 0)], [(g, 0)], [F32], [], name=name)


def _post_bwd(dy, h, g, w, name):
    def fn(dyv, hv, gv):
        _, vjp = jax.vjp(lambda a, b: w * _rms(a, b), hv, gv)
        return vjp(dyv)
    return _ew(fn, [(dy, 0), (h, 0)], [(g, 0)], [BF16], [g.shape], name=name)


def _mm_swiglu(x, g_in, blk, name, tm=512):
    T, K = x.shape
    tn = g_in.shape[-1]
    F = 2 * tn
    tm = _tile(T, tm)

    def body(a_ref, bg_ref, bu_ref, g_ref, u_ref, act_ref):
        a = a_ref[...].astype(BF16)
        g = jnp.dot(a, bg_ref[...].astype(BF16), preferred_element_type=F32)
        u = jnp.dot(a, bu_ref[...].astype(BF16), preferred_element_type=F32)
        g_ref[...] = g.astype(g_ref.dtype)
        u_ref[...] = u.astype(u_ref.dtype)
        act_ref[...] = (_silu(g) * u).astype(act_ref.dtype)

    out = jax.ShapeDtypeStruct((T, F), BF16)
    o_spec = pl.BlockSpec((tm, tn), lambda n, m: (m, n))
    return pl.pallas_call(
        body, name=name, grid=(2, T // tm),
        in_specs=[pl.BlockSpec((tm, K), lambda n, m: (m, 0)),
                  pl.BlockSpec((None, K, tn), lambda n, m: (n, blk, 0)),
                  pl.BlockSpec((None, K, tn), lambda n, m: (n + 2, blk, 0))],
        out_specs=[o_spec] * 3, out_shape=[out] * 3, compiler_params=_params("parallel", "parallel"),
    )(x, g_in, g_in)


def _ffn_mm_out(a, g_out, blk, name, tm=512):
    T, F = a.shape
    rl, D = F // N_CHIPS, g_out.shape[-1]
    tm = _tile(T, tm)

    def body(a_ref, b0_ref, b1_ref, o_ref, acc_ref):
        k = pl.program_id(1)
        b = jnp.concatenate([b0_ref[...], b1_ref[...]], axis=0).astype(BF16)
        p = jnp.dot(a_ref[...].astype(BF16), b, preferred_element_type=F32)

        @pl.when(k == 0)
        def _():
            acc_ref[...] = p

        @pl.when(k == 1)
        def _():
            o_ref[...] = acc_ref[...] + p

    return pl.pallas_call(
        body, name=name, grid=(T // tm, 2),
        in_specs=[pl.BlockSpec((tm, 2 * rl), lambda m, k: (m, k)),
                  pl.BlockSpec((None, rl, D), lambda m, k: (2 * k, blk, 0)),
                  pl.BlockSpec((None, rl, D), lambda m, k: (2 * k + 1, blk, 0))],
        out_specs=pl.BlockSpec((tm, D), lambda m, k: (m, 0)), out_shape=jax.ShapeDtypeStruct((T, D), F32),
        scratch_shapes=[pltpu.VMEM((tm, D), F32)], compiler_params=_params("parallel", "arbitrary"),
    )(a, g_out, g_out)


def _ffn_mm_da(dh, g_out, blk, F, name, tm=512):
    T, D = dh.shape
    rl = F // N_CHIPS
    tm = _tile(T, tm)

    def body(a_ref, b0_ref, b1_ref, o_ref):
        b = jnp.concatenate([b0_ref[...], b1_ref[...]], axis=0).astype(BF16)
        o_ref[...] = lax.dot_general(a_ref[...].astype(BF16), b, NT, preferred_element_type=F32).astype(o_ref.dtype)

    return pl.pallas_call(
        body, name=name, grid=(2, T // tm),
        in_specs=[pl.BlockSpec((tm, D), lambda n, m: (m, 0)),
                  pl.BlockSpec((None, rl, D), lambda n, m: (2 * n, blk, 0)),
                  pl.BlockSpec((None, rl, D), lambda n, m: (2 * n + 1, blk, 0))],
        out_specs=pl.BlockSpec((tm, 2 * rl), lambda n, m: (m, n)), out_shape=jax.ShapeDtypeStruct((T, F), BF16),
        compiler_params=_params("parallel", "parallel"),
    )(dh, g_out, g_out)


def _ffn_mm_dn(dgu, g_in, blk, D, name, tm=512):
    T = dgu.shape[0]
    cw = g_in.shape[-1]
    tm = _tile(T, tm)

    def body(a_ref, b0_ref, b1_ref, o_ref, acc_ref):
        k = pl.program_id(1)
        a = a_ref[...].astype(BF16)
        p = lax.dot_general(a[:, :cw], b0_ref[...].astype(BF16), NT, preferred_element_type=F32)
        p = p + lax.dot_general(a[:, cw:], b1_ref[...].astype(BF16), NT, preferred_element_type=F32)

        @pl.when(k == 0)
        def _():
            acc_ref[...] = p

        @pl.when(k == 1)
        def _():
            o_ref[...] = acc_ref[...] + p

    return pl.pallas_call(
        body, name=name, grid=(T // tm, 2),
        in_specs=[pl.BlockSpec((tm, 2 * cw), lambda m, k: (m, k)),
                  pl.BlockSpec((None, D, cw), lambda m, k: (2 * k, blk, 0)),
                  pl.BlockSpec((None, D, cw), lambda m, k: (2 * k + 1, blk, 0))],
        out_specs=pl.BlockSpec((tm, D), lambda m, k: (m, 0)), out_shape=jax.ShapeDtypeStruct((T, D), F32),
        scratch_shapes=[pltpu.VMEM((tm, D), F32)], compiler_params=_params("parallel", "arbitrary"),
    )(dgu, g_in, g_in)


def _mm_tn_shards(a, b, by_rows, name, tk=1024):
    (K, M), (_, N) = a.shape, b.shape
    tk = _tile(K, tk)
    nk = K // tk
    if by_rows:
        rl = M // N_CHIPS
        a_spec = pl.BlockSpec((tk, 2 * rl), lambda j, k: (k, j))
        b_spec = pl.BlockSpec((tk, N), lambda j, k: (k, 0))
        o_spec = pl.BlockSpec((2, rl, N), lambda j, k: (j, 0, 0))
        out_shape, acc_shape, steps = (N_CHIPS, rl, N), (2 * rl, N), 2
    else:
        cw = N // N_CHIPS
        a_spec = pl.BlockSpec((tk, M), lambda j, k: (k, 0))
        b_spec = pl.BlockSpec((tk, cw), lambda j, k: (k, j))
        o_spec = pl.BlockSpec((None, M, cw), lambda j, k: (j, 0, 0))
        out_shape, acc_shape, steps = (N_CHIPS, M, cw), (M, cw), N_CHIPS

    def body(a_ref, b_ref, o_ref, acc_ref):
        k = pl.program_id(1)
        p = lax.dot_general(a_ref[...].astype(BF16), b_ref[...].astype(BF16), TN, preferred_element_type=F32)

        @pl.when(k == 0)
        def _():
            acc_ref[...] = p

        @pl.when(k > 0)
        def _():
            acc_ref[...] += p

        @pl.when(k == nk - 1)
        def _():
            if by_rows:
                o_ref[0] = acc_ref[:rl, :].astype(o_ref.dtype)
                o_ref[1] = acc_ref[rl:, :].astype(o_ref.dtype)
            else:
                o_ref[...] = acc_ref[...].astype(o_ref.dtype)

    return pl.pallas_call(
        body, name=name, grid=(steps, nk), in_specs=[a_spec, b_spec], out_specs=o_spec,
        out_shape=jax.ShapeDtypeStruct(out_shape, BF16), scratch_shapes=[pltpu.VMEM(acc_shape, F32)],
        compiler_params=_params("parallel", "arbitrary"),
    )(a, b)


def _swiglu_bwd(gate, up, da, name):
    T, F = gate.shape
    tm = _tile(T, 256)

    def body(g_ref, u_ref, da_ref, o_ref):
        g, u, d = g_ref[...].astype(F32), u_ref[...].astype(F32), da_ref[...].astype(F32)
        s = _sigmoid(g)
        o_ref[:, :F] = (d * u * (s * (1.0 + g * (1.0 - s)))).astype(o_ref.dtype)
        o_ref[:, F:] = (d * g * s).astype(o_ref.dtype)

    spec = pl.BlockSpec((tm, F), lambda i: (i, 0))
    return pl.pallas_call(
        body, name=name, grid=(T // tm,), in_specs=[spec] * 3, out_specs=pl.BlockSpec((tm, 2 * F), lambda i: (i, 0)),
        out_shape=jax.ShapeDtypeStruct((T, 2 * F), BF16), compiler_params=_params("parallel"),
    )(gate, up, da)


def _ffn_fwd(x, g_pre, g_post, g_in, g_out, blk, tag):
    n = _rms_fwd(x, g_pre, f"{tag}_rms")
    gate, up, a = _mm_swiglu(n, g_in, blk, f"{tag}_mm_in")
    h = _ffn_mm_out(a, g_out, blk, f"{tag}_mm_out")
    return _post_fwd(x, h, g_post, 0.5, f"{tag}_post"), (x, n, gate, up, a, h)


def _ffn_bwd(dy, saved, g_pre, g_post, g_in, g_out, blk, tag):
    x, n, gate, up, a, h = saved
    dh, dg_post = _post_bwd(dy, h, g_post, 0.5, f"{tag}_post_b")
    da = _ffn_mm_da(dh, g_out, blk, a.shape[1], f"{tag}_mm_da")
    dw_out = _mm_tn_shards(a, dh, True, f"{tag}_mm_dwout")
    dgu = _swiglu_bwd(gate, up, da, f"{tag}_swiglu_b")
    dn = _ffn_mm_dn(dgu, g_in, blk, x.shape[1], f"{tag}_mm_dn")
    dw_in = _mm_tn_shards(n, dgu, False, f"{tag}_mm_dwin")
    dy2, dg_pre = _rms_bwd_add(x, g_pre, dn, dy, f"{tag}_rms_b")
    return dy2, dg_pre, dg_post, dw_in, dw_out


def _shift_down(x, s):
    if s == 0:
        return x
    rows = lax.broadcasted_iota(jnp.int32, x.shape, 0)
    return jnp.where(rows >= s, pltpu.roll(x, s, 0), 0.0)


def _shift_up(x, s):
    if s == 0:
        return x
    T = x.shape[0]
    rows = lax.broadcasted_iota(jnp.int32, x.shape, 0)
    return jnp.where(rows < T - s, pltpu.roll(x, T - s, 0), 0.0)


def _conv_pre(x, w, b):
    y = b
    for k in range(CONV_K):
        y = y + w[k:k + 1, :] * _shift_down(x, CONV_K - 1 - k)
    return y


def _conv_fwd(x, col0, C, w, b, act, name, ct=256):
    T = x.shape[0]
    off = col0 // ct

    def body(x_ref, w_ref, b_ref, o_ref):
        y = _conv_pre(x_ref[...], w_ref[...], b_ref[...])
        o_ref[...] = _silu(y) if act else y

    return pl.pallas_call(
        body, name=name, grid=(C // ct,),
        in_specs=[pl.BlockSpec((T, ct), lambda j: (0, j + off)), pl.BlockSpec((CONV_K, ct), lambda j: (0, j)),
                  pl.BlockSpec((1, ct), lambda j: (0, j))],
        out_specs=pl.BlockSpec((T, ct), lambda j: (0, j)), out_shape=jax.ShapeDtypeStruct((T, C), F32),
        compiler_params=_params("parallel"),
    )(x, w, b)


def _conv_bwd(x, col0, C, w, b, dyact, act, name, ct=256):
    T = x.shape[0]
    off = col0 // ct

    def body(x_ref, w_ref, b_ref, dy_ref, dx_ref, dw_ref, db_ref):
        xv, wv = x_ref[...], w_ref[...]
        dy = dy_ref[...].astype(F32)
        if act:
            pre = _conv_pre(xv, wv, b_ref[...])
            s = _sigmoid(pre)
            dy = dy * (s * (1.0 + pre * (1.0 - s)))
        dx = jnp.zeros_like(dy)
        dws = []
        for k in range(CONV_K):
            dx = dx + wv[k:k + 1, :] * _shift_up(dy, CONV_K - 1 - k)
            dws.append(jnp.sum(dy * _shift_down(xv, CONV_K - 1 - k), axis=0, keepdims=True))
        dx_ref[...] = dx.astype(dx_ref.dtype)
        dw_ref[...] = jnp.concatenate(dws, axis=0)
        db_ref[...] = jnp.sum(dy, axis=0, keepdims=True)

    return pl.pallas_call(
        body, name=name, grid=(C // ct,),
        in_specs=[pl.BlockSpec((T, ct), lambda j: (0, j + off)), pl.BlockSpec((CONV_K, ct), lambda j: (0, j)),
                  pl.BlockSpec((1, ct), lambda j: (0, j)), pl.BlockSpec((T, ct), lambda j: (0, j))],
        out_specs=[pl.BlockSpec((T, ct), lambda j: (0, j)), pl.BlockSpec((CONV_K, ct), lambda j: (0, j)),
                   pl.BlockSpec((1, ct), lambda j: (0, j))],
        out_shape=[jax.ShapeDtypeStruct((T, C), BF16), jax.ShapeDtypeStruct((CONV_K, C), F32),
                   jax.ShapeDtypeStruct((1, C), F32)],
        compiler_params=_params("parallel"),
    )(x, w, b, dyact)


def _gate_act(v, n_ssm):
    lane = lax.broadcasted_iota(jnp.int32, v.shape, 1)
    return jnp.where(lane < n_ssm, _softplus(v), _log_sigmoid(v))


def _gates_fwd(proj, col0, bias, mult, n_ssm, name, tb=512):
    T = proj.shape[0]
    tb = _tile(T, tb)
    off = col0 // LANES

    def body(s_ref, bias_ref, mult_ref, act_ref, cs_ref, carry_ref):
        i = pl.program_id(0)

        @pl.when(i == 0)
        def _():
            carry_ref[...] = jnp.zeros_like(carry_ref)

        act = _gate_act(s_ref[...] + bias_ref[...], n_ssm)
        inc = act * mult_ref[...]
        r = lax.broadcasted_iota(jnp.int32, (tb, tb), 0)
        c = lax.broadcasted_iota(jnp.int32, (tb, tb), 1)
        tri = jnp.where(r >= c, 1.0, 0.0).astype(F32)
        cs = jnp.dot(tri, inc, precision=lax.Precision.HIGHEST, preferred_element_type=F32) + carry_ref[...]
        act_ref[...] = act
        cs_ref[...] = cs
        carry_ref[...] = cs[tb - 1:tb, :]

    return pl.pallas_call(
        body, name=name, grid=(T // tb,),
        in_specs=[pl.BlockSpec((tb, LANES), lambda i: (i, off)), pl.BlockSpec((1, LANES), lambda i: (0, 0)),
                  pl.BlockSpec((1, LANES), lambda i: (0, 0))],
        out_specs=[pl.BlockSpec((tb, LANES), lambda i: (i, 0))] * 2,
        out_shape=[jax.ShapeDtypeStruct((T, LANES), F32)] * 2,
        scratch_shapes=[pltpu.VMEM((1, LANES), F32)], compiler_params=_params("arbitrary"),
    )(proj, bias, mult)


def _gates_bwd(proj, col0, bias, mult, n_ssm, dact, dcs, name, tb=512):
    T = proj.shape[0]
    tb = _tile(T, tb)
    nb = T // tb
    off = col0 // LANES

    def body(s_ref, bias_ref, mult_ref, dact_ref, dcs_ref, ds_ref, dmult_ref, dbias_ref, carry_ref):
        i = pl.program_id(0)

        @pl.when(i == 0)
        def _():
            carry_ref[...] = jnp.zeros_like(carry_ref)
            dmult_ref[...] = jnp.zeros_like(dmult_ref)
            dbias_ref[...] = jnp.zeros_like(dbias_ref)

        v = s_ref[...] + bias_ref[...]
        act = _gate_act(v, n_ssm)
        r = lax.broadcasted_iota(jnp.int32, (tb, tb), 0)
        c = lax.broadcasted_iota(jnp.int32, (tb, tb), 1)
        tri = jnp.where(r <= c, 1.0, 0.0).astype(F32)
        dinc = jnp.dot(tri, dcs_ref[...], precision=lax.Precision.HIGHEST, preferred_element_type=F32) + carry_ref[...]
        carry_ref[...] = dinc[0:1, :]
        da = dact_ref[...] + dinc * mult_ref[...]
        sg = _sigmoid(v)
        lane = lax.broadcasted_iota(jnp.int32, v.shape, 1)
        dv = da * jnp.where(lane < n_ssm, sg, 1.0 - sg)
        ds_ref[...] = dv.astype(ds_ref.dtype)
        dmult_ref[...] += jnp.sum(dinc * act, axis=0, keepdims=True)
        dbias_ref[...] += jnp.sum(dv, axis=0, keepdims=True)

    rev = lambda i: (nb - 1 - i, 0)
    return pl.pallas_call(
        body, name=name, grid=(nb,),
        in_specs=[pl.BlockSpec((tb, LANES), lambda i: (nb - 1 - i, off)), pl.BlockSpec((1, LANES), lambda i: (0, 0)),
                  pl.BlockSpec((1, LANES), lambda i: (0, 0)), pl.BlockSpec((tb, LANES), rev),
                  pl.BlockSpec((tb, LANES), rev)],
        out_specs=[pl.BlockSpec((tb, LANES), rev), pl.BlockSpec((1, LANES), lambda i: (0, 0)),
                   pl.BlockSpec((1, LANES), lambda i: (0, 0))],
        out_shape=[jax.ShapeDtypeStruct((T, LANES), BF16), jax.ShapeDtypeStruct((1, LANES), F32),
                   jax.ShapeDtypeStruct((1, LANES), F32)],
        scratch_shapes=[pltpu.VMEM((1, LANES), F32)], compiler_params=_params("arbitrary"),
    )(proj, bias, mult, dact, dcs)


def _rep_layout(v):
    return jnp.repeat(v, LANES, axis=1)


def _row_layout(v, tk):
    T, H = v.shape
    return v.T.reshape(H, T // tk, 1, tk)


def _causal(tq):
    r = lax.broadcasted_iota(jnp.int32, (tq, tq), 0)
    c = lax.broadcasted_iota(jnp.int32, (tq, tq), 1)
    return r >= c


def _ssd_fwd(xbc, X, cs_rep, cs_row, name, tq=SSD_TILE):
    T = X.shape[0]
    tq = _tile(T, tq)
    nq = T // tq
    d_ssm = X.shape[1]
    gw = d_ssm // SSM_GROUPS
    hpg = gw // SSM_HEAD_DIM
    b_off = d_ssm // SSM_STATE
    c_off = b_off + SSM_GROUPS

    def body(c_ref, b_ref, x_ref, csq_ref, csk_ref, y_ref):
        i = pl.program_id(1)
        c = c_ref[...].astype(BF16)
        half = lax.broadcasted_iota(jnp.int32, (tq, LANES), 1) // SSM_HEAD_DIM
        mask = _causal(tq)

        def step(j, acc, masked):
            r0 = pl.multiple_of(j * tq, tq)
            s = lax.dot_general(c, b_ref[pl.ds(r0, tq), :].astype(BF16), NT, preferred_element_type=F32)
            out = []
            for p in range(hpg // 2):
                xp = x_ref[pl.ds(r0, tq), p * LANES:(p + 1) * LANES]
                a = acc[p]
                for e in range(2):
                    h = 2 * p + e
                    diff = jnp.tile(csq_ref[:, h * LANES:(h + 1) * LANES], (1, tq // LANES)) - csk_ref[h, j]
                    if masked:
                        diff = jnp.where(mask, diff, NEG)
                    pm = (s * jnp.exp(diff)).astype(BF16)
                    xm = jnp.where(half == e, xp, jnp.zeros_like(xp))
                    a = a + jnp.dot(pm, xm, preferred_element_type=F32)
                out.append(a)
            return tuple(out)

        acc = tuple(jnp.zeros((tq, LANES), F32) for _ in range(hpg // 2))
        acc = lax.fori_loop(0, i, lambda j, a: step(j, a, False), acc)
        acc = step(i, acc, True)
        y_ref[...] = jnp.concatenate(acc, axis=1)

    return pl.pallas_call(
        body, name=name, grid=(SSM_GROUPS, nq),
        in_specs=[pl.BlockSpec((tq, SSM_STATE), lambda g, i: (i, c_off + g)),
                  pl.BlockSpec((T, SSM_STATE), lambda g, i: (0, b_off + g)),
                  pl.BlockSpec((T, gw), lambda g, i: (0, g)),
                  pl.BlockSpec((tq, hpg * LANES), lambda g, i: (i, g)),
                  pl.BlockSpec((hpg, nq, 1, tq), lambda g, i: (g, 0, 0, 0))],
        out_specs=pl.BlockSpec((tq, gw), lambda g, i: (i, g)),
        out_shape=jax.ShapeDtypeStruct((T, d_ssm), F32), compiler_params=_params("parallel", "arbitrary"),
    )(xbc, xbc, X, cs_rep, cs_row)


def _ssd_bwd(xbc, X, cs_rep, cs_row, dY, name, tq=SSD_TILE):
    T = X.shape[0]
    tq = _tile(T, tq)
    nq = T // tq
    d_ssm = X.shape[1]
    gw = d_ssm // SSM_GROUPS
    hpg = gw // SSM_HEAD_DIM
    nheads = d_ssm // SSM_HEAD_DIM
    b_off = d_ssm // SSM_STATE
    c_off = b_off + SSM_GROUPS

    def body(c_ref, b_ref, x_ref, dy_ref, csq_ref, csk_ref, dc_ref, db_ref, dx_ref, dcsq_ref, dcsk_ref):
        i = pl.program_id(1)

        @pl.when(i == 0)
        def _():
            db_ref[...] = jnp.zeros_like(db_ref)
            dx_ref[...] = jnp.zeros_like(dx_ref)
            dcsk_ref[...] = jnp.zeros_like(dcsk_ref)

        c = c_ref[...].astype(BF16)
        half = lax.broadcasted_iota(jnp.int32, (tq, LANES), 1) // SSM_HEAD_DIM
        mask = _causal(tq)

        def step(j, carry, masked):
            dc_acc, rows = carry
            rows = list(rows)
            r0 = pl.multiple_of(j * tq, tq)
            b = b_ref[pl.ds(r0, tq), :].astype(BF16)
            s = lax.dot_general(c, b, NT, preferred_element_type=F32)
            ds_tot = jnp.zeros((tq, tq), F32)
            for p in range(hpg // 2):
                cols = slice(p * LANES, (p + 1) * LANES)
                xp = x_ref[pl.ds(r0, tq), cols]
                dyp = dy_ref[:, cols]
                dx_p = jnp.zeros((tq, LANES), F32)
                for e in range(2):
                    h = 2 * p + e
                    diff = jnp.tile(csq_ref[:, h * LANES:(h + 1) * LANES], (1, tq // LANES)) - csk_ref[h, j]
                    if masked:
                        diff = jnp.where(mask, diff, NEG)
                    decay = jnp.exp(diff)
                    dym = jnp.where(half == e, dyp, jnp.zeros_like(dyp))
                    g = lax.dot_general(dym, xp, NT, preferred_element_type=F32) * decay
                    ds_tot = ds_tot + g
                    m = g * s
                    rows[h] = rows[h] + jnp.sum(m, axis=1, keepdims=True)
                    dcsk_ref[h, j] -= jnp.sum(m, axis=0, keepdims=True)
                    pm = (s * decay).astype(BF16)
                    dx_p = dx_p + lax.dot_general(pm, dym, TN, preferred_element_type=F32)
                dx_ref[pl.ds(r0, tq), cols] += dx_p
            dsb = ds_tot.astype(BF16)
            dc_acc = dc_acc + jnp.dot(dsb, b, preferred_element_type=F32)
            db_ref[pl.ds(r0, tq), :] += lax.dot_general(dsb, c, TN, preferred_element_type=F32)
            return dc_acc, tuple(rows)

        carry = (jnp.zeros((tq, SSM_STATE), F32), tuple(jnp.zeros((tq, 1), F32) for _ in range(hpg)))
        carry = lax.fori_loop(0, i, lambda j, cr: step(j, cr, False), carry)
        dc_acc, rows = step(i, carry, True)
        dc_ref[...] = dc_acc
        dcsq_ref[...] = jnp.concatenate([jnp.broadcast_to(r, (tq, LANES)) for r in rows], axis=1)

    return pl.pallas_call(
        body, name=name, grid=(SSM_GROUPS, nq),
        in_specs=[pl.BlockSpec((tq, SSM_STATE), lambda g, i: (i, c_off + g)),
                  pl.BlockSpec((T, SSM_STATE), lambda g, i: (0, b_off + g)),
                  pl.BlockSpec((T, gw), lambda g, i: (0, g)),
                  pl.BlockSpec((tq, gw), lambda g, i: (i, g)),
                  pl.BlockSpec((tq, hpg * LANES), lambda g, i: (i, g)),
                  pl.BlockSpec((hpg, nq, 1, tq), lambda g, i: (g, 0, 0, 0))],
        out_specs=[pl.BlockSpec((tq, SSM_STATE), lambda g, i: (i, g)),
                   pl.BlockSpec((T, SSM_STATE), lambda g, i: (0, g)),
                   pl.BlockSpec((T, gw), lambda g, i: (0, g)),
                   pl.BlockSpec((tq, hpg * LANES), lambda g, i: (i, g)),
                   pl.BlockSpec((hpg, nq, 1, tq), lambda g, i: (g, 0, 0, 0))],
        out_shape=[jax.ShapeDtypeStruct((T, SSM_GROUPS * SSM_STATE), F32),
                   jax.ShapeDtypeStruct((T, SSM_GROUPS * SSM_STATE), F32),
                   jax.ShapeDtypeStruct((T, d_ssm), F32),
                   jax.ShapeDtypeStruct((T, nheads * LANES), F32),
                   jax.ShapeDtypeStruct((nheads, nq, 1, tq), F32)],
        compiler_params=_params("arbitrary", "arbitrary"),
    )(xbc, xbc, X, dY, cs_rep, cs_row)


def _ssd2_fwd(xbc, X, XK, cs_rep, cs_row, cs_full, r_end, name, tq=SSD_TILE):
    T = X.shape[0]
    tq = _tile(T, tq)
    nq = T // tq
    d_ssm = X.shape[1]
    gw = d_ssm // SSM_GROUPS
    hpg = gw // SSM_HEAD_DIM
    b_off = d_ssm // SSM_STATE
    c_off = b_off + SSM_GROUPS

    def body(c_ref, b_ref, x_ref, xk_ref, csq_ref, csk_ref, csf_ref, rend_ref, y_ref):
        i = pl.program_id(1)
        c = c_ref[...].astype(BF16)
        csf = csf_ref[...]
        r = csf[0:1, :]

        def off(j, acc):
            r0 = pl.multiple_of(j * tq, tq)
            s = lax.dot_general(c, b_ref[pl.ds(r0, tq), :].astype(BF16), NT, preferred_element_type=F32)
            z = jnp.dot(s.astype(BF16), xk_ref[pl.ds(r0, tq), :], preferred_element_type=F32)
            return acc + z * jnp.exp(r - rend_ref[j])

        y_off = jnp.exp(csf - r) * lax.fori_loop(0, i, off, jnp.zeros((tq, gw), F32))

        r0 = pl.multiple_of(i * tq, tq)
        half = lax.broadcasted_iota(jnp.int32, (tq, LANES), 1) // SSM_HEAD_DIM
        mask = _causal(tq)
        s = lax.dot_general(c, b_ref[pl.ds(r0, tq), :].astype(BF16), NT, preferred_element_type=F32)
        out = []
        for p in range(hpg // 2):
            xp = x_ref[:, p * LANES:(p + 1) * LANES]
            a = jnp.zeros((tq, LANES), F32)
            for e in range(2):
                h = 2 * p + e
                diff = jnp.tile(csq_ref[:, h * LANES:(h + 1) * LANES], (1, tq // LANES)) - csk_ref[h, i]
                pm = (s * jnp.exp(jnp.where(mask, diff, NEG))).astype(BF16)
                xm = jnp.where(half == e, xp, jnp.zeros_like(xp))
                a = a + jnp.dot(pm, xm, preferred_element_type=F32)
            out.append(a)
        y_ref[...] = y_off + jnp.concatenate(out, axis=1)

    return pl.pallas_call(
        body, name=name, grid=(SSM_GROUPS, nq),
        in_specs=[pl.BlockSpec((tq, SSM_STATE), lambda g, i: (i, c_off + g)),
                  pl.BlockSpec((T, SSM_STATE), lambda g, i: (0, b_off + g)),
                  pl.BlockSpec((tq, gw), lambda g, i: (i, g)),
                  pl.BlockSpec((T, gw), lambda g, i: (0, g)),
                  pl.BlockSpec((tq, hpg * LANES), lambda g, i: (i, g)),
                  pl.BlockSpec((hpg, nq, 1, tq), lambda g, i: (g, 0, 0, 0)),
                  pl.BlockSpec((tq, gw), lambda g, i: (i, g)),
                  pl.BlockSpec((nq, 1, gw), lambda g, i: (0, 0, g))],
        out_specs=pl.BlockSpec((tq, gw), lambda g, i: (i, g)),
        out_shape=jax.ShapeDtypeStruct((T, d_ssm), F32), compiler_params=_params("parallel", "arbitrary"),
    )(xbc, xbc, X, XK, cs_rep, cs_row, cs_full, r_end)


def _ssd2_bwd(xbc, X, XK, cs_rep, cs_row, cs_full, r_end, dY, name, tq=SSD_TILE):
    T = X.shape[0]
    tq = _tile(T, tq)
    nq = T // tq
    d_ssm = X.shape[1]
    gw = d_ssm // SSM_GROUPS
    hpg = gw // SSM_HEAD_DIM
    nheads = d_ssm // SSM_HEAD_DIM
    b_off = d_ssm // SSM_STATE
    c_off = b_off + SSM_GROUPS

    def body(c_ref, b_ref, x_ref, xk_ref, dy_ref, csq_ref, csk_ref, csf_ref, rend_ref,
             dc_ref, db_ref, dx_ref, dxk_ref, dcsq_ref, dcsk_ref, dcsf_ref):
        i = pl.program_id(1)

        @pl.when(i == 0)
        def _():
            db_ref[...] = jnp.zeros_like(db_ref)
            dxk_ref[...] = jnp.zeros_like(dxk_ref)

        c = c_ref[...].astype(BF16)
        csf = csf_ref[...]
        r = csf[0:1, :]
        dyt = dy_ref[...].astype(F32) * jnp.exp(csf - r)

        def off(j, carry):
            dc_acc, rs_acc = carry
            r0 = pl.multiple_of(j * tq, tq)
            b = b_ref[pl.ds(r0, tq), :].astype(BF16)
            xk = xk_ref[pl.ds(r0, tq), :]
            sb = lax.dot_general(c, b, NT, preferred_element_type=F32).astype(BF16)
            dye = dyt * jnp.exp(r - rend_ref[j])
            dyb = dye.astype(BF16)
            rs_acc = rs_acc + dyb.astype(F32) * jnp.dot(sb, xk, preferred_element_type=F32)
            dxk_ref[pl.ds(r0, tq), :] += lax.dot_general(sb, dyb, TN, preferred_element_type=F32)
            dsb = lax.dot_general(dyb, xk, NT, preferred_element_type=F32).astype(BF16)
            dc_acc = dc_acc + jnp.dot(dsb, b, preferred_element_type=F32)
            db_ref[pl.ds(r0, tq), :] += lax.dot_general(dsb, c, TN, preferred_element_type=F32)
            return dc_acc, rs_acc

        dc_acc, rs_acc = lax.fori_loop(0, i, off, (jnp.zeros((tq, SSM_STATE), F32), jnp.zeros((tq, gw), F32)))
        dcsf_ref[...] = rs_acc

        r0 = pl.multiple_of(i * tq, tq)
        half = lax.broadcasted_iota(jnp.int32, (tq, LANES), 1) // SSM_HEAD_DIM
        mask = _causal(tq)
        b = b_ref[pl.ds(r0, tq), :].astype(BF16)
        s = lax.dot_general(c, b, NT, preferred_element_type=F32)
        ds_tot = jnp.zeros((tq, tq), F32)
        rows, dxs = [], []
        for p in range(hpg // 2):
            cols = slice(p * LANES, (p + 1) * LANES)
            xp = x_ref[:, cols]
            dyp = dy_ref[:, cols]
            dx_p = jnp.zeros((tq, LANES), F32)
            for e in range(2):
                h = 2 * p + e
                diff = jnp.tile(csq_ref[:, h * LANES:(h + 1) * LANES], (1, tq // LANES)) - csk_ref[h, i]
                decay = jnp.exp(jnp.where(mask, diff, NEG))
                dym = jnp.where(half == e, dyp, jnp.zeros_like(dyp))
                g = lax.dot_general(dym, xp, NT, preferred_element_type=F32) * decay
                ds_tot = ds_tot + g
                m = g * s
                rows.append(jnp.broadcast_to(jnp.sum(m, axis=1, keepdims=True), (tq, LANES)))
                dcsk_ref[h, i] = -jnp.sum(m, axis=0, keepdims=True)
                dx_p = dx_p + lax.dot_general((s * decay).astype(BF16), dym, TN, preferred_element_type=F32)
            dxs.append(dx_p)
        dsb = ds_tot.astype(BF16)
        dc_ref[...] = dc_acc + jnp.dot(dsb, b, preferred_element_type=F32)
        db_ref[pl.ds(r0, tq), :] += lax.dot_general(dsb, c, TN, preferred_element_type=F32)
        dx_ref[...] = jnp.concatenate(dxs, axis=1)
        dcsq_ref[...] = jnp.concatenate(rows, axis=1)

    return pl.pallas_call(
        body, name=name, grid=(SSM_GROUPS, nq),
        in_specs=[pl.BlockSpec((tq, SSM_STATE), lambda g, i: (i, c_off + g)),
                  pl.BlockSpec((T, SSM_STATE), lambda g, i: (0, b_off + g)),
                  pl.BlockSpec((tq, gw), lambda g, i: (i, g)),
                  pl.BlockSpec((T, gw), lambda g, i: (0, g)),
                  pl.BlockSpec((tq, gw), lambda g, i: (i, g)),
                  pl.BlockSpec((tq, hpg * LANES), lambda g, i: (i, g)),
                  pl.BlockSpec((hpg, nq, 1, tq), lambda g, i: (g, 0, 0, 0)),
                  pl.BlockSpec((tq, gw), lambda g, i: (i, g)),
                  pl.BlockSpec((nq, 1, gw), lambda g, i: (0, 0, g))],
        out_specs=[pl.BlockSpec((tq, SSM_STATE), lambda g, i: (i, g)),
                   pl.BlockSpec((T, SSM_STATE), lambda g, i: (0, g)),
                   pl.BlockSpec((tq, gw), lambda g, i: (i, g)),
                   pl.BlockSpec((T, gw), lambda g, i: (0, g)),
                   pl.BlockSpec((tq, hpg * LANES), lambda g, i: (i, g)),
                   pl.BlockSpec((hpg, nq, 1, tq), lambda g, i: (g, 0, 0, 0)),
                   pl.BlockSpec((tq, gw), lambda g, i: (i, g))],
        out_shape=[jax.ShapeDtypeStruct((T, SSM_GROUPS * SSM_STATE), F32),
                   jax.ShapeDtypeStruct((T, SSM_GROUPS * SSM_STATE), F32),
                   jax.ShapeDtypeStruct((T, d_ssm), F32),
                   jax.ShapeDtypeStruct((T, d_ssm), F32),
                   jax.ShapeDtypeStruct((T, nheads * LANES), F32),
                   jax.ShapeDtypeStruct((nheads, nq, 1, tq), F32),
                   jax.ShapeDtypeStruct((T, d_ssm), F32)],
        compiler_params=_params("arbitrary", "arbitrary"),
    )(xbc, xbc, X, XK, dY, cs_rep, cs_row, cs_full, r_end)


def _fox_fwd(proj, q0, k0, v0, nh, cum_rep, cum_row, name, tq=FOX_TILE):
    T = proj.shape[0]
    tq = _tile(T, tq)
    nq = T // tq
    hd = FOX_HEAD_DIM
    scale = hd ** -0.5
    qo, ko, vo = q0 // hd, k0 // hd, v0 // hd

    def body(q_ref, k_ref, v_ref, cq_ref, ck_ref, o_ref, lse_ref):
        i = pl.program_id(1)
        q = q_ref[...].astype(BF16)
        cq = jnp.tile(cq_ref[...], (1, tq // LANES))
        mask = _causal(tq)

        def step(j, carry, masked):
            m, l, acc = carry
            r0 = pl.multiple_of(j * tq, tq)
            k = k_ref[pl.ds(r0, tq), :].astype(BF16)
            v = v_ref[pl.ds(r0, tq), :].astype(BF16)
            s = lax.dot_general(q, k, NT, preferred_element_type=F32) * scale + cq - ck_ref[j]
            if masked:
                s = jnp.where(mask, s, NEG)
            m_new = jnp.maximum(m, jnp.max(s, axis=1, keepdims=True))
            alpha = jnp.exp(m - m_new)
            p = jnp.exp(s - m_new)
            l = alpha * l + jnp.sum(p, axis=1, keepdims=True)
            acc = alpha * acc + jnp.dot(p.astype(BF16), v, preferred_element_type=F32)
            return m_new, l, acc

        carry = (jnp.full((tq, 1), NEG, F32), jnp.zeros((tq, 1), F32), jnp.zeros((tq, hd), F32))
        carry = lax.fori_loop(0, i, lambda j, cr: step(j, cr, False), carry)
        m, l, acc = step(i, carry, True)
        o_ref[...] = (acc / l).astype(o_ref.dtype)
        lse_ref[...] = jnp.broadcast_to(m + jnp.log(l), (tq, LANES))

    return pl.pallas_call(
        body, name=name, grid=(nh, nq),
        in_specs=[pl.BlockSpec((tq, hd), lambda h, i: (i, qo + h)), pl.BlockSpec((T, hd), lambda h, i: (0, ko + h)),
                  pl.BlockSpec((T, hd), lambda h, i: (0, vo + h)), pl.BlockSpec((tq, LANES), lambda h, i: (i, h)),
                  pl.BlockSpec((None, nq, 1, tq), lambda h, i: (h, 0, 0, 0))],
        out_specs=[pl.BlockSpec((tq, hd), lambda h, i: (i, h)), pl.BlockSpec((tq, LANES), lambda h, i: (i, h))],
        out_shape=[jax.ShapeDtypeStruct((T, nh * hd), BF16), jax.ShapeDtypeStruct((T, nh * LANES), F32)],
        compiler_params=_params("parallel", "arbitrary"),
    )(proj, proj, proj, cum_rep, cum_row)


def _fox_bwd(proj, q0, k0, v0, nh, cum_rep, cum_row, o, lse, dcat, do0, name, tq=FOX_TILE):
    T = proj.shape[0]
    tq = _tile(T, tq)
    nq = T // tq
    hd = FOX_HEAD_DIM
    scale = hd ** -0.5
    qo, ko, vo, doo = q0 // hd, k0 // hd, v0 // hd, do0 // hd

    def body(q_ref, k_ref, v_ref, do_ref, o_ref, lse_ref, cq_ref, ck_ref, dq_ref, dk_ref, dv_ref, dck_ref, dcq_ref):
        i = pl.program_id(1)

        @pl.when(i == 0)
        def _():
            dk_ref[...] = jnp.zeros_like(dk_ref)
            dv_ref[...] = jnp.zeros_like(dv_ref)
            dck_ref[...] = jnp.zeros_like(dck_ref)

        q = q_ref[...].astype(BF16)
        do = do_ref[...].astype(F32)
        dob = do.astype(BF16)
        delta = jnp.sum(do * o_ref[...].astype(F32), axis=1, keepdims=True)
        bias = jnp.tile(cq_ref[...] - lse_ref[...], (1, tq // LANES))
        mask = _causal(tq)

        def step(j, carry, masked):
            dq, rows = carry
            r0 = pl.multiple_of(j * tq, tq)
            k = k_ref[pl.ds(r0, tq), :].astype(BF16)
            v = v_ref[pl.ds(r0, tq), :].astype(BF16)
            s = lax.dot_general(q, k, NT, preferred_element_type=F32) * scale + bias - ck_ref[j]
            if masked:
                s = jnp.where(mask, s, NEG)
            p = jnp.exp(s)
            dp = lax.dot_general(dob, v, NT, preferred_element_type=F32)
            ds = p * (dp - delta)
            dsb = ds.astype(BF16)
            dq = dq + jnp.dot(dsb, k, preferred_element_type=F32) * scale
            dk_ref[pl.ds(r0, tq), :] += lax.dot_general(dsb, q, TN, preferred_element_type=F32) * scale
            dv_ref[pl.ds(r0, tq), :] += lax.dot_general(p.astype(BF16), dob, TN, preferred_element_type=F32)
            dck_ref[j] -= jnp.sum(ds, axis=0, keepdims=True)
            return dq, rows + jnp.sum(ds, axis=1, keepdims=True)

        carry = (jnp.zeros((tq, hd), F32), jnp.zeros((tq, 1), F32))
        carry = lax.fori_loop(0, i, lambda j, cr: step(j, cr, False), carry)
        dq, rows = step(i, carry, True)
        dq_ref[...] = dq.astype(dq_ref.dtype)
        dcq_ref[...] = jnp.broadcast_to(rows, (tq, LANES))

    return pl.pallas_call(
        body, name=name, grid=(nh, nq),
        in_specs=[pl.BlockSpec((tq, hd), lambda h, i: (i, qo + h)), pl.BlockSpec((T, hd), lambda h, i: (0, ko + h)),
                  pl.BlockSpec((T, hd), lambda h, i: (0, vo + h)), pl.BlockSpec((tq, hd), lambda h, i: (i, doo + h)),
                  pl.BlockSpec((tq, hd), lambda h, i: (i, h)), pl.BlockSpec((tq, LANES), lambda h, i: (i, h)),
                  pl.BlockSpec((tq, LANES), lambda h, i: (i, h)),
                  pl.BlockSpec((None, nq, 1, tq), lambda h, i: (h, 0, 0, 0))],
        out_specs=[pl.BlockSpec((tq, hd), lambda h, i: (i, h)), pl.BlockSpec((T, hd), lambda h, i: (0, h)),
                   pl.BlockSpec((T, hd), lambda h, i: (0, h)), pl.BlockSpec((None, nq, 1, tq), lambda h, i: (h, 0, 0, 0)),
                   pl.BlockSpec((tq, LANES), lambda h, i: (i, h))],
        out_shape=[jax.ShapeDtypeStruct((T, nh * hd), BF16), jax.ShapeDtypeStruct((T, nh * hd), F32),
                   jax.ShapeDtypeStruct((T, nh * hd), F32), jax.ShapeDtypeStruct((nh, nq, 1, tq), F32),
                   jax.ShapeDtypeStruct((T, nh * LANES), F32)],
        compiler_params=_params("arbitrary", "arbitrary"),
    )(proj, proj, proj, dcat, o, lse, cum_rep, cum_row)


def _scan_fwd(a, u, name, ct=256):
    T, C = a.shape

    def body(a_ref, u_ref, h_ref):
        def blk(tb, h):
            r0 = pl.multiple_of(tb * SUBLANES, SUBLANES)
            ab, ub = a_ref[pl.ds(r0, SUBLANES), :], u_ref[pl.ds(r0, SUBLANES), :]
            rows = []
            for r in range(SUBLANES):
                h = ab[r:r + 1, :] * h + ub[r:r + 1, :]
                rows.append(h)
            h_ref[pl.ds(r0, SUBLANES), :] = jnp.concatenate(rows, axis=0)
            return h

        lax.fori_loop(0, T // SUBLANES, blk, jnp.zeros((1, ct), F32))

    spec = pl.BlockSpec((T, ct), lambda j: (0, j))
    return pl.pallas_call(body, name=name, grid=(C // ct,), in_specs=[spec, spec], out_specs=spec,
                          out_shape=jax.ShapeDtypeStruct((T, C), F32), compiler_params=_params("parallel"))(a, u)


def _scan_bwd(a, dh, h, name, ct=128):
    T, C = a.shape
    nb = T // SUBLANES

    def body(a_ref, dh_ref, h_ref, g_ref, da_ref):
        def blk(t, carry):
            r0 = pl.multiple_of((nb - 1 - t) * SUBLANES, SUBLANES)
            ab, db = a_ref[pl.ds(r0, SUBLANES), :], dh_ref[pl.ds(r0, SUBLANES), :]
            rows = [None] * SUBLANES
            for r in range(SUBLANES - 1, -1, -1):
                g = db[r:r + 1, :] + carry
                carry = ab[r:r + 1, :] * g
                rows[r] = g
            g_ref[pl.ds(r0, SUBLANES), :] = jnp.concatenate(rows, axis=0)
            return carry

        lax.fori_loop(0, nb, blk, jnp.zeros((1, ct), F32))
        da_ref[...] = g_ref[...] * _shift_down(h_ref[...], 1)

    spec = pl.BlockSpec((T, ct), lambda j: (0, j))
    return pl.pallas_call(body, name=name, grid=(C // ct,), in_specs=[spec] * 3, out_specs=[spec] * 2,
                          out_shape=[jax.ShapeDtypeStruct((T, C), F32)] * 2,
                          compiler_params=_params("parallel"))(a, dh, h)


def _lru_elem(xc, ra, ia, lam):
    r, i = _sigmoid(ra), _sigmoid(ia)
    log_a = RG_LRU_C * r * _log_sigmoid(lam)
    return jnp.exp(log_a), jnp.sqrt(_neg_expm1(2.0 * log_a)) * (i * xc)


def _lru_gates_fwd(xc, w_a, b_a, w_x, b_x, lam, name):
    def fn(xv, ba, bx, lm, wa, wx):
        xb = xv.astype(BF16)
        ra = jnp.dot(xb, wa.astype(BF16), preferred_element_type=F32) + ba
        ia = jnp.dot(xb, wx.astype(BF16), preferred_element_type=F32) + bx
        return _lru_elem(xv, ra, ia, lm)
    nb = xc.shape[1] // RNN_BLOCK
    return _ew(fn, [(xc, 0)], [(b_a, 0), (b_x, 0), (lam, 0), (w_a, 0), (w_x, 0)], [F32, F32], [],
               name=name, tm=512, cb=RNN_BLOCK, ncb=nb)


def _lru_gates_bwd(xc, w_a, b_a, w_x, b_x, lam, da, du, name):
    def fn(xv, dav, duv, ba, bx, lm, wa, wx):
        xb, wab, wxb = xv.astype(BF16), wa.astype(BF16), wx.astype(BF16)
        ra = jnp.dot(xb, wab, preferred_element_type=F32) + ba
        ia = jnp.dot(xb, wxb, preferred_element_type=F32) + bx
        _, vjp = jax.vjp(_lru_elem, xv, ra, ia, lm)
        dx, dra, dia, dlm = vjp((dav, duv))
        drb, dib = dra.astype(BF16), dia.astype(BF16)
        dx = dx + lax.dot_general(drb, wab, NT, preferred_element_type=F32)
        dx = dx + lax.dot_general(dib, wxb, NT, preferred_element_type=F32)
        dwa = lax.dot_general(xb, drb, TN, preferred_element_type=F32)
        dwx = lax.dot_general(xb, dib, TN, preferred_element_type=F32)
        return (dx, jnp.sum(dra, axis=0, keepdims=True), jnp.sum(dia, axis=0, keepdims=True), dlm, dwa, dwx)
    nb = xc.shape[1] // RNN_BLOCK
    return _ew(fn, [(xc, 0), (da, 0), (du, 0)], [(b_a, 0), (b_x, 0), (lam, 0), (w_a, 0), (w_x, 0)], [F32],
               [b_a.shape, b_x.shape, lam.shape, w_a.shape, w_x.shape], name=name, tm=512, cb=RNN_BLOCK, ncb=nb)


def _hyb_cols(D):
    conv = D + 2 * SSM_GROUPS * SSM_STATE
    z0, x0, q0 = 0, D, D + conv
    return dict(z=z0, xbc=x0, q=q0, k=q0 + D, v=q0 + 2 * D, small=q0 + 3 * D, total=q0 + 3 * D + LANES, conv=conv)


def _hyb_w_in_reorder(w, D):
    cols = _hyb_cols(D)
    nh_s, nh_f = D // SSM_HEAD_DIM, D // FOX_HEAD_DIM
    a = D + cols["conv"]
    pad = jnp.zeros((w.shape[0], LANES - nh_s - nh_f), w.dtype)
    return jnp.concatenate([w[:, :a], w[:, a + nh_s:a + nh_s + 3 * D], w[:, a:a + nh_s], w[:, a + nh_s + 3 * D:], pad], axis=1)


def _hyb_w_in_restore(dw, D):
    cols = _hyb_cols(D)
    nh_s, nh_f = D // SSM_HEAD_DIM, D // FOX_HEAD_DIM
    a = D + cols["conv"]
    s = cols["small"]
    return jnp.concatenate([dw[:, :a], dw[:, s:s + nh_s], dw[:, a:s], dw[:, s + nh_s:s + nh_s + nh_f]], axis=1)


def _ssm_out(Y, xs, z, dfull, ng):
    y = (Y + dfull * xs) * _silu(z)
    return y * lax.rsqrt(jnp.mean(y * y, axis=-1, keepdims=True) + NORM_EPS) * ng


def _hyb_fwd(x, g_pre, g_post, p, tag):
    T, D = x.shape
    cols = _hyb_cols(D)
    nh_s, nh_f = D // SSM_HEAD_DIM, D // FOX_HEAD_DIM
    n = _rms_fwd(x, g_pre, f"{tag}_rms")
    proj = _mm(n, p["w_in"], "nn", F32, f"{tag}_mm_in", tn=1152, tk=1024)
    a_neg = -jnp.exp(p["a_log"])
    bias = jnp.concatenate([p["dt_bias"], p["b_f"], jnp.zeros((LANES - nh_s - nh_f,), F32)])[None]
    mult = jnp.concatenate([a_neg, jnp.ones((nh_f,), F32), jnp.zeros((LANES - nh_s - nh_f,), F32)])[None]
    act, cs = _gates_fwd(proj, cols["small"], bias, mult, nh_s, f"{tag}_gates")
    dt, cs_s, cum = act[:, :nh_s], cs[:, :nh_s], cs[:, nh_s:nh_s + nh_f]
    dtf = jnp.repeat(dt, SSM_HEAD_DIM, axis=1)
    cs_rep, cs_row = _rep_layout(cs_s), _row_layout(cs_s, _tile(T, SSD_TILE))
    cum_rep, cum_row = _rep_layout(cum), _row_layout(cum, _tile(T, FOX_TILE))
    xbc = _conv_fwd(proj, cols["xbc"], cols["conv"], p["conv_w"], p["conv_b"], True, f"{tag}_conv")
    tqs = _tile(T, SSD_TILE)
    cs_full = jnp.repeat(cs_s, SSM_HEAD_DIM, axis=1)
    r_end = cs_full.reshape(T // tqs, tqs, D)[:, tqs - 1:, :]
    r_exp = jnp.broadcast_to(r_end, (T // tqs, tqs, D)).reshape(T, D)
    X, XK = _ew(lambda xv, dv, cv, rv: (xv * dv, xv * dv * jnp.exp(rv - cv)),
                [(xbc, 0), (dtf, 0), (cs_full, 0), (r_exp, 0)], [], [BF16, BF16], [], name=f"{tag}_xdt",
                cb=512, ncb=D // 512)
    Y = _ssd2_fwd(xbc, X, XK, cs_rep, cs_row, cs_full, r_end, f"{tag}_ssd")
    dfull = jnp.repeat(p["d"], SSM_HEAD_DIM)[None]
    gw = D // SSM_GROUPS
    y_ssm = _ew(_ssm_out, [(Y, 0), (xbc, 0), (proj, cols["z"] // gw)], [(dfull, 0), (p["norm_g"], 0)], [BF16], [],
                name=f"{tag}_ssm_out", cb=gw, ncb=SSM_GROUPS)
    o, lse = _fox_fwd(proj, cols["q"], cols["k"], cols["v"], nh_f, cum_rep, cum_row, f"{tag}_fox")
    mix = _mm_parts([(y_ssm, p["w_out"][:D]), (o, p["w_out"][D:])], "nn", F32, f"{tag}_mm_out", tn=1024)
    x2 = _post_fwd(x, mix, g_post, 1.0, f"{tag}_post")
    return x2, (x, n, proj, bias, mult, dtf, cs_rep, cs_row, cum_rep, cum_row, xbc, X, Y, dfull, o, lse, y_ssm, mix,
                XK, cs_full, r_end, r_exp)


def _hyb_bwd(dy, saved, g_pre, g_post, p, tag):
    (x, n, proj, bias, mult, dtf, cs_rep, cs_row, cum_rep, cum_row, xbc, X, Y, dfull, o, lse, y_ssm, mix,
     XK, cs_full, r_end, r_exp) = saved
    T, D = x.shape
    cols = _hyb_cols(D)
    nh_s, nh_f = D // SSM_HEAD_DIM, D // FOX_HEAD_DIM
    gw = D // SSM_GROUPS
    dmix, dg_post = _post_bwd(dy, mix, g_post, 1.0, f"{tag}_post_b")
    dcat = _mm(dmix, p["w_out"], "nt", BF16, f"{tag}_mm_dcat", tn=1024, tk=1024)
    dw_out = jnp.concatenate([_mm(y_ssm, dmix, "tn", BF16, f"{tag}_mm_dwout_s", tm=1024, tn=1024, tk=1024),
                              _mm(o, dmix, "tn", BF16, f"{tag}_mm_dwout_f", tm=1024, tn=1024, tk=1024)], axis=0)

    def ssm_out_b(Yv, xv, zv, dv, dfv, ngv):
        _, vjp = jax.vjp(_ssm_out, Yv, xv, zv, dfv, ngv)
        return vjp(dv.astype(F32))
    dY, dxs_skip, dz, ddfull, dng = _ew(
        ssm_out_b, [(Y, 0), (xbc, 0), (proj, cols["z"] // gw), (dcat, 0)], [(dfull, 0), (p["norm_g"], 0)],
        [BF16, F32, BF16], [dfull.shape, p["norm_g"].shape], name=f"{tag}_ssm_out_b", cb=gw, ncb=SSM_GROUPS)
    dC, dB, dXd, dXK, dcs_q, dcs_k, dcs_f = _ssd2_bwd(xbc, X, XK, cs_rep, cs_row, cs_full, r_end, dY, f"{tag}_ssd_b")
    def xdt_b(dXdv, dXKv, skv, xv, dv, cv, rv, xkv):
        dX = dXdv + dXKv * jnp.exp(rv - cv)
        return dX * dv + skv, dX * xv, dXKv * xkv.astype(F32)
    dxs, ddtf, dcs_kf = _ew(
        xdt_b, [(dXd, 0), (dXK, 0), (dxs_skip, 0), (xbc, 0), (dtf, 0), (cs_full, 0), (r_exp, 0), (XK, 0)],
        [], [F32, F32, F32], [], name=f"{tag}_xdt_b", cb=512, ncb=D // 512)
    ddt = ddtf.reshape(T, nh_s, SSM_HEAD_DIM).sum(-1)
    dcs_s = (dcs_q[:, ::LANES] + dcs_k.reshape(nh_s, T).T
             + (dcs_f - dcs_kf).reshape(T, nh_s, SSM_HEAD_DIM).sum(-1))
    dq, dk, dv, dcum_k, dcum_q = _fox_bwd(proj, cols["q"], cols["k"], cols["v"], nh_f, cum_rep, cum_row, o, lse, dcat, D,
                                  f"{tag}_fox_b")
    dcum = dcum_q[:, ::LANES] + dcum_k.reshape(nh_f, T).T
    zpad = jnp.zeros((T, LANES - nh_s - nh_f), F32)
    dact = jnp.concatenate([ddt, jnp.zeros((T, nh_f), F32), zpad], axis=1)
    dcs = jnp.concatenate([dcs_s, dcum, zpad], axis=1)
    dsmall, dmult, dbias = _gates_bwd(proj, cols["small"], bias, mult, nh_s, dact, dcs, f"{tag}_gates_b")
    dxbc_act = jnp.concatenate([dxs, dB, dC], axis=1)
    dxbc, dconv_w, dconv_b = _conv_bwd(proj, cols["xbc"], cols["conv"], p["conv_w"], p["conv_b"], dxbc_act, True,
                                       f"{tag}_conv_b")
    pieces = [(dz, "z"), (dxbc, "xbc"), (dq, "q"), (dk, "k"), (dv, "v"), (dsmall, "small")]
    w_cols = lambda d, key: p["w_in"][:, cols[key]:cols[key] + d.shape[1]]
    dn = _mm_parts([(d, w_cols(d, key)) for d, key in pieces], "nt", F32, f"{tag}_mm_dn", tm=256, tn=512)
    dw_in = jnp.concatenate([_mm(n, d, "tn", BF16, f"{tag}_mm_dwin_{key}", tm=1024, tn=1024, tk=1024)
                             for d, key in pieces], axis=1)
    dy2, dg_pre = _rms_bwd_add(x, g_pre, dn, dy, f"{tag}_rms_b")
    grads = dict(w_in=dw_in, w_out=dw_out, conv_w=dconv_w, conv_b=dconv_b[0], dt_bias=dbias[0, :nh_s],
                 a_log=dmult[0, :nh_s] * mult[0, :nh_s], d=ddfull.reshape(nh_s, SSM_HEAD_DIM).sum(-1),
                 norm_g=dng[0], b_f=dbias[0, nh_s:nh_s + nh_f])
    return dy2, dg_pre, dg_post, grads


def _rec_fwd(x, g_pre, g_post, p, tag):
    T, D = x.shape
    n = _rms_fwd(x, g_pre, f"{tag}_rms")
    pr = _mm(n, p["w_in"], "nn", F32, f"{tag}_mm_in", tn=1024, tk=1024)
    xc = _conv_fwd(pr, D, D, p["conv_w"], p["conv_b"], False, f"{tag}_conv")
    a, u = _lru_gates_fwd(xc, p["w_a"], p["b_a"], p["w_x"], p["b_x"], p["lam"], f"{tag}_lru")
    hs = _scan_fwd(a, u, f"{tag}_scan")
    og = _ew(lambda hv, gv: hv * _gelu_tanh(gv), [(hs, 0), (pr, 0)], [], [BF16], [], name=f"{tag}_gate", cb=D)
    mix = _mm(og, p["w_out"], "nn", F32, f"{tag}_mm_out", tn=1024, tk=1024)
    x2 = _post_fwd(x, mix, g_post, 1.0, f"{tag}_post")
    return x2, (x, n, pr, xc, a, hs, og, mix)


def _rec_bwd(dy, saved, g_pre, g_post, p, tag):
    x, n, pr, xc, a, hs, og, mix = saved
    T, D = x.shape
    dmix, dg_post = _post_bwd(dy, mix, g_post, 1.0, f"{tag}_post_b")
    dog = _mm(dmix, p["w_out"], "nt", F32, f"{tag}_mm_dog", tn=1024, tk=1024)
    dw_out = _mm(og, dmix, "tn", BF16, f"{tag}_mm_dwout", tm=1024, tn=1024, tk=1024)

    def gate_b(hv, gv, dv):
        _, vjp = jax.vjp(lambda h_, g_: h_ * _gelu_tanh(g_), hv, gv)
        return vjp(dv)
    dhs, dgate = _ew(gate_b, [(hs, 0), (pr, 0), (dog, 0)], [], [F32, BF16], [], name=f"{tag}_gate_b", cb=D)
    du, da = _scan_bwd(a, dhs, hs, f"{tag}_scan_b")
    dxc, db_a, db_x, dlam, dw_a, dw_x = _lru_gates_bwd(xc, p["w_a"], p["b_a"], p["w_x"], p["b_x"], p["lam"], da, du,
                                                       f"{tag}_lru_b")
    dxr, dconv_w, dconv_b = _conv_bwd(pr, D, D, p["conv_w"], p["conv_b"], dxc, False, f"{tag}_conv_b")
    dn = _mm_parts([(dgate, p["w_in"][:, :D]), (dxr, p["w_in"][:, D:])], "nt", F32, f"{tag}_mm_dn", tn=1024)
    dw_in = jnp.concatenate([_mm(n, dgate, "tn", BF16, f"{tag}_mm_dwin_g", tm=1024, tn=1024, tk=1024),
                             _mm(n, dxr, "tn", BF16, f"{tag}_mm_dwin_x", tm=1024, tn=1024, tk=1024)], axis=1)
    dy2, dg_pre = _rms_bwd_add(x, g_pre, dn, dy, f"{tag}_rms_b")
    grads = dict(w_in=dw_in, w_out=dw_out, conv_w=dconv_w, conv_b=dconv_b[0], w_a=dw_a, b_a=db_a[0], w_x=dw_x,
                 b_x=db_x[0], lam=dlam[0])
    return dy2, dg_pre, dg_post, grads


def _hyb_params(W, i, D):
    return dict(w_in=_hyb_w_in_reorder(W["hyb_w_in"][i], D), w_out=W["hyb_w_out"][i], conv_w=W["ssm_conv_w"][i],
                conv_b=W["ssm_conv_b"][i][None], dt_bias=W["ssm_dt_bias"][i], a_log=W["ssm_a_log"][i],
                d=W["ssm_d"][i], norm_g=W["ssm_norm_g"][i][None], b_f=W["fox_b_f"][i])


def _rec_params(W, j):
    return dict(w_in=W["rec_w_in"][j], w_out=W["rec_w_out"][j], conv_w=W["rec_conv_w"][j],
                conv_b=W["rec_conv_b"][j][None], w_a=W["rec_w_a"][j], b_a=W["rec_b_a"][j][None],
                w_x=W["rec_w_x"][j], b_x=W["rec_b_x"][j][None], lam=W["rec_lambda"][j][None])


def _local_step(x, target, W):
    T, D = x.shape
    depth = W["norm_g"].shape[0]
    g = lambda l, k: W["norm_g"][l, k][None]
    saved = []
    for l in range(depth):
        x, s0 = _ffn_fwd(x, g(l, 0), g(l, 1), W["ffn_w_in"], W["ffn_w_out"], 2 * l, f"l{l}_ffn0")
        if l % 2 == 0:
            pm = _hyb_params(W, l // 2, D)
            x, s1 = _hyb_fwd(x, g(l, 2), g(l, 3), pm, f"l{l}_hyb")
        else:
            pm = _rec_params(W, l // 2)
            x, s1 = _rec_fwd(x, g(l, 2), g(l, 3), pm, f"l{l}_rec")
        x, s2 = _ffn_fwd(x, g(l, 4), g(l, 5), W["ffn_w_in"], W["ffn_w_out"], 2 * l + 1, f"l{l}_ffn1")
        saved.append((s0, s1, s2, pm))

    def loss_fn(yv, tv):
        err = yv - tv
        part = 0.5 * jnp.sum(jnp.sum(err * err, axis=1, keepdims=True), axis=0, keepdims=True) / D
        return err * (1.0 / D), jnp.broadcast_to(part, (1, D))
    dy, loss_row = _ew(loss_fn, [(x, 0), (target, 0)], [], [F32], [(1, D)], name="loss")
    loss = loss_row[0, 0]

    gn = [[None] * 6 for _ in range(depth)]
    g_ffn_in = [[None, None] for _ in range(depth)]
    g_ffn_out = [[None, None] for _ in range(depth)]
    g_hyb, g_rec = [], []
    for l in reversed(range(depth)):
        s0, s1, s2, pm = saved[l]
        dy, gn[l][4], gn[l][5], g_ffn_in[l][1], g_ffn_out[l][1] = _ffn_bwd(
            dy, s2, g(l, 4), g(l, 5), W["ffn_w_in"], W["ffn_w_out"], 2 * l + 1, f"l{l}_ffn1")
        if l % 2 == 0:
            dy, gn[l][2], gn[l][3], gm = _hyb_bwd(dy, s1, g(l, 2), g(l, 3), pm, f"l{l}_hyb")
            g_hyb.insert(0, gm)
        else:
            dy, gn[l][2], gn[l][3], gm = _rec_bwd(dy, s1, g(l, 2), g(l, 3), pm, f"l{l}_rec")
            g_rec.insert(0, gm)
        dy, gn[l][0], gn[l][1], g_ffn_in[l][0], g_ffn_out[l][0] = _ffn_bwd(
            dy, s0, g(l, 0), g(l, 1), W["ffn_w_in"], W["ffn_w_out"], 2 * l, f"l{l}_ffn0")

    st = lambda items: jnp.stack(items)
    grads = {
        "norm_g": st([st([r[0] for r in row]) for row in gn]),
        "ffn_w_in": [piece for row in g_ffn_in for piece in row],
        "ffn_w_out": [piece for row in g_ffn_out for piece in row],
        "hyb_w_in": st([_hyb_w_in_restore(m["w_in"], D) for m in g_hyb]),
        "ssm_conv_w": st([m["conv_w"] for m in g_hyb]), "ssm_conv_b": st([m["conv_b"] for m in g_hyb]),
        "ssm_dt_bias": st([m["dt_bias"] for m in g_hyb]), "ssm_a_log": st([m["a_log"] for m in g_hyb]),
        "ssm_d": st([m["d"] for m in g_hyb]), "ssm_norm_g": st([m["norm_g"] for m in g_hyb]),
        "fox_b_f": st([m["b_f"] for m in g_hyb]), "hyb_w_out": st([m["w_out"] for m in g_hyb]),
        "rec_w_in": st([m["w_in"] for m in g_rec]), "rec_conv_w": st([m["conv_w"] for m in g_rec]),
        "rec_conv_b": st([m["conv_b"] for m in g_rec]), "rec_w_a": st([m["w_a"] for m in g_rec]),
        "rec_b_a": st([m["b_a"] for m in g_rec]), "rec_w_x": st([m["w_x"] for m in g_rec]),
        "rec_b_x": st([m["b_x"] for m in g_rec]), "rec_lambda": st([m["lam"] for m in g_rec]),
        "rec_w_out": st([m["w_out"] for m in g_rec]),
    }
    return loss, dy, grads


MESH_AXES = ("x", "y", "c")
N_CHIPS = 4
N_DEV = 8
HBM = pl.BlockSpec(memory_space=pltpu.HBM)


def _mesh_pos():
    return tuple(lax.axis_index(n) for n in MESH_AXES)


def _other_chips(x, y):
    chips = [(1 - x, y), (x, 1 - y), (1 - x, 1 - y)]
    return chips, [2 * cx + cy for cx, cy in chips]


def _rcopy(src, dst, send_sem, recv_sem, dev):
    return pltpu.make_async_remote_copy(src_ref=src, dst_ref=dst, send_sem=send_sem, recv_sem=recv_sem,
                                        device_id=dev, device_id_type=pl.DeviceIdType.MESH)


def _comm_params():
    return pltpu.CompilerParams()


DMA_CHUNK_BYTES = 1 << 20
DMA_ROW_ALIGN = 16


def _nchunks(rows, row_bytes):
    n = max(1, min(rows // DMA_ROW_ALIGN, (rows * row_bytes) // DMA_CHUNK_BYTES))
    while n > 1 and (rows % n or (rows // n) % DMA_ROW_ALIGN):
        n -= 1
    return n


def _row_bytes(ref):
    return ref.shape[-1] * jnp.dtype(ref.dtype).itemsize


def _all_gather(arrs, name):
    n = len(arrs)

    def body(*refs):
        ins, outs = refs[:n], refs[n:2 * n]
        send, recv, fsend, frecv = refs[2 * n:]
        x, y, c = _mesh_pos()
        k = 2 * x + y
        sibling = (x, y, 1 - c)
        chips, chip_k = _other_chips(x, y)
        halves = [r.shape[0] // 2 for r in ins]
        nchs = [_nchunks(h, _row_bytes(r)) for h, r in zip(halves, ins)]
        for a in range(n):
            h, step = halves[a], halves[a] // nchs[a]
            for j, chip in enumerate(chips):
                for q in range(nchs[a]):
                    rows = pl.ds(c * h + q * step, step)
                    _rcopy(ins[a].at[rows], outs[a].at[k, rows], send.at[a, j], recv.at[a, j], (*chip, c)).start()
        for a in range(n):
            h, step = halves[a], halves[a] // nchs[a]
            for j, chip in enumerate(chips):
                blk = outs[a].at[chip_k[j], pl.ds(c * h, h)]
                _rcopy(blk, blk, send.at[a, j], recv.at[a, j], (*chip, c)).wait_recv()
                for q in range(nchs[a]):
                    part = outs[a].at[chip_k[j], pl.ds(c * h + q * step, step)]
                    _rcopy(part, part, fsend.at[a, j], frecv.at[a, j], sibling).start()
        for a in range(n):
            h = halves[a]
            for j, chip in enumerate(chips):
                blk = outs[a].at[chip_k[j], pl.ds((1 - c) * h, h)]
                _rcopy(blk, blk, fsend.at[a, j], frecv.at[a, j], sibling).wait_recv()
        for a in range(n):
            h = halves[a]
            for j, chip in enumerate(chips):
                _rcopy(ins[a].at[pl.ds(c * h, h)], outs[a].at[k, pl.ds(c * h, h)], send.at[a, j], recv.at[a, j],
                       (*chip, c)).wait_send()
                blk = outs[a].at[chip_k[j], pl.ds(c * h, h)]
                _rcopy(blk, blk, fsend.at[a, j], frecv.at[a, j], sibling).wait_send()

    outs = pl.pallas_call(
        body, name=name, in_specs=[HBM] * n, out_specs=[HBM] * n,
        out_shape=[jax.ShapeDtypeStruct((N_CHIPS,) + a.shape, a.dtype) for a in arrs],
        scratch_shapes=[pltpu.SemaphoreType.DMA((n, 3))] * 4, compiler_params=_comm_params(),
    )(*arrs)
    k = 2 * lax.axis_index("x") + lax.axis_index("y")
    return [lax.dynamic_update_index_in_dim(o, a, k, 0) for o, a in zip(outs, arrs)]


def _pair_exchange(gs, name):
    n = len(gs)

    def body(*refs):
        ins, theirs = refs[:n], refs[n:2 * n]
        send, recv = refs[2 * n:]
        x, y, c = _mesh_pos()
        sibling = (x, y, 1 - c)
        for a in range(n):
            h = ins[a].shape[1] // 2
            nch = _nchunks(h, _row_bytes(ins[a]))
            step = h // nch
            for kk in range(N_CHIPS):
                for q in range(nch):
                    _rcopy(ins[a].at[kk, pl.ds((1 - c) * h + q * step, step)], theirs[a].at[kk, pl.ds(q * step, step)],
                           send.at[a], recv.at[a], sibling).start()
        for a in range(n):
            h = ins[a].shape[1] // 2
            _rcopy(ins[a].at[:, pl.ds((1 - c) * h, h)], theirs[a], send.at[a], recv.at[a], sibling).wait()

    half = [jax.ShapeDtypeStruct((a.shape[0], a.shape[1] // 2) + a.shape[2:], a.dtype) for a in gs]
    theirs = pl.pallas_call(
        body, name=name, in_specs=[HBM] * n, out_specs=[HBM] * n, out_shape=half,
        scratch_shapes=[pltpu.SemaphoreType.DMA((n,))] * 2, compiler_params=_comm_params(),
    )(*gs)
    c = lax.axis_index("c")
    mine = [lax.dynamic_slice_in_dim(g, c * (g.shape[1] // 2), g.shape[1] // 2, axis=1) for g in gs]
    return mine, theirs


def _chip_exchange(ss, name):
    n = len(ss)

    def body(*refs):
        ins = refs[:n]
        got = [refs[n + 3 * a:n + 3 * a + 3] for a in range(n)]
        send, recv = refs[4 * n:]
        x, y, c = _mesh_pos()
        chips, chip_k = _other_chips(x, y)
        for a in range(n):
            h = ins[a].shape[1]
            nch = _nchunks(h, _row_bytes(ins[a]))
            step = h // nch
            for q in range(nch):
                rows = pl.ds(q * step, step)
                for j, chip in enumerate(chips):
                    _rcopy(ins[a].at[chip_k[j], rows], got[a][j].at[rows], send.at[a, j], recv.at[a, j],
                           (*chip, c)).start()
        for a in range(n):
            for j, chip in enumerate(chips):
                _rcopy(ins[a].at[chip_k[j]], got[a][j], send.at[a, j], recv.at[a, j], (*chip, c)).wait()

    got = [jax.ShapeDtypeStruct(a.shape[1:], a.dtype) for a in ss for _ in range(3)]
    res = pl.pallas_call(
        body, name=name, in_specs=[HBM] * n, out_specs=[HBM] * (3 * n), out_shape=got,
        scratch_shapes=[pltpu.SemaphoreType.DMA((n, 3))] * 2, compiler_params=_comm_params(),
    )(*ss)
    k = 2 * lax.axis_index("x") + lax.axis_index("y")
    own = [lax.dynamic_index_in_dim(s, k, 0, keepdims=False) for s in ss]
    return own, [res[3 * a:3 * a + 3] for a in range(n)]


def _pair_share(rs, name):
    n = len(rs)

    def body(*refs):
        ins, outs = refs[:n], refs[n:2 * n]
        send, recv = refs[2 * n:]
        x, y, c = _mesh_pos()
        sibling = (x, y, 1 - c)
        for a in range(n):
            h = ins[a].shape[0]
            nch = _nchunks(h, _row_bytes(ins[a]))
            step = h // nch
            for q in range(nch):
                rows = pl.ds(q * step, step)
                _rcopy(ins[a].at[rows], outs[a].at[rows], send.at[a], recv.at[a], sibling).start()
        for a in range(n):
            _rcopy(ins[a], outs[a], send.at[a], recv.at[a], sibling).wait()

    theirs = pl.pallas_call(
        body, name=name, in_specs=[HBM] * n, out_specs=[HBM] * n,
        out_shape=[jax.ShapeDtypeStruct(a.shape, a.dtype) for a in rs],
        scratch_shapes=[pltpu.SemaphoreType.DMA((n,))] * 2, compiler_params=_comm_params(),
    )(*rs)
    c = lax.axis_index("c")
    return [jnp.concatenate([jnp.where(c == 0, r, t), jnp.where(c == 0, t, r)], axis=0) for r, t in zip(rs, theirs)]


def _exchange_all(vec, name):
    def body(v_ref, out_ref, send, recv):
        x, y, c = _mesh_pos()
        flip = lambda p, f: 1 - p if f else p
        me = 4 * x + 2 * y + c
        rows_all = v_ref.shape[0]
        nch = _nchunks(rows_all, _row_bytes(v_ref))
        step = rows_all // nch
        for j in range(1, N_DEV):
            fx, fy, fc = (j >> 2) & 1, (j >> 1) & 1, j & 1
            for q in range(nch):
                rows = pl.ds(q * step, step)
                _rcopy(v_ref.at[rows], out_ref.at[me, rows], send.at[j - 1], recv.at[j - 1],
                       (flip(x, fx), flip(y, fy), flip(c, fc))).start()
        for j in range(1, N_DEV):
            fx, fy, fc = (j >> 2) & 1, (j >> 1) & 1, j & 1
            slot = out_ref.at[4 * flip(x, fx) + 2 * flip(y, fy) + flip(c, fc)]
            _rcopy(slot, slot, send.at[j - 1], recv.at[j - 1], (x, y, c)).wait_recv()
        for j in range(1, N_DEV):
            _rcopy(v_ref, out_ref.at[me], send.at[j - 1], recv.at[j - 1], (x, y, c)).wait_send()

    out = pl.pallas_call(
        body, name=name, in_specs=[HBM], out_specs=HBM,
        out_shape=jax.ShapeDtypeStruct((N_DEV,) + vec.shape, vec.dtype),
        scratch_shapes=[pltpu.SemaphoreType.DMA((N_DEV - 1,))] * 2, compiler_params=_comm_params(),
    )(vec)
    me = 4 * lax.axis_index("x") + 2 * lax.axis_index("y") + lax.axis_index("c")
    return lax.dynamic_update_index_in_dim(out, vec, me, 0)


def _rows(a):
    return a.reshape(-1, a.shape[-1])


def _sum_kernel(parts, out_dtype, name):
    def fn(*vals):
        acc = vals[0].astype(F32)
        for v in vals[1:]:
            acc = acc + v.astype(F32)
        return acc
    out = _ew(fn, [(_rows(p), 0) for p in parts], [], [out_dtype], [], name=name)
    return out.reshape(parts[0].shape)


def _reduce_scatter(gs, tag):
    mine, theirs = _pair_exchange(gs, f"{tag}_pair")
    pair = [_sum_kernel([m, t], BF16, f"{tag}_add_pair{a}") for a, (m, t) in enumerate(zip(mine, theirs))]
    own, got = _chip_exchange(pair, f"{tag}_chips")
    red = [_sum_kernel([o, g[0], g[1], g[2]], F32, f"{tag}_add_chips{a}") for a, (o, g) in enumerate(zip(own, got))]
    return _pair_share(red, f"{tag}_share")


def _pack(arrs, row_mult):
    flat = jnp.concatenate([a.reshape(-1).astype(F32) for a in arrs])
    unit = row_mult * LANES
    pad = (-flat.size) % unit
    return jnp.pad(flat, (0, pad)).reshape(-1, LANES)


def _unpack(mat, shapes):
    flat, out, pos = mat.reshape(-1), [], 0
    for s in shapes:
        size = 1
        for d in s:
            size *= d
        out.append(flat[pos:pos + size].reshape(s))
        pos += size
    return out


def _to_shards(a, axis):
    sh = a.shape
    a = a.reshape(sh[:axis] + (N_CHIPS, sh[axis] // N_CHIPS) + sh[axis + 1:])
    return jnp.moveaxis(a, axis, 0)


def _from_shards(g, axis):
    g = jnp.moveaxis(g, 0, axis)
    sh = g.shape
    return g.reshape(sh[:axis] + (sh[axis] * sh[axis + 1],) + sh[axis + 2:])


def _adamw_fn(w, g, m, v):
    m2 = ADAM_B1 * m + (1.0 - ADAM_B1) * g
    v2 = ADAM_B2 * v + (1.0 - ADAM_B2) * (g * g)
    m_hat = m2 / (1.0 - ADAM_B1 ** ADAM_STEP)
    v_hat = v2 / (1.0 - ADAM_B2 ** ADAM_STEP)
    return -ADAM_LR * (m_hat / (jnp.sqrt(v_hat) + ADAM_EPS) + ADAM_WD * w), m2, v2


def _adamw(w, g, m, v, name):
    res = _ew(_adamw_fn, [(_rows(a), 0) for a in (w, g, m, v)], [], [F32, F32, F32], [], name=name)
    return tuple(r.reshape(w.shape) for r in res)


def kernel(x, norm_g, ffn_w_in, ffn_w_out, hyb_w_in, ssm_conv_w, ssm_conv_b, ssm_dt_bias, ssm_a_log, ssm_d, ssm_norm_g, fox_b_f, hyb_w_out, rec_w_in, rec_conv_w, rec_conv_b, rec_w_a, rec_b_a, rec_w_x, rec_b_x, rec_lambda, rec_w_out, loss_target, m_norm_g, m_ffn_w_in, m_ffn_w_out, m_hyb_w_in, m_ssm_conv_w, m_ssm_conv_b, m_ssm_dt_bias, m_ssm_a_log, m_ssm_d, m_ssm_norm_g, m_fox_b_f, m_hyb_w_out, m_rec_w_in, m_rec_conv_w, m_rec_conv_b, m_rec_w_a, m_rec_b_a, m_rec_w_x, m_rec_b_x, m_rec_lambda, m_rec_w_out, v_norm_g, v_ffn_w_in, v_ffn_w_out, v_hyb_w_in, v_ssm_conv_w, v_ssm_conv_b, v_ssm_dt_bias, v_ssm_a_log, v_ssm_d, v_ssm_norm_g, v_fox_b_f, v_hyb_w_out, v_rec_w_in, v_rec_conv_w, v_rec_conv_b, v_rec_w_a, v_rec_b_a, v_rec_w_x, v_rec_b_x, v_rec_lambda, v_rec_w_out):
    given = dict(locals())
    w = {n: given[n] for n in WEIGHTS}
    m = {n: given["m_" + n] for n in WEIGHTS}
    v = {n: given["v_" + n] for n in WEIGHTS}
    k = 2 * lax.axis_index("x") + lax.axis_index("y")

    big_bf16 = [_ew(lambda t: t, [(_rows(w[n]), 0)], [], [BF16], [], name=f"cast_{n}") for n in BIG]
    small_shapes = [w[n].shape for n in SMALL_SHARDED]
    small_pack = _pack([w[n] for n in SMALL_SHARDED], 2 * SUBLANES)
    gathered = _all_gather(big_bf16 + [small_pack], "gather_weights")
    W = {n: w[n] for n in SMALL_REPL}
    for n, g in zip(BIG, gathered[:-1]):
        W[n] = g if n in FFN else _from_shards(g.reshape((N_CHIPS,) + w[n].shape), SHARD_AXIS[n])
    per_chip = [_unpack(gathered[-1][kk], small_shapes) for kk in range(N_CHIPS)]
    for idx, n in enumerate(SMALL_SHARDED):
        W[n] = jnp.concatenate([per_chip[kk][idx] for kk in range(N_CHIPS)], axis=SHARD_AXIS[n])

    loss_part, dy, grads = _local_step(x[0], loss_target[0], W)
    loss = lax.psum(loss_part, MESH_AXES)

    pieces = {n: grads[n] if n in FFN else [_to_shards(grads[n], SHARD_AXIS[n]).reshape(N_CHIPS, -1, w[n].shape[-1])]
              for n in BIG}
    red_big = _reduce_scatter([piece for n in BIG for piece in pieces[n]], "rs")
    g_out, pos = {}, 0
    for n in BIG:
        cnt = len(pieces[n])
        g_out[n] = jnp.concatenate(red_big[pos:pos + cnt], axis=0).reshape(w[n].shape)
        pos += cnt
    small_names = SMALL_SHARDED + SMALL_REPL
    slots = _exchange_all(_pack([grads[n] for n in small_names], SUBLANES), "gather_small_grads")
    small_sum = _sum_kernel([slots[d] for d in range(N_DEV)], F32, "add_small_grads")
    for n, g in zip(small_names, _unpack(small_sum, [grads[n].shape for n in small_names])):
        if n in SHARD_AXIS:
            loc = g.shape[SHARD_AXIS[n]] // N_CHIPS
            g = lax.dynamic_slice_in_dim(g, k * loc, loc, axis=SHARD_AXIS[n])
        g_out[n] = g

    delta, new_m, new_v = {}, {}, {}
    for n in BIG:
        delta[n], new_m[n], new_v[n] = _adamw(w[n], g_out[n], m[n], v[n], f"adamw_{n}")
    shapes = [w[n].shape for n in small_names]
    packed = [_pack([d[n] for n in small_names], SUBLANES) for d in (w, g_out, m, v)]
    for d, mat in zip((delta, new_m, new_v), _adamw(*packed, "adamw_small")):
        d.update(zip(small_names, _unpack(mat, shapes)))

    return (loss, dy[None], *[g_out[n] for n in WEIGHTS], *[delta[n] for n in WEIGHTS],
            *[new_m[n] for n in WEIGHTS], *[new_v[n] for n in WEIGHTS])
```

```python
import functools

import jax
import jax.numpy as jnp
from jax import lax
from jax.experimental import pallas as pl
from jax.experimental.pallas import tpu as pltpu

F32, BF16 = jnp.float32, jnp.bfloat16

NORM_EPS = 1e-6
CONV_K = 4
SSM_HEAD_DIM = 64
SSM_STATE = 128
SSM_GROUPS = 2
FOX_HEAD_DIM = 128
RNN_BLOCK = 128
RG_LRU_C = 8.0
ADAM_LR, ADAM_B1, ADAM_B2, ADAM_EPS, ADAM_WD, ADAM_STEP = 0.001, 0.9, 0.999, 1e-08, 0.01, 10

LANES = 128
SUBLANES = 8
VMEM_LIMIT = 48 * 1024 * 1024
SSD_TILE = 256
FOX_TILE = 512
NEG = -1e30
LOG2E = 1.4426950408889634

NT = (((1,), (1,)), ((), ()))
NN = (((1,), (0,)), ((), ()))
TN = (((0,), (0,)), ((), ()))

BIG = ("ffn_w_in", "ffn_w_out", "hyb_w_in", "hyb_w_out", "rec_w_in", "rec_w_out")
FFN = ("ffn_w_in", "ffn_w_out")
SMALL_SHARDED = ("norm_g", "ssm_conv_w", "rec_conv_w", "rec_conv_b", "rec_b_a", "rec_b_x", "rec_lambda")
SMALL_REPL = ("ssm_conv_b", "ssm_dt_bias", "ssm_a_log", "ssm_d", "ssm_norm_g", "fox_b_f", "rec_w_a", "rec_w_x")
WEIGHTS = ("norm_g", "ffn_w_in", "ffn_w_out", "hyb_w_in", "ssm_conv_w", "ssm_conv_b", "ssm_dt_bias", "ssm_a_log",
           "ssm_d", "ssm_norm_g", "fox_b_f", "hyb_w_out", "rec_w_in", "rec_conv_w", "rec_conv_b", "rec_w_a",
           "rec_b_a", "rec_w_x", "rec_b_x", "rec_lambda", "rec_w_out")
SHARD_AXIS = {"norm_g": 2, "ffn_w_in": 3, "ffn_w_out": 2, "hyb_w_in": 2, "ssm_conv_w": 2, "hyb_w_out": 1,
              "rec_w_in": 2, "rec_conv_w": 2, "rec_conv_b": 1, "rec_b_a": 1, "rec_b_x": 1, "rec_lambda": 1,
              "rec_w_out": 1}


def _params(*sem):
    return pltpu.CompilerParams(dimension_semantics=sem if sem else None, vmem_limit_bytes=VMEM_LIMIT)


def _tile(dim, pref):
    if dim <= pref:
        return dim
    for align in (LANES, SUBLANES):
        t = (pref // align) * align
        while t >= align:
            if dim % t == 0:
                return t
            t -= align
    return dim


def _sigmoid(x):
    return 1.0 / (1.0 + jnp.exp(-x))


def _softplus(x):
    return jnp.maximum(x, 0.0) + jnp.log(1.0 + jnp.exp(-jnp.abs(x)))


def _log_sigmoid(x):
    return -_softplus(-x)


def _silu(x):
    return x * _sigmoid(x)


def _gelu_tanh(x):
    return 0.5 * x * (1.0 + jnp.tanh(0.7978845608028654 * (x + 0.044715 * x * x * x)))


def _neg_expm1(x):
    series = -x * (1.0 + x * (0.5 + x * (1.0 / 6.0)))
    return jnp.where(x > -1e-2, series, 1.0 - jnp.exp(x))


def _rms(x, g):
    xf = x.astype(F32)
    return xf * lax.rsqrt(jnp.mean(xf * xf, axis=-1, keepdims=True) + NORM_EPS) * g


def _mm(a, b, mode, out_dtype, name, tm=512, tn=512, tk=512, b_sel=()):
    bshape = b.shape[len(b_sel):]
    if mode == "nn":
        (M, K), (K2, N) = a.shape, bshape
    elif mode == "nt":
        (M, K), (N, K2) = a.shape, bshape
    else:
        (K, M), (K2, N) = a.shape, bshape
    assert K == K2, (a.shape, b.shape, mode)
    tm, tn, tk = _tile(M, tm), _tile(N, tn), _tile(K, tk)
    nk = K // tk
    dims = {"nn": NN, "nt": NT, "tn": TN}[mode]
    lead = (None,) * len(b_sel)
    if mode == "tn":
        a_spec = pl.BlockSpec((tk, tm), lambda n, m, k: (k, m))
    else:
        a_spec = pl.BlockSpec((tm, tk), lambda n, m, k: (m, k))
    if mode == "nt":
        b_spec = pl.BlockSpec(lead + (tn, tk), lambda n, m, k: (*b_sel, n, k))
    else:
        b_spec = pl.BlockSpec(lead + (tk, tn), lambda n, m, k: (*b_sel, k, n))

    def body(a_ref, b_ref, o_ref, *acc):
        p = lax.dot_general(a_ref[...].astype(BF16), b_ref[...].astype(BF16), dims, preferred_element_type=F32)
        if nk == 1:
            o_ref[...] = p.astype(o_ref.dtype)
        else:
            acc_ref, = acc
            k = pl.program_id(2)

            @pl.when(k == 0)
            def _():
                acc_ref[...] = p

            @pl.when(k > 0)
            def _():
                acc_ref[...] += p

            @pl.when(k == nk - 1)
            def _():
                o_ref[...] = acc_ref[...].astype(o_ref.dtype)

    return pl.pallas_call(
        body, name=name, grid=(N // tn, M // tm, nk),
        in_specs=[a_spec, b_spec], out_specs=pl.BlockSpec((tm, tn), lambda n, m, k: (m, n)),
        out_shape=jax.ShapeDtypeStruct((M, N), out_dtype),
        scratch_shapes=[pltpu.VMEM((tm, tn), F32)] if nk > 1 else [],
        compiler_params=_params("parallel", "parallel", "arbitrary"),
    )(a, b)


def _mm_parts(parts, mode, out_dtype, name, tm=512, tn=512):
    M = parts[0][0].shape[0]
    N = parts[0][1].shape[1] if mode == "nn" else parts[0][1].shape[0]
    tm, tn = _tile(M, tm), _tile(N, tn)
    dims = NN if mode == "nn" else NT
    in_specs, args = [], []
    for a, b in parts:
        K = a.shape[1]
        in_specs.append(pl.BlockSpec((tm, K), lambda n, m: (m, 0)))
        if mode == "nn":
            in_specs.append(pl.BlockSpec((K, tn), lambda n, m: (0, n)))
        else:
            in_specs.append(pl.BlockSpec((tn, K), lambda n, m: (n, 0)))
        args += [a, b]

    def body(*refs):
        o_ref = refs[-1]
        acc = None
        for p in range(len(parts)):
            d = lax.dot_general(refs[2 * p][...].astype(BF16), refs[2 * p + 1][...].astype(BF16), dims,
                                preferred_element_type=F32)
            acc = d if acc is None else acc + d
        o_ref[...] = acc.astype(o_ref.dtype)

    return pl.pallas_call(
        body, name=name, grid=(N // tn, M // tm), in_specs=in_specs,
        out_specs=pl.BlockSpec((tm, tn), lambda n, m: (m, n)), out_shape=jax.ShapeDtypeStruct((M, N), out_dtype),
        compiler_params=_params("parallel", "parallel"),
    )(*args)


def _ew(fn, tiled, params, outs, reds, *, name, tm=256, cb=None, ncb=1):
    T = tiled[0][0].shape[0]
    cb = tiled[0][0].shape[1] if cb is None else cb
    tm = _tile(T, tm)
    in_specs, args = [], []
    for arr, off in tiled:
        in_specs.append(pl.BlockSpec((tm, cb), functools.partial(lambda n, i, o: (i, n + o), o=off)))
        args.append(arr)
    for arr, off in params:
        if arr.ndim == 2:
            in_specs.append(pl.BlockSpec((arr.shape[0], cb), functools.partial(lambda n, i, o: (0, n + o), o=off)))
        else:
            in_specs.append(pl.BlockSpec((None,) + arr.shape[1:], lambda n, i: (n, 0, 0)))
        args.append(arr)
    out_shape = [jax.ShapeDtypeStruct((T, cb * ncb), dt) for dt in outs]
    out_specs = [pl.BlockSpec((tm, cb), lambda n, i: (i, n)) for _ in outs]
    for shape in reds:
        out_shape.append(jax.ShapeDtypeStruct(shape, F32))
        if len(shape) == 2:
            out_specs.append(pl.BlockSpec((shape[0], cb), lambda n, i: (0, n)))
        else:
            out_specs.append(pl.BlockSpec((None,) + tuple(shape[1:]), lambda n, i: (n, 0, 0)))
    n_in, n_out = len(args), len(outs)

    def body(*refs):
        i = pl.program_id(1)
        res = fn(*[r[...] for r in refs[:n_in]])
        res = res if isinstance(res, (tuple, list)) else (res,)
        for r, v in zip(refs[n_in:n_in + n_out], res[:n_out]):
            r[...] = v.astype(r.dtype)
        for r, v in zip(refs[n_in + n_out:], res[n_out:]):
            @pl.when(i == 0)
            def _():
                r[...] = jnp.zeros(r.shape, r.dtype)

            r[...] += v.astype(r.dtype).reshape(r.shape)

    res = pl.pallas_call(
        body, name=name, grid=(ncb, T // tm), in_specs=in_specs, out_specs=out_specs, out_shape=out_shape,
        compiler_params=_params("arbitrary", "arbitrary"),
    )(*args)
    return res[0] if len(res) == 1 else tuple(res)


def _rms_fwd(x, g, name):
    return _ew(lambda xv, gv: _rms(xv, gv), [(x, 0)], [(g, 0)], [BF16], [], name=name)


def _rms_bwd_add(x, g, dn, dres, name):
    def fn(xv, dnv, drv, gv):
        _, vjp = jax.vjp(_rms, xv, gv)
        dx, dg = vjp(dnv.astype(F32))
        return drv + dx, dg
    return _ew(fn, [(x, 0), (dn, 0), (dres, 0)], [(g, 0)], [F32], [g.shape], name=name)


def _post_fwd(x, h, g, w, name):
    return _ew(lambda xv, hv, gv: xv + w * _rms(hv, gv), [(x, 0), (h, 0)], [(g, 0)], [F32], [], name=name)


def _post_bwd(dy, h, g, w, name):
    def fn(dyv, hv, gv):
        _, vjp = jax.vjp(lambda a, b: w * _rms(a, b), hv, gv)
        return vjp(dyv)
    return _ew(fn, [(dy, 0), (h, 0)], [(g, 0)], [BF16], [g.shape], name=name)


def _mm_swiglu(x, g_in, blk, name, tm=512):
    T, K = x.shape
    tn = g_in.shape[-1]
    F = 2 * tn
    tm = _tile(T, tm)

    def body(a_ref, bg_ref, bu_ref, g_ref, u_ref, act_ref):
        a = a_ref[...].astype(BF16)
        g = jnp.dot(a, bg_ref[...].astype(BF16), preferred_element_type=F32)
        u = jnp.dot(a, bu_ref[...].astype(BF16), preferred_element_type=F32)
        g_ref[...] = g.astype(g_ref.dtype)
        u_ref[...] = u.astype(u_ref.dtype)
        act_ref[...] = (_silu(g) * u).astype(act_ref.dtype)

    out = jax.ShapeDtypeStruct((T, F), BF16)
    o_spec = pl.BlockSpec((tm, tn), lambda n, m: (m, n))
    return pl.pallas_call(
        body, name=name, grid=(2, T // tm),
        in_specs=[pl.BlockSpec((tm, K), lambda n, m: (m, 0)),
                  pl.BlockSpec((None, K, tn), lambda n, m: (n, blk, 0)),
                  pl.BlockSpec((None, K, tn), lambda n, m: (n + 2, blk, 0))],
        out_specs=[o_spec] * 3, out_shape=[out] * 3, compiler_params=_params("parallel", "parallel"),
    )(x, g_in, g_in)


def _ffn_mm_out(a, g_out, blk, name, tm=512):
    T, F = a.shape
    rl, D = F // N_CHIPS, g_out.shape[-1]
    tm = _tile(T, tm)

    def body(a_ref, b0_ref, b1_ref, o_ref, acc_ref):
        k = pl.program_id(1)
        b = jnp.concatenate([b0_ref[...], b1_ref[...]], axis=0).astype(BF16)
        p = jnp.dot(a_ref[...].astype(BF16), b, preferred_element_type=F32)

        @pl.when(k == 0)
        def _():
            acc_ref[...] = p

        @pl.when(k == 1)
        def _():
            o_ref[...] = acc_ref[...] + p

    return pl.pallas_call(
        body, name=name, grid=(T // tm, 2),
        in_specs=[pl.BlockSpec((tm, 2 * rl), lambda m, k: (m, k)),
                  pl.BlockSpec((None, rl, D), lambda m, k: (2 * k, blk, 0)),
                  pl.BlockSpec((None, rl, D), lambda m, k: (2 * k + 1, blk, 0))],
        out_specs=pl.BlockSpec((tm, D), lambda m, k: (m, 0)), out_shape=jax.ShapeDtypeStruct((T, D), F32),
        scratch_shapes=[pltpu.VMEM((tm, D), F32)], compiler_params=_params("parallel", "arbitrary"),
    )(a, g_out, g_out)


def _ffn_mm_da(dh, g_out, blk, F, name, tm=512):
    T, D = dh.shape
    rl = F // N_CHIPS
    tm = _tile(T, tm)

    def body(a_ref, b0_ref, b1_ref, o_ref):
        b = jnp.concatenate([b0_ref[...], b1_ref[...]], axis=0).astype(BF16)
        o_ref[...] = lax.dot_general(a_ref[...].astype(BF16), b, NT, preferred_element_type=F32).astype(o_ref.dtype)

    return pl.pallas_call(
        body, name=name, grid=(2, T // tm),
        in_specs=[pl.BlockSpec((tm, D), lambda n, m: (m, 0)),
                  pl.BlockSpec((None, rl, D), lambda n, m: (2 * n, blk, 0)),
                  pl.BlockSpec((None, rl, D), lambda n, m: (2 * n + 1, blk, 0))],
        out_specs=pl.BlockSpec((tm, 2 * rl), lambda n, m: (m, n)), out_shape=jax.ShapeDtypeStruct((T, F), BF16),
        compiler_params=_params("parallel", "parallel"),
    )(dh, g_out, g_out)


def _ffn_mm_dn(dgu, g_in, blk, D, name, tm=512):
    T = dgu.shape[0]
    cw = g_in.shape[-1]
    tm = _tile(T, tm)

    def body(a_ref, b0_ref, b1_ref, o_ref, acc_ref):
        k = pl.program_id(1)
        a = a_ref[...].astype(BF16)
        p = lax.dot_general(a[:, :cw], b0_ref[...].astype(BF16), NT, preferred_element_type=F32)
        p = p + lax.dot_general(a[:, cw:], b1_ref[...].astype(BF16), NT, preferred_element_type=F32)

        @pl.when(k == 0)
        def _():
            acc_ref[...] = p

        @pl.when(k == 1)
        def _():
            o_ref[...] = acc_ref[...] + p

    return pl.pallas_call(
        body, name=name, grid=(T // tm, 2),
        in_specs=[pl.BlockSpec((tm, 2 * cw), lambda m, k: (m, k)),
                  pl.BlockSpec((None, D, cw), lambda m, k: (2 * k, blk, 0)),
                  pl.BlockSpec((None, D, cw), lambda m, k: (2 * k + 1, blk, 0))],
        out_specs=pl.BlockSpec((tm, D), lambda m, k: (m, 0)), out_shape=jax.ShapeDtypeStruct((T, D), F32),
        scratch_shapes=[pltpu.VMEM((tm, D), F32)], compiler_params=_params("parallel", "arbitrary"),
    )(dgu, g_in, g_in)


def _mm_tn_shards(a, b, by_rows, name, tk=1024):
    (K, M), (_, N) = a.shape, b.shape
    tk = _tile(K, tk)
    nk = K // tk
    if by_rows:
        rl = M // N_CHIPS
        a_spec = pl.BlockSpec((tk, 2 * rl), lambda j, k: (k, j))
        b_spec = pl.BlockSpec((tk, N), lambda j, k: (k, 0))
        o_spec = pl.BlockSpec((2, rl, N), lambda j, k: (j, 0, 0))
        out_shape, acc_shape, steps = (N_CHIPS, rl, N), (2 * rl, N), 2
    else:
        cw = N // N_CHIPS
        a_spec = pl.BlockSpec((tk, M), lambda j, k: (k, 0))
        b_spec = pl.BlockSpec((tk, cw), lambda j, k: (k, j))
        o_spec = pl.BlockSpec((None, M, cw), lambda j, k: (j, 0, 0))
        out_shape, acc_shape, steps = (N_CHIPS, M, cw), (M, cw), N_CHIPS

    def body(a_ref, b_ref, o_ref, acc_ref):
        k = pl.program_id(1)
        p = lax.dot_general(a_ref[...].astype(BF16), b_ref[...].astype(BF16), TN, preferred_element_type=F32)

        @pl.when(k == 0)
        def _():
            acc_ref[...] = p

        @pl.when(k > 0)
        def _():
            acc_ref[...] += p

        @pl.when(k == nk - 1)
        def _():
            if by_rows:
                o_ref[0] = acc_ref[:rl, :].astype(o_ref.dtype)
                o_ref[1] = acc_ref[rl:, :].astype(o_ref.dtype)
            else:
                o_ref[...] = acc_ref[...].astype(o_ref.dtype)

    return pl.pallas_call(
        body, name=name, grid=(steps, nk), in_specs=[a_spec, b_spec], out_specs=o_spec,
        out_shape=jax.ShapeDtypeStruct(out_shape, BF16), scratch_shapes=[pltpu.VMEM(acc_shape, F32)],
        compiler_params=_params("parallel", "arbitrary"),
    )(a, b)


def _swiglu_bwd(gate, up, da, name):
    T, F = gate.shape
    tm = _tile(T, 256)

    def body(g_ref, u_ref, da_ref, o_ref):
        g, u, d = g_ref[...].astype(F32), u_ref[...].astype(F32), da_ref[...].astype(F32)
        s = _sigmoid(g)
        o_ref[:, :F] = (d * u * (s * (1.0 + g * (1.0 - s)))).astype(o_ref.dtype)
        o_ref[:, F:] = (d * g * s).astype(o_ref.dtype)

    spec = pl.BlockSpec((tm, F), lambda i: (i, 0))
    return pl.pallas_call(
        body, name=name, grid=(T // tm,), in_specs=[spec] * 3, out_specs=pl.BlockSpec((tm, 2 * F), lambda i: (i, 0)),
        out_shape=jax.ShapeDtypeStruct((T, 2 * F), BF16), compiler_params=_params("parallel"),
    )(gate, up, da)


def _ffn_fwd(x, g_pre, g_post, g_in, g_out, blk, tag):
    n = _rms_fwd(x, g_pre, f"{tag}_rms")
    gate, up, a = _mm_swiglu(n, g_in, blk, f"{tag}_mm_in")
    h = _ffn_mm_out(a, g_out, blk, f"{tag}_mm_out")
    return _post_fwd(x, h, g_post, 0.5, f"{tag}_post"), (x, n, gate, up, a, h)


def _ffn_bwd(dy, saved, g_pre, g_post, g_in, g_out, blk, tag):
    x, n, gate, up, a, h = saved
    dh, dg_post = _post_bwd(dy, h, g_post, 0.5, f"{tag}_post_b")
    da = _ffn_mm_da(dh, g_out, blk, a.shape[1], f"{tag}_mm_da")
    dw_out = _mm_tn_shards(a, dh, True, f"{tag}_mm_dwout")
    dgu = _swiglu_bwd(gate, up, da, f"{tag}_swiglu_b")
    dn = _ffn_mm_dn(dgu, g_in, blk, x.shape[1], f"{tag}_mm_dn")
    dw_in = _mm_tn_shards(n, dgu, False, f"{tag}_mm_dwin")
    dy2, dg_pre = _rms_bwd_add(x, g_pre, dn, dy, f"{tag}_rms_b")
    return dy2, dg_pre, dg_post, dw_in, dw_out


def _shift_down(x, s):
    if s == 0:
        return x
    rows = lax.broadcasted_iota(jnp.int32, x.shape, 0)
    return jnp.where(rows >= s, pltpu.roll(x, s, 0), 0.0)


def _shift_up(x, s):
    if s == 0:
        return x
    T = x.shape[0]
    rows = lax.broadcasted_iota(jnp.int32, x.shape, 0)
    return jnp.where(rows < T - s, pltpu.roll(x, T - s, 0), 0.0)


def _conv_pre(x, w, b):
    y = b
    for k in range(CONV_K):
        y = y + w[k:k + 1, :] * _shift_down(x, CONV_K - 1 - k)
    return y


def _conv_fwd(x, col0, C, w, b, act, name, ct=256):
    T = x.shape[0]
    off = col0 // ct

    def body(x_ref, w_ref, b_ref, o_ref):
        y = _conv_pre(x_ref[...], w_ref[...], b_ref[...])
        o_ref[...] = _silu(y) if act else y

    return pl.pallas_call(
        body, name=name, grid=(C // ct,),
        in_specs=[pl.BlockSpec((T, ct), lambda j: (0, j + off)), pl.BlockSpec((CONV_K, ct), lambda j: (0, j)),
                  pl.BlockSpec((1, ct), lambda j: (0, j))],
        out_specs=pl.BlockSpec((T, ct), lambda j: (0, j)), out_shape=jax.ShapeDtypeStruct((T, C), F32),
        compiler_params=_params("parallel"),
    )(x, w, b)


def _conv_bwd(x, col0, C, w, b, dyact, act, name, ct=256):
    T = x.shape[0]
    off = col0 // ct

    def body(x_ref, w_ref, b_ref, dy_ref, dx_ref, dw_ref, db_ref):
        xv, wv = x_ref[...], w_ref[...]
        dy = dy_ref[...].astype(F32)
        if act:
            pre = _conv_pre(xv, wv, b_ref[...])
            s = _sigmoid(pre)
            dy = dy * (s * (1.0 + pre * (1.0 - s)))
        dx = jnp.zeros_like(dy)
        dws = []
        for k in range(CONV_K):
            dx = dx + wv[k:k + 1, :] * _shift_up(dy, CONV_K - 1 - k)
            dws.append(jnp.sum(dy * _shift_down(xv, CONV_K - 1 - k), axis=0, keepdims=True))
        dx_ref[...] = dx.astype(dx_ref.dtype)
        dw_ref[...] = jnp.concatenate(dws, axis=0)
        db_ref[...] = jnp.sum(dy, axis=0, keepdims=True)

    return pl.pallas_call(
        body, name=name, grid=(C // ct,),
        in_specs=[pl.BlockSpec((T, ct), lambda j: (0, j + off)), pl.BlockSpec((CONV_K, ct), lambda j: (0, j)),
                  pl.BlockSpec((1, ct), lambda j: (0, j)), pl.BlockSpec((T, ct), lambda j: (0, j))],
        out_specs=[pl.BlockSpec((T, ct), lambda j: (0, j)), pl.BlockSpec((CONV_K, ct), lambda j: (0, j)),
                   pl.BlockSpec((1, ct), lambda j: (0, j))],
        out_shape=[jax.ShapeDtypeStruct((T, C), BF16), jax.ShapeDtypeStruct((CONV_K, C), F32),
                   jax.ShapeDtypeStruct((1, C), F32)],
        compiler_params=_params("parallel"),
    )(x, w, b, dyact)


def _gate_act(v, n_ssm):
    lane = lax.broadcasted_iota(jnp.int32, v.shape, 1)
    return jnp.where(lane < n_ssm, _softplus(v), _log_sigmoid(v))


def _gates_fwd(proj, col0, bias, mult, n_ssm, name, tb=512):
    T = proj.shape[0]
    tb = _tile(T, tb)
    off = col0 // LANES

    def body(s_ref, bias_ref, mult_ref, act_ref, cs_ref, carry_ref):
        i = pl.program_id(0)

        @pl.when(i == 0)
        def _():
            carry_ref[...] = jnp.zeros_like(carry_ref)

        act = _gate_act(s_ref[...] + bias_ref[...], n_ssm)
        inc = act * mult_ref[...]
        r = lax.broadcasted_iota(jnp.int32, (tb, tb), 0)
        c = lax.broadcasted_iota(jnp.int32, (tb, tb), 1)
        tri = jnp.where(r >= c, 1.0, 0.0).astype(F32)
        cs = jnp.dot(tri, inc, precision=lax.Precision.HIGHEST, preferred_element_type=F32) + carry_ref[...]
        act_ref[...] = act
        cs_ref[...] = cs
        carry_ref[...] = cs[tb - 1:tb, :]

    return pl.pallas_call(
        body, name=name, grid=(T // tb,),
        in_specs=[pl.BlockSpec((tb, LANES), lambda i: (i, off)), pl.BlockSpec((1, LANES), lambda i: (0, 0)),
                  pl.BlockSpec((1, LANES), lambda i: (0, 0))],
        out_specs=[pl.BlockSpec((tb, LANES), lambda i: (i, 0))] * 2,
        out_shape=[jax.ShapeDtypeStruct((T, LANES), F32)] * 2,
        scratch_shapes=[pltpu.VMEM((1, LANES), F32)], compiler_params=_params("arbitrary"),
    )(proj, bias, mult)


def _gates_bwd(proj, col0, bias, mult, n_ssm, dact, dcs, name, tb=512):
    T = proj.shape[0]
    tb = _tile(T, tb)
    nb = T // tb
    off = col0 // LANES

    def body(s_ref, bias_ref, mult_ref, dact_ref, dcs_ref, ds_ref, dmult_ref, dbias_ref, carry_ref):
        i = pl.program_id(0)

        @pl.when(i == 0)
        def _():
            carry_ref[...] = jnp.zeros_like(carry_ref)
            dmult_ref[...] = jnp.zeros_like(dmult_ref)
            dbias_ref[...] = jnp.zeros_like(dbias_ref)

        v = s_ref[...] + bias_ref[...]
        act = _gate_act(v, n_ssm)
        r = lax.broadcasted_iota(jnp.int32, (tb, tb), 0)
        c = lax.broadcasted_iota(jnp.int32, (tb, tb), 1)
        tri = jnp.where(r <= c, 1.0, 0.0).astype(F32)
        dinc = jnp.dot(tri, dcs_ref[...], precision=lax.Precision.HIGHEST, preferred_element_type=F32) + carry_ref[...]
        carry_ref[...] = dinc[0:1, :]
        da = dact_ref[...] + dinc * mult_ref[...]
        sg = _sigmoid(v)
        lane = lax.broadcasted_iota(jnp.int32, v.shape, 1)
        dv = da * jnp.where(lane < n_ssm, sg, 1.0 - sg)
        ds_ref[...] = dv.astype(ds_ref.dtype)
        dmult_ref[...] += jnp.sum(dinc * act, axis=0, keepdims=True)
        dbias_ref[...] += jnp.sum(dv, axis=0, keepdims=True)

    rev = lambda i: (nb - 1 - i, 0)
    return pl.pallas_call(
        body, name=name, grid=(nb,),
        in_specs=[pl.BlockSpec((tb, LANES), lambda i: (nb - 1 - i, off)), pl.BlockSpec((1, LANES), lambda i: (0, 0)),
                  pl.BlockSpec((1, LANES), lambda i: (0, 0)), pl.BlockSpec((tb, LANES), rev),
                  pl.BlockSpec((tb, LANES), rev)],
        out_specs=[pl.BlockSpec((tb, LANES), rev), pl.BlockSpec((1, LANES), lambda i: (0, 0)),
                   pl.BlockSpec((1, LANES), lambda i: (0, 0))],
        out_shape=[jax.ShapeDtypeStruct((T, LANES), BF16), jax.ShapeDtypeStruct((1, LANES), F32),
                   jax.ShapeDtypeStruct((1, LANES), F32)],
        scratch_shapes=[pltpu.VMEM((1, LANES), F32)], compiler_params=_params("arbitrary"),
    )(proj, bias, mult, dact, dcs)


def _rep_layout(v):
    return jnp.repeat(v, LANES, axis=1)


def _row_layout(v, tk):
    T, H = v.shape
    return v.T.reshape(H, T // tk, 1, tk)


def _causal(tq):
    r = lax.broadcasted_iota(jnp.int32, (tq, tq), 0)
    c = lax.broadcasted_iota(jnp.int32, (tq, tq), 1)
    return r >= c


def _ssd_fwd(xbc, X, cs_rep, cs_row, name, tq=SSD_TILE):
    T = X.shape[0]
    tq = _tile(T, tq)
    nq = T // tq
    d_ssm = X.shape[1]
    gw = d_ssm // SSM_GROUPS
    hpg = gw // SSM_HEAD_DIM
    b_off = d_ssm // SSM_STATE
    c_off = b_off + SSM_GROUPS

    def body(c_ref, b_ref, x_ref, csq_ref, csk_ref, y_ref):
        i = pl.program_id(1)
        c = c_ref[...].astype(BF16)
        half = lax.broadcasted_iota(jnp.int32, (tq, LANES), 1) // SSM_HEAD_DIM
        mask = _causal(tq)

        def step(j, acc, masked):
            r0 = pl.multiple_of(j * tq, tq)
            s = lax.dot_general(c, b_ref[pl.ds(r0, tq), :].astype(BF16), NT, preferred_element_type=F32)
            out = []
            for p in range(hpg // 2):
                xp = x_ref[pl.ds(r0, tq), p * LANES:(p + 1) * LANES]
                a = acc[p]
                for e in range(2):
                    h = 2 * p + e
                    diff = jnp.tile(csq_ref[:, h * LANES:(h + 1) * LANES], (1, tq // LANES)) - csk_ref[h, j]
                    if masked:
                        diff = jnp.where(mask, diff, NEG)
                    pm = (s * jnp.exp(diff)).astype(BF16)
                    xm = jnp.where(half == e, xp, jnp.zeros_like(xp))
                    a = a + jnp.dot(pm, xm, preferred_element_type=F32)
                out.append(a)
            return tuple(out)

        acc = tuple(jnp.zeros((tq, LANES), F32) for _ in range(hpg // 2))
        acc = lax.fori_loop(0, i, lambda j, a: step(j, a, False), acc)
        acc = step(i, acc, True)
        y_ref[...] = jnp.concatenate(acc, axis=1)

    return pl.pallas_call(
        body, name=name, grid=(SSM_GROUPS, nq),
        in_specs=[pl.BlockSpec((tq, SSM_STATE), lambda g, i: (i, c_off + g)),
                  pl.BlockSpec((T, SSM_STATE), lambda g, i: (0, b_off + g)),
                  pl.BlockSpec((T, gw), lambda g, i: (0, g)),
                  pl.BlockSpec((tq, hpg * LANES), lambda g, i: (i, g)),
                  pl.BlockSpec((hpg, nq, 1, tq), lambda g, i: (g, 0, 0, 0))],
        out_specs=pl.BlockSpec((tq, gw), lambda g, i: (i, g)),
        out_shape=jax.ShapeDtypeStruct((T, d_ssm), F32), compiler_params=_params("parallel", "arbitrary"),
    )(xbc, xbc, X, cs_rep, cs_row)


def _ssd_bwd(xbc, X, cs_rep, cs_row, dY, name, tq=SSD_TILE):
    T = X.shape[0]
    tq = _tile(T, tq)
    nq = T // tq
    d_ssm = X.shape[1]
    gw = d_ssm // SSM_GROUPS
    hpg = gw // SSM_HEAD_DIM
    nheads = d_ssm // SSM_HEAD_DIM
    b_off = d_ssm // SSM_STATE
    c_off = b_off + SSM_GROUPS

    def body(c_ref, b_ref, x_ref, dy_ref, csq_ref, csk_ref, dc_ref, db_ref, dx_ref, dcsq_ref, dcsk_ref):
        i = pl.program_id(1)

        @pl.when(i == 0)
        def _():
            db_ref[...] = jnp.zeros_like(db_ref)
            dx_ref[...] = jnp.zeros_like(dx_ref)
            dcsk_ref[...] = jnp.zeros_like(dcsk_ref)

        c = c_ref[...].astype(BF16)
        half = lax.broadcasted_iota(jnp.int32, (tq, LANES), 1) // SSM_HEAD_DIM
        mask = _causal(tq)

        def step(j, carry, masked):
            dc_acc, rows = carry
            rows = list(rows)
            r0 = pl.multiple_of(j * tq, tq)
            b = b_ref[pl.ds(r0, tq), :].astype(BF16)
            s = lax.dot_general(c, b, NT, preferred_element_type=F32)
            ds_tot = jnp.zeros((tq, tq), F32)
            for p in range(hpg // 2):
                cols = slice(p * LANES, (p + 1) * LANES)
                xp = x_ref[pl.ds(r0, tq), cols]
                dyp = dy_ref[:, cols]
                dx_p = jnp.zeros((tq, LANES), F32)
                for e in range(2):
                    h = 2 * p + e
                    diff = jnp.tile(csq_ref[:, h * LANES:(h + 1) * LANES], (1, tq // LANES)) - csk_ref[h, j]
                    if masked:
                        diff = jnp.where(mask, diff, NEG)
                    decay = jnp.exp(diff)
                    dym = jnp.where(half == e, dyp, jnp.zeros_like(dyp))
                    g = lax.dot_general(dym, xp, NT, preferred_element_type=F32) * decay
                    ds_tot = ds_tot + g
                    m = g * s
                    rows[h] = rows[h] + jnp.sum(m, axis=1, keepdims=True)
                    dcsk_ref[h, j] -= jnp.sum(m, axis=0, keepdims=True)
                    pm = (s * decay).astype(BF16)
                    dx_p = dx_p + lax.dot_general(pm, dym, TN, preferred_element_type=F32)
                dx_ref[pl.ds(r0, tq), cols] += dx_p
            dsb = ds_tot.astype(BF16)
            dc_acc = dc_acc + jnp.dot(dsb, b, preferred_element_type=F32)
            db_ref[pl.ds(r0, tq), :] += lax.dot_general(dsb, c, TN, preferred_element_type=F32)
            return dc_acc, tuple(rows)

        carry = (jnp.zeros((tq, SSM_STATE), F32), tuple(jnp.zeros((tq, 1), F32) for _ in range(hpg)))
        carry = lax.fori_loop(0, i, lambda j, cr: step(j, cr, False), carry)
        dc_acc, rows = step(i, carry, True)
        dc_ref[...] = dc_acc
        dcsq_ref[...] = jnp.concatenate([jnp.broadcast_to(r, (tq, LANES)) for r in rows], axis=1)

    return pl.pallas_call(
        body, name=name, grid=(SSM_GROUPS, nq),
        in_specs=[pl.BlockSpec((tq, SSM_STATE), lambda g, i: (i, c_off + g)),
                  pl.BlockSpec((T, SSM_STATE), lambda g, i: (0, b_off + g)),
                  pl.BlockSpec((T, gw), lambda g, i: (0, g)),
                  pl.BlockSpec((tq, gw), lambda g, i: (i, g)),
                  pl.BlockSpec((tq, hpg * LANES), lambda g, i: (i, g)),
                  pl.BlockSpec((hpg, nq, 1, tq), lambda g, i: (g, 0, 0, 0))],
        out_specs=[pl.BlockSpec((tq, SSM_STATE), lambda g, i: (i, g)),
                   pl.BlockSpec((T, SSM_STATE), lambda g, i: (0, g)),
                   pl.BlockSpec((T, gw), lambda g, i: (0, g)),
                   pl.BlockSpec((tq, hpg * LANES), lambda g, i: (i, g)),
                   pl.BlockSpec((hpg, nq, 1, tq), lambda g, i: (g, 0, 0, 0))],
        out_shape=[jax.ShapeDtypeStruct((T, SSM_GROUPS * SSM_STATE), F32),
                   jax.ShapeDtypeStruct((T, SSM_GROUPS * SSM_STATE), F32),
                   jax.ShapeDtypeStruct((T, d_ssm), F32),
                   jax.ShapeDtypeStruct((T, nheads * LANES), F32),
                   jax.ShapeDtypeStruct((nheads, nq, 1, tq), F32)],
        compiler_params=_params("arbitrary", "arbitrary"),
    )(xbc, xbc, X, dY, cs_rep, cs_row)


def _ssd2_fwd(xbc, X, XK, cs_rep, cs_row, cs_full, r_end, name, tq=SSD_TILE):
    T = X.shape[0]
    tq = _tile(T, tq)
    nq = T // tq
    d_ssm = X.shape[1]
    gw = d_ssm // SSM_GROUPS
    hpg = gw // SSM_HEAD_DIM
    b_off = d_ssm // SSM_STATE
    c_off = b_off + SSM_GROUPS

    def body(c_ref, b_ref, x_ref, xk_ref, csq_ref, csk_ref, csf_ref, rend_ref, y_ref):
        i = pl.program_id(1)
        c = c_ref[...].astype(BF16)
        csf = csf_ref[...]
        r = csf[0:1, :]

        def off(j, acc):
            r0 = pl.multiple_of(j * tq, tq)
            s = lax.dot_general(c, b_ref[pl.ds(r0, tq), :].astype(BF16), NT, preferred_element_type=F32)
            z = jnp.dot(s.astype(BF16), xk_ref[pl.ds(r0, tq), :], preferred_element_type=F32)
            return acc + z * jnp.exp(r - rend_ref[j])

        y_off = jnp.exp(csf - r) * lax.fori_loop(0, i, off, jnp.zeros((tq, gw), F32))

        r0 = pl.multiple_of(i * tq, tq)
        half = lax.broadcasted_iota(jnp.int32, (tq, LANES), 1) // SSM_HEAD_DIM
        mask = _causal(tq)
        s = lax.dot_general(c, b_ref[pl.ds(r0, tq), :].astype(BF16), NT, preferred_element_type=F32)
        out = []
        for p in range(hpg // 2):
            xp = x_ref[:, p * LANES:(p + 1) * LANES]
            a = jnp.zeros((tq, LANES), F32)
            for e in range(2):
                h = 2 * p + e
                diff = jnp.tile(csq_ref[:, h * LANES:(h + 1) * LANES], (1, tq // LANES)) - csk_ref[h, i]
                pm = (s * jnp.exp(jnp.where(mask, diff, NEG))).astype(BF16)
                xm = jnp.where(half == e, xp, jnp.zeros_like(xp))
                a = a + jnp.dot(pm, xm, preferred_element_type=F32)
            out.append(a)
        y_ref[...] = y_off + jnp.concatenate(out, axis=1)

    return pl.pallas_call(
        body, name=name, grid=(SSM_GROUPS, nq),
        in_specs=[pl.BlockSpec((tq, SSM_STATE), lambda g, i: (i, c_off + g)),
                  pl.BlockSpec((T, SSM_STATE), lambda g, i: (0, b_off + g)),
                  pl.BlockSpec((tq, gw), lambda g, i: (i, g)),
                  pl.BlockSpec((T, gw), lambda g, i: (0, g)),
                  pl.BlockSpec((tq, hpg * LANES), lambda g, i: (i, g)),
                  pl.BlockSpec((hpg, nq, 1, tq), lambda g, i: (g, 0, 0, 0)),
                  pl.BlockSpec((tq, gw), lambda g, i: (i, g)),
                  pl.BlockSpec((nq, 1, gw), lambda g, i: (0, 0, g))],
        out_specs=pl.BlockSpec((tq, gw), lambda g, i: (i, g)),
        out_shape=jax.ShapeDtypeStruct((T, d_ssm), F32), compiler_params=_params("parallel", "arbitrary"),
    )(xbc, xbc, X, XK, cs_rep, cs_row, cs_full, r_end)


def _ssd2_bwd(xbc, X, XK, cs_rep, cs_row, cs_full, r_end, dY, name, tq=SSD_TILE):
    T = X.shape[0]
    tq = _tile(T, tq)
    nq = T // tq
    d_ssm = X.shape[1]
    gw = d_ssm // SSM_GROUPS
    hpg = gw // SSM_HEAD_DIM
    nheads = d_ssm // SSM_HEAD_DIM
    b_off = d_ssm // SSM_STATE
    c_off = b_off + SSM_GROUPS

    def body(c_ref, b_ref, x_ref, xk_ref, dy_ref, csq_ref, csk_ref, csf_ref, rend_ref,
             dc_ref, db_ref, dx_ref, dxk_ref, dcsq_ref, dcsk_ref, dcsf_ref):
        i = pl.program_id(1)

        @pl.when(i == 0)
        def _():
            db_ref[...] = jnp.zeros_like(db_ref)
            dxk_ref[...] = jnp.zeros_like(dxk_ref)

        c = c_ref[...].astype(BF16)
        csf = csf_ref[...]
        r = csf[0:1, :]
        dyt = dy_ref[...].astype(F32) * jnp.exp(csf - r)

        def off(j, carry):
            dc_acc, rs_acc = carry
            r0 = pl.multiple_of(j * tq, tq)
            b = b_ref[pl.ds(r0, tq), :].astype(BF16)
            xk = xk_ref[pl.ds(r0, tq), :]
            sb = lax.dot_general(c, b, NT, preferred_element_type=F32).astype(BF16)
            dye = dyt * jnp.exp(r - rend_ref[j])
            dyb = dye.astype(BF16)
            rs_acc = rs_acc + dyb.astype(F32) * jnp.dot(sb, xk, preferred_element_type=F32)
            dxk_ref[pl.ds(r0, tq), :] += lax.dot_general(sb, dyb, TN, preferred_element_type=F32)
            dsb = lax.dot_general(dyb, xk, NT, preferred_element_type=F32).astype(BF16)
            dc_acc = dc_acc + jnp.dot(dsb, b, preferred_element_type=F32)
            db_ref[pl.ds(r0, tq), :] += lax.dot_general(dsb, c, TN, preferred_element_type=F32)
            return dc_acc, rs_acc

        dc_acc, rs_acc = lax.fori_loop(0, i, off, (jnp.zeros((tq, SSM_STATE), F32), jnp.zeros((tq, gw), F32)))
        dcsf_ref[...] = rs_acc

        r0 = pl.multiple_of(i * tq, tq)
        half = lax.broadcasted_iota(jnp.int32, (tq, LANES), 1) // SSM_HEAD_DIM
        mask = _causal(tq)
        b = b_ref[pl.ds(r0, tq), :].astype(BF16)
        s = lax.dot_general(c, b, NT, preferred_element_type=F32)
        ds_tot = jnp.zeros((tq, tq), F32)
        rows, dxs = [], []
        for p in range(hpg // 2):
            cols = slice(p * LANES, (p + 1) * LANES)
            xp = x_ref[:, cols]
            dyp = dy_ref[:, cols]
            dx_p = jnp.zeros((tq, LANES), F32)
            for e in range(2):
                h = 2 * p + e
                diff = jnp.tile(csq_ref[:, h * LANES:(h + 1) * LANES], (1, tq // LANES)) - csk_ref[h, i]
                decay = jnp.exp(jnp.where(mask, diff, NEG))
                dym = jnp.where(half == e, dyp, jnp.zeros_like(dyp))
                g = lax.dot_general(dym, xp, NT, preferred_element_type=F32) * decay
                ds_tot = ds_tot + g
                m = g * s
                rows.append(jnp.broadcast_to(jnp.sum(m, axis=1, keepdims=True), (tq, LANES)))
                dcsk_ref[h, i] = -jnp.sum(m, axis=0, keepdims=True)
                dx_p = dx_p + lax.dot_general((s * decay).astype(BF16), dym, TN, preferred_element_type=F32)
            dxs.append(dx_p)
        dsb = ds_tot.astype(BF16)
        dc_ref[...] = dc_acc + jnp.dot(dsb, b, preferred_element_type=F32)
        db_ref[pl.ds(r0, tq), :] += lax.dot_general(dsb, c, TN, preferred_element_type=F32)
        dx_ref[...] = jnp.concatenate(dxs, axis=1)
        dcsq_ref[...] = jnp.concatenate(rows, axis=1)

    return pl.pallas_call(
        body, name=name, grid=(SSM_GROUPS, nq),
        in_specs=[pl.BlockSpec((tq, SSM_STATE), lambda g, i: (i, c_off + g)),
                  pl.BlockSpec((T, SSM_STATE), lambda g, i: (0, b_off + g)),
                  pl.BlockSpec((tq, gw), lambda g, i: (i, g)),
                  pl.BlockSpec((T, gw), lambda g, i: (0, g)),
                  pl.BlockSpec((tq, gw), lambda g, i: (i, g)),
                  pl.BlockSpec((tq, hpg * LANES), lambda g, i: (i, g)),
                  pl.BlockSpec((hpg, nq, 1, tq), lambda g, i: (g, 0, 0, 0)),
                  pl.BlockSpec((tq, gw), lambda g, i: (i, g)),
                  pl.BlockSpec((nq, 1, gw), lambda g, i: (0, 0, g))],
        out_specs=[pl.BlockSpec((tq, SSM_STATE), lambda g, i: (i, g)),
                   pl.BlockSpec((T, SSM_STATE), lambda g, i: (0, g)),
                   pl.BlockSpec((tq, gw), lambda g, i: (i, g)),
                   pl.BlockSpec((T, gw), lambda g, i: (0, g)),
                   pl.BlockSpec((tq, hpg * LANES), lambda g, i: (i, g)),
                   pl.BlockSpec((hpg, nq, 1, tq), lambda g, i: (g, 0, 0, 0)),
                   pl.BlockSpec((tq, gw), lambda g, i: (i, g))],
        out_shape=[jax.ShapeDtypeStruct((T, SSM_GROUPS * SSM_STATE), F32),
                   jax.ShapeDtypeStruct((T, SSM_GROUPS * SSM_STATE), F32),
                   jax.ShapeDtypeStruct((T, d_ssm), F32),
                   jax.ShapeDtypeStruct((T, d_ssm), F32),
                   jax.ShapeDtypeStruct((T, nheads * LANES), F32),
                   jax.ShapeDtypeStruct((nheads, nq, 1, tq), F32),
                   jax.ShapeDtypeStruct((T, d_ssm), F32)],
        compiler_params=_params("arbitrary", "arbitrary"),
    )(xbc, xbc, X, XK, dY, cs_rep, cs_row, cs_full, r_end)


def _fox_fwd(proj, q0, k0, v0, nh, cum_row, name, tq=FOX_TILE):
    T = proj.shape[0]
    tq = _tile(T, tq)
    nq = T // tq
    hd = FOX_HEAD_DIM
    scale = hd ** -0.5
    qo, ko, vo = q0 // hd, k0 // hd, v0 // hd

    def body(q_ref, k_ref, v_ref, ck_ref, o_ref, lse_ref):
        i = pl.program_id(1)
        q = (q_ref[...] * (scale * LOG2E)).astype(BF16)
        mask = _causal(tq)

        def step(j, carry, masked):
            m, l, acc = carry
            r0 = pl.multiple_of(j * tq, tq)
            k = k_ref[pl.ds(r0, tq), :].astype(BF16)
            v = v_ref[pl.ds(r0, tq), :].astype(BF16)
            s = lax.dot_general(q, k, NT, preferred_element_type=F32) - ck_ref[j]
            if masked:
                s = jnp.where(mask, s, NEG)
            m_new = jnp.maximum(m, jnp.max(s, axis=1, keepdims=True))
            alpha = jnp.exp2(m - m_new)
            p = jnp.exp2(s - m_new)
            l = alpha * l + jnp.sum(p, axis=1, keepdims=True)
            acc = alpha * acc + jnp.dot(p.astype(BF16), v, preferred_element_type=F32)
            return m_new, l, acc

        carry = (jnp.full((tq, 1), NEG, F32), jnp.zeros((tq, 1), F32), jnp.zeros((tq, hd), F32))
        carry = lax.fori_loop(0, i, lambda j, cr: step(j, cr, False), carry)
        m, l, acc = step(i, carry, True)
        o_ref[...] = (acc / l).astype(o_ref.dtype)
        lse_ref[...] = jnp.broadcast_to(m + jnp.log2(l), (tq, LANES))

    return pl.pallas_call(
        body, name=name, grid=(nh, nq),
        in_specs=[pl.BlockSpec((tq, hd), lambda h, i: (i, qo + h)), pl.BlockSpec((T, hd), lambda h, i: (0, ko + h)),
                  pl.BlockSpec((T, hd), lambda h, i: (0, vo + h)),
                  pl.BlockSpec((None, nq, 1, tq), lambda h, i: (h, 0, 0, 0))],
        out_specs=[pl.BlockSpec((tq, hd), lambda h, i: (i, h)), pl.BlockSpec((tq, LANES), lambda h, i: (i, h))],
        out_shape=[jax.ShapeDtypeStruct((T, nh * hd), BF16), jax.ShapeDtypeStruct((T, nh * LANES), F32)],
        compiler_params=_params("parallel", "arbitrary"),
    )(proj, proj, proj, cum_row * LOG2E)


def _fox_bwd(proj, q0, k0, v0, nh, cum_row, o, lse, dcat, do0, name, tq=FOX_TILE):
    T = proj.shape[0]
    tq = _tile(T, tq)
    nq = T // tq
    hd = FOX_HEAD_DIM
    scale = hd ** -0.5
    qo, ko, vo, doo = q0 // hd, k0 // hd, v0 // hd, do0 // hd

    def body(q_ref, k_ref, v_ref, do_ref, o_ref, lse_ref, ck_ref, dq_ref, dk_ref, dv_ref, dck_ref, dcq_ref):
        i = pl.program_id(1)

        @pl.when(i == 0)
        def _():
            dk_ref[...] = jnp.zeros_like(dk_ref)
            dv_ref[...] = jnp.zeros_like(dv_ref)
            dck_ref[...] = jnp.zeros_like(dck_ref)

        q = (q_ref[...] * (scale * LOG2E)).astype(BF16)
        do = do_ref[...].astype(F32)
        dob = do.astype(BF16)
        delta = jnp.sum(do * o_ref[...].astype(F32), axis=1, keepdims=True)
        lse = jnp.tile(lse_ref[...], (1, tq // LANES))
        mask = _causal(tq)

        def step(j, carry, masked):
            dq, rows = carry
            r0 = pl.multiple_of(j * tq, tq)
            k = k_ref[pl.ds(r0, tq), :].astype(BF16)
            v = v_ref[pl.ds(r0, tq), :].astype(BF16)
            s = lax.dot_general(q, k, NT, preferred_element_type=F32) - lse - ck_ref[j]
            if masked:
                s = jnp.where(mask, s, NEG)
            p = jnp.exp2(s)
            dp = lax.dot_general(dob, v, NT, preferred_element_type=F32)
            ds = p * (dp - delta)
            dsb = ds.astype(BF16)
            dq = dq + jnp.dot(dsb, k, preferred_element_type=F32) * scale
            dk_ref[pl.ds(r0, tq), :] += lax.dot_general(dsb, q, TN, preferred_element_type=F32) * (1.0 / LOG2E)
            dv_ref[pl.ds(r0, tq), :] += lax.dot_general(p.astype(BF16), dob, TN, preferred_element_type=F32)
            dck_ref[j] -= jnp.sum(ds, axis=0, keepdims=True)
            return dq, rows + jnp.sum(ds, axis=1, keepdims=True)

        carry = (jnp.zeros((tq, hd), F32), jnp.zeros((tq, 1), F32))
        carry = lax.fori_loop(0, i, lambda j, cr: step(j, cr, False), carry)
        dq, rows = step(i, carry, True)
        dq_ref[...] = dq.astype(dq_ref.dtype)
        dcq_ref[...] = jnp.broadcast_to(rows, (tq, LANES))

    return pl.pallas_call(
        body, name=name, grid=(nh, nq),
        in_specs=[pl.BlockSpec((tq, hd), lambda h, i: (i, qo + h)), pl.BlockSpec((T, hd), lambda h, i: (0, ko + h)),
                  pl.BlockSpec((T, hd), lambda h, i: (0, vo + h)), pl.BlockSpec((tq, hd), lambda h, i: (i, doo + h)),
                  pl.BlockSpec((tq, hd), lambda h, i: (i, h)), pl.BlockSpec((tq, LANES), lambda h, i: (i, h)),
                  pl.BlockSpec((None, nq, 1, tq), lambda h, i: (h, 0, 0, 0))],
        out_specs=[pl.BlockSpec((tq, hd), lambda h, i: (i, h)), pl.BlockSpec((T, hd), lambda h, i: (0, h)),
                   pl.BlockSpec((T, hd), lambda h, i: (0, h)), pl.BlockSpec((None, nq, 1, tq), lambda h, i: (h, 0, 0, 0)),
                   pl.BlockSpec((tq, LANES), lambda h, i: (i, h))],
        out_shape=[jax.ShapeDtypeStruct((T, nh * hd), BF16), jax.ShapeDtypeStruct((T, nh * hd), F32),
                   jax.ShapeDtypeStruct((T, nh * hd), F32), jax.ShapeDtypeStruct((nh, nq, 1, tq), F32),
                   jax.ShapeDtypeStruct((T, nh * LANES), F32)],
        compiler_params=_params("arbitrary", "arbitrary"),
    )(proj, proj, proj, dcat, o, lse, cum_row * LOG2E)


def _scan_fwd(a, u, name, ct=256):
    T, C = a.shape

    def body(a_ref, u_ref, h_ref):
        def blk(tb, h):
            r0 = pl.multiple_of(tb * SUBLANES, SUBLANES)
            ab, ub = a_ref[pl.ds(r0, SUBLANES), :], u_ref[pl.ds(r0, SUBLANES), :]
            rows = []
            for r in range(SUBLANES):
                h = ab[r:r + 1, :] * h + ub[r:r + 1, :]
                rows.append(h)
            h_ref[pl.ds(r0, SUBLANES), :] = jnp.concatenate(rows, axis=0)
            return h

        lax.fori_loop(0, T // SUBLANES, blk, jnp.zeros((1, ct), F32))

    spec = pl.BlockSpec((T, ct), lambda j: (0, j))
    return pl.pallas_call(body, name=name, grid=(C // ct,), in_specs=[spec, spec], out_specs=spec,
                          out_shape=jax.ShapeDtypeStruct((T, C), F32), compiler_params=_params("parallel"))(a, u)


def _scan_bwd(a, dh, h, name, ct=256):
    T, C = a.shape
    nb = T // SUBLANES

    def body(a_ref, dh_ref, h_ref, g_ref, da_ref):
        def blk(t, carry):
            r0 = pl.multiple_of((nb - 1 - t) * SUBLANES, SUBLANES)
            ab, db = a_ref[pl.ds(r0, SUBLANES), :], dh_ref[pl.ds(r0, SUBLANES), :]
            rows = [None] * SUBLANES
            for r in range(SUBLANES - 1, -1, -1):
                g = db[r:r + 1, :] + carry
                carry = ab[r:r + 1, :] * g
                rows[r] = g
            g_ref[pl.ds(r0, SUBLANES), :] = jnp.concatenate(rows, axis=0)
            return carry

        lax.fori_loop(0, nb, blk, jnp.zeros((1, ct), F32))
        da_ref[...] = g_ref[...] * _shift_down(h_ref[...], 1)

    spec = pl.BlockSpec((T, ct), lambda j: (0, j))
    return pl.pallas_call(body, name=name, grid=(C // ct,), in_specs=[spec] * 3, out_specs=[spec] * 2,
                          out_shape=[jax.ShapeDtypeStruct((T, C), F32)] * 2,
                          compiler_params=_params("parallel"))(a, dh, h)


def _lru_elem(xc, ra, ia, lam):
    r, i = _sigmoid(ra), _sigmoid(ia)
    log_a = RG_LRU_C * r * _log_sigmoid(lam)
    return jnp.exp(log_a), jnp.sqrt(_neg_expm1(2.0 * log_a)) * (i * xc)


def _lru_gates_fwd(xc, w_a, b_a, w_x, b_x, lam, name):
    def fn(xv, ba, bx, lm, wa, wx):
        xb = xv.astype(BF16)
        ra = jnp.dot(xb, wa.astype(BF16), preferred_element_type=F32) + ba
        ia = jnp.dot(xb, wx.astype(BF16), preferred_element_type=F32) + bx
        return _lru_elem(xv, ra, ia, lm)
    nb = xc.shape[1] // RNN_BLOCK
    return _ew(fn, [(xc, 0)], [(b_a, 0), (b_x, 0), (lam, 0), (w_a, 0), (w_x, 0)], [F32, F32], [],
               name=name, tm=512, cb=RNN_BLOCK, ncb=nb)


def _lru_gates_bwd(xc, w_a, b_a, w_x, b_x, lam, da, du, name):
    def fn(xv, dav, duv, ba, bx, lm, wa, wx):
        xb, wab, wxb = xv.astype(BF16), wa.astype(BF16), wx.astype(BF16)
        ra = jnp.dot(xb, wab, preferred_element_type=F32) + ba
        ia = jnp.dot(xb, wxb, preferred_element_type=F32) + bx
        _, vjp = jax.vjp(_lru_elem, xv, ra, ia, lm)
        dx, dra, dia, dlm = vjp((dav, duv))
        drb, dib = dra.astype(BF16), dia.astype(BF16)
        dx = dx + lax.dot_general(drb, wab, NT, preferred_element_type=F32)
        dx = dx + lax.dot_general(dib, wxb, NT, preferred_element_type=F32)
        dwa = lax.dot_general(xb, drb, TN, preferred_element_type=F32)
        dwx = lax.dot_general(xb, dib, TN, preferred_element_type=F32)
        return (dx, jnp.sum(dra, axis=0, keepdims=True), jnp.sum(dia, axis=0, keepdims=True), dlm, dwa, dwx)
    nb = xc.shape[1] // RNN_BLOCK
    return _ew(fn, [(xc, 0), (da, 0), (du, 0)], [(b_a, 0), (b_x, 0), (lam, 0), (w_a, 0), (w_x, 0)], [F32],
               [b_a.shape, b_x.shape, lam.shape, w_a.shape, w_x.shape], name=name, tm=512, cb=RNN_BLOCK, ncb=nb)


def _hyb_cols(D):
    conv = D + 2 * SSM_GROUPS * SSM_STATE
    z0, x0, q0 = 0, D, D + conv
    return dict(z=z0, xbc=x0, q=q0, k=q0 + D, v=q0 + 2 * D, small=q0 + 3 * D, total=q0 + 3 * D + LANES, conv=conv)


def _hyb_w_in_reorder(w, D):
    cols = _hyb_cols(D)
    nh_s, nh_f = D // SSM_HEAD_DIM, D // FOX_HEAD_DIM
    a = D + cols["conv"]
    pad = jnp.zeros((w.shape[0], LANES - nh_s - nh_f), w.dtype)
    return jnp.concatenate([w[:, :a], w[:, a + nh_s:a + nh_s + 3 * D], w[:, a:a + nh_s], w[:, a + nh_s + 3 * D:], pad], axis=1)


def _hyb_w_in_restore(dw, D):
    cols = _hyb_cols(D)
    nh_s, nh_f = D // SSM_HEAD_DIM, D // FOX_HEAD_DIM
    a = D + cols["conv"]
    s = cols["small"]
    return jnp.concatenate([dw[:, :a], dw[:, s:s + nh_s], dw[:, a:s], dw[:, s + nh_s:s + nh_s + nh_f]], axis=1)


def _ssm_out(Y, xs, z, dfull, ng):
    y = (Y + dfull * xs) * _silu(z)
    return y * lax.rsqrt(jnp.mean(y * y, axis=-1, keepdims=True) + NORM_EPS) * ng


def _hyb_fwd(x, g_pre, g_post, p, tag):
    T, D = x.shape
    cols = _hyb_cols(D)
    nh_s, nh_f = D // SSM_HEAD_DIM, D // FOX_HEAD_DIM
    n = _rms_fwd(x, g_pre, f"{tag}_rms")
    proj = _mm(n, p["w_in"], "nn", F32, f"{tag}_mm_in", tn=1152, tk=1024)
    a_neg = -jnp.exp(p["a_log"])
    bias = jnp.concatenate([p["dt_bias"], p["b_f"], jnp.zeros((LANES - nh_s - nh_f,), F32)])[None]
    mult = jnp.concatenate([a_neg, jnp.ones((nh_f,), F32), jnp.zeros((LANES - nh_s - nh_f,), F32)])[None]
    act, cs = _gates_fwd(proj, cols["small"], bias, mult, nh_s, f"{tag}_gates")
    dt, cs_s, cum = act[:, :nh_s], cs[:, :nh_s], cs[:, nh_s:nh_s + nh_f]
    dtf = jnp.repeat(dt, SSM_HEAD_DIM, axis=1)
    cs_rep, cs_row = _rep_layout(cs_s), _row_layout(cs_s, _tile(T, SSD_TILE))
    cum_row = _row_layout(cum, _tile(T, FOX_TILE))
    xbc = _conv_fwd(proj, cols["xbc"], cols["conv"], p["conv_w"], p["conv_b"], True, f"{tag}_conv")
    tqs = _tile(T, SSD_TILE)
    cs_full = jnp.repeat(cs_s, SSM_HEAD_DIM, axis=1)
    r_end = cs_full.reshape(T // tqs, tqs, D)[:, tqs - 1:, :]
    r_exp = jnp.broadcast_to(r_end, (T // tqs, tqs, D)).reshape(T, D)
    X, XK = _ew(lambda xv, dv, cv, rv: (xv * dv, xv * dv * jnp.exp(rv - cv)),
                [(xbc, 0), (dtf, 0), (cs_full, 0), (r_exp, 0)], [], [BF16, BF16], [], name=f"{tag}_xdt",
                cb=512, ncb=D // 512)
    Y = _ssd2_fwd(xbc, X, XK, cs_rep, cs_row, cs_full, r_end, f"{tag}_ssd")
    dfull = jnp.repeat(p["d"], SSM_HEAD_DIM)[None]
    gw = D // SSM_GROUPS
    y_ssm = _ew(_ssm_out, [(Y, 0), (xbc, 0), (proj, cols["z"] // gw)], [(dfull, 0), (p["norm_g"], 0)], [BF16], [],
                name=f"{tag}_ssm_out", cb=gw, ncb=SSM_GROUPS)
    o, lse = _fox_fwd(proj, cols["q"], cols["k"], cols["v"], nh_f, cum_row, f"{tag}_fox")
    mix = _mm_parts([(y_ssm, p["w_out"][:D]), (o, p["w_out"][D:])], "nn", F32, f"{tag}_mm_out", tn=1024)
    x2 = _post_fwd(x, mix, g_post, 1.0, f"{tag}_post")
    return x2, (x, n, proj, bias, mult, dtf, cs_rep, cs_row, cum_row, xbc, X, Y, dfull, o, lse, y_ssm, mix,
                XK, cs_full, r_end, r_exp)


def _hyb_bwd(dy, saved, g_pre, g_post, p, tag):
    (x, n, proj, bias, mult, dtf, cs_rep, cs_row, cum_row, xbc, X, Y, dfull, o, lse, y_ssm, mix,
     XK, cs_full, r_end, r_exp) = saved
    T, D = x.shape
    cols = _hyb_cols(D)
    nh_s, nh_f = D // SSM_HEAD_DIM, D // FOX_HEAD_DIM
    gw = D // SSM_GROUPS
    dmix, dg_post = _post_bwd(dy, mix, g_post, 1.0, f"{tag}_post_b")
    dcat = _mm(dmix, p["w_out"], "nt", BF16, f"{tag}_mm_dcat", tn=1024, tk=1024)
    dw_out = jnp.concatenate([_mm(y_ssm, dmix, "tn", BF16, f"{tag}_mm_dwout_s", tm=1024, tn=1024, tk=1024),
                              _mm(o, dmix, "tn", BF16, f"{tag}_mm_dwout_f", tm=1024, tn=1024, tk=1024)], axis=0)

    def ssm_out_b(Yv, xv, zv, dv, dfv, ngv):
        _, vjp = jax.vjp(_ssm_out, Yv, xv, zv, dfv, ngv)
        return vjp(dv.astype(F32))
    dY, dxs_skip, dz, ddfull, dng = _ew(
        ssm_out_b, [(Y, 0), (xbc, 0), (proj, cols["z"] // gw), (dcat, 0)], [(dfull, 0), (p["norm_g"], 0)],
        [BF16, F32, BF16], [dfull.shape, p["norm_g"].shape], name=f"{tag}_ssm_out_b", cb=gw, ncb=SSM_GROUPS)
    dC, dB, dXd, dXK, dcs_q, dcs_k, dcs_f = _ssd2_bwd(xbc, X, XK, cs_rep, cs_row, cs_full, r_end, dY, f"{tag}_ssd_b")
    def xdt_b(dXdv, dXKv, skv, xv, dv, cv, rv, xkv):
        dX = dXdv + dXKv * jnp.exp(rv - cv)
        return dX * dv + skv, dX * xv, dXKv * xkv.astype(F32)
    dxs, ddtf, dcs_kf = _ew(
        xdt_b, [(dXd, 0), (dXK, 0), (dxs_skip, 0), (xbc, 0), (dtf, 0), (cs_full, 0), (r_exp, 0), (XK, 0)],
        [], [F32, F32, F32], [], name=f"{tag}_xdt_b", cb=512, ncb=D // 512)
    ddt = ddtf.reshape(T, nh_s, SSM_HEAD_DIM).sum(-1)
    dcs_s = (dcs_q[:, ::LANES] + dcs_k.reshape(nh_s, T).T
             + (dcs_f - dcs_kf).reshape(T, nh_s, SSM_HEAD_DIM).sum(-1))
    dq, dk, dv, dcum_k, dcum_q = _fox_bwd(proj, cols["q"], cols["k"], cols["v"], nh_f, cum_row, o, lse, dcat, D,
                                  f"{tag}_fox_b")
    dcum = dcum_q[:, ::LANES] + dcum_k.reshape(nh_f, T).T
    zpad = jnp.zeros((T, LANES - nh_s - nh_f), F32)
    dact = jnp.concatenate([ddt, jnp.zeros((T, nh_f), F32), zpad], axis=1)
    dcs = jnp.concatenate([dcs_s, dcum, zpad], axis=1)
    dsmall, dmult, dbias = _gates_bwd(proj, cols["small"], bias, mult, nh_s, dact, dcs, f"{tag}_gates_b")
    dxbc_act = jnp.concatenate([dxs, dB, dC], axis=1)
    dxbc, dconv_w, dconv_b = _conv_bwd(proj, cols["xbc"], cols["conv"], p["conv_w"], p["conv_b"], dxbc_act, True,
                                       f"{tag}_conv_b")
    pieces = [(dz, "z"), (dxbc, "xbc"), (dq, "q"), (dk, "k"), (dv, "v"), (dsmall, "small")]
    w_cols = lambda d, key: p["w_in"][:, cols[key]:cols[key] + d.shape[1]]
    dn = _mm_parts([(d, w_cols(d, key)) for d, key in pieces], "nt", F32, f"{tag}_mm_dn", tm=256, tn=512)
    dw_in = jnp.concatenate([_mm(n, d, "tn", BF16, f"{tag}_mm_dwin_{key}", tm=1024, tn=1024, tk=1024)
                             for d, key in pieces], axis=1)
    dy2, dg_pre = _rms_bwd_add(x, g_pre, dn, dy, f"{tag}_rms_b")
    grads = dict(w_in=dw_in, w_out=dw_out, conv_w=dconv_w, conv_b=dconv_b[0], dt_bias=dbias[0, :nh_s],
                 a_log=dmult[0, :nh_s] * mult[0, :nh_s], d=ddfull.reshape(nh_s, SSM_HEAD_DIM).sum(-1),
                 norm_g=dng[0], b_f=dbias[0, nh_s:nh_s + nh_f])
    return dy2, dg_pre, dg_post, grads


def _rec_fwd(x, g_pre, g_post, p, tag):
    T, D = x.shape
    n = _rms_fwd(x, g_pre, f"{tag}_rms")
    pr = _mm(n, p["w_in"], "nn", F32, f"{tag}_mm_in", tn=1024, tk=1024)
    xc = _conv_fwd(pr, D, D, p["conv_w"], p["conv_b"], False, f"{tag}_conv")
    a, u = _lru_gates_fwd(xc, p["w_a"], p["b_a"], p["w_x"], p["b_x"], p["lam"], f"{tag}_lru")
    hs = _scan_fwd(a, u, f"{tag}_scan")
    og = _ew(lambda hv, gv: hv * _gelu_tanh(gv), [(hs, 0), (pr, 0)], [], [BF16], [], name=f"{tag}_gate", cb=D)
    mix = _mm(og, p["w_out"], "nn", F32, f"{tag}_mm_out", tn=1024, tk=1024)
    x2 = _post_fwd(x, mix, g_post, 1.0, f"{tag}_post")
    return x2, (x, n, pr, xc, a, hs, og, mix)


def _rec_bwd(dy, saved, g_pre, g_post, p, tag):
    x, n, pr, xc, a, hs, og, mix = saved
    T, D = x.shape
    dmix, dg_post = _post_bwd(dy, mix, g_post, 1.0, f"{tag}_post_b")
    dog = _mm(dmix, p["w_out"], "nt", F32, f"{tag}_mm_dog", tn=1024, tk=1024)
    dw_out = _mm(og, dmix, "tn", BF16, f"{tag}_mm_dwout", tm=1024, tn=1024, tk=1024)

    def gate_b(hv, gv, dv):
        _, vjp = jax.vjp(lambda h_, g_: h_ * _gelu_tanh(g_), hv, gv)
        return vjp(dv)
    dhs, dgate = _ew(gate_b, [(hs, 0), (pr, 0), (dog, 0)], [], [F32, BF16], [], name=f"{tag}_gate_b", cb=D)
    du, da = _scan_bwd(a, dhs, hs, f"{tag}_scan_b")
    dxc, db_a, db_x, dlam, dw_a, dw_x = _lru_gates_bwd(xc, p["w_a"], p["b_a"], p["w_x"], p["b_x"], p["lam"], da, du,
                                                       f"{tag}_lru_b")
    dxr, dconv_w, dconv_b = _conv_bwd(pr, D, D, p["conv_w"], p["conv_b"], dxc, False, f"{tag}_conv_b")
    dn = _mm_parts([(dgate, p["w_in"][:, :D]), (dxr, p["w_in"][:, D:])], "nt", F32, f"{tag}_mm_dn", tn=1024)
    dw_in = jnp.concatenate([_mm(n, dgate, "tn", BF16, f"{tag}_mm_dwin_g", tm=1024, tn=1024, tk=1024),
                             _mm(n, dxr, "tn", BF16, f"{tag}_mm_dwin_x", tm=1024, tn=1024, tk=1024)], axis=1)
    dy2, dg_pre = _rms_bwd_add(x, g_pre, dn, dy, f"{tag}_rms_b")
    grads = dict(w_in=dw_in, w_out=dw_out, conv_w=dconv_w, conv_b=dconv_b[0], w_a=dw_a, b_a=db_a[0], w_x=dw_x,
                 b_x=db_x[0], lam=dlam[0])
    return dy2, dg_pre, dg_post, grads


def _hyb_params(W, i, D):
    return dict(w_in=_hyb_w_in_reorder(W["hyb_w_in"][i], D), w_out=W["hyb_w_out"][i], conv_w=W["ssm_conv_w"][i],
                conv_b=W["ssm_conv_b"][i][None], dt_bias=W["ssm_dt_bias"][i], a_log=W["ssm_a_log"][i],
                d=W["ssm_d"][i], norm_g=W["ssm_norm_g"][i][None], b_f=W["fox_b_f"][i])


def _rec_params(W, j):
    return dict(w_in=W["rec_w_in"][j], w_out=W["rec_w_out"][j], conv_w=W["rec_conv_w"][j],
                conv_b=W["rec_conv_b"][j][None], w_a=W["rec_w_a"][j], b_a=W["rec_b_a"][j][None],
                w_x=W["rec_w_x"][j], b_x=W["rec_b_x"][j][None], lam=W["rec_lambda"][j][None])


def _local_step(x, target, W):
    T, D = x.shape
    depth = W["norm_g"].shape[0]
    g = lambda l, k: W["norm_g"][l, k][None]
    saved = []
    for l in range(depth):
        x, s0 = _ffn_fwd(x, g(l, 0), g(l, 1), W["ffn_w_in"], W["ffn_w_out"], 2 * l, f"l{l}_ffn0")
        if l % 2 == 0:
            pm = _hyb_params(W, l // 2, D)
            x, s1 = _hyb_fwd(x, g(l, 2), g(l, 3), pm, f"l{l}_hyb")
        else:
            pm = _rec_params(W, l // 2)
            x, s1 = _rec_fwd(x, g(l, 2), g(l, 3), pm, f"l{l}_rec")
        x, s2 = _ffn_fwd(x, g(l, 4), g(l, 5), W["ffn_w_in"], W["ffn_w_out"], 2 * l + 1, f"l{l}_ffn1")
        saved.append((s0, s1, s2, pm))

    def loss_fn(yv, tv):
        err = yv - tv
        part = 0.5 * jnp.sum(jnp.sum(err * err, axis=1, keepdims=True), axis=0, keepdims=True) / D
        return err * (1.0 / D), jnp.broadcast_to(part, (1, D))
    dy, loss_row = _ew(loss_fn, [(x, 0), (target, 0)], [], [F32], [(1, D)], name="loss")
    loss = loss_row[0, 0]

    gn = [[None] * 6 for _ in range(depth)]
    g_ffn_in = [[None, None] for _ in range(depth)]
    g_ffn_out = [[None, None] for _ in range(depth)]
    g_hyb, g_rec = [], []
    for l in reversed(range(depth)):
        s0, s1, s2, pm = saved[l]
        dy, gn[l][4], gn[l][5], g_ffn_in[l][1], g_ffn_out[l][1] = _ffn_bwd(
            dy, s2, g(l, 4), g(l, 5), W["ffn_w_in"], W["ffn_w_out"], 2 * l + 1, f"l{l}_ffn1")
        if l % 2 == 0:
            dy, gn[l][2], gn[l][3], gm = _hyb_bwd(dy, s1, g(l, 2), g(l, 3), pm, f"l{l}_hyb")
            g_hyb.insert(0, gm)
        else:
            dy, gn[l][2], gn[l][3], gm = _rec_bwd(dy, s1, g(l, 2), g(l, 3), pm, f"l{l}_rec")
            g_rec.insert(0, gm)
        dy, gn[l][0], gn[l][1], g_ffn_in[l][0], g_ffn_out[l][0] = _ffn_bwd(
            dy, s0, g(l, 0), g(l, 1), W["ffn_w_in"], W["ffn_w_out"], 2 * l, f"l{l}_ffn0")

    st = lambda items: jnp.stack(items)
    grads = {
        "norm_g": st([st([r[0] for r in row]) for row in gn]),
        "ffn_w_in": [piece for row in g_ffn_in for piece in row],
        "ffn_w_out": [piece for row in g_ffn_out for piece in row],
        "hyb_w_in": st([_hyb_w_in_restore(m["w_in"], D) for m in g_hyb]),
        "ssm_conv_w": st([m["conv_w"] for m in g_hyb]), "ssm_conv_b": st([m["conv_b"] for m in g_hyb]),
        "ssm_dt_bias": st([m["dt_bias"] for m in g_hyb]), "ssm_a_log": st([m["a_log"] for m in g_hyb]),
        "ssm_d": st([m["d"] for m in g_hyb]), "ssm_norm_g": st([m["norm_g"] for m in g_hyb]),
        "fox_b_f": st([m["b_f"] for m in g_hyb]), "hyb_w_out": st([m["w_out"] for m in g_hyb]),
        "rec_w_in": st([m["w_in"] for m in g_rec]), "rec_conv_w": st([m["conv_w"] for m in g_rec]),
        "rec_conv_b": st([m["conv_b"] for m in g_rec]), "rec_w_a": st([m["w_a"] for m in g_rec]),
        "rec_b_a": st([m["b_a"] for m in g_rec]), "rec_w_x": st([m["w_x"] for m in g_rec]),
        "rec_b_x": st([m["b_x"] for m in g_rec]), "rec_lambda": st([m["lam"] for m in g_rec]),
        "rec_w_out": st([m["w_out"] for m in g_rec]),
    }
    return loss, dy, grads


MESH_AXES = ("x", "y", "c")
N_CHIPS = 4
N_DEV = 8
HBM = pl.BlockSpec(memory_space=pltpu.HBM)


def _mesh_pos():
    return tuple(lax.axis_index(n) for n in MESH_AXES)


def _other_chips(x, y):
    chips = [(1 - x, y), (x, 1 - y), (1 - x, 1 - y)]
    return chips, [2 * cx + cy for cx, cy in chips]


def _rcopy(src, dst, send_sem, recv_sem, dev):
    return pltpu.make_async_remote_copy(src_ref=src, dst_ref=dst, send_sem=send_sem, recv_sem=recv_sem,
                                        device_id=dev, device_id_type=pl.DeviceIdType.MESH)


def _comm_params():
    return pltpu.CompilerParams()


DMA_CHUNK_BYTES = 1 << 20
DMA_ROW_ALIGN = 16


def _nchunks(rows, row_bytes):
    n = max(1, min(rows // DMA_ROW_ALIGN, (rows * row_bytes) // DMA_CHUNK_BYTES))
    while n > 1 and (rows % n or (rows // n) % DMA_ROW_ALIGN):
        n -= 1
    return n


def _row_bytes(ref):
    return ref.shape[-1] * jnp.dtype(ref.dtype).itemsize


def _all_gather(arrs, name):
    n = len(arrs)

    def body(*refs):
        ins, outs = refs[:n], refs[n:2 * n]
        send, recv, fsend, frecv = refs[2 * n:]
        x, y, c = _mesh_pos()
        k = 2 * x + y
        sibling = (x, y, 1 - c)
        chips, chip_k = _other_chips(x, y)
        halves = [r.shape[0] // 2 for r in ins]
        nchs = [_nchunks(h, _row_bytes(r)) for h, r in zip(halves, ins)]
        for a in range(n):
            h, step = halves[a], halves[a] // nchs[a]
            for j, chip in enumerate(chips):
                for q in range(nchs[a]):
                    rows = pl.ds(c * h + q * step, step)
                    _rcopy(ins[a].at[rows], outs[a].at[k, rows], send.at[a, j], recv.at[a, j], (*chip, c)).start()
        for a in range(n):
            h, step = halves[a], halves[a] // nchs[a]
            for j, chip in enumerate(chips):
                blk = outs[a].at[chip_k[j], pl.ds(c * h, h)]
                _rcopy(blk, blk, send.at[a, j], recv.at[a, j], (*chip, c)).wait_recv()
                for q in range(nchs[a]):
                    part = outs[a].at[chip_k[j], pl.ds(c * h + q * step, step)]
                    _rcopy(part, part, fsend.at[a, j], frecv.at[a, j], sibling).start()
        for a in range(n):
            h = halves[a]
            for j, chip in enumerate(chips):
                blk = outs[a].at[chip_k[j], pl.ds((1 - c) * h, h)]
                _rcopy(blk, blk, fsend.at[a, j], frecv.at[a, j], sibling).wait_recv()
        for a in range(n):
            h = halves[a]
            for j, chip in enumerate(chips):
                _rcopy(ins[a].at[pl.ds(c * h, h)], outs[a].at[k, pl.ds(c * h, h)], send.at[a, j], recv.at[a, j],
                       (*chip, c)).wait_send()
                blk = outs[a].at[chip_k[j], pl.ds(c * h, h)]
                _rcopy(blk, blk, fsend.at[a, j], frecv.at[a, j], sibling).wait_send()

    outs = pl.pallas_call(
        body, name=name, in_specs=[HBM] * n, out_specs=[HBM] * n,
        out_shape=[jax.ShapeDtypeStruct((N_CHIPS,) + a.shape, a.dtype) for a in arrs],
        scratch_shapes=[pltpu.SemaphoreType.DMA((n, 3))] * 4, compiler_params=_comm_params(),
    )(*arrs)
    k = 2 * lax.axis_index("x") + lax.axis_index("y")
    return [lax.dynamic_update_index_in_dim(o, a, k, 0) for o, a in zip(outs, arrs)]


def _pair_exchange(gs, name):
    n = len(gs)

    def body(*refs):
        ins, theirs = refs[:n], refs[n:2 * n]
        send, recv = refs[2 * n:]
        x, y, c = _mesh_pos()
        sibling = (x, y, 1 - c)
        for a in range(n):
            h = ins[a].shape[1] // 2
            nch = _nchunks(h, _row_bytes(ins[a]))
            step = h // nch
            for kk in range(N_CHIPS):
                for q in range(nch):
                    _rcopy(ins[a].at[kk, pl.ds((1 - c) * h + q * step, step)], theirs[a].at[kk, pl.ds(q * step, step)],
                           send.at[a], recv.at[a], sibling).start()
        for a in range(n):
            h = ins[a].shape[1] // 2
            _rcopy(ins[a].at[:, pl.ds((1 - c) * h, h)], theirs[a], send.at[a], recv.at[a], sibling).wait()

    half = [jax.ShapeDtypeStruct((a.shape[0], a.shape[1] // 2) + a.shape[2:], a.dtype) for a in gs]
    theirs = pl.pallas_call(
        body, name=name, in_specs=[HBM] * n, out_specs=[HBM] * n, out_shape=half,
        scratch_shapes=[pltpu.SemaphoreType.DMA((n,))] * 2, compiler_params=_comm_params(),
    )(*gs)
    c = lax.axis_index("c")
    mine = [lax.dynamic_slice_in_dim(g, c * (g.shape[1] // 2), g.shape[1] // 2, axis=1) for g in gs]
    return mine, theirs


def _chip_exchange(ss, name):
    n = len(ss)

    def body(*refs):
        ins = refs[:n]
        got = [refs[n + 3 * a:n + 3 * a + 3] for a in range(n)]
        send, recv = refs[4 * n:]
        x, y, c = _mesh_pos()
        chips, chip_k = _other_chips(x, y)
        for a in range(n):
            h = ins[a].shape[1]
            nch = _nchunks(h, _row_bytes(ins[a]))
            step = h // nch
            for q in range(nch):
                rows = pl.ds(q * step, step)
                for j, chip in enumerate(chips):
                    _rcopy(ins[a].at[chip_k[j], rows], got[a][j].at[rows], send.at[a, j], recv.at[a, j],
                           (*chip, c)).start()
        for a in range(n):
            for j, chip in enumerate(chips):
                _rcopy(ins[a].at[chip_k[j]], got[a][j], send.at[a, j], recv.at[a, j], (*chip, c)).wait()

    got = [jax.ShapeDtypeStruct(a.shape[1:], a.dtype) for a in ss for _ in range(3)]
    res = pl.pallas_call(
        body, name=name, in_specs=[HBM] * n, out_specs=[HBM] * (3 * n), out_shape=got,
        scratch_shapes=[pltpu.SemaphoreType.DMA((n, 3))] * 2, compiler_params=_comm_params(),
    )(*ss)
    k = 2 * lax.axis_index("x") + lax.axis_index("y")
    own = [lax.dynamic_index_in_dim(s, k, 0, keepdims=False) for s in ss]
    return own, [res[3 * a:3 * a + 3] for a in range(n)]


def _pair_share(rs, name):
    n = len(rs)

    def body(*refs):
        ins, outs = refs[:n], refs[n:2 * n]
        send, recv = refs[2 * n:]
        x, y, c = _mesh_pos()
        sibling = (x, y, 1 - c)
        for a in range(n):
            h = ins[a].shape[0]
            nch = _nchunks(h, _row_bytes(ins[a]))
            step = h // nch
            for q in range(nch):
                rows = pl.ds(q * step, step)
                _rcopy(ins[a].at[rows], outs[a].at[rows], send.at[a], recv.at[a], sibling).start()
        for a in range(n):
            _rcopy(ins[a], outs[a], send.at[a], recv.at[a], sibling).wait()

    theirs = pl.pallas_call(
        body, name=name, in_specs=[HBM] * n, out_specs=[HBM] * n,
        out_shape=[jax.ShapeDtypeStruct(a.shape, a.dtype) for a in rs],
        scratch_shapes=[pltpu.SemaphoreType.DMA((n,))] * 2, compiler_params=_comm_params(),
    )(*rs)
    c = lax.axis_index("c")
    return [jnp.concatenate([jnp.where(c == 0, r, t), jnp.where(c == 0, t, r)], axis=0) for r, t in zip(rs, theirs)]


def _exchange_all(vec, name):
    def body(v_ref, out_ref, send, recv):
        x, y, c = _mesh_pos()
        flip = lambda p, f: 1 - p if f else p
        me = 4 * x + 2 * y + c
        rows_all = v_ref.shape[0]
        nch = _nchunks(rows_all, _row_bytes(v_ref))
        step = rows_all // nch
        for j in range(1, N_DEV):
            fx, fy, fc = (j >> 2) & 1, (j >> 1) & 1, j & 1
            for q in range(nch):
                rows = pl.ds(q * step, step)
                _rcopy(v_ref.at[rows], out_ref.at[me, rows], send.at[j - 1], recv.at[j - 1],
                       (flip(x, fx), flip(y, fy), flip(c, fc))).start()
        for j in range(1, N_DEV):
            fx, fy, fc = (j >> 2) & 1, (j >> 1) & 1, j & 1
            slot = out_ref.at[4 * flip(x, fx) + 2 * flip(y, fy) + flip(c, fc)]
            _rcopy(slot, slot, send.at[j - 1], recv.at[j - 1], (x, y, c)).wait_recv()
        for j in range(1, N_DEV):
            _rcopy(v_ref, out_ref.at[me], send.at[j - 1], recv.at[j - 1], (x, y, c)).wait_send()

    out = pl.pallas_call(
        body, name=name, in_specs=[HBM], out_specs=HBM,
        out_shape=jax.ShapeDtypeStruct((N_DEV,) + vec.shape, vec.dtype),
        scratch_shapes=[pltpu.SemaphoreType.DMA((N_DEV - 1,))] * 2, compiler_params=_comm_params(),
    )(vec)
    me = 4 * lax.axis_index("x") + 2 * lax.axis_index("y") + lax.axis_index("c")
    return lax.dynamic_update_index_in_dim(out, vec, me, 0)


def _rows(a):
    return a.reshape(-1, a.shape[-1])


def _sum_kernel(parts, out_dtype, name):
    def fn(*vals):
        acc = vals[0].astype(F32)
        for v in vals[1:]:
            acc = acc + v.astype(F32)
        return acc
    out = _ew(fn, [(_rows(p), 0) for p in parts], [], [out_dtype], [], name=name)
    return out.reshape(parts[0].shape)


def _reduce_scatter(gs, tag):
    mine, theirs = _pair_exchange(gs, f"{tag}_pair")
    pair = [_sum_kernel([m, t], BF16, f"{tag}_add_pair{a}") for a, (m, t) in enumerate(zip(mine, theirs))]
    own, got = _chip_exchange(pair, f"{tag}_chips")
    red = [_sum_kernel([o, g[0], g[1], g[2]], F32, f"{tag}_add_chips{a}") for a, (o, g) in enumerate(zip(own, got))]
    return _pair_share(red, f"{tag}_share")


def _pack(arrs, row_mult):
    flat = jnp.concatenate([a.reshape(-1).astype(F32) for a in arrs])
    unit = row_mult * LANES
    pad = (-flat.size) % unit
    return jnp.pad(flat, (0, pad)).reshape(-1, LANES)


def _unpack(mat, shapes):
    flat, out, pos = mat.reshape(-1), [], 0
    for s in shapes:
        size = 1
        for d in s:
            size *= d
        out.append(flat[pos:pos + size].reshape(s))
        pos += size
    return out


def _to_shards(a, axis):
    sh = a.shape
    a = a.reshape(sh[:axis] + (N_CHIPS, sh[axis] // N_CHIPS) + sh[axis + 1:])
    return jnp.moveaxis(a, axis, 0)


def _from_shards(g, axis):
    g = jnp.moveaxis(g, 0, axis)
    sh = g.shape
    return g.reshape(sh[:axis] + (sh[axis] * sh[axis + 1],) + sh[axis + 2:])


def _adamw_fn(w, g, m, v):
    m2 = ADAM_B1 * m + (1.0 - ADAM_B1) * g
    v2 = ADAM_B2 * v + (1.0 - ADAM_B2) * (g * g)
    m_hat = m2 / (1.0 - ADAM_B1 ** ADAM_STEP)
    v_hat = v2 / (1.0 - ADAM_B2 ** ADAM_STEP)
    return -ADAM_LR * (m_hat / (jnp.sqrt(v_hat) + ADAM_EPS) + ADAM_WD * w), m2, v2


def _adamw(w, g, m, v, name):
    res = _ew(_adamw_fn, [(_rows(a), 0) for a in (w, g, m, v)], [], [F32, F32, F32], [], name=name)
    return tuple(r.reshape(w.shape) for r in res)


def kernel(x, norm_g, ffn_w_in, ffn_w_out, hyb_w_in, ssm_conv_w, ssm_conv_b, ssm_dt_bias, ssm_a_log, ssm_d, ssm_norm_g, fox_b_f, hyb_w_out, rec_w_in, rec_conv_w, rec_conv_b, rec_w_a, rec_b_a, rec_w_x, rec_b_x, rec_lambda, rec_w_out, loss_target, m_norm_g, m_ffn_w_in, m_ffn_w_out, m_hyb_w_in, m_ssm_conv_w, m_ssm_conv_b, m_ssm_dt_bias, m_ssm_a_log, m_ssm_d, m_ssm_norm_g, m_fox_b_f, m_hyb_w_out, m_rec_w_in, m_rec_conv_w, m_rec_conv_b, m_rec_w_a, m_rec_b_a, m_rec_w_x, m_rec_b_x, m_rec_lambda, m_rec_w_out, v_norm_g, v_ffn_w_in, v_ffn_w_out, v_hyb_w_in, v_ssm_conv_w, v_ssm_conv_b, v_ssm_dt_bias, v_ssm_a_log, v_ssm_d, v_ssm_norm_g, v_fox_b_f, v_hyb_w_out, v_rec_w_in, v_rec_conv_w, v_rec_conv_b, v_rec_w_a, v_rec_b_a, v_rec_w_x, v_rec_b_x, v_rec_lambda, v_rec_w_out):
    given = dict(locals())
    w = {n: given[n] for n in WEIGHTS}
    m = {n: given["m_" + n] for n in WEIGHTS}
    v = {n: given["v_" + n] for n in WEIGHTS}
    k = 2 * lax.axis_index("x") + lax.axis_index("y")

    big_bf16 = [_ew(lambda t: t, [(_rows(w[n]), 0)], [], [BF16], [], name=f"cast_{n}") for n in BIG]
    small_shapes = [w[n].shape for n in SMALL_SHARDED]
    small_pack = _pack([w[n] for n in SMALL_SHARDED], 2 * SUBLANES)
    gathered = _all_gather(big_bf16 + [small_pack], "gather_weights")
    W = {n: w[n] for n in SMALL_REPL}
    for n, g in zip(BIG, gathered[:-1]):
        W[n] = g if n in FFN else _from_shards(g.reshape((N_CHIPS,) + w[n].shape), SHARD_AXIS[n])
    per_chip = [_unpack(gathered[-1][kk], small_shapes) for kk in range(N_CHIPS)]
    for idx, n in enumerate(SMALL_SHARDED):
        W[n] = jnp.concatenate([per_chip[kk][idx] for kk in range(N_CHIPS)], axis=SHARD_AXIS[n])

    loss_part, dy, grads = _local_step(x[0], loss_target[0], W)
    loss = lax.psum(loss_part, MESH_AXES)

    pieces = {n: grads[n] if n in FFN else [_to_shards(grads[n], SHARD_AXIS[n]).reshape(N_CHIPS, -1, w[n].shape[-1])]
              for n in BIG}
    red_big = _reduce_scatter([piece for n in BIG for piece in pieces[n]], "rs")
    g_out, pos = {}, 0
    for n in BIG:
        cnt = len(pieces[n])
        g_out[n] = jnp.concatenate(red_big[pos:pos + cnt], axis=0).reshape(w[n].shape)
        pos += cnt
    small_names = SMALL_SHARDED + SMALL_REPL
    slots = _exchange_all(_pack([grads[n] for n in small_names], DMA_ROW_ALIGN).astype(BF16), "gather_small_grads")
    small_sum = _sum_kernel([slots[d] for d in range(N_DEV)], F32, "add_small_grads")
    for n, g in zip(small_names, _unpack(small_sum, [grads[n].shape for n in small_names])):
        if n in SHARD_AXIS:
            loc = g.shape[SHARD_AXIS[n]] // N_CHIPS
            g = lax.dynamic_slice_in_dim(g, k * loc, loc, axis=SHARD_AXIS[n])
        g_out[n] = g

    delta, new_m, new_v = {}, {}, {}
    for n in BIG:
        delta[n], new_m[n], new_v[n] = _adamw(w[n], g_out[n], m[n], v[n], f"adamw_{n}")
    shapes = [w[n].shape for n in small_names]
    packed = [_pack([d[n] for n in small_names], SUBLANES) for d in (w, g_out, m, v)]
    for d, mat in zip((delta, new_m, new_v), _adamw(*packed, "adamw_small")):
        d.update(zip(small_names, _unpack(mat, shapes)))

    return (loss, dy[None], *[g_out[n] for n in WEIGHTS], *[delta[n] for n in WEIGHTS],
            *[new_m[n] for n in WEIGHTS], *[new_v[n] for n in WEIGHTS])
```

```python
import functools

import jax
import jax.numpy as jnp
from jax import lax
from jax.experimental import pallas as pl
from jax.experimental.pallas import tpu as pltpu

F32, BF16 = jnp.float32, jnp.bfloat16

NORM_EPS = 1e-6
CONV_K = 4
SSM_HEAD_DIM = 64
SSM_STATE = 128
SSM_GROUPS = 2
FOX_HEAD_DIM = 128
RNN_BLOCK = 128
RG_LRU_C = 8.0
ADAM_LR, ADAM_B1, ADAM_B2, ADAM_EPS, ADAM_WD, ADAM_STEP = 0.001, 0.9, 0.999, 1e-08, 0.01, 10

LANES = 128
SUBLANES = 8
VMEM_LIMIT = 48 * 1024 * 1024
SSD_TILE = 256
FOX_TILE = 512
NEG = -1e30
LOG2E = 1.4426950408889634

NT = (((1,), (1,)), ((), ()))
NN = (((1,), (0,)), ((), ()))
TN = (((0,), (0,)), ((), ()))

BIG = ("ffn_w_in", "ffn_w_out", "hyb_w_in", "hyb_w_out", "rec_w_in", "rec_w_out")
FFN = ("ffn_w_in", "ffn_w_out")
SMALL_SHARDED = ("norm_g", "ssm_conv_w", "rec_conv_w", "rec_conv_b", "rec_b_a", "rec_b_x", "rec_lambda")
SMALL_REPL = ("ssm_conv_b", "ssm_dt_bias", "ssm_a_log", "ssm_d", "ssm_norm_g", "fox_b_f", "rec_w_a", "rec_w_x")
WEIGHTS = ("norm_g", "ffn_w_in", "ffn_w_out", "hyb_w_in", "ssm_conv_w", "ssm_conv_b", "ssm_dt_bias", "ssm_a_log",
           "ssm_d", "ssm_norm_g", "fox_b_f", "hyb_w_out", "rec_w_in", "rec_conv_w", "rec_conv_b", "rec_w_a",
           "rec_b_a", "rec_w_x", "rec_b_x", "rec_lambda", "rec_w_out")
SHARD_AXIS = {"norm_g": 2, "ffn_w_in": 3, "ffn_w_out": 2, "hyb_w_in": 2, "ssm_conv_w": 2, "hyb_w_out": 1,
              "rec_w_in": 2, "rec_conv_w": 2, "rec_conv_b": 1, "rec_b_a": 1, "rec_b_x": 1, "rec_lambda": 1,
              "rec_w_out": 1}


def _params(*sem):
    return pltpu.CompilerParams(dimension_semantics=sem if sem else None, vmem_limit_bytes=VMEM_LIMIT)


def _tile(dim, pref):
    if dim <= pref:
        return dim
    for align in (LANES, SUBLANES):
        t = (pref // align) * align
        while t >= align:
            if dim % t == 0:
                return t
            t -= align
    return dim


def _sigmoid(x):
    return 1.0 / (1.0 + jnp.exp(-x))


def _softplus(x):
    return jnp.maximum(x, 0.0) + jnp.log(1.0 + jnp.exp(-jnp.abs(x)))


def _log_sigmoid(x):
    return -_softplus(-x)


def _silu(x):
    return x * _sigmoid(x)


def _gelu_tanh(x):
    return 0.5 * x * (1.0 + jnp.tanh(0.7978845608028654 * (x + 0.044715 * x * x * x)))


def _neg_expm1(x):
    series = -x * (1.0 + x * (0.5 + x * (1.0 / 6.0)))
    return jnp.where(x > -1e-2, series, 1.0 - jnp.exp(x))


def _rms(x, g):
    xf = x.astype(F32)
    return xf * lax.rsqrt(jnp.mean(xf * xf, axis=-1, keepdims=True) + NORM_EPS) * g


def _mm(a, b, mode, out_dtype, name, tm=512, tn=512, tk=512, b_sel=()):
    bshape = b.shape[len(b_sel):]
    if mode == "nn":
        (M, K), (K2, N) = a.shape, bshape
    elif mode == "nt":
        (M, K), (N, K2) = a.shape, bshape
    else:
        (K, M), (K2, N) = a.shape, bshape
    assert K == K2, (a.shape, b.shape, mode)
    tm, tn, tk = _tile(M, tm), _tile(N, tn), _tile(K, tk)
    nk = K // tk
    dims = {"nn": NN, "nt": NT, "tn": TN}[mode]
    lead = (None,) * len(b_sel)
    if mode == "tn":
        a_spec = pl.BlockSpec((tk, tm), lambda n, m, k: (k, m))
    else:
        a_spec = pl.BlockSpec((tm, tk), lambda n, m, k: (m, k))
    if mode == "nt":
        b_spec = pl.BlockSpec(lead + (tn, tk), lambda n, m, k: (*b_sel, n, k))
    else:
        b_spec = pl.BlockSpec(lead + (tk, tn), lambda n, m, k: (*b_sel, k, n))

    def body(a_ref, b_ref, o_ref, *acc):
        p = lax.dot_general(a_ref[...].astype(BF16), b_ref[...].astype(BF16), dims, preferred_element_type=F32)
        if nk == 1:
            o_ref[...] = p.astype(o_ref.dtype)
        else:
            acc_ref, = acc
            k = pl.program_id(2)

            @pl.when(k == 0)
            def _():
                acc_ref[...] = p

            @pl.when(k > 0)
            def _():
                acc_ref[...] += p

            @pl.when(k == nk - 1)
            def _():
                o_ref[...] = acc_ref[...].astype(o_ref.dtype)

    return pl.pallas_call(
        body, name=name, grid=(N // tn, M // tm, nk),
        in_specs=[a_spec, b_spec], out_specs=pl.BlockSpec((tm, tn), lambda n, m, k: (m, n)),
        out_shape=jax.ShapeDtypeStruct((M, N), out_dtype),
        scratch_shapes=[pltpu.VMEM((tm, tn), F32)] if nk > 1 else [],
        compiler_params=_params("parallel", "parallel", "arbitrary"),
    )(a, b)


def _mm_parts(parts, mode, out_dtype, name, tm=512, tn=512):
    M = parts[0][0].shape[0]
    N = parts[0][1].shape[1] if mode == "nn" else parts[0][1].shape[0]
    tm, tn = _tile(M, tm), _tile(N, tn)
    dims = NN if mode == "nn" else NT
    in_specs, args = [], []
    for a, b in parts:
        K = a.shape[1]
        in_specs.append(pl.BlockSpec((tm, K), lambda n, m: (m, 0)))
        if mode == "nn":
            in_specs.append(pl.BlockSpec((K, tn), lambda n, m: (0, n)))
        else:
            in_specs.append(pl.BlockSpec((tn, K), lambda n, m: (n, 0)))
        args += [a, b]

    def body(*refs):
        o_ref = refs[-1]
        acc = None
        for p in range(len(parts)):
            d = lax.dot_general(refs[2 * p][...].astype(BF16), refs[2 * p + 1][...].astype(BF16), dims,
                                preferred_element_type=F32)
            acc = d if acc is None else acc + d
        o_ref[...] = acc.astype(o_ref.dtype)

    return pl.pallas_call(
        body, name=name, grid=(N // tn, M // tm), in_specs=in_specs,
        out_specs=pl.BlockSpec((tm, tn), lambda n, m: (m, n)), out_shape=jax.ShapeDtypeStruct((M, N), out_dtype),
        compiler_params=_params("parallel", "parallel"),
    )(*args)


def _ew(fn, tiled, params, outs, reds, *, name, tm=256, cb=None, ncb=1):
    T = tiled[0][0].shape[0]
    cb = tiled[0][0].shape[1] if cb is None else cb
    tm = _tile(T, tm)
    in_specs, args = [], []
    for arr, off in tiled:
        in_specs.append(pl.BlockSpec((tm, cb), functools.partial(lambda n, i, o: (i, n + o), o=off)))
        args.append(arr)
    for arr, off in params:
        if arr.ndim == 2:
            in_specs.append(pl.BlockSpec((arr.shape[0], cb), functools.partial(lambda n, i, o: (0, n + o), o=off)))
        else:
            in_specs.append(pl.BlockSpec((None,) + arr.shape[1:], lambda n, i: (n, 0, 0)))
        args.append(arr)
    out_shape = [jax.ShapeDtypeStruct((T, cb * ncb), dt) for dt in outs]
    out_specs = [pl.BlockSpec((tm, cb), lambda n, i: (i, n)) for _ in outs]
    for shape in reds:
        out_shape.append(jax.ShapeDtypeStruct(shape, F32))
        if len(shape) == 2:
            out_specs.append(pl.BlockSpec((shape[0], cb), lambda n, i: (0, n)))
        else:
            out_specs.append(pl.BlockSpec((None,) + tuple(shape[1:]), lambda n, i: (n, 0, 0)))
    n_in, n_out = len(args), len(outs)

    def body(*refs):
        i = pl.program_id(1)
        res = fn(*[r[...] for r in refs[:n_in]])
        res = res if isinstance(res, (tuple, list)) else (res,)
        for r, v in zip(refs[n_in:n_in + n_out], res[:n_out]):
            r[...] = v.astype(r.dtype)
        for r, v in zip(refs[n_in + n_out:], res[n_out:]):
            @pl.when(i == 0)
            def _():
                r[...] = jnp.zeros(r.shape, r.dtype)

            r[...] += v.astype(r.dtype).reshape(r.shape)

    res = pl.pallas_call(
        body, name=name, grid=(ncb, T // tm), in_specs=in_specs, out_specs=out_specs, out_shape=out_shape,
        compiler_params=_params("arbitrary", "arbitrary"),
    )(*args)
    return res[0] if len(res) == 1 else tuple(res)


def _rms_fwd(x, g, name):
    return _ew(lambda xv, gv: _rms(xv, gv), [(x, 0)], [(g, 0)], [BF16], [], name=name)


def _rms_bwd_add(x, g, dn, dres, name):
    def fn(xv, dnv, drv, gv):
        _, vjp = jax.vjp(_rms, xv, gv)
        dx, dg = vjp(dnv.astype(F32))
        return drv + dx, dg
    return _ew(fn, [(x, 0), (dn, 0), (dres, 0)], [(g, 0)], [F32], [g.shape], name=name)


def _post_fwd(x, h, g, w, name):
    return _ew(lambda xv, hv, gv: xv + w * _rms(hv, gv), [(x, 0), (h, 0)], [(g, 0)], [F32], [], name=name)


def _post_bwd(dy, h, g, w, name):
    def fn(dyv, hv, gv):
        _, vjp = jax.vjp(lambda a, b: w * _rms(a, b), hv, gv)
        return vjp(dyv)
    return _ew(fn, [(dy, 0), (h, 0)], [(g, 0)], [BF16], [g.shape], name=name)


def _mm_swiglu(x, g_in, blk, name, tm=512):
    T, K = x.shape
    tn = g_in.shape[-1]
    F = 2 * tn
    tm = _tile(T, tm)

    def body(a_ref, bg_ref, bu_ref, g_ref, u_ref, act_ref):
        a = a_ref[...].astype(BF16)
        g = jnp.dot(a, bg_ref[...].astype(BF16), preferred_element_type=F32)
        u = jnp.dot(a, bu_ref[...].astype(BF16), preferred_element_type=F32)
        g_ref[...] = g.astype(g_ref.dtype)
        u_ref[...] = u.astype(u_ref.dtype)
        act_ref[...] = (_silu(g) * u).astype(act_ref.dtype)

    out = jax.ShapeDtypeStruct((T, F), BF16)
    o_spec = pl.BlockSpec((tm, tn), lambda n, m: (m, n))
    return pl.pallas_call(
        body, name=name, grid=(2, T // tm),
        in_specs=[pl.BlockSpec((tm, K), lambda n, m: (m, 0)),
                  pl.BlockSpec((None, K, tn), lambda n, m: (n, blk, 0)),
                  pl.BlockSpec((None, K, tn), lambda n, m: (n + 2, blk, 0))],
        out_specs=[o_spec] * 3, out_shape=[out] * 3, compiler_params=_params("parallel", "parallel"),
    )(x, g_in, g_in)


def _ffn_mm_out(a, g_out, blk, x, g_post, name, tm=512):
    T, F = a.shape
    rl, D = F // N_CHIPS, g_out.shape[-1]
    tm = _tile(T, tm)

    def body(a_ref, b0_ref, b1_ref, x_ref, g_ref, h_ref, x2_ref, acc_ref):
        k = pl.program_id(1)
        b = jnp.concatenate([b0_ref[...], b1_ref[...]], axis=0).astype(BF16)
        p = jnp.dot(a_ref[...].astype(BF16), b, preferred_element_type=F32)

        @pl.when(k == 0)
        def _():
            acc_ref[...] = p

        @pl.when(k == 1)
        def _():
            h = acc_ref[...] + p
            h_ref[...] = h
            x2_ref[...] = x_ref[...] + 0.5 * _rms(h, g_ref[...])

    row = pl.BlockSpec((tm, D), lambda m, k: (m, 0))
    return pl.pallas_call(
        body, name=name, grid=(T // tm, 2),
        in_specs=[pl.BlockSpec((tm, 2 * rl), lambda m, k: (m, k)),
                  pl.BlockSpec((None, rl, D), lambda m, k: (2 * k, blk, 0)),
                  pl.BlockSpec((None, rl, D), lambda m, k: (2 * k + 1, blk, 0)),
                  row, pl.BlockSpec((1, D), lambda m, k: (0, 0))],
        out_specs=[row, row], out_shape=[jax.ShapeDtypeStruct((T, D), F32)] * 2,
        scratch_shapes=[pltpu.VMEM((tm, D), F32)], compiler_params=_params("parallel", "arbitrary"),
    )(a, g_out, g_out, x, g_post)


def _ffn_mm_da(dh, g_out, blk, F, name, tm=512):
    T, D = dh.shape
    rl = F // N_CHIPS
    tm = _tile(T, tm)

    def body(a_ref, b0_ref, b1_ref, o_ref):
        b = jnp.concatenate([b0_ref[...], b1_ref[...]], axis=0).astype(BF16)
        o_ref[...] = lax.dot_general(a_ref[...].astype(BF16), b, NT, preferred_element_type=F32).astype(o_ref.dtype)

    return pl.pallas_call(
        body, name=name, grid=(2, T // tm),
        in_specs=[pl.BlockSpec((tm, D), lambda n, m: (m, 0)),
                  pl.BlockSpec((None, rl, D), lambda n, m: (2 * n, blk, 0)),
                  pl.BlockSpec((None, rl, D), lambda n, m: (2 * n + 1, blk, 0))],
        out_specs=pl.BlockSpec((tm, 2 * rl), lambda n, m: (m, n)), out_shape=jax.ShapeDtypeStruct((T, F), BF16),
        compiler_params=_params("parallel", "parallel"),
    )(dh, g_out, g_out)


def _ffn_mm_dn(dgu, g_in, blk, D, name, tm=512):
    T = dgu.shape[0]
    cw = g_in.shape[-1]
    tm = _tile(T, tm)

    def body(a_ref, b0_ref, b1_ref, o_ref, acc_ref):
        k = pl.program_id(1)
        a = a_ref[...].astype(BF16)
        p = lax.dot_general(a[:, :cw], b0_ref[...].astype(BF16), NT, preferred_element_type=F32)
        p = p + lax.dot_general(a[:, cw:], b1_ref[...].astype(BF16), NT, preferred_element_type=F32)

        @pl.when(k == 0)
        def _():
            acc_ref[...] = p

        @pl.when(k == 1)
        def _():
            o_ref[...] = acc_ref[...] + p

    return pl.pallas_call(
        body, name=name, grid=(T // tm, 2),
        in_specs=[pl.BlockSpec((tm, 2 * cw), lambda m, k: (m, k)),
                  pl.BlockSpec((None, D, cw), lambda m, k: (2 * k, blk, 0)),
                  pl.BlockSpec((None, D, cw), lambda m, k: (2 * k + 1, blk, 0))],
        out_specs=pl.BlockSpec((tm, D), lambda m, k: (m, 0)), out_shape=jax.ShapeDtypeStruct((T, D), F32),
        scratch_shapes=[pltpu.VMEM((tm, D), F32)], compiler_params=_params("parallel", "arbitrary"),
    )(dgu, g_in, g_in)


def _mm_tn_shards(a, b, by_rows, name, tk=1024):
    (K, M), (_, N) = a.shape, b.shape
    tk = _tile(K, tk)
    nk = K // tk
    if by_rows:
        rl = M // N_CHIPS
        a_spec = pl.BlockSpec((tk, 2 * rl), lambda j, k: (k, j))
        b_spec = pl.BlockSpec((tk, N), lambda j, k: (k, 0))
        o_spec = pl.BlockSpec((2, rl, N), lambda j, k: (j, 0, 0))
        out_shape, acc_shape, steps = (N_CHIPS, rl, N), (2 * rl, N), 2
    else:
        cw = N // N_CHIPS
        a_spec = pl.BlockSpec((tk, M), lambda j, k: (k, 0))
        b_spec = pl.BlockSpec((tk, cw), lambda j, k: (k, j))
        o_spec = pl.BlockSpec((None, M, cw), lambda j, k: (j, 0, 0))
        out_shape, acc_shape, steps = (N_CHIPS, M, cw), (M, cw), N_CHIPS

    def body(a_ref, b_ref, o_ref, acc_ref):
        k = pl.program_id(1)
        p = lax.dot_general(a_ref[...].astype(BF16), b_ref[...].astype(BF16), TN, preferred_element_type=F32)

        @pl.when(k == 0)
        def _():
            acc_ref[...] = p

        @pl.when(k > 0)
        def _():
            acc_ref[...] += p

        @pl.when(k == nk - 1)
        def _():
            if by_rows:
                o_ref[0] = acc_ref[:rl, :].astype(o_ref.dtype)
                o_ref[1] = acc_ref[rl:, :].astype(o_ref.dtype)
            else:
                o_ref[...] = acc_ref[...].astype(o_ref.dtype)

    return pl.pallas_call(
        body, name=name, grid=(steps, nk), in_specs=[a_spec, b_spec], out_specs=o_spec,
        out_shape=jax.ShapeDtypeStruct(out_shape, BF16), scratch_shapes=[pltpu.VMEM(acc_shape, F32)],
        compiler_params=_params("parallel", "arbitrary"),
    )(a, b)


def _swiglu_bwd(gate, up, da, name):
    T, F = gate.shape
    tm = _tile(T, 256)

    def body(g_ref, u_ref, da_ref, o_ref):
        g, u, d = g_ref[...].astype(F32), u_ref[...].astype(F32), da_ref[...].astype(F32)
        s = _sigmoid(g)
        o_ref[:, :F] = (d * u * (s * (1.0 + g * (1.0 - s)))).astype(o_ref.dtype)
        o_ref[:, F:] = (d * g * s).astype(o_ref.dtype)

    spec = pl.BlockSpec((tm, F), lambda i: (i, 0))
    return pl.pallas_call(
        body, name=name, grid=(T // tm,), in_specs=[spec] * 3, out_specs=pl.BlockSpec((tm, 2 * F), lambda i: (i, 0)),
        out_shape=jax.ShapeDtypeStruct((T, 2 * F), BF16), compiler_params=_params("parallel"),
    )(gate, up, da)


def _ffn_fwd(x, g_pre, g_post, g_in, g_out, blk, tag):
    n = _rms_fwd(x, g_pre, f"{tag}_rms")
    gate, up, a = _mm_swiglu(n, g_in, blk, f"{tag}_mm_in")
    h, x2 = _ffn_mm_out(a, g_out, blk, x, g_post, f"{tag}_mm_out")
    return x2, (x, n, gate, up, a, h)


def _ffn_bwd(dy, saved, g_pre, g_post, g_in, g_out, blk, tag):
    x, n, gate, up, a, h = saved
    dh, dg_post = _post_bwd(dy, h, g_post, 0.5, f"{tag}_post_b")
    da = _ffn_mm_da(dh, g_out, blk, a.shape[1], f"{tag}_mm_da")
    dw_out = _mm_tn_shards(a, dh, True, f"{tag}_mm_dwout")
    dgu = _swiglu_bwd(gate, up, da, f"{tag}_swiglu_b")
    dn = _ffn_mm_dn(dgu, g_in, blk, x.shape[1], f"{tag}_mm_dn")
    dw_in = _mm_tn_shards(n, dgu, False, f"{tag}_mm_dwin")
    dy2, dg_pre = _rms_bwd_add(x, g_pre, dn, dy, f"{tag}_rms_b")
    return dy2, dg_pre, dg_post, dw_in, dw_out


def _shift_down(x, s):
    if s == 0:
        return x
    rows = lax.broadcasted_iota(jnp.int32, x.shape, 0)
    return jnp.where(rows >= s, pltpu.roll(x, s, 0), 0.0)


def _shift_up(x, s):
    if s == 0:
        return x
    T = x.shape[0]
    rows = lax.broadcasted_iota(jnp.int32, x.shape, 0)
    return jnp.where(rows < T - s, pltpu.roll(x, T - s, 0), 0.0)


def _conv_pre(x, w, b):
    y = b
    for k in range(CONV_K):
        y = y + w[k:k + 1, :] * _shift_down(x, CONV_K - 1 - k)
    return y


def _conv_fwd(x, col0, C, w, b, act, name, ct=256):
    T = x.shape[0]
    off = col0 // ct

    def body(x_ref, w_ref, b_ref, o_ref):
        y = _conv_pre(x_ref[...], w_ref[...], b_ref[...])
        o_ref[...] = _silu(y) if act else y

    return pl.pallas_call(
        body, name=name, grid=(C // ct,),
        in_specs=[pl.BlockSpec((T, ct), lambda j: (0, j + off)), pl.BlockSpec((CONV_K, ct), lambda j: (0, j)),
                  pl.BlockSpec((1, ct), lambda j: (0, j))],
        out_specs=pl.BlockSpec((T, ct), lambda j: (0, j)), out_shape=jax.ShapeDtypeStruct((T, C), F32),
        compiler_params=_params("parallel"),
    )(x, w, b)


def _conv_bwd(x, col0, C, w, b, dyact, act, name, ct=256):
    T = x.shape[0]
    off = col0 // ct

    def body(x_ref, w_ref, b_ref, dy_ref, dx_ref, dw_ref, db_ref):
        xv, wv = x_ref[...], w_ref[...]
        dy = dy_ref[...].astype(F32)
        if act:
            pre = _conv_pre(xv, wv, b_ref[...])
            s = _sigmoid(pre)
            dy = dy * (s * (1.0 + pre * (1.0 - s)))
        dx = jnp.zeros_like(dy)
        dws = []
        for k in range(CONV_K):
            dx = dx + wv[k:k + 1, :] * _shift_up(dy, CONV_K - 1 - k)
            dws.append(jnp.sum(dy * _shift_down(xv, CONV_K - 1 - k), axis=0, keepdims=True))
        dx_ref[...] = dx.astype(dx_ref.dtype)
        dw_ref[...] = jnp.concatenate(dws, axis=0)
        db_ref[...] = jnp.sum(dy, axis=0, keepdims=True)

    return pl.pallas_call(
        body, name=name, grid=(C // ct,),
        in_specs=[pl.BlockSpec((T, ct), lambda j: (0, j + off)), pl.BlockSpec((CONV_K, ct), lambda j: (0, j)),
                  pl.BlockSpec((1, ct), lambda j: (0, j)), pl.BlockSpec((T, ct), lambda j: (0, j))],
        out_specs=[pl.BlockSpec((T, ct), lambda j: (0, j)), pl.BlockSpec((CONV_K, ct), lambda j: (0, j)),
                   pl.BlockSpec((1, ct), lambda j: (0, j))],
        out_shape=[jax.ShapeDtypeStruct((T, C), BF16), jax.ShapeDtypeStruct((CONV_K, C), F32),
                   jax.ShapeDtypeStruct((1, C), F32)],
        compiler_params=_params("parallel"),
    )(x, w, b, dyact)


def _gate_act(v, n_ssm):
    lane = lax.broadcasted_iota(jnp.int32, v.shape, 1)
    return jnp.where(lane < n_ssm, _softplus(v), _log_sigmoid(v))


def _gates_fwd(proj, col0, bias, mult, n_ssm, name, tb=512):
    T = proj.shape[0]
    tb = _tile(T, tb)
    off = col0 // LANES

    def body(s_ref, bias_ref, mult_ref, act_ref, cs_ref, carry_ref):
        i = pl.program_id(0)

        @pl.when(i == 0)
        def _():
            carry_ref[...] = jnp.zeros_like(carry_ref)

        act = _gate_act(s_ref[...] + bias_ref[...], n_ssm)
        inc = act * mult_ref[...]
        r = lax.broadcasted_iota(jnp.int32, (tb, tb), 0)
        c = lax.broadcasted_iota(jnp.int32, (tb, tb), 1)
        tri = jnp.where(r >= c, 1.0, 0.0).astype(F32)
        cs = jnp.dot(tri, inc, precision=lax.Precision.HIGHEST, preferred_element_type=F32) + carry_ref[...]
        act_ref[...] = act
        cs_ref[...] = cs
        carry_ref[...] = cs[tb - 1:tb, :]

    return pl.pallas_call(
        body, name=name, grid=(T // tb,),
        in_specs=[pl.BlockSpec((tb, LANES), lambda i: (i, off)), pl.BlockSpec((1, LANES), lambda i: (0, 0)),
                  pl.BlockSpec((1, LANES), lambda i: (0, 0))],
        out_specs=[pl.BlockSpec((tb, LANES), lambda i: (i, 0))] * 2,
        out_shape=[jax.ShapeDtypeStruct((T, LANES), F32)] * 2,
        scratch_shapes=[pltpu.VMEM((1, LANES), F32)], compiler_params=_params("arbitrary"),
    )(proj, bias, mult)


def _gates_bwd(proj, col0, bias, mult, n_ssm, dact, dcs, name, tb=512):
    T = proj.shape[0]
    tb = _tile(T, tb)
    nb = T // tb
    off = col0 // LANES

    def body(s_ref, bias_ref, mult_ref, dact_ref, dcs_ref, ds_ref, dmult_ref, dbias_ref, carry_ref):
        i = pl.program_id(0)

        @pl.when(i == 0)
        def _():
            carry_ref[...] = jnp.zeros_like(carry_ref)
            dmult_ref[...] = jnp.zeros_like(dmult_ref)
            dbias_ref[...] = jnp.zeros_like(dbias_ref)

        v = s_ref[...] + bias_ref[...]
        act = _gate_act(v, n_ssm)
        r = lax.broadcasted_iota(jnp.int32, (tb, tb), 0)
        c = lax.broadcasted_iota(jnp.int32, (tb, tb), 1)
        tri = jnp.where(r <= c, 1.0, 0.0).astype(F32)
        dinc = jnp.dot(tri, dcs_ref[...], precision=lax.Precision.HIGHEST, preferred_element_type=F32) + carry_ref[...]
        carry_ref[...] = dinc[0:1, :]
        da = dact_ref[...] + dinc * mult_ref[...]
        sg = _sigmoid(v)
        lane = lax.broadcasted_iota(jnp.int32, v.shape, 1)
        dv = da * jnp.where(lane < n_ssm, sg, 1.0 - sg)
        ds_ref[...] = dv.astype(ds_ref.dtype)
        dmult_ref[...] += jnp.sum(dinc * act, axis=0, keepdims=True)
        dbias_ref[...] += jnp.sum(dv, axis=0, keepdims=True)

    rev = lambda i: (nb - 1 - i, 0)
    return pl.pallas_call(
        body, name=name, grid=(nb,),
        in_specs=[pl.BlockSpec((tb, LANES), lambda i: (nb - 1 - i, off)), pl.BlockSpec((1, LANES), lambda i: (0, 0)),
                  pl.BlockSpec((1, LANES), lambda i: (0, 0)), pl.BlockSpec((tb, LANES), rev),
                  pl.BlockSpec((tb, LANES), rev)],
        out_specs=[pl.BlockSpec((tb, LANES), rev), pl.BlockSpec((1, LANES), lambda i: (0, 0)),
                   pl.BlockSpec((1, LANES), lambda i: (0, 0))],
        out_shape=[jax.ShapeDtypeStruct((T, LANES), BF16), jax.ShapeDtypeStruct((1, LANES), F32),
                   jax.ShapeDtypeStruct((1, LANES), F32)],
        scratch_shapes=[pltpu.VMEM((1, LANES), F32)], compiler_params=_params("arbitrary"),
    )(proj, bias, mult, dact, dcs)


def _rep_layout(v):
    return jnp.repeat(v, LANES, axis=1)


def _row_layout(v, tk):
    T, H = v.shape
    return v.T.reshape(H, T // tk, 1, tk)


def _causal(tq):
    r = lax.broadcasted_iota(jnp.int32, (tq, tq), 0)
    c = lax.broadcasted_iota(jnp.int32, (tq, tq), 1)
    return r >= c


def _ssd2_fwd(xbc, X, XK, cs_rep, cs_row, cs_full, r_end, name, tq=SSD_TILE):
    T = X.shape[0]
    tq = _tile(T, tq)
    nq = T // tq
    d_ssm = X.shape[1]
    gw = d_ssm // SSM_GROUPS
    hpg = gw // SSM_HEAD_DIM
    b_off = d_ssm // SSM_STATE
    c_off = b_off + SSM_GROUPS

    def body(c_ref, b_ref, x_ref, xk_ref, csq_ref, csk_ref, csf_ref, rend_ref, y_ref):
        i = pl.program_id(1)
        c = c_ref[...].astype(BF16)
        csf = csf_ref[...]
        r = csf[0:1, :]

        def off(j, acc):
            r0 = pl.multiple_of(j * tq, tq)
            s = lax.dot_general(c, b_ref[pl.ds(r0, tq), :].astype(BF16), NT, preferred_element_type=F32)
            z = jnp.dot(s.astype(BF16), xk_ref[pl.ds(r0, tq), :], preferred_element_type=F32)
            return acc + z * jnp.exp(r - rend_ref[j])

        y_off = jnp.exp(csf - r) * lax.fori_loop(0, i, off, jnp.zeros((tq, gw), F32))

        r0 = pl.multiple_of(i * tq, tq)
        half = lax.broadcasted_iota(jnp.int32, (tq, LANES), 1) // SSM_HEAD_DIM
        mask = _causal(tq)
        s = lax.dot_general(c, b_ref[pl.ds(r0, tq), :].astype(BF16), NT, preferred_element_type=F32)
        out = []
        for p in range(hpg // 2):
            xp = x_ref[:, p * LANES:(p + 1) * LANES]
            a = jnp.zeros((tq, LANES), F32)
            for e in range(2):
                h = 2 * p + e
                diff = jnp.tile(csq_ref[:, h * LANES:(h + 1) * LANES], (1, tq // LANES)) - csk_ref[h, i]
                pm = (s * jnp.exp(jnp.where(mask, diff, NEG))).astype(BF16)
                xm = jnp.where(half == e, xp, jnp.zeros_like(xp))
                a = a + jnp.dot(pm, xm, preferred_element_type=F32)
            out.append(a)
        y_ref[...] = y_off + jnp.concatenate(out, axis=1)

    return pl.pallas_call(
        body, name=name, grid=(SSM_GROUPS, nq),
        in_specs=[pl.BlockSpec((tq, SSM_STATE), lambda g, i: (i, c_off + g)),
                  pl.BlockSpec((T, SSM_STATE), lambda g, i: (0, b_off + g)),
                  pl.BlockSpec((tq, gw), lambda g, i: (i, g)),
                  pl.BlockSpec((T, gw), lambda g, i: (0, g)),
                  pl.BlockSpec((tq, hpg * LANES), lambda g, i: (i, g)),
                  pl.BlockSpec((hpg, nq, 1, tq), lambda g, i: (g, 0, 0, 0)),
                  pl.BlockSpec((tq, gw), lambda g, i: (i, g)),
                  pl.BlockSpec((nq, 1, gw), lambda g, i: (0, 0, g))],
        out_specs=pl.BlockSpec((tq, gw), lambda g, i: (i, g)),
        out_shape=jax.ShapeDtypeStruct((T, d_ssm), F32), compiler_params=_params("parallel", "arbitrary"),
    )(xbc, xbc, X, XK, cs_rep, cs_row, cs_full, r_end)


def _ssd2_bwd(xbc, X, XK, cs_rep, cs_row, cs_full, r_end, dY, name, tq=SSD_TILE):
    T = X.shape[0]
    tq = _tile(T, tq)
    nq = T // tq
    d_ssm = X.shape[1]
    gw = d_ssm // SSM_GROUPS
    hpg = gw // SSM_HEAD_DIM
    nheads = d_ssm // SSM_HEAD_DIM
    b_off = d_ssm // SSM_STATE
    c_off = b_off + SSM_GROUPS

    def body(c_ref, b_ref, x_ref, xk_ref, dy_ref, csq_ref, csk_ref, csf_ref, rend_ref,
             dc_ref, db_ref, dx_ref, dxk_ref, dcsq_ref, dcsk_ref, dcsf_ref):
        i = pl.program_id(1)

        @pl.when(i == 0)
        def _():
            db_ref[...] = jnp.zeros_like(db_ref)
            dxk_ref[...] = jnp.zeros_like(dxk_ref)

        c = c_ref[...].astype(BF16)
        csf = csf_ref[...]
        r = csf[0:1, :]
        dyt = dy_ref[...].astype(F32) * jnp.exp(csf - r)

        def off(j, carry):
            dc_acc, rs_acc = carry
            r0 = pl.multiple_of(j * tq, tq)
            b = b_ref[pl.ds(r0, tq), :].astype(BF16)
            xk = xk_ref[pl.ds(r0, tq), :]
            sb = lax.dot_general(c, b, NT, preferred_element_type=F32).astype(BF16)
            dye = dyt * jnp.exp(r - rend_ref[j])
            dyb = dye.astype(BF16)
            rs_acc = rs_acc + dyb.astype(F32) * jnp.dot(sb, xk, preferred_element_type=F32)
            dxk_ref[pl.ds(r0, tq), :] += lax.dot_general(sb, dyb, TN, preferred_element_type=F32)
            dsb = lax.dot_general(dyb, xk, NT, preferred_element_type=F32).astype(BF16)
            dc_acc = dc_acc + jnp.dot(dsb, b, preferred_element_type=F32)
            db_ref[pl.ds(r0, tq), :] += lax.dot_general(dsb, c, TN, preferred_element_type=F32)
            return dc_acc, rs_acc

        dc_acc, rs_acc = lax.fori_loop(0, i, off, (jnp.zeros((tq, SSM_STATE), F32), jnp.zeros((tq, gw), F32)))
        dcsf_ref[...] = rs_acc

        r0 = pl.multiple_of(i * tq, tq)
        half = lax.broadcasted_iota(jnp.int32, (tq, LANES), 1) // SSM_HEAD_DIM
        mask = _causal(tq)
        b = b_ref[pl.ds(r0, tq), :].astype(BF16)
        s = lax.dot_general(c, b, NT, preferred_element_type=F32)
        ds_tot = jnp.zeros((tq, tq), F32)
        rows, dxs = [], []
        for p in range(hpg // 2):
            cols = slice(p * LANES, (p + 1) * LANES)
            xp = x_ref[:, cols]
            dyp = dy_ref[:, cols]
            dx_p = jnp.zeros((tq, LANES), F32)
            for e in range(2):
                h = 2 * p + e
                diff = jnp.tile(csq_ref[:, h * LANES:(h + 1) * LANES], (1, tq // LANES)) - csk_ref[h, i]
                decay = jnp.exp(jnp.where(mask, diff, NEG))
                dym = jnp.where(half == e, dyp, jnp.zeros_like(dyp))
                g = lax.dot_general(dym, xp, NT, preferred_element_type=F32) * decay
                ds_tot = ds_tot + g
                m = g * s
                rows.append(jnp.broadcast_to(jnp.sum(m, axis=1, keepdims=True), (tq, LANES)))
                dcsk_ref[h, i] = -jnp.sum(m, axis=0, keepdims=True)
                dx_p = dx_p + lax.dot_general((s * decay).astype(BF16), dym, TN, preferred_element_type=F32)
            dxs.append(dx_p)
        dsb = ds_tot.astype(BF16)
        dc_ref[...] = dc_acc + jnp.dot(dsb, b, preferred_element_type=F32)
        db_ref[pl.ds(r0, tq), :] += lax.dot_general(dsb, c, TN, preferred_element_type=F32)
        dx_ref[...] = jnp.concatenate(dxs, axis=1)
        dcsq_ref[...] = jnp.concatenate(rows, axis=1)

    return pl.pallas_call(
        body, name=name, grid=(SSM_GROUPS, nq),
        in_specs=[pl.BlockSpec((tq, SSM_STATE), lambda g, i: (i, c_off + g)),
                  pl.BlockSpec((T, SSM_STATE), lambda g, i: (0, b_off + g)),
                  pl.BlockSpec((tq, gw), lambda g, i: (i, g)),
                  pl.BlockSpec((T, gw), lambda g, i: (0, g)),
                  pl.BlockSpec((tq, gw), lambda g, i: (i, g)),
                  pl.BlockSpec((tq, hpg * LANES), lambda g, i: (i, g)),
                  pl.BlockSpec((hpg, nq, 1, tq), lambda g, i: (g, 0, 0, 0)),
                  pl.BlockSpec((tq, gw), lambda g, i: (i, g)),
                  pl.BlockSpec((nq, 1, gw), lambda g, i: (0, 0, g))],
        out_specs=[pl.BlockSpec((tq, SSM_STATE), lambda g, i: (i, g)),
                   pl.BlockSpec((T, SSM_STATE), lambda g, i: (0, g)),
                   pl.BlockSpec((tq, gw), lambda g, i: (i, g)),
                   pl.BlockSpec((T, gw), lambda g, i: (0, g)),
                   pl.BlockSpec((tq, hpg * LANES), lambda g, i: (i, g)),
                   pl.BlockSpec((hpg, nq, 1, tq), lambda g, i: (g, 0, 0, 0)),
                   pl.BlockSpec((tq, gw), lambda g, i: (i, g))],
        out_shape=[jax.ShapeDtypeStruct((T, SSM_GROUPS * SSM_STATE), F32),
                   jax.ShapeDtypeStruct((T, SSM_GROUPS * SSM_STATE), F32),
                   jax.ShapeDtypeStruct((T, d_ssm), F32),
                   jax.ShapeDtypeStruct((T, d_ssm), F32),
                   jax.ShapeDtypeStruct((T, nheads * LANES), F32),
                   jax.ShapeDtypeStruct((nheads, nq, 1, tq), F32),
                   jax.ShapeDtypeStruct((T, d_ssm), F32)],
        compiler_params=_params("arbitrary", "arbitrary"),
    )(xbc, xbc, X, XK, dY, cs_rep, cs_row, cs_full, r_end)


def _fox_fwd(proj, q0, k0, v0, nh, cum_row, name, tq=FOX_TILE):
    T = proj.shape[0]
    tq = _tile(T, tq)
    nq = T // tq
    hd = FOX_HEAD_DIM
    scale = hd ** -0.5
    qo, ko, vo = q0 // hd, k0 // hd, v0 // hd

    def body(q_ref, k_ref, v_ref, ck_ref, o_ref, lse_ref):
        i = pl.program_id(1)
        q = (q_ref[...] * (scale * LOG2E)).astype(BF16)
        mask = _causal(tq)

        def step(j, carry, masked):
            m, l, acc = carry
            r0 = pl.multiple_of(j * tq, tq)
            k = k_ref[pl.ds(r0, tq), :].astype(BF16)
            v = v_ref[pl.ds(r0, tq), :].astype(BF16)
            s = lax.dot_general(q, k, NT, preferred_element_type=F32) - ck_ref[j]
            if masked:
                s = jnp.where(mask, s, NEG)
            m_new = jnp.maximum(m, jnp.max(s, axis=1, keepdims=True))
            alpha = jnp.exp2(m - m_new)
            p = jnp.exp2(s - m_new)
            l = alpha * l + jnp.sum(p, axis=1, keepdims=True)
            acc = alpha * acc + jnp.dot(p.astype(BF16), v, preferred_element_type=F32)
            return m_new, l, acc

        carry = (jnp.full((tq, 1), NEG, F32), jnp.zeros((tq, 1), F32), jnp.zeros((tq, hd), F32))
        carry = lax.fori_loop(0, i, lambda j, cr: step(j, cr, False), carry)
        m, l, acc = step(i, carry, True)
        o_ref[...] = (acc / l).astype(o_ref.dtype)
        lse_ref[...] = jnp.broadcast_to(m + jnp.log2(l), (tq, LANES))

    return pl.pallas_call(
        body, name=name, grid=(nh, nq),
        in_specs=[pl.BlockSpec((tq, hd), lambda h, i: (i, qo + h)), pl.BlockSpec((T, hd), lambda h, i: (0, ko + h)),
                  pl.BlockSpec((T, hd), lambda h, i: (0, vo + h)),
                  pl.BlockSpec((None, nq, 1, tq), lambda h, i: (h, 0, 0, 0))],
        out_specs=[pl.BlockSpec((tq, hd), lambda h, i: (i, h)), pl.BlockSpec((tq, LANES), lambda h, i: (i, h))],
        out_shape=[jax.ShapeDtypeStruct((T, nh * hd), BF16), jax.ShapeDtypeStruct((T, nh * LANES), F32)],
        compiler_params=_params("parallel", "arbitrary"),
    )(proj, proj, proj, cum_row * LOG2E)


def _fox_bwd(proj, q0, k0, v0, nh, cum_row, o, lse, dcat, do0, name, tq=FOX_TILE):
    T = proj.shape[0]
    tq = _tile(T, tq)
    nq = T // tq
    hd = FOX_HEAD_DIM
    scale = hd ** -0.5
    qo, ko, vo, doo = q0 // hd, k0 // hd, v0 // hd, do0 // hd

    def body(q_ref, k_ref, v_ref, do_ref, o_ref, lse_ref, ck_ref, dq_ref, dk_ref, dv_ref, dck_ref, dcq_ref):
        i = pl.program_id(1)

        @pl.when(i == 0)
        def _():
            dk_ref[...] = jnp.zeros_like(dk_ref)
            dv_ref[...] = jnp.zeros_like(dv_ref)
            dck_ref[...] = jnp.zeros_like(dck_ref)

        q = (q_ref[...] * (scale * LOG2E)).astype(BF16)
        do = do_ref[...].astype(F32)
        dob = do.astype(BF16)
        delta = jnp.sum(do * o_ref[...].astype(F32), axis=1, keepdims=True)
        lse = jnp.tile(lse_ref[...], (1, tq // LANES))
        mask = _causal(tq)

        def step(j, carry, masked):
            dq, rows = carry
            r0 = pl.multiple_of(j * tq, tq)
            k = k_ref[pl.ds(r0, tq), :].astype(BF16)
            v = v_ref[pl.ds(r0, tq), :].astype(BF16)
            s = lax.dot_general(q, k, NT, preferred_element_type=F32) - lse - ck_ref[j]
            if masked:
                s = jnp.where(mask, s, NEG)
            p = jnp.exp2(s)
            dp = lax.dot_general(dob, v, NT, preferred_element_type=F32)
            ds = p * (dp - delta)
            dsb = ds.astype(BF16)
            dq = dq + jnp.dot(dsb, k, preferred_element_type=F32) * scale
            dk_ref[pl.ds(r0, tq), :] += lax.dot_general(dsb, q, TN, preferred_element_type=F32) * (1.0 / LOG2E)
            dv_ref[pl.ds(r0, tq), :] += lax.dot_general(p.astype(BF16), dob, TN, preferred_element_type=F32)
            dck_ref[j] -= jnp.sum(ds, axis=0, keepdims=True)
            return dq, rows + jnp.sum(ds, axis=1, keepdims=True)

        carry = (jnp.zeros((tq, hd), F32), jnp.zeros((tq, 1), F32))
        carry = lax.fori_loop(0, i, lambda j, cr: step(j, cr, False), carry)
        dq, rows = step(i, carry, True)
        dq_ref[...] = dq.astype(dq_ref.dtype)
        dcq_ref[...] = jnp.broadcast_to(rows, (tq, LANES))

    return pl.pallas_call(
        body, name=name, grid=(nh, nq),
        in_specs=[pl.BlockSpec((tq, hd), lambda h, i: (i, qo + h)), pl.BlockSpec((T, hd), lambda h, i: (0, ko + h)),
                  pl.BlockSpec((T, hd), lambda h, i: (0, vo + h)), pl.BlockSpec((tq, hd), lambda h, i: (i, doo + h)),
                  pl.BlockSpec((tq, hd), lambda h, i: (i, h)), pl.BlockSpec((tq, LANES), lambda h, i: (i, h)),
                  pl.BlockSpec((None, nq, 1, tq), lambda h, i: (h, 0, 0, 0))],
        out_specs=[pl.BlockSpec((tq, hd), lambda h, i: (i, h)), pl.BlockSpec((T, hd), lambda h, i: (0, h)),
                   pl.BlockSpec((T, hd), lambda h, i: (0, h)), pl.BlockSpec((None, nq, 1, tq), lambda h, i: (h, 0, 0, 0)),
                   pl.BlockSpec((tq, LANES), lambda h, i: (i, h))],
        out_shape=[jax.ShapeDtypeStruct((T, nh * hd), BF16), jax.ShapeDtypeStruct((T, nh * hd), F32),
                   jax.ShapeDtypeStruct((T, nh * hd), F32), jax.ShapeDtypeStruct((nh, nq, 1, tq), F32),
                   jax.ShapeDtypeStruct((T, nh * LANES), F32)],
        compiler_params=_params("arbitrary", "arbitrary"),
    )(proj, proj, proj, dcat, o, lse, cum_row * LOG2E)


def _scan_fwd(a, u, name, ct=256):
    T, C = a.shape

    def body(a_ref, u_ref, h_ref):
        def blk(tb, h):
            r0 = pl.multiple_of(tb * SUBLANES, SUBLANES)
            ab, ub = a_ref[pl.ds(r0, SUBLANES), :], u_ref[pl.ds(r0, SUBLANES), :]
            rows = []
            for r in range(SUBLANES):
                h = ab[r:r + 1, :] * h + ub[r:r + 1, :]
                rows.append(h)
            h_ref[pl.ds(r0, SUBLANES), :] = jnp.concatenate(rows, axis=0)
            return h

        lax.fori_loop(0, T // SUBLANES, blk, jnp.zeros((1, ct), F32))

    spec = pl.BlockSpec((T, ct), lambda j: (0, j))
    return pl.pallas_call(body, name=name, grid=(C // ct,), in_specs=[spec, spec], out_specs=spec,
                          out_shape=jax.ShapeDtypeStruct((T, C), F32), compiler_params=_params("parallel"))(a, u)


def _scan_bwd(a, dh, h, name, ct=256):
    T, C = a.shape
    nb = T // SUBLANES

    def body(a_ref, dh_ref, h_ref, g_ref, da_ref):
        def blk(t, carry):
            r0 = pl.multiple_of((nb - 1 - t) * SUBLANES, SUBLANES)
            ab, db = a_ref[pl.ds(r0, SUBLANES), :], dh_ref[pl.ds(r0, SUBLANES), :]
            rows = [None] * SUBLANES
            for r in range(SUBLANES - 1, -1, -1):
                g = db[r:r + 1, :] + carry
                carry = ab[r:r + 1, :] * g
                rows[r] = g
            g_ref[pl.ds(r0, SUBLANES), :] = jnp.concatenate(rows, axis=0)
            return carry

        lax.fori_loop(0, nb, blk, jnp.zeros((1, ct), F32))
        da_ref[...] = g_ref[...] * _shift_down(h_ref[...], 1)

    spec = pl.BlockSpec((T, ct), lambda j: (0, j))
    return pl.pallas_call(body, name=name, grid=(C // ct,), in_specs=[spec] * 3, out_specs=[spec] * 2,
                          out_shape=[jax.ShapeDtypeStruct((T, C), F32)] * 2,
                          compiler_params=_params("parallel"))(a, dh, h)


def _lru_elem(xc, ra, ia, lam):
    r, i = _sigmoid(ra), _sigmoid(ia)
    log_a = RG_LRU_C * r * _log_sigmoid(lam)
    return jnp.exp(log_a), jnp.sqrt(_neg_expm1(2.0 * log_a)) * (i * xc)


def _lru_gates_fwd(xc, w_a, b_a, w_x, b_x, lam, name):
    def fn(xv, ba, bx, lm, wa, wx):
        xb = xv.astype(BF16)
        ra = jnp.dot(xb, wa.astype(BF16), preferred_element_type=F32) + ba
        ia = jnp.dot(xb, wx.astype(BF16), preferred_element_type=F32) + bx
        return _lru_elem(xv, ra, ia, lm)
    nb = xc.shape[1] // RNN_BLOCK
    return _ew(fn, [(xc, 0)], [(b_a, 0), (b_x, 0), (lam, 0), (w_a, 0), (w_x, 0)], [F32, F32], [],
               name=name, tm=512, cb=RNN_BLOCK, ncb=nb)


def _lru_gates_bwd(xc, w_a, b_a, w_x, b_x, lam, da, du, name):
    def fn(xv, dav, duv, ba, bx, lm, wa, wx):
        xb, wab, wxb = xv.astype(BF16), wa.astype(BF16), wx.astype(BF16)
        ra = jnp.dot(xb, wab, preferred_element_type=F32) + ba
        ia = jnp.dot(xb, wxb, preferred_element_type=F32) + bx
        _, vjp = jax.vjp(_lru_elem, xv, ra, ia, lm)
        dx, dra, dia, dlm = vjp((dav, duv))
        drb, dib = dra.astype(BF16), dia.astype(BF16)
        dx = dx + lax.dot_general(drb, wab, NT, preferred_element_type=F32)
        dx = dx + lax.dot_general(dib, wxb, NT, preferred_element_type=F32)
        dwa = lax.dot_general(xb, drb, TN, preferred_element_type=F32)
        dwx = lax.dot_general(xb, dib, TN, preferred_element_type=F32)
        return (dx, jnp.sum(dra, axis=0, keepdims=True), jnp.sum(dia, axis=0, keepdims=True), dlm, dwa, dwx)
    nb = xc.shape[1] // RNN_BLOCK
    return _ew(fn, [(xc, 0), (da, 0), (du, 0)], [(b_a, 0), (b_x, 0), (lam, 0), (w_a, 0), (w_x, 0)], [F32],
               [b_a.shape, b_x.shape, lam.shape, w_a.shape, w_x.shape], name=name, tm=512, cb=RNN_BLOCK, ncb=nb)


def _hyb_cols(D):
    conv = D + 2 * SSM_GROUPS * SSM_STATE
    z0, x0, q0 = 0, D, D + conv
    return dict(z=z0, xbc=x0, q=q0, k=q0 + D, v=q0 + 2 * D, small=q0 + 3 * D, total=q0 + 3 * D + LANES, conv=conv)


def _hyb_w_in_reorder(w, D):
    cols = _hyb_cols(D)
    nh_s, nh_f = D // SSM_HEAD_DIM, D // FOX_HEAD_DIM
    a = D + cols["conv"]
    pad = jnp.zeros((w.shape[0], LANES - nh_s - nh_f), w.dtype)
    return jnp.concatenate([w[:, :a], w[:, a + nh_s:a + nh_s + 3 * D], w[:, a:a + nh_s], w[:, a + nh_s + 3 * D:], pad], axis=1)


def _hyb_w_in_restore(dw, D):
    cols = _hyb_cols(D)
    nh_s, nh_f = D // SSM_HEAD_DIM, D // FOX_HEAD_DIM
    a = D + cols["conv"]
    s = cols["small"]
    return jnp.concatenate([dw[:, :a], dw[:, s:s + nh_s], dw[:, a:s], dw[:, s + nh_s:s + nh_s + nh_f]], axis=1)


def _ssm_out(Y, xs, z, dfull, ng):
    y = (Y + dfull * xs) * _silu(z)
    return y * lax.rsqrt(jnp.mean(y * y, axis=-1, keepdims=True) + NORM_EPS) * ng


def _hyb_fwd(x, g_pre, g_post, p, tag):
    T, D = x.shape
    cols = _hyb_cols(D)
    nh_s, nh_f = D // SSM_HEAD_DIM, D // FOX_HEAD_DIM
    n = _rms_fwd(x, g_pre, f"{tag}_rms")
    proj = _mm(n, p["w_in"], "nn", F32, f"{tag}_mm_in", tn=1152, tk=1024)
    a_neg = -jnp.exp(p["a_log"])
    bias = jnp.concatenate([p["dt_bias"], p["b_f"], jnp.zeros((LANES - nh_s - nh_f,), F32)])[None]
    mult = jnp.concatenate([a_neg, jnp.ones((nh_f,), F32), jnp.zeros((LANES - nh_s - nh_f,), F32)])[None]
    act, cs = _gates_fwd(proj, cols["small"], bias, mult, nh_s, f"{tag}_gates")
    dt, cs_s, cum = act[:, :nh_s], cs[:, :nh_s], cs[:, nh_s:nh_s + nh_f]
    dtf = jnp.repeat(dt, SSM_HEAD_DIM, axis=1)
    cs_rep, cs_row = _rep_layout(cs_s), _row_layout(cs_s, _tile(T, SSD_TILE))
    cum_row = _row_layout(cum, _tile(T, FOX_TILE))
    xbc = _conv_fwd(proj, cols["xbc"], cols["conv"], p["conv_w"], p["conv_b"], True, f"{tag}_conv")
    tqs = _tile(T, SSD_TILE)
    cs_full = jnp.repeat(cs_s, SSM_HEAD_DIM, axis=1)
    r_end = cs_full.reshape(T // tqs, tqs, D)[:, tqs - 1:, :]
    r_exp = jnp.broadcast_to(r_end, (T // tqs, tqs, D)).reshape(T, D)
    X, XK = _ew(lambda xv, dv, cv, rv: (xv * dv, xv * dv * jnp.exp(rv - cv)),
                [(xbc, 0), (dtf, 0), (cs_full, 0), (r_exp, 0)], [], [BF16, BF16], [], name=f"{tag}_xdt",
                cb=512, ncb=D // 512)
    Y = _ssd2_fwd(xbc, X, XK, cs_rep, cs_row, cs_full, r_end, f"{tag}_ssd")
    dfull = jnp.repeat(p["d"], SSM_HEAD_DIM)[None]
    gw = D // SSM_GROUPS
    y_ssm = _ew(_ssm_out, [(Y, 0), (xbc, 0), (proj, cols["z"] // gw)], [(dfull, 0), (p["norm_g"], 0)], [BF16], [],
                name=f"{tag}_ssm_out", cb=gw, ncb=SSM_GROUPS)
    o, lse = _fox_fwd(proj, cols["q"], cols["k"], cols["v"], nh_f, cum_row, f"{tag}_fox")
    mix = _mm_parts([(y_ssm, p["w_out"][:D]), (o, p["w_out"][D:])], "nn", F32, f"{tag}_mm_out", tn=1024)
    x2 = _post_fwd(x, mix, g_post, 1.0, f"{tag}_post")
    return x2, (x, n, proj, bias, mult, dtf, cs_rep, cs_row, cum_row, xbc, X, Y, dfull, o, lse, y_ssm, mix,
                XK, cs_full, r_end, r_exp)


def _hyb_bwd(dy, saved, g_pre, g_post, p, tag):
    (x, n, proj, bias, mult, dtf, cs_rep, cs_row, cum_row, xbc, X, Y, dfull, o, lse, y_ssm, mix,
     XK, cs_full, r_end, r_exp) = saved
    T, D = x.shape
    cols = _hyb_cols(D)
    nh_s, nh_f = D // SSM_HEAD_DIM, D // FOX_HEAD_DIM
    gw = D // SSM_GROUPS
    dmix, dg_post = _post_bwd(dy, mix, g_post, 1.0, f"{tag}_post_b")
    dcat = _mm(dmix, p["w_out"], "nt", BF16, f"{tag}_mm_dcat", tn=1024, tk=1024)
    dw_out = jnp.concatenate([_mm(y_ssm, dmix, "tn", BF16, f"{tag}_mm_dwout_s", tm=1024, tn=1024, tk=1024),
                              _mm(o, dmix, "tn", BF16, f"{tag}_mm_dwout_f", tm=1024, tn=1024, tk=1024)], axis=0)

    def ssm_out_b(Yv, xv, zv, dv, dfv, ngv):
        _, vjp = jax.vjp(_ssm_out, Yv, xv, zv, dfv, ngv)
        return vjp(dv.astype(F32))
    dY, dxs_skip, dz, ddfull, dng = _ew(
        ssm_out_b, [(Y, 0), (xbc, 0), (proj, cols["z"] // gw), (dcat, 0)], [(dfull, 0), (p["norm_g"], 0)],
        [BF16, F32, BF16], [dfull.shape, p["norm_g"].shape], name=f"{tag}_ssm_out_b", cb=gw, ncb=SSM_GROUPS)
    dC, dB, dXd, dXK, dcs_q, dcs_k, dcs_f = _ssd2_bwd(xbc, X, XK, cs_rep, cs_row, cs_full, r_end, dY, f"{tag}_ssd_b")
    def xdt_b(dXdv, dXKv, skv, xv, dv, cv, rv, xkv):
        dX = dXdv + dXKv * jnp.exp(rv - cv)
        return dX * dv + skv, dX * xv, dXKv * xkv.astype(F32)
    dxs, ddtf, dcs_kf = _ew(
        xdt_b, [(dXd, 0), (dXK, 0), (dxs_skip, 0), (xbc, 0), (dtf, 0), (cs_full, 0), (r_exp, 0), (XK, 0)],
        [], [F32, F32, F32], [], name=f"{tag}_xdt_b", cb=512, ncb=D // 512)
    ddt = ddtf.reshape(T, nh_s, SSM_HEAD_DIM).sum(-1)
    dcs_s = (dcs_q[:, ::LANES] + dcs_k.reshape(nh_s, T).T
             + (dcs_f - dcs_kf).reshape(T, nh_s, SSM_HEAD_DIM).sum(-1))
    dq, dk, dv, dcum_k, dcum_q = _fox_bwd(proj, cols["q"], cols["k"], cols["v"], nh_f, cum_row, o, lse, dcat, D,
                                  f"{tag}_fox_b")
    dcum = dcum_q[:, ::LANES] + dcum_k.reshape(nh_f, T).T
    zpad = jnp.zeros((T, LANES - nh_s - nh_f), F32)
    dact = jnp.concatenate([ddt, jnp.zeros((T, nh_f), F32), zpad], axis=1)
    dcs = jnp.concatenate([dcs_s, dcum, zpad], axis=1)
    dsmall, dmult, dbias = _gates_bwd(proj, cols["small"], bias, mult, nh_s, dact, dcs, f"{tag}_gates_b")
    dxbc_act = jnp.concatenate([dxs, dB, dC], axis=1)
    dxbc, dconv_w, dconv_b = _conv_bwd(proj, cols["xbc"], cols["conv"], p["conv_w"], p["conv_b"], dxbc_act, True,
                                       f"{tag}_conv_b")
    pieces = [(dz, "z"), (dxbc, "xbc"), (dq, "q"), (dk, "k"), (dv, "v"), (dsmall, "small")]
    w_cols = lambda d, key: p["w_in"][:, cols[key]:cols[key] + d.shape[1]]
    dn = _mm_parts([(d, w_cols(d, key)) for d, key in pieces], "nt", F32, f"{tag}_mm_dn", tm=256, tn=512)
    dw_in = jnp.concatenate([_mm(n, d, "tn", BF16, f"{tag}_mm_dwin_{key}", tm=1024, tn=1024, tk=1024)
                             for d, key in pieces], axis=1)
    dy2, dg_pre = _rms_bwd_add(x, g_pre, dn, dy, f"{tag}_rms_b")
    grads = dict(w_in=dw_in, w_out=dw_out, conv_w=dconv_w, conv_b=dconv_b[0], dt_bias=dbias[0, :nh_s],
                 a_log=dmult[0, :nh_s] * mult[0, :nh_s], d=ddfull.reshape(nh_s, SSM_HEAD_DIM).sum(-1),
                 norm_g=dng[0], b_f=dbias[0, nh_s:nh_s + nh_f])
    return dy2, dg_pre, dg_post, grads


def _rec_fwd(x, g_pre, g_post, p, tag):
    T, D = x.shape
    n = _rms_fwd(x, g_pre, f"{tag}_rms")
    pr = _mm(n, p["w_in"], "nn", F32, f"{tag}_mm_in", tn=1024, tk=1024)
    xc = _conv_fwd(pr, D, D, p["conv_w"], p["conv_b"], False, f"{tag}_conv")
    a, u = _lru_gates_fwd(xc, p["w_a"], p["b_a"], p["w_x"], p["b_x"], p["lam"], f"{tag}_lru")
    hs = _scan_fwd(a, u, f"{tag}_scan")
    og = _ew(lambda hv, gv: hv * _gelu_tanh(gv), [(hs, 0), (pr, 0)], [], [BF16], [], name=f"{tag}_gate", cb=D)
    mix = _mm(og, p["w_out"], "nn", F32, f"{tag}_mm_out", tn=1024, tk=1024)
    x2 = _post_fwd(x, mix, g_post, 1.0, f"{tag}_post")
    return x2, (x, n, pr, xc, a, hs, og, mix)


def _rec_bwd(dy, saved, g_pre, g_post, p, tag):
    x, n, pr, xc, a, hs, og, mix = saved
    T, D = x.shape
    dmix, dg_post = _post_bwd(dy, mix, g_post, 1.0, f"{tag}_post_b")
    dog = _mm(dmix, p["w_out"], "nt", F32, f"{tag}_mm_dog", tn=1024, tk=1024)
    dw_out = _mm(og, dmix, "tn", BF16, f"{tag}_mm_dwout", tm=1024, tn=1024, tk=1024)

    def gate_b(hv, gv, dv):
        _, vjp = jax.vjp(lambda h_, g_: h_ * _gelu_tanh(g_), hv, gv)
        return vjp(dv)
    dhs, dgate = _ew(gate_b, [(hs, 0), (pr, 0), (dog, 0)], [], [F32, BF16], [], name=f"{tag}_gate_b", cb=D)
    du, da = _scan_bwd(a, dhs, hs, f"{tag}_scan_b")
    dxc, db_a, db_x, dlam, dw_a, dw_x = _lru_gates_bwd(xc, p["w_a"], p["b_a"], p["w_x"], p["b_x"], p["lam"], da, du,
                                                       f"{tag}_lru_b")
    dxr, dconv_w, dconv_b = _conv_bwd(pr, D, D, p["conv_w"], p["conv_b"], dxc, False, f"{tag}_conv_b")
    dn = _mm_parts([(dgate, p["w_in"][:, :D]), (dxr, p["w_in"][:, D:])], "nt", F32, f"{tag}_mm_dn", tn=1024)
    dw_in = jnp.concatenate([_mm(n, dgate, "tn", BF16, f"{tag}_mm_dwin_g", tm=1024, tn=1024, tk=1024),
                             _mm(n, dxr, "tn", BF16, f"{tag}_mm_dwin_x", tm=1024, tn=1024, tk=1024)], axis=1)
    dy2, dg_pre = _rms_bwd_add(x, g_pre, dn, dy, f"{tag}_rms_b")
    grads = dict(w_in=dw_in, w_out=dw_out, conv_w=dconv_w, conv_b=dconv_b[0], w_a=dw_a, b_a=db_a[0], w_x=dw_x,
                 b_x=db_x[0], lam=dlam[0])
    return dy2, dg_pre, dg_post, grads


def _hyb_params(W, i, D):
    return dict(w_in=_hyb_w_in_reorder(W["hyb_w_in"][i], D), w_out=W["hyb_w_out"][i], conv_w=W["ssm_conv_w"][i],
                conv_b=W["ssm_conv_b"][i][None], dt_bias=W["ssm_dt_bias"][i], a_log=W["ssm_a_log"][i],
                d=W["ssm_d"][i], norm_g=W["ssm_norm_g"][i][None], b_f=W["fox_b_f"][i])


def _rec_params(W, j):
    return dict(w_in=W["rec_w_in"][j], w_out=W["rec_w_out"][j], conv_w=W["rec_conv_w"][j],
                conv_b=W["rec_conv_b"][j][None], w_a=W["rec_w_a"][j], b_a=W["rec_b_a"][j][None],
                w_x=W["rec_w_x"][j], b_x=W["rec_b_x"][j][None], lam=W["rec_lambda"][j][None])


def _local_step(x, target, W):
    T, D = x.shape
    depth = W["norm_g"].shape[0]
    g = lambda l, k: W["norm_g"][l, k][None]
    saved = []
    for l in range(depth):
        x, s0 = _ffn_fwd(x, g(l, 0), g(l, 1), W["ffn_w_in"], W["ffn_w_out"], 2 * l, f"l{l}_ffn0")
        if l % 2 == 0:
            pm = _hyb_params(W, l // 2, D)
            x, s1 = _hyb_fwd(x, g(l, 2), g(l, 3), pm, f"l{l}_hyb")
        else:
            pm = _rec_params(W, l // 2)
            x, s1 = _rec_fwd(x, g(l, 2), g(l, 3), pm, f"l{l}_rec")
        x, s2 = _ffn_fwd(x, g(l, 4), g(l, 5), W["ffn_w_in"], W["ffn_w_out"], 2 * l + 1, f"l{l}_ffn1")
        saved.append((s0, s1, s2, pm))

    def loss_fn(yv, tv):
        err = yv - tv
        part = 0.5 * jnp.sum(jnp.sum(err * err, axis=1, keepdims=True), axis=0, keepdims=True) / D
        return err * (1.0 / D), jnp.broadcast_to(part, (1, D))
    dy, loss_row = _ew(loss_fn, [(x, 0), (target, 0)], [], [F32], [(1, D)], name="loss")
    loss = loss_row[0, 0]

    gn = [[None] * 6 for _ in range(depth)]
    g_ffn_in = [[None, None] for _ in range(depth)]
    g_ffn_out = [[None, None] for _ in range(depth)]
    g_hyb, g_rec = [], []
    for l in reversed(range(depth)):
        s0, s1, s2, pm = saved[l]
        dy, gn[l][4], gn[l][5], g_ffn_in[l][1], g_ffn_out[l][1] = _ffn_bwd(
            dy, s2, g(l, 4), g(l, 5), W["ffn_w_in"], W["ffn_w_out"], 2 * l + 1, f"l{l}_ffn1")
        if l % 2 == 0:
            dy, gn[l][2], gn[l][3], gm = _hyb_bwd(dy, s1, g(l, 2), g(l, 3), pm, f"l{l}_hyb")
            g_hyb.insert(0, gm)
        else:
            dy, gn[l][2], gn[l][3], gm = _rec_bwd(dy, s1, g(l, 2), g(l, 3), pm, f"l{l}_rec")
            g_rec.insert(0, gm)
        dy, gn[l][0], gn[l][1], g_ffn_in[l][0], g_ffn_out[l][0] = _ffn_bwd(
            dy, s0, g(l, 0), g(l, 1), W["ffn_w_in"], W["ffn_w_out"], 2 * l, f"l{l}_ffn0")

    st = lambda items: jnp.stack(items)
    grads = {
        "norm_g": st([st([r[0] for r in row]) for row in gn]),
        "ffn_w_in": [piece for row in g_ffn_in for piece in row],
        "ffn_w_out": [piece for row in g_ffn_out for piece in row],
        "hyb_w_in": st([_hyb_w_in_restore(m["w_in"], D) for m in g_hyb]),
        "ssm_conv_w": st([m["conv_w"] for m in g_hyb]), "ssm_conv_b": st([m["conv_b"] for m in g_hyb]),
        "ssm_dt_bias": st([m["dt_bias"] for m in g_hyb]), "ssm_a_log": st([m["a_log"] for m in g_hyb]),
        "ssm_d": st([m["d"] for m in g_hyb]), "ssm_norm_g": st([m["norm_g"] for m in g_hyb]),
        "fox_b_f": st([m["b_f"] for m in g_hyb]), "hyb_w_out": st([m["w_out"] for m in g_hyb]),
        "rec_w_in": st([m["w_in"] for m in g_rec]), "rec_conv_w": st([m["conv_w"] for m in g_rec]),
        "rec_conv_b": st([m["conv_b"] for m in g_rec]), "rec_w_a": st([m["w_a"] for m in g_rec]),
        "rec_b_a": st([m["b_a"] for m in g_rec]), "rec_w_x": st([m["w_x"] for m in g_rec]),
        "rec_b_x": st([m["b_x"] for m in g_rec]), "rec_lambda": st([m["lam"] for m in g_rec]),
        "rec_w_out": st([m["w_out"] for m in g_rec]),
    }
    return loss, dy, grads


MESH_AXES = ("x", "y", "c")
N_CHIPS = 4
N_DEV = 8
HBM = pl.BlockSpec(memory_space=pltpu.HBM)


def _mesh_pos():
    return tuple(lax.axis_index(n) for n in MESH_AXES)


def _other_chips(x, y):
    chips = [(1 - x, y), (x, 1 - y), (1 - x, 1 - y)]
    return chips, [2 * cx + cy for cx, cy in chips]


def _rcopy(src, dst, send_sem, recv_sem, dev):
    return pltpu.make_async_remote_copy(src_ref=src, dst_ref=dst, send_sem=send_sem, recv_sem=recv_sem,
                                        device_id=dev, device_id_type=pl.DeviceIdType.MESH)


def _comm_params():
    return pltpu.CompilerParams()


DMA_CHUNK_BYTES = 1 << 20
DMA_ROW_ALIGN = 16


def _nchunks(rows, row_bytes):
    n = max(1, min(rows // DMA_ROW_ALIGN, (rows * row_bytes) // DMA_CHUNK_BYTES))
    while n > 1 and (rows % n or (rows // n) % DMA_ROW_ALIGN):
        n -= 1
    return n


def _row_bytes(ref):
    return ref.shape[-1] * jnp.dtype(ref.dtype).itemsize


def _all_gather(arrs, name):
    n = len(arrs)

    def body(*refs):
        ins, outs = refs[:n], refs[n:2 * n]
        send, recv, fsend, frecv = refs[2 * n:]
        x, y, c = _mesh_pos()
        k = 2 * x + y
        sibling = (x, y, 1 - c)
        chips, chip_k = _other_chips(x, y)
        halves = [r.shape[0] // 2 for r in ins]
        nchs = [_nchunks(h, _row_bytes(r)) for h, r in zip(halves, ins)]
        for a in range(n):
            h, step = halves[a], halves[a] // nchs[a]
            for j, chip in enumerate(chips):
                for q in range(nchs[a]):
                    rows = pl.ds(c * h + q * step, step)
                    _rcopy(ins[a].at[rows], outs[a].at[k, rows], send.at[a, j], recv.at[a, j], (*chip, c)).start()
        for a in range(n):
            h, step = halves[a], halves[a] // nchs[a]
            for j, chip in enumerate(chips):
                blk = outs[a].at[chip_k[j], pl.ds(c * h, h)]
                _rcopy(blk, blk, send.at[a, j], recv.at[a, j], (*chip, c)).wait_recv()
                for q in range(nchs[a]):
                    part = outs[a].at[chip_k[j], pl.ds(c * h + q * step, step)]
                    _rcopy(part, part, fsend.at[a, j], frecv.at[a, j], sibling).start()
        for a in range(n):
            h = halves[a]
            for j, chip in enumerate(chips):
                blk = outs[a].at[chip_k[j], pl.ds((1 - c) * h, h)]
                _rcopy(blk, blk, fsend.at[a, j], frecv.at[a, j], sibling).wait_recv()
        for a in range(n):
            h = halves[a]
            for j, chip in enumerate(chips):
                _rcopy(ins[a].at[pl.ds(c * h, h)], outs[a].at[k, pl.ds(c * h, h)], send.at[a, j], recv.at[a, j],
                       (*chip, c)).wait_send()
                blk = outs[a].at[chip_k[j], pl.ds(c * h, h)]
                _rcopy(blk, blk, fsend.at[a, j], frecv.at[a, j], sibling).wait_send()

    outs = pl.pallas_call(
        body, name=name, in_specs=[HBM] * n, out_specs=[HBM] * n,
        out_shape=[jax.ShapeDtypeStruct((N_CHIPS,) + a.shape, a.dtype) for a in arrs],
        scratch_shapes=[pltpu.SemaphoreType.DMA((n, 3))] * 4, compiler_params=_comm_params(),
    )(*arrs)
    k = 2 * lax.axis_index("x") + lax.axis_index("y")
    return [lax.dynamic_update_index_in_dim(o, a, k, 0) for o, a in zip(outs, arrs)]


def _pair_exchange(gs, name):
    n = len(gs)

    def body(*refs):
        ins, theirs = refs[:n], refs[n:2 * n]
        send, recv = refs[2 * n:]
        x, y, c = _mesh_pos()
        sibling = (x, y, 1 - c)
        for a in range(n):
            h = ins[a].shape[1] // 2
            nch = _nchunks(h, _row_bytes(ins[a]))
            step = h // nch
            for kk in range(N_CHIPS):
                for q in range(nch):
                    _rcopy(ins[a].at[kk, pl.ds((1 - c) * h + q * step, step)], theirs[a].at[kk, pl.ds(q * step, step)],
                           send.at[a], recv.at[a], sibling).start()
        for a in range(n):
            h = ins[a].shape[1] // 2
            _rcopy(ins[a].at[:, pl.ds((1 - c) * h, h)], theirs[a], send.at[a], recv.at[a], sibling).wait()

    half = [jax.ShapeDtypeStruct((a.shape[0], a.shape[1] // 2) + a.shape[2:], a.dtype) for a in gs]
    theirs = pl.pallas_call(
        body, name=name, in_specs=[HBM] * n, out_specs=[HBM] * n, out_shape=half,
        scratch_shapes=[pltpu.SemaphoreType.DMA((n,))] * 2, compiler_params=_comm_params(),
    )(*gs)
    c = lax.axis_index("c")
    mine = [lax.dynamic_slice_in_dim(g, c * (g.shape[1] // 2), g.shape[1] // 2, axis=1) for g in gs]
    return mine, theirs


def _chip_exchange(ss, name):
    n = len(ss)

    def body(*refs):
        ins = refs[:n]
        got = [refs[n + 3 * a:n + 3 * a + 3] for a in range(n)]
        send, recv = refs[4 * n:]
        x, y, c = _mesh_pos()
        chips, chip_k = _other_chips(x, y)
        for a in range(n):
            h = ins[a].shape[1]
            nch = _nchunks(h, _row_bytes(ins[a]))
            step = h // nch
            for q in range(nch):
                rows = pl.ds(q * step, step)
                for j, chip in enumerate(chips):
                    _rcopy(ins[a].at[chip_k[j], rows], got[a][j].at[rows], send.at[a, j], recv.at[a, j],
                           (*chip, c)).start()
        for a in range(n):
            for j, chip in enumerate(chips):
                _rcopy(ins[a].at[chip_k[j]], got[a][j], send.at[a, j], recv.at[a, j], (*chip, c)).wait()

    got = [jax.ShapeDtypeStruct(a.shape[1:], a.dtype) for a in ss for _ in range(3)]
    res = pl.pallas_call(
        body, name=name, in_specs=[HBM] * n, out_specs=[HBM] * (3 * n), out_shape=got,
        scratch_shapes=[pltpu.SemaphoreType.DMA((n, 3))] * 2, compiler_params=_comm_params(),
    )(*ss)
    k = 2 * lax.axis_index("x") + lax.axis_index("y")
    own = [lax.dynamic_index_in_dim(s, k, 0, keepdims=False) for s in ss]
    return own, [res[3 * a:3 * a + 3] for a in range(n)]


def _pair_share(rs, name):
    n = len(rs)

    def body(*refs):
        ins, outs = refs[:n], refs[n:2 * n]
        send, recv = refs[2 * n:]
        x, y, c = _mesh_pos()
        sibling = (x, y, 1 - c)
        for a in range(n):
            h = ins[a].shape[0]
            nch = _nchunks(h, _row_bytes(ins[a]))
            step = h // nch
            for q in range(nch):
                rows = pl.ds(q * step, step)
                _rcopy(ins[a].at[rows], outs[a].at[rows], send.at[a], recv.at[a], sibling).start()
        for a in range(n):
            _rcopy(ins[a], outs[a], send.at[a], recv.at[a], sibling).wait()

    theirs = pl.pallas_call(
        body, name=name, in_specs=[HBM] * n, out_specs=[HBM] * n,
        out_shape=[jax.ShapeDtypeStruct(a.shape, a.dtype) for a in rs],
        scratch_shapes=[pltpu.SemaphoreType.DMA((n,))] * 2, compiler_params=_comm_params(),
    )(*rs)
    c = lax.axis_index("c")
    return [jnp.concatenate([jnp.where(c == 0, r, t), jnp.where(c == 0, t, r)], axis=0) for r, t in zip(rs, theirs)]


def _exchange_all(vec, name):
    def body(v_ref, out_ref, send, recv):
        x, y, c = _mesh_pos()
        flip = lambda p, f: 1 - p if f else p
        me = 4 * x + 2 * y + c
        rows_all = v_ref.shape[0]
        nch = _nchunks(rows_all, _row_bytes(v_ref))
        step = rows_all // nch
        for j in range(1, N_DEV):
            fx, fy, fc = (j >> 2) & 1, (j >> 1) & 1, j & 1
            for q in range(nch):
                rows = pl.ds(q * step, step)
                _rcopy(v_ref.at[rows], out_ref.at[me, rows], send.at[j - 1], recv.at[j - 1],
                       (flip(x, fx), flip(y, fy), flip(c, fc))).start()
        for j in range(1, N_DEV):
            fx, fy, fc = (j >> 2) & 1, (j >> 1) & 1, j & 1
            slot = out_ref.at[4 * flip(x, fx) + 2 * flip(y, fy) + flip(c, fc)]
            _rcopy(slot, slot, send.at[j - 1], recv.at[j - 1], (x, y, c)).wait_recv()
        for j in range(1, N_DEV):
            _rcopy(v_ref, out_ref.at[me], send.at[j - 1], recv.at[j - 1], (x, y, c)).wait_send()

    out = pl.pallas_call(
        body, name=name, in_specs=[HBM], out_specs=HBM,
        out_shape=jax.ShapeDtypeStruct((N_DEV,) + vec.shape, vec.dtype),
        scratch_shapes=[pltpu.SemaphoreType.DMA((N_DEV - 1,))] * 2, compiler_params=_comm_params(),
    )(vec)
    me = 4 * lax.axis_index("x") + 2 * lax.axis_index("y") + lax.axis_index("c")
    return lax.dynamic_update_index_in_dim(out, vec, me, 0)


def _rows(a):
    return a.reshape(-1, a.shape[-1])


def _sum_kernel(parts, out_dtype, name):
    def fn(*vals):
        acc = vals[0].astype(F32)
        for v in vals[1:]:
            acc = acc + v.astype(F32)
        return acc
    out = _ew(fn, [(_rows(p), 0) for p in parts], [], [out_dtype], [], name=name)
    return out.reshape(parts[0].shape)


def _reduce_scatter(gs, tag):
    mine, theirs = _pair_exchange(gs, f"{tag}_pair")
    pair = [_sum_kernel([m, t], BF16, f"{tag}_add_pair{a}") for a, (m, t) in enumerate(zip(mine, theirs))]
    own, got = _chip_exchange(pair, f"{tag}_chips")
    red = [_sum_kernel([o, g[0], g[1], g[2]], F32, f"{tag}_add_chips{a}") for a, (o, g) in enumerate(zip(own, got))]
    return _pair_share(red, f"{tag}_share")


def _pack(arrs, row_mult):
    flat = jnp.concatenate([a.reshape(-1).astype(F32) for a in arrs])
    unit = row_mult * LANES
    pad = (-flat.size) % unit
    return jnp.pad(flat, (0, pad)).reshape(-1, LANES)


def _unpack(mat, shapes):
    flat, out, pos = mat.reshape(-1), [], 0
    for s in shapes:
        size = 1
        for d in s:
            size *= d
        out.append(flat[pos:pos + size].reshape(s))
        pos += size
    return out


def _to_shards(a, axis):
    sh = a.shape
    a = a.reshape(sh[:axis] + (N_CHIPS, sh[axis] // N_CHIPS) + sh[axis + 1:])
    return jnp.moveaxis(a, axis, 0)


def _from_shards(g, axis):
    g = jnp.moveaxis(g, 0, axis)
    sh = g.shape
    return g.reshape(sh[:axis] + (sh[axis] * sh[axis + 1],) + sh[axis + 2:])


def _adamw_fn(w, g, m, v):
    m2 = ADAM_B1 * m + (1.0 - ADAM_B1) * g
    v2 = ADAM_B2 * v + (1.0 - ADAM_B2) * (g * g)
    m_hat = m2 / (1.0 - ADAM_B1 ** ADAM_STEP)
    v_hat = v2 / (1.0 - ADAM_B2 ** ADAM_STEP)
    return -ADAM_LR * (m_hat / (jnp.sqrt(v_hat) + ADAM_EPS) + ADAM_WD * w), m2, v2


def _adamw(w, g, m, v, name):
    res = _ew(_adamw_fn, [(_rows(a), 0) for a in (w, g, m, v)], [], [F32, F32, F32], [], name=name)
    return tuple(r.reshape(w.shape) for r in res)


def kernel(x, norm_g, ffn_w_in, ffn_w_out, hyb_w_in, ssm_conv_w, ssm_conv_b, ssm_dt_bias, ssm_a_log, ssm_d, ssm_norm_g, fox_b_f, hyb_w_out, rec_w_in, rec_conv_w, rec_conv_b, rec_w_a, rec_b_a, rec_w_x, rec_b_x, rec_lambda, rec_w_out, loss_target, m_norm_g, m_ffn_w_in, m_ffn_w_out, m_hyb_w_in, m_ssm_conv_w, m_ssm_conv_b, m_ssm_dt_bias, m_ssm_a_log, m_ssm_d, m_ssm_norm_g, m_fox_b_f, m_hyb_w_out, m_rec_w_in, m_rec_conv_w, m_rec_conv_b, m_rec_w_a, m_rec_b_a, m_rec_w_x, m_rec_b_x, m_rec_lambda, m_rec_w_out, v_norm_g, v_ffn_w_in, v_ffn_w_out, v_hyb_w_in, v_ssm_conv_w, v_ssm_conv_b, v_ssm_dt_bias, v_ssm_a_log, v_ssm_d, v_ssm_norm_g, v_fox_b_f, v_hyb_w_out, v_rec_w_in, v_rec_conv_w, v_rec_conv_b, v_rec_w_a, v_rec_b_a, v_rec_w_x, v_rec_b_x, v_rec_lambda, v_rec_w_out):
    given = dict(locals())
    w = {n: given[n] for n in WEIGHTS}
    m = {n: given["m_" + n] for n in WEIGHTS}
    v = {n: given["v_" + n] for n in WEIGHTS}
    k = 2 * lax.axis_index("x") + lax.axis_index("y")

    big_bf16 = [_ew(lambda t: t, [(_rows(w[n]), 0)], [], [BF16], [], name=f"cast_{n}") for n in BIG]
    small_shapes = [w[n].shape for n in SMALL_SHARDED]
    small_pack = _pack([w[n] for n in SMALL_SHARDED], 2 * SUBLANES)
    gathered = _all_gather(big_bf16 + [small_pack], "gather_weights")
    W = {n: w[n] for n in SMALL_REPL}
    for n, g in zip(BIG, gathered[:-1]):
        W[n] = g if n in FFN else _from_shards(g.reshape((N_CHIPS,) + w[n].shape), SHARD_AXIS[n])
    per_chip = [_unpack(gathered[-1][kk], small_shapes) for kk in range(N_CHIPS)]
    for idx, n in enumerate(SMALL_SHARDED):
        W[n] = jnp.concatenate([per_chip[kk][idx] for kk in range(N_CHIPS)], axis=SHARD_AXIS[n])

    loss_part, dy, grads = _local_step(x[0], loss_target[0], W)
    loss = lax.psum(loss_part, MESH_AXES)

    pieces = {n: grads[n] if n in FFN else [_to_shards(grads[n], SHARD_AXIS[n]).reshape(N_CHIPS, -1, w[n].shape[-1])]
              for n in BIG}
    red_big = _reduce_scatter([piece for n in BIG for piece in pieces[n]], "rs")
    g_out, pos = {}, 0
    for n in BIG:
        cnt = len(pieces[n])
        g_out[n] = jnp.concatenate(red_big[pos:pos + cnt], axis=0).reshape(w[n].shape)
        pos += cnt
    small_names = SMALL_SHARDED + SMALL_REPL
    slots = _exchange_all(_pack([grads[n] for n in small_names], DMA_ROW_ALIGN).astype(BF16), "gather_small_grads")
    small_sum = _sum_kernel([slots[d] for d in range(N_DEV)], F32, "add_small_grads")
    for n, g in zip(small_names, _unpack(small_sum, [grads[n].shape for n in small_names])):
        if n in SHARD_AXIS:
            loc = g.shape[SHARD_AXIS[n]] // N_CHIPS
            g = lax.dynamic_slice_in_dim(g, k * loc, loc, axis=SHARD_AXIS[n])
        g_out[n] = g

    delta, new_m, new_v = {}, {}, {}
    for n in BIG:
        delta[n], new_m[n], new_v[n] = _adamw(w[n], g_out[n], m[n], v[n], f"adamw_{n}")
    shapes = [w[n].shape for n in small_names]
    packed = [_pack([d[n] for n in small_names], SUBLANES) for d in (w, g_out, m, v)]
    for d, mat in zip((delta, new_m, new_v), _adamw(*packed, "adamw_small")):
        d.update(zip(small_names, _unpack(mat, shapes)))

    return (loss, dy[None], *[g_out[n] for n in WEIGHTS], *[delta[n] for n in WEIGHTS],
            *[new_m[n] for n in WEIGHTS], *[new_v[n] for n in WEIGHTS])
```

```python
import functools

import jax
import jax.numpy as jnp
from jax import lax
from jax.experimental import pallas as pl
from jax.experimental.pallas import tpu as pltpu

F32, BF16 = jnp.float32, jnp.bfloat16

NORM_EPS = 1e-6
CONV_K = 4
SSM_HEAD_DIM = 64
SSM_STATE = 128
SSM_GROUPS = 2
FOX_HEAD_DIM = 128
RNN_BLOCK = 128
RG_LRU_C = 8.0
ADAM_LR, ADAM_B1, ADAM_B2, ADAM_EPS, ADAM_WD, ADAM_STEP = 0.001, 0.9, 0.999, 1e-08, 0.01, 10

LANES = 128
SUBLANES = 8
VMEM_LIMIT = 48 * 1024 * 1024
SSD_TILE = 256
FOX_TILE = 512
NEG = -1e30
LOG2E = 1.4426950408889634

NT = (((1,), (1,)), ((), ()))
NN = (((1,), (0,)), ((), ()))
TN = (((0,), (0,)), ((), ()))

BIG = ("ffn_w_in", "ffn_w_out", "hyb_w_in", "hyb_w_out", "rec_w_in", "rec_w_out")
FFN = ("ffn_w_in", "ffn_w_out")
SMALL_SHARDED = ("norm_g", "ssm_conv_w", "rec_conv_w", "rec_conv_b", "rec_b_a", "rec_b_x", "rec_lambda")
SMALL_REPL = ("ssm_conv_b", "ssm_dt_bias", "ssm_a_log", "ssm_d", "ssm_norm_g", "fox_b_f", "rec_w_a", "rec_w_x")
WEIGHTS = ("norm_g", "ffn_w_in", "ffn_w_out", "hyb_w_in", "ssm_conv_w", "ssm_conv_b", "ssm_dt_bias", "ssm_a_log",
           "ssm_d", "ssm_norm_g", "fox_b_f", "hyb_w_out", "rec_w_in", "rec_conv_w", "rec_conv_b", "rec_w_a",
           "rec_b_a", "rec_w_x", "rec_b_x", "rec_lambda", "rec_w_out")
SHARD_AXIS = {"norm_g": 2, "ffn_w_in": 3, "ffn_w_out": 2, "hyb_w_in": 2, "ssm_conv_w": 2, "hyb_w_out": 1,
              "rec_w_in": 2, "rec_conv_w": 2, "rec_conv_b": 1, "rec_b_a": 1, "rec_b_x": 1, "rec_lambda": 1,
              "rec_w_out": 1}


def _params(*sem):
    return pltpu.CompilerParams(dimension_semantics=sem if sem else None, vmem_limit_bytes=VMEM_LIMIT)


def _tile(dim, pref):
    if dim <= pref:
        return dim
    for align in (LANES, SUBLANES):
        t = (pref // align) * align
        while t >= align:
            if dim % t == 0:
                return t
            t -= align
    return dim


def _sigmoid(x):
    return 1.0 / (1.0 + jnp.exp(-x))


def _softplus(x):
    return jnp.maximum(x, 0.0) + jnp.log(1.0 + jnp.exp(-jnp.abs(x)))


def _log_sigmoid(x):
    return -_softplus(-x)


def _silu(x):
    return x * _sigmoid(x)


def _gelu_tanh(x):
    return 0.5 * x * (1.0 + jnp.tanh(0.7978845608028654 * (x + 0.044715 * x * x * x)))


def _neg_expm1(x):
    series = -x * (1.0 + x * (0.5 + x * (1.0 / 6.0)))
    return jnp.where(x > -1e-2, series, 1.0 - jnp.exp(x))


def _rms(x, g):
    xf = x.astype(F32)
    return xf * lax.rsqrt(jnp.mean(xf * xf, axis=-1, keepdims=True) + NORM_EPS) * g


def _mm(a, b, mode, out_dtype, name, tm=512, tn=512, tk=512, b_sel=()):
    bshape = b.shape[len(b_sel):]
    if mode == "nn":
        (M, K), (K2, N) = a.shape, bshape
    elif mode == "nt":
        (M, K), (N, K2) = a.shape, bshape
    else:
        (K, M), (K2, N) = a.shape, bshape
    assert K == K2, (a.shape, b.shape, mode)
    tm, tn, tk = _tile(M, tm), _tile(N, tn), _tile(K, tk)
    nk = K // tk
    dims = {"nn": NN, "nt": NT, "tn": TN}[mode]
    lead = (None,) * len(b_sel)
    if mode == "tn":
        a_spec = pl.BlockSpec((tk, tm), lambda n, m, k: (k, m))
    else:
        a_spec = pl.BlockSpec((tm, tk), lambda n, m, k: (m, k))
    if mode == "nt":
        b_spec = pl.BlockSpec(lead + (tn, tk), lambda n, m, k: (*b_sel, n, k))
    else:
        b_spec = pl.BlockSpec(lead + (tk, tn), lambda n, m, k: (*b_sel, k, n))

    def body(a_ref, b_ref, o_ref, *acc):
        p = lax.dot_general(a_ref[...].astype(BF16), b_ref[...].astype(BF16), dims, preferred_element_type=F32)
        if nk == 1:
            o_ref[...] = p.astype(o_ref.dtype)
        else:
            acc_ref, = acc
            k = pl.program_id(2)

            @pl.when(k == 0)
            def _():
                acc_ref[...] = p

            @pl.when(k > 0)
            def _():
                acc_ref[...] += p

            @pl.when(k == nk - 1)
            def _():
                o_ref[...] = acc_ref[...].astype(o_ref.dtype)

    return pl.pallas_call(
        body, name=name, grid=(N // tn, M // tm, nk),
        in_specs=[a_spec, b_spec], out_specs=pl.BlockSpec((tm, tn), lambda n, m, k: (m, n)),
        out_shape=jax.ShapeDtypeStruct((M, N), out_dtype),
        scratch_shapes=[pltpu.VMEM((tm, tn), F32)] if nk > 1 else [],
        compiler_params=_params("parallel", "parallel", "arbitrary"),
    )(a, b)


def _mm_parts(parts, mode, out_dtype, name, tm=512, tn=512):
    M = parts[0][0].shape[0]
    N = parts[0][1].shape[1] if mode == "nn" else parts[0][1].shape[0]
    tm, tn = _tile(M, tm), _tile(N, tn)
    dims = NN if mode == "nn" else NT
    in_specs, args = [], []
    for a, b in parts:
        K = a.shape[1]
        in_specs.append(pl.BlockSpec((tm, K), lambda n, m: (m, 0)))
        if mode == "nn":
            in_specs.append(pl.BlockSpec((K, tn), lambda n, m: (0, n)))
        else:
            in_specs.append(pl.BlockSpec((tn, K), lambda n, m: (n, 0)))
        args += [a, b]

    def body(*refs):
        o_ref = refs[-1]
        acc = None
        for p in range(len(parts)):
            d = lax.dot_general(refs[2 * p][...].astype(BF16), refs[2 * p + 1][...].astype(BF16), dims,
                                preferred_element_type=F32)
            acc = d if acc is None else acc + d
        o_ref[...] = acc.astype(o_ref.dtype)

    return pl.pallas_call(
        body, name=name, grid=(N // tn, M // tm), in_specs=in_specs,
        out_specs=pl.BlockSpec((tm, tn), lambda n, m: (m, n)), out_shape=jax.ShapeDtypeStruct((M, N), out_dtype),
        compiler_params=_params("parallel", "parallel"),
    )(*args)


def _ew(fn, tiled, params, outs, reds, *, name, tm=256, cb=None, ncb=1):
    T = tiled[0][0].shape[0]
    cb = tiled[0][0].shape[1] if cb is None else cb
    tm = _tile(T, tm)
    in_specs, args = [], []
    for arr, off in tiled:
        in_specs.append(pl.BlockSpec((tm, cb), functools.partial(lambda n, i, o: (i, n + o), o=off)))
        args.append(arr)
    for arr, off in params:
        if arr.ndim == 2:
            in_specs.append(pl.BlockSpec((arr.shape[0], cb), functools.partial(lambda n, i, o: (0, n + o), o=off)))
        else:
            in_specs.append(pl.BlockSpec((None,) + arr.shape[1:], lambda n, i: (n, 0, 0)))
        args.append(arr)
    out_shape = [jax.ShapeDtypeStruct((T, cb * ncb), dt) for dt in outs]
    out_specs = [pl.BlockSpec((tm, cb), lambda n, i: (i, n)) for _ in outs]
    for shape in reds:
        out_shape.append(jax.ShapeDtypeStruct(shape, F32))
        if len(shape) == 2:
            out_specs.append(pl.BlockSpec((shape[0], cb), lambda n, i: (0, n)))
        else:
            out_specs.append(pl.BlockSpec((None,) + tuple(shape[1:]), lambda n, i: (n, 0, 0)))
    n_in, n_out = len(args), len(outs)

    def body(*refs):
        i = pl.program_id(1)
        res = fn(*[r[...] for r in refs[:n_in]])
        res = res if isinstance(res, (tuple, list)) else (res,)
        for r, v in zip(refs[n_in:n_in + n_out], res[:n_out]):
            r[...] = v.astype(r.dtype)
        for r, v in zip(refs[n_in + n_out:], res[n_out:]):
            @pl.when(i == 0)
            def _():
                r[...] = jnp.zeros(r.shape, r.dtype)

            r[...] += v.astype(r.dtype).reshape(r.shape)

    res = pl.pallas_call(
        body, name=name, grid=(ncb, T // tm), in_specs=in_specs, out_specs=out_specs, out_shape=out_shape,
        compiler_params=_params("arbitrary", "arbitrary"),
    )(*args)
    return res[0] if len(res) == 1 else tuple(res)


def _rms_fwd(x, g, name):
    return _ew(lambda xv, gv: _rms(xv, gv), [(x, 0)], [(g, 0)], [BF16], [], name=name)


def _rms_bwd_add(x, g, dn, dres, name):
    def fn(xv, dnv, drv, gv):
        _, vjp = jax.vjp(_rms, xv, gv)
        dx, dg = vjp(dnv.astype(F32))
        return drv + dx, dg
    return _ew(fn, [(x, 0), (dn, 0), (dres, 0)], [(g, 0)], [F32], [g.shape], name=name)


def _post_fwd(x, h, g, w, name):
    return _ew(lambda xv, hv, gv: xv + w * _rms(hv, gv), [(x, 0), (h, 0)], [(g, 0)], [F32], [], name=name)


def _post_bwd(dy, h, g, w, name):
    def fn(dyv, hv, gv):
        _, vjp = jax.vjp(lambda a, b: w * _rms(a, b), hv, gv)
        return vjp(dyv)
    return _ew(fn, [(dy, 0), (h, 0)], [(g, 0)], [BF16], [g.shape], name=name)


def _mm_swiglu(x, g_in, blk, name, tm=512):
    T, K = x.shape
    tn = g_in.shape[-1]
    F = 2 * tn
    tm = _tile(T, tm)

    def body(a_ref, bg_ref, bu_ref, g_ref, u_ref, act_ref):
        a = a_ref[...].astype(BF16)
        g = jnp.dot(a, bg_ref[...].astype(BF16), preferred_element_type=F32)
        u = jnp.dot(a, bu_ref[...].astype(BF16), preferred_element_type=F32)
        g_ref[...] = g.astype(g_ref.dtype)
        u_ref[...] = u.astype(u_ref.dtype)
        act_ref[...] = (_silu(g) * u).astype(act_ref.dtype)

    out = jax.ShapeDtypeStruct((T, F), BF16)
    o_spec = pl.BlockSpec((tm, tn), lambda n, m: (m, n))
    return pl.pallas_call(
        body, name=name, grid=(2, T // tm),
        in_specs=[pl.BlockSpec((tm, K), lambda n, m: (m, 0)),
                  pl.BlockSpec((None, K, tn), lambda n, m: (n, blk, 0)),
                  pl.BlockSpec((None, K, tn), lambda n, m: (n + 2, blk, 0))],
        out_specs=[o_spec] * 3, out_shape=[out] * 3, compiler_params=_params("parallel", "parallel"),
    )(x, g_in, g_in)


def _ffn_mm_out(a, g_out, blk, x, g_post, name, tm=512):
    T, F = a.shape
    rl, D = F // N_CHIPS, g_out.shape[-1]
    tm = _tile(T, tm)

    def body(a_ref, b0_ref, b1_ref, x_ref, g_ref, h_ref, x2_ref, acc_ref):
        k = pl.program_id(1)
        b = jnp.concatenate([b0_ref[...], b1_ref[...]], axis=0).astype(BF16)
        p = jnp.dot(a_ref[...].astype(BF16), b, preferred_element_type=F32)

        @pl.when(k == 0)
        def _():
            acc_ref[...] = p

        @pl.when(k == 1)
        def _():
            h = acc_ref[...] + p
            h_ref[...] = h
            x2_ref[...] = x_ref[...] + 0.5 * _rms(h, g_ref[...])

    row = pl.BlockSpec((tm, D), lambda m, k: (m, 0))
    return pl.pallas_call(
        body, name=name, grid=(T // tm, 2),
        in_specs=[pl.BlockSpec((tm, 2 * rl), lambda m, k: (m, k)),
                  pl.BlockSpec((None, rl, D), lambda m, k: (2 * k, blk, 0)),
                  pl.BlockSpec((None, rl, D), lambda m, k: (2 * k + 1, blk, 0)),
                  row, pl.BlockSpec((1, D), lambda m, k: (0, 0))],
        out_specs=[row, row], out_shape=[jax.ShapeDtypeStruct((T, D), F32)] * 2,
        scratch_shapes=[pltpu.VMEM((tm, D), F32)], compiler_params=_params("parallel", "arbitrary"),
    )(a, g_out, g_out, x, g_post)


def _ffn_mm_da(dh, g_out, blk, F, name, tm=512):
    T, D = dh.shape
    rl = F // N_CHIPS
    tm = _tile(T, tm)

    def body(a_ref, b0_ref, b1_ref, o_ref):
        b = jnp.concatenate([b0_ref[...], b1_ref[...]], axis=0).astype(BF16)
        o_ref[...] = lax.dot_general(a_ref[...].astype(BF16), b, NT, preferred_element_type=F32).astype(o_ref.dtype)

    return pl.pallas_call(
        body, name=name, grid=(2, T // tm),
        in_specs=[pl.BlockSpec((tm, D), lambda n, m: (m, 0)),
                  pl.BlockSpec((None, rl, D), lambda n, m: (2 * n, blk, 0)),
                  pl.BlockSpec((None, rl, D), lambda n, m: (2 * n + 1, blk, 0))],
        out_specs=pl.BlockSpec((tm, 2 * rl), lambda n, m: (m, n)), out_shape=jax.ShapeDtypeStruct((T, F), BF16),
        compiler_params=_params("parallel", "parallel"),
    )(dh, g_out, g_out)


def _ffn_mm_dn(dgu, g_in, blk, D, name, tm=512):
    T = dgu.shape[0]
    cw = g_in.shape[-1]
    tm = _tile(T, tm)

    def body(a_ref, b0_ref, b1_ref, o_ref, acc_ref):
        k = pl.program_id(1)
        a = a_ref[...].astype(BF16)
        p = lax.dot_general(a[:, :cw], b0_ref[...].astype(BF16), NT, preferred_element_type=F32)
        p = p + lax.dot_general(a[:, cw:], b1_ref[...].astype(BF16), NT, preferred_element_type=F32)

        @pl.when(k == 0)
        def _():
            acc_ref[...] = p

        @pl.when(k == 1)
        def _():
            o_ref[...] = acc_ref[...] + p

    return pl.pallas_call(
        body, name=name, grid=(T // tm, 2),
        in_specs=[pl.BlockSpec((tm, 2 * cw), lambda m, k: (m, k)),
                  pl.BlockSpec((None, D, cw), lambda m, k: (2 * k, blk, 0)),
                  pl.BlockSpec((None, D, cw), lambda m, k: (2 * k + 1, blk, 0))],
        out_specs=pl.BlockSpec((tm, D), lambda m, k: (m, 0)), out_shape=jax.ShapeDtypeStruct((T, D), F32),
        scratch_shapes=[pltpu.VMEM((tm, D), F32)], compiler_params=_params("parallel", "arbitrary"),
    )(dgu, g_in, g_in)


def _mm_tn_shards(a, b, by_rows, name, tk=1024):
    (K, M), (_, N) = a.shape, b.shape
    tk = _tile(K, tk)
    nk = K // tk
    if by_rows:
        rl = M // N_CHIPS
        a_spec = pl.BlockSpec((tk, 2 * rl), lambda j, k: (k, j))
        b_spec = pl.BlockSpec((tk, N), lambda j, k: (k, 0))
        o_spec = pl.BlockSpec((2, rl, N), lambda j, k: (j, 0, 0))
        out_shape, acc_shape, steps = (N_CHIPS, rl, N), (2 * rl, N), 2
    else:
        cw = N // N_CHIPS
        a_spec = pl.BlockSpec((tk, M), lambda j, k: (k, 0))
        b_spec = pl.BlockSpec((tk, cw), lambda j, k: (k, j))
        o_spec = pl.BlockSpec((None, M, cw), lambda j, k: (j, 0, 0))
        out_shape, acc_shape, steps = (N_CHIPS, M, cw), (M, cw), N_CHIPS

    def body(a_ref, b_ref, o_ref, acc_ref):
        k = pl.program_id(1)
        p = lax.dot_general(a_ref[...].astype(BF16), b_ref[...].astype(BF16), TN, preferred_element_type=F32)

        @pl.when(k == 0)
        def _():
            acc_ref[...] = p

        @pl.when(k > 0)
        def _():
            acc_ref[...] += p

        @pl.when(k == nk - 1)
        def _():
            if by_rows:
                o_ref[0] = acc_ref[:rl, :].astype(o_ref.dtype)
                o_ref[1] = acc_ref[rl:, :].astype(o_ref.dtype)
            else:
                o_ref[...] = acc_ref[...].astype(o_ref.dtype)

    return pl.pallas_call(
        body, name=name, grid=(steps, nk), in_specs=[a_spec, b_spec], out_specs=o_spec,
        out_shape=jax.ShapeDtypeStruct(out_shape, BF16), scratch_shapes=[pltpu.VMEM(acc_shape, F32)],
        compiler_params=_params("parallel", "arbitrary"),
    )(a, b)


def _swiglu_bwd(gate, up, da, name):
    T, F = gate.shape
    tm = _tile(T, 256)

    def body(g_ref, u_ref, da_ref, o_ref):
        g, u, d = g_ref[...].astype(F32), u_ref[...].astype(F32), da_ref[...].astype(F32)
        s = _sigmoid(g)
        o_ref[:, :F] = (d * u * (s * (1.0 + g * (1.0 - s)))).astype(o_ref.dtype)
        o_ref[:, F:] = (d * g * s).astype(o_ref.dtype)

    spec = pl.BlockSpec((tm, F), lambda i: (i, 0))
    return pl.pallas_call(
        body, name=name, grid=(T // tm,), in_specs=[spec] * 3, out_specs=pl.BlockSpec((tm, 2 * F), lambda i: (i, 0)),
        out_shape=jax.ShapeDtypeStruct((T, 2 * F), BF16), compiler_params=_params("parallel"),
    )(gate, up, da)


def _ffn_fwd(x, g_pre, g_post, g_in, g_out, blk, tag):
    n = _rms_fwd(x, g_pre, f"{tag}_rms")
    gate, up, a = _mm_swiglu(n, g_in, blk, f"{tag}_mm_in")
    h, x2 = _ffn_mm_out(a, g_out, blk, x, g_post, f"{tag}_mm_out")
    return x2, (x, n, gate, up, a, h)


def _ffn_bwd(dy, saved, g_pre, g_post, g_in, g_out, blk, tag):
    x, n, gate, up, a, h = saved
    dh, dg_post = _post_bwd(dy, h, g_post, 0.5, f"{tag}_post_b")
    da = _ffn_mm_da(dh, g_out, blk, a.shape[1], f"{tag}_mm_da")
    dw_out = _mm_tn_shards(a, dh, True, f"{tag}_mm_dwout")
    dgu = _swiglu_bwd(gate, up, da, f"{tag}_swiglu_b")
    dn = _ffn_mm_dn(dgu, g_in, blk, x.shape[1], f"{tag}_mm_dn")
    dw_in = _mm_tn_shards(n, dgu, False, f"{tag}_mm_dwin")
    dy2, dg_pre = _rms_bwd_add(x, g_pre, dn, dy, f"{tag}_rms_b")
    return dy2, dg_pre, dg_post, dw_in, dw_out


def _shift_down(x, s):
    if s == 0:
        return x
    rows = lax.broadcasted_iota(jnp.int32, x.shape, 0)
    return jnp.where(rows >= s, pltpu.roll(x, s, 0), 0.0)


def _shift_up(x, s):
    if s == 0:
        return x
    T = x.shape[0]
    rows = lax.broadcasted_iota(jnp.int32, x.shape, 0)
    return jnp.where(rows < T - s, pltpu.roll(x, T - s, 0), 0.0)


def _conv_pre(x, w, b):
    y = b
    for k in range(CONV_K):
        y = y + w[k:k + 1, :] * _shift_down(x, CONV_K - 1 - k)
    return y


def _conv_fwd(x, col0, C, w, b, act, name, ct=256):
    T = x.shape[0]
    off = col0 // ct

    def body(x_ref, w_ref, b_ref, o_ref):
        y = _conv_pre(x_ref[...], w_ref[...], b_ref[...])
        o_ref[...] = _silu(y) if act else y

    return pl.pallas_call(
        body, name=name, grid=(C // ct,),
        in_specs=[pl.BlockSpec((T, ct), lambda j: (0, j + off)), pl.BlockSpec((CONV_K, ct), lambda j: (0, j)),
                  pl.BlockSpec((1, ct), lambda j: (0, j))],
        out_specs=pl.BlockSpec((T, ct), lambda j: (0, j)), out_shape=jax.ShapeDtypeStruct((T, C), F32),
        compiler_params=_params("parallel"),
    )(x, w, b)


def _conv_bwd(x, col0, C, w, b, dyact, act, name, ct=256):
    T = x.shape[0]
    off = col0 // ct

    def body(x_ref, w_ref, b_ref, dy_ref, dx_ref, dw_ref, db_ref):
        xv, wv = x_ref[...], w_ref[...]
        dy = dy_ref[...].astype(F32)
        if act:
            pre = _conv_pre(xv, wv, b_ref[...])
            s = _sigmoid(pre)
            dy = dy * (s * (1.0 + pre * (1.0 - s)))
        dx = jnp.zeros_like(dy)
        dws = []
        for k in range(CONV_K):
            dx = dx + wv[k:k + 1, :] * _shift_up(dy, CONV_K - 1 - k)
            dws.append(jnp.sum(dy * _shift_down(xv, CONV_K - 1 - k), axis=0, keepdims=True))
        dx_ref[...] = dx.astype(dx_ref.dtype)
        dw_ref[...] = jnp.concatenate(dws, axis=0)
        db_ref[...] = jnp.sum(dy, axis=0, keepdims=True)

    return pl.pallas_call(
        body, name=name, grid=(C // ct,),
        in_specs=[pl.BlockSpec((T, ct), lambda j: (0, j + off)), pl.BlockSpec((CONV_K, ct), lambda j: (0, j)),
                  pl.BlockSpec((1, ct), lambda j: (0, j)), pl.BlockSpec((T, ct), lambda j: (0, j))],
        out_specs=[pl.BlockSpec((T, ct), lambda j: (0, j)), pl.BlockSpec((CONV_K, ct), lambda j: (0, j)),
                   pl.BlockSpec((1, ct), lambda j: (0, j))],
        out_shape=[jax.ShapeDtypeStruct((T, C), BF16), jax.ShapeDtypeStruct((CONV_K, C), F32),
                   jax.ShapeDtypeStruct((1, C), F32)],
        compiler_params=_params("parallel"),
    )(x, w, b, dyact)


def _gate_act(v, n_ssm):
    lane = lax.broadcasted_iota(jnp.int32, v.shape, 1)
    return jnp.where(lane < n_ssm, _softplus(v), _log_sigmoid(v))


def _gates_fwd(proj, col0, bias, mult, n_ssm, name, tb=512):
    T = proj.shape[0]
    tb = _tile(T, tb)
    off = col0 // LANES

    def body(s_ref, bias_ref, mult_ref, act_ref, cs_ref, carry_ref):
        i = pl.program_id(0)

        @pl.when(i == 0)
        def _():
            carry_ref[...] = jnp.zeros_like(carry_ref)

        act = _gate_act(s_ref[...] + bias_ref[...], n_ssm)
        inc = act * mult_ref[...]
        r = lax.broadcasted_iota(jnp.int32, (tb, tb), 0)
        c = lax.broadcasted_iota(jnp.int32, (tb, tb), 1)
        tri = jnp.where(r >= c, 1.0, 0.0).astype(F32)
        cs = jnp.dot(tri, inc, precision=lax.Precision.HIGHEST, preferred_element_type=F32) + carry_ref[...]
        act_ref[...] = act
        cs_ref[...] = cs
        carry_ref[...] = cs[tb - 1:tb, :]

    return pl.pallas_call(
        body, name=name, grid=(T // tb,),
        in_specs=[pl.BlockSpec((tb, LANES), lambda i: (i, off)), pl.BlockSpec((1, LANES), lambda i: (0, 0)),
                  pl.BlockSpec((1, LANES), lambda i: (0, 0))],
        out_specs=[pl.BlockSpec((tb, LANES), lambda i: (i, 0))] * 2,
        out_shape=[jax.ShapeDtypeStruct((T, LANES), F32)] * 2,
        scratch_shapes=[pltpu.VMEM((1, LANES), F32)], compiler_params=_params("arbitrary"),
    )(proj, bias, mult)


def _gates_bwd(proj, col0, bias, mult, n_ssm, dact, dcs, name, tb=512):
    T = proj.shape[0]
    tb = _tile(T, tb)
    nb = T // tb
    off = col0 // LANES

    def body(s_ref, bias_ref, mult_ref, dact_ref, dcs_ref, ds_ref, dmult_ref, dbias_ref, carry_ref):
        i = pl.program_id(0)

        @pl.when(i == 0)
        def _():
            carry_ref[...] = jnp.zeros_like(carry_ref)
            dmult_ref[...] = jnp.zeros_like(dmult_ref)
            dbias_ref[...] = jnp.zeros_like(dbias_ref)

        v = s_ref[...] + bias_ref[...]
        act = _gate_act(v, n_ssm)
        r = lax.broadcasted_iota(jnp.int32, (tb, tb), 0)
        c = lax.broadcasted_iota(jnp.int32, (tb, tb), 1)
        tri = jnp.where(r <= c, 1.0, 0.0).astype(F32)
        dinc = jnp.dot(tri, dcs_ref[...], precision=lax.Precision.HIGHEST, preferred_element_type=F32) + carry_ref[...]
        carry_ref[...] = dinc[0:1, :]
        da = dact_ref[...] + dinc * mult_ref[...]
        sg = _sigmoid(v)
        lane = lax.broadcasted_iota(jnp.int32, v.shape, 1)
        dv = da * jnp.where(lane < n_ssm, sg, 1.0 - sg)
        ds_ref[...] = dv.astype(ds_ref.dtype)
        dmult_ref[...] += jnp.sum(dinc * act, axis=0, keepdims=True)
        dbias_ref[...] += jnp.sum(dv, axis=0, keepdims=True)

    rev = lambda i: (nb - 1 - i, 0)
    return pl.pallas_call(
        body, name=name, grid=(nb,),
        in_specs=[pl.BlockSpec((tb, LANES), lambda i: (nb - 1 - i, off)), pl.BlockSpec((1, LANES), lambda i: (0, 0)),
                  pl.BlockSpec((1, LANES), lambda i: (0, 0)), pl.BlockSpec((tb, LANES), rev),
                  pl.BlockSpec((tb, LANES), rev)],
        out_specs=[pl.BlockSpec((tb, LANES), rev), pl.BlockSpec((1, LANES), lambda i: (0, 0)),
                   pl.BlockSpec((1, LANES), lambda i: (0, 0))],
        out_shape=[jax.ShapeDtypeStruct((T, LANES), BF16), jax.ShapeDtypeStruct((1, LANES), F32),
                   jax.ShapeDtypeStruct((1, LANES), F32)],
        scratch_shapes=[pltpu.VMEM((1, LANES), F32)], compiler_params=_params("arbitrary"),
    )(proj, bias, mult, dact, dcs)


def _rep_layout(v):
    return jnp.repeat(v, LANES, axis=1)


def _row_layout(v, tk):
    T, H = v.shape
    return v.T.reshape(H, T // tk, 1, tk)


def _fori_pairs(n, body, init):
    carry = lax.fori_loop(0, n // 2, lambda t, c: body(2 * t + 1, body(2 * t, c)), init)
    return lax.fori_loop(2 * (n // 2), n, body, carry)


def _causal(tq):
    r = lax.broadcasted_iota(jnp.int32, (tq, tq), 0)
    c = lax.broadcasted_iota(jnp.int32, (tq, tq), 1)
    return r >= c


def _ssd2_fwd(xbc, X, XK, cs_rep, cs_row, cs_full, r_end, name, tq=SSD_TILE):
    T = X.shape[0]
    tq = _tile(T, tq)
    nq = T // tq
    d_ssm = X.shape[1]
    gw = d_ssm // SSM_GROUPS
    hpg = gw // SSM_HEAD_DIM
    b_off = d_ssm // SSM_STATE
    c_off = b_off + SSM_GROUPS

    def body(c_ref, b_ref, x_ref, xk_ref, csq_ref, csk_ref, csf_ref, rend_ref, y_ref):
        i = pl.program_id(1)
        c = c_ref[...].astype(BF16)
        csf = csf_ref[...]
        r = csf[0:1, :]

        def off(j, acc):
            r0 = pl.multiple_of(j * tq, tq)
            s = lax.dot_general(c, b_ref[pl.ds(r0, tq), :].astype(BF16), NT, preferred_element_type=F32)
            z = jnp.dot(s.astype(BF16), xk_ref[pl.ds(r0, tq), :], preferred_element_type=F32)
            return acc + z * jnp.exp(r - rend_ref[j])

        y_off = jnp.exp(csf - r) * _fori_pairs(i, off, jnp.zeros((tq, gw), F32))

        r0 = pl.multiple_of(i * tq, tq)
        half = lax.broadcasted_iota(jnp.int32, (tq, LANES), 1) // SSM_HEAD_DIM
        mask = _causal(tq)
        s = lax.dot_general(c, b_ref[pl.ds(r0, tq), :].astype(BF16), NT, preferred_element_type=F32)
        out = []
        for p in range(hpg // 2):
            xp = x_ref[:, p * LANES:(p + 1) * LANES]
            a = jnp.zeros((tq, LANES), F32)
            for e in range(2):
                h = 2 * p + e
                diff = jnp.tile(csq_ref[:, h * LANES:(h + 1) * LANES], (1, tq // LANES)) - csk_ref[h, i]
                pm = (s * jnp.exp(jnp.where(mask, diff, NEG))).astype(BF16)
                xm = jnp.where(half == e, xp, jnp.zeros_like(xp))
                a = a + jnp.dot(pm, xm, preferred_element_type=F32)
            out.append(a)
        y_ref[...] = y_off + jnp.concatenate(out, axis=1)

    return pl.pallas_call(
        body, name=name, grid=(SSM_GROUPS, nq),
        in_specs=[pl.BlockSpec((tq, SSM_STATE), lambda g, i: (i, c_off + g)),
                  pl.BlockSpec((T, SSM_STATE), lambda g, i: (0, b_off + g)),
                  pl.BlockSpec((tq, gw), lambda g, i: (i, g)),
                  pl.BlockSpec((T, gw), lambda g, i: (0, g)),
                  pl.BlockSpec((tq, hpg * LANES), lambda g, i: (i, g)),
                  pl.BlockSpec((hpg, nq, 1, tq), lambda g, i: (g, 0, 0, 0)),
                  pl.BlockSpec((tq, gw), lambda g, i: (i, g)),
                  pl.BlockSpec((nq, 1, gw), lambda g, i: (0, 0, g))],
        out_specs=pl.BlockSpec((tq, gw), lambda g, i: (i, g)),
        out_shape=jax.ShapeDtypeStruct((T, d_ssm), F32), compiler_params=_params("parallel", "arbitrary"),
    )(xbc, xbc, X, XK, cs_rep, cs_row, cs_full, r_end)


def _ssd2_bwd(xbc, X, XK, cs_rep, cs_row, cs_full, r_end, dY, name, tq=SSD_TILE):
    T = X.shape[0]
    tq = _tile(T, tq)
    nq = T // tq
    d_ssm = X.shape[1]
    gw = d_ssm // SSM_GROUPS
    hpg = gw // SSM_HEAD_DIM
    nheads = d_ssm // SSM_HEAD_DIM
    b_off = d_ssm // SSM_STATE
    c_off = b_off + SSM_GROUPS

    def body(c_ref, b_ref, x_ref, xk_ref, dy_ref, csq_ref, csk_ref, csf_ref, rend_ref,
             dc_ref, db_ref, dx_ref, dxk_ref, dcsq_ref, dcsk_ref, dcsf_ref):
        i = pl.program_id(1)

        @pl.when(i == 0)
        def _():
            db_ref[...] = jnp.zeros_like(db_ref)
            dxk_ref[...] = jnp.zeros_like(dxk_ref)

        c = c_ref[...].astype(BF16)
        csf = csf_ref[...]
        r = csf[0:1, :]
        dyt = dy_ref[...].astype(F32) * jnp.exp(csf - r)

        def off(j, carry):
            dc_acc, rs_acc = carry
            r0 = pl.multiple_of(j * tq, tq)
            b = b_ref[pl.ds(r0, tq), :].astype(BF16)
            xk = xk_ref[pl.ds(r0, tq), :]
            sb = lax.dot_general(c, b, NT, preferred_element_type=F32).astype(BF16)
            dye = dyt * jnp.exp(r - rend_ref[j])
            dyb = dye.astype(BF16)
            rs_acc = rs_acc + dyb.astype(F32) * jnp.dot(sb, xk, preferred_element_type=F32)
            dxk_ref[pl.ds(r0, tq), :] += lax.dot_general(sb, dyb, TN, preferred_element_type=F32)
            dsb = lax.dot_general(dyb, xk, NT, preferred_element_type=F32).astype(BF16)
            dc_acc = dc_acc + jnp.dot(dsb, b, preferred_element_type=F32)
            db_ref[pl.ds(r0, tq), :] += lax.dot_general(dsb, c, TN, preferred_element_type=F32)
            return dc_acc, rs_acc

        dc_acc, rs_acc = _fori_pairs(i, off, (jnp.zeros((tq, SSM_STATE), F32), jnp.zeros((tq, gw), F32)))
        dcsf_ref[...] = rs_acc

        r0 = pl.multiple_of(i * tq, tq)
        half = lax.broadcasted_iota(jnp.int32, (tq, LANES), 1) // SSM_HEAD_DIM
        mask = _causal(tq)
        b = b_ref[pl.ds(r0, tq), :].astype(BF16)
        s = lax.dot_general(c, b, NT, preferred_element_type=F32)
        ds_tot = jnp.zeros((tq, tq), F32)
        rows, dxs = [], []
        for p in range(hpg // 2):
            cols = slice(p * LANES, (p + 1) * LANES)
            xp = x_ref[:, cols]
            dyp = dy_ref[:, cols]
            dx_p = jnp.zeros((tq, LANES), F32)
            for e in range(2):
                h = 2 * p + e
                diff = jnp.tile(csq_ref[:, h * LANES:(h + 1) * LANES], (1, tq // LANES)) - csk_ref[h, i]
                decay = jnp.exp(jnp.where(mask, diff, NEG))
                dym = jnp.where(half == e, dyp, jnp.zeros_like(dyp))
                g = lax.dot_general(dym, xp, NT, preferred_element_type=F32) * decay
                ds_tot = ds_tot + g
                m = g * s
                rows.append(jnp.broadcast_to(jnp.sum(m, axis=1, keepdims=True), (tq, LANES)))
                dcsk_ref[h, i] = -jnp.sum(m, axis=0, keepdims=True)
                dx_p = dx_p + lax.dot_general((s * decay).astype(BF16), dym, TN, preferred_element_type=F32)
            dxs.append(dx_p)
        dsb = ds_tot.astype(BF16)
        dc_ref[...] = dc_acc + jnp.dot(dsb, b, preferred_element_type=F32)
        db_ref[pl.ds(r0, tq), :] += lax.dot_general(dsb, c, TN, preferred_element_type=F32)
        dx_ref[...] = jnp.concatenate(dxs, axis=1)
        dcsq_ref[...] = jnp.concatenate(rows, axis=1)

    return pl.pallas_call(
        body, name=name, grid=(SSM_GROUPS, nq),
        in_specs=[pl.BlockSpec((tq, SSM_STATE), lambda g, i: (i, c_off + g)),
                  pl.BlockSpec((T, SSM_STATE), lambda g, i: (0, b_off + g)),
                  pl.BlockSpec((tq, gw), lambda g, i: (i, g)),
                  pl.BlockSpec((T, gw), lambda g, i: (0, g)),
                  pl.BlockSpec((tq, gw), lambda g, i: (i, g)),
                  pl.BlockSpec((tq, hpg * LANES), lambda g, i: (i, g)),
                  pl.BlockSpec((hpg, nq, 1, tq), lambda g, i: (g, 0, 0, 0)),
                  pl.BlockSpec((tq, gw), lambda g, i: (i, g)),
                  pl.BlockSpec((nq, 1, gw), lambda g, i: (0, 0, g))],
        out_specs=[pl.BlockSpec((tq, SSM_STATE), lambda g, i: (i, g)),
                   pl.BlockSpec((T, SSM_STATE), lambda g, i: (0, g)),
                   pl.BlockSpec((tq, gw), lambda g, i: (i, g)),
                   pl.BlockSpec((T, gw), lambda g, i: (0, g)),
                   pl.BlockSpec((tq, hpg * LANES), lambda g, i: (i, g)),
                   pl.BlockSpec((hpg, nq, 1, tq), lambda g, i: (g, 0, 0, 0)),
                   pl.BlockSpec((tq, gw), lambda g, i: (i, g))],
        out_shape=[jax.ShapeDtypeStruct((T, SSM_GROUPS * SSM_STATE), F32),
                   jax.ShapeDtypeStruct((T, SSM_GROUPS * SSM_STATE), F32),
                   jax.ShapeDtypeStruct((T, d_ssm), F32),
                   jax.ShapeDtypeStruct((T, d_ssm), F32),
                   jax.ShapeDtypeStruct((T, nheads * LANES), F32),
                   jax.ShapeDtypeStruct((nheads, nq, 1, tq), F32),
                   jax.ShapeDtypeStruct((T, d_ssm), F32)],
        compiler_params=_params("arbitrary", "arbitrary"),
    )(xbc, xbc, X, XK, dY, cs_rep, cs_row, cs_full, r_end)


def _fox_fwd(proj, q0, k0, v0, nh, cum_row, name, tq=FOX_TILE):
    T = proj.shape[0]
    tq = _tile(T, tq)
    nq = T // tq
    hd = FOX_HEAD_DIM
    scale = hd ** -0.5
    qo, ko, vo = q0 // hd, k0 // hd, v0 // hd

    def body(q_ref, k_ref, v_ref, ck_ref, o_ref, lse_ref):
        i = pl.program_id(1)
        q = (q_ref[...] * (scale * LOG2E)).astype(BF16)
        mask = _causal(tq)

        def step(j, carry, masked):
            m, l, acc = carry
            r0 = pl.multiple_of(j * tq, tq)
            k = k_ref[pl.ds(r0, tq), :].astype(BF16)
            v = v_ref[pl.ds(r0, tq), :].astype(BF16)
            s = lax.dot_general(q, k, NT, preferred_element_type=F32) - ck_ref[j]
            if masked:
                s = jnp.where(mask, s, NEG)
            m_new = jnp.maximum(m, jnp.max(s, axis=1, keepdims=True))
            alpha = jnp.exp2(m - m_new)
            p = jnp.exp2(s - m_new)
            l = alpha * l + jnp.sum(p, axis=1, keepdims=True)
            acc = alpha * acc + jnp.dot(p.astype(BF16), v, preferred_element_type=F32)
            return m_new, l, acc

        carry = (jnp.full((tq, 1), NEG, F32), jnp.zeros((tq, 1), F32), jnp.zeros((tq, hd), F32))
        carry = lax.fori_loop(0, i, lambda j, cr: step(j, cr, False), carry)
        m, l, acc = step(i, carry, True)
        o_ref[...] = (acc / l).astype(o_ref.dtype)
        lse_ref[...] = jnp.broadcast_to(m + jnp.log2(l), (tq, LANES))

    return pl.pallas_call(
        body, name=name, grid=(nh, nq),
        in_specs=[pl.BlockSpec((tq, hd), lambda h, i: (i, qo + h)), pl.BlockSpec((T, hd), lambda h, i: (0, ko + h)),
                  pl.BlockSpec((T, hd), lambda h, i: (0, vo + h)),
                  pl.BlockSpec((None, nq, 1, tq), lambda h, i: (h, 0, 0, 0))],
        out_specs=[pl.BlockSpec((tq, hd), lambda h, i: (i, h)), pl.BlockSpec((tq, LANES), lambda h, i: (i, h))],
        out_shape=[jax.ShapeDtypeStruct((T, nh * hd), BF16), jax.ShapeDtypeStruct((T, nh * LANES), F32)],
        compiler_params=_params("parallel", "arbitrary"),
    )(proj, proj, proj, cum_row * LOG2E)


def _fox_bwd(proj, q0, k0, v0, nh, cum_row, o, lse, dcat, do0, name, tq=FOX_TILE):
    T = proj.shape[0]
    tq = _tile(T, tq)
    nq = T // tq
    hd = FOX_HEAD_DIM
    scale = hd ** -0.5
    qo, ko, vo, doo = q0 // hd, k0 // hd, v0 // hd, do0 // hd

    def body(q_ref, k_ref, v_ref, do_ref, o_ref, lse_ref, ck_ref, dq_ref, dk_ref, dv_ref, dck_ref, dcq_ref):
        i = pl.program_id(1)

        @pl.when(i == 0)
        def _():
            dk_ref[...] = jnp.zeros_like(dk_ref)
            dv_ref[...] = jnp.zeros_like(dv_ref)
            dck_ref[...] = jnp.zeros_like(dck_ref)

        q = (q_ref[...] * (scale * LOG2E)).astype(BF16)
        do = do_ref[...].astype(F32)
        dob = do.astype(BF16)
        delta = jnp.sum(do * o_ref[...].astype(F32), axis=1, keepdims=True)
        lse = jnp.tile(lse_ref[...], (1, tq // LANES))
        mask = _causal(tq)

        def step(j, carry, masked):
            dq, rows = carry
            r0 = pl.multiple_of(j * tq, tq)
            k = k_ref[pl.ds(r0, tq), :].astype(BF16)
            v = v_ref[pl.ds(r0, tq), :].astype(BF16)
            s = lax.dot_general(q, k, NT, preferred_element_type=F32) - lse - ck_ref[j]
            if masked:
                s = jnp.where(mask, s, NEG)
            p = jnp.exp2(s)
            dp = lax.dot_general(dob, v, NT, preferred_element_type=F32)
            ds = p * (dp - delta)
            dsb = ds.astype(BF16)
            dq = dq + jnp.dot(dsb, k, preferred_element_type=F32) * scale
            dk_ref[pl.ds(r0, tq), :] += lax.dot_general(dsb, q, TN, preferred_element_type=F32) * (1.0 / LOG2E)
            dv_ref[pl.ds(r0, tq), :] += lax.dot_general(p.astype(BF16), dob, TN, preferred_element_type=F32)
            dck_ref[j] -= jnp.sum(ds, axis=0, keepdims=True)
            return dq, rows + jnp.sum(ds, axis=1, keepdims=True)

        carry = (jnp.zeros((tq, hd), F32), jnp.zeros((tq, 1), F32))
        carry = lax.fori_loop(0, i, lambda j, cr: step(j, cr, False), carry)
        dq, rows = step(i, carry, True)
        dq_ref[...] = dq.astype(dq_ref.dtype)
        dcq_ref[...] = jnp.broadcast_to(rows, (tq, LANES))

    return pl.pallas_call(
        body, name=name, grid=(nh, nq),
        in_specs=[pl.BlockSpec((tq, hd), lambda h, i: (i, qo + h)), pl.BlockSpec((T, hd), lambda h, i: (0, ko + h)),
                  pl.BlockSpec((T, hd), lambda h, i: (0, vo + h)), pl.BlockSpec((tq, hd), lambda h, i: (i, doo + h)),
                  pl.BlockSpec((tq, hd), lambda h, i: (i, h)), pl.BlockSpec((tq, LANES), lambda h, i: (i, h)),
                  pl.BlockSpec((None, nq, 1, tq), lambda h, i: (h, 0, 0, 0))],
        out_specs=[pl.BlockSpec((tq, hd), lambda h, i: (i, h)), pl.BlockSpec((T, hd), lambda h, i: (0, h)),
                   pl.BlockSpec((T, hd), lambda h, i: (0, h)), pl.BlockSpec((None, nq, 1, tq), lambda h, i: (h, 0, 0, 0)),
                   pl.BlockSpec((tq, LANES), lambda h, i: (i, h))],
        out_shape=[jax.ShapeDtypeStruct((T, nh * hd), BF16), jax.ShapeDtypeStruct((T, nh * hd), F32),
                   jax.ShapeDtypeStruct((T, nh * hd), F32), jax.ShapeDtypeStruct((nh, nq, 1, tq), F32),
                   jax.ShapeDtypeStruct((T, nh * LANES), F32)],
        compiler_params=_params("arbitrary", "arbitrary"),
    )(proj, proj, proj, dcat, o, lse, cum_row * LOG2E)


def _scan_fwd(a, u, name, ct=256):
    T, C = a.shape

    def body(a_ref, u_ref, h_ref):
        def blk(tb, h):
            r0 = pl.multiple_of(tb * SUBLANES, SUBLANES)
            ab, ub = a_ref[pl.ds(r0, SUBLANES), :], u_ref[pl.ds(r0, SUBLANES), :]
            rows = []
            for r in range(SUBLANES):
                h = ab[r:r + 1, :] * h + ub[r:r + 1, :]
                rows.append(h)
            h_ref[pl.ds(r0, SUBLANES), :] = jnp.concatenate(rows, axis=0)
            return h

        lax.fori_loop(0, T // SUBLANES, blk, jnp.zeros((1, ct), F32))

    spec = pl.BlockSpec((T, ct), lambda j: (0, j))
    return pl.pallas_call(body, name=name, grid=(C // ct,), in_specs=[spec, spec], out_specs=spec,
                          out_shape=jax.ShapeDtypeStruct((T, C), F32), compiler_params=_params("parallel"))(a, u)


def _scan_bwd(a, dh, h, name, ct=256):
    T, C = a.shape
    nb = T // SUBLANES

    def body(a_ref, dh_ref, h_ref, g_ref, da_ref):
        def blk(t, carry):
            r0 = pl.multiple_of((nb - 1 - t) * SUBLANES, SUBLANES)
            ab, db = a_ref[pl.ds(r0, SUBLANES), :], dh_ref[pl.ds(r0, SUBLANES), :]
            rows = [None] * SUBLANES
            for r in range(SUBLANES - 1, -1, -1):
                g = db[r:r + 1, :] + carry
                carry = ab[r:r + 1, :] * g
                rows[r] = g
            g_ref[pl.ds(r0, SUBLANES), :] = jnp.concatenate(rows, axis=0)
            return carry

        lax.fori_loop(0, nb, blk, jnp.zeros((1, ct), F32))
        da_ref[...] = g_ref[...] * _shift_down(h_ref[...], 1)

    spec = pl.BlockSpec((T, ct), lambda j: (0, j))
    return pl.pallas_call(body, name=name, grid=(C // ct,), in_specs=[spec] * 3, out_specs=[spec] * 2,
                          out_shape=[jax.ShapeDtypeStruct((T, C), F32)] * 2,
                          compiler_params=_params("parallel"))(a, dh, h)


def _lru_elem(xc, ra, ia, lam):
    r, i = _sigmoid(ra), _sigmoid(ia)
    log_a = RG_LRU_C * r * _log_sigmoid(lam)
    return jnp.exp(log_a), jnp.sqrt(_neg_expm1(2.0 * log_a)) * (i * xc)


def _lru_gates_fwd(xc, w_a, b_a, w_x, b_x, lam, name):
    def fn(xv, ba, bx, lm, wa, wx):
        xb = xv.astype(BF16)
        ra = jnp.dot(xb, wa.astype(BF16), preferred_element_type=F32) + ba
        ia = jnp.dot(xb, wx.astype(BF16), preferred_element_type=F32) + bx
        return _lru_elem(xv, ra, ia, lm)
    nb = xc.shape[1] // RNN_BLOCK
    return _ew(fn, [(xc, 0)], [(b_a, 0), (b_x, 0), (lam, 0), (w_a, 0), (w_x, 0)], [F32, F32], [],
               name=name, tm=512, cb=RNN_BLOCK, ncb=nb)


def _lru_gates_bwd(xc, w_a, b_a, w_x, b_x, lam, da, du, name):
    def fn(xv, dav, duv, ba, bx, lm, wa, wx):
        xb, wab, wxb = xv.astype(BF16), wa.astype(BF16), wx.astype(BF16)
        ra = jnp.dot(xb, wab, preferred_element_type=F32) + ba
        ia = jnp.dot(xb, wxb, preferred_element_type=F32) + bx
        _, vjp = jax.vjp(_lru_elem, xv, ra, ia, lm)
        dx, dra, dia, dlm = vjp((dav, duv))
        drb, dib = dra.astype(BF16), dia.astype(BF16)
        dx = dx + lax.dot_general(drb, wab, NT, preferred_element_type=F32)
        dx = dx + lax.dot_general(dib, wxb, NT, preferred_element_type=F32)
        dwa = lax.dot_general(xb, drb, TN, preferred_element_type=F32)
        dwx = lax.dot_general(xb, dib, TN, preferred_element_type=F32)
        return (dx, jnp.sum(dra, axis=0, keepdims=True), jnp.sum(dia, axis=0, keepdims=True), dlm, dwa, dwx)
    nb = xc.shape[1] // RNN_BLOCK
    return _ew(fn, [(xc, 0), (da, 0), (du, 0)], [(b_a, 0), (b_x, 0), (lam, 0), (w_a, 0), (w_x, 0)], [F32],
               [b_a.shape, b_x.shape, lam.shape, w_a.shape, w_x.shape], name=name, tm=512, cb=RNN_BLOCK, ncb=nb)


def _hyb_cols(D):
    conv = D + 2 * SSM_GROUPS * SSM_STATE
    z0, x0, q0 = 0, D, D + conv
    return dict(z=z0, xbc=x0, q=q0, k=q0 + D, v=q0 + 2 * D, small=q0 + 3 * D, total=q0 + 3 * D + LANES, conv=conv)


def _hyb_w_in_reorder(w, D):
    cols = _hyb_cols(D)
    nh_s, nh_f = D // SSM_HEAD_DIM, D // FOX_HEAD_DIM
    a = D + cols["conv"]
    pad = jnp.zeros((w.shape[0], LANES - nh_s - nh_f), w.dtype)
    return jnp.concatenate([w[:, :a], w[:, a + nh_s:a + nh_s + 3 * D], w[:, a:a + nh_s], w[:, a + nh_s + 3 * D:], pad], axis=1)


def _hyb_w_in_restore(dw, D):
    cols = _hyb_cols(D)
    nh_s, nh_f = D // SSM_HEAD_DIM, D // FOX_HEAD_DIM
    a = D + cols["conv"]
    s = cols["small"]
    return jnp.concatenate([dw[:, :a], dw[:, s:s + nh_s], dw[:, a:s], dw[:, s + nh_s:s + nh_s + nh_f]], axis=1)


def _ssm_out(Y, xs, z, dfull, ng):
    y = (Y + dfull * xs) * _silu(z)
    return y * lax.rsqrt(jnp.mean(y * y, axis=-1, keepdims=True) + NORM_EPS) * ng


def _hyb_fwd(x, g_pre, g_post, p, tag):
    T, D = x.shape
    cols = _hyb_cols(D)
    nh_s, nh_f = D // SSM_HEAD_DIM, D // FOX_HEAD_DIM
    n = _rms_fwd(x, g_pre, f"{tag}_rms")
    proj = _mm(n, p["w_in"], "nn", F32, f"{tag}_mm_in", tn=1152, tk=1024)
    a_neg = -jnp.exp(p["a_log"])
    bias = jnp.concatenate([p["dt_bias"], p["b_f"], jnp.zeros((LANES - nh_s - nh_f,), F32)])[None]
    mult = jnp.concatenate([a_neg, jnp.ones((nh_f,), F32), jnp.zeros((LANES - nh_s - nh_f,), F32)])[None]
    act, cs = _gates_fwd(proj, cols["small"], bias, mult, nh_s, f"{tag}_gates")
    dt, cs_s, cum = act[:, :nh_s], cs[:, :nh_s], cs[:, nh_s:nh_s + nh_f]
    dtf = jnp.repeat(dt, SSM_HEAD_DIM, axis=1)
    cs_rep, cs_row = _rep_layout(cs_s), _row_layout(cs_s, _tile(T, SSD_TILE))
    cum_row = _row_layout(cum, _tile(T, FOX_TILE))
    xbc = _conv_fwd(proj, cols["xbc"], cols["conv"], p["conv_w"], p["conv_b"], True, f"{tag}_conv")
    tqs = _tile(T, SSD_TILE)
    cs_full = jnp.repeat(cs_s, SSM_HEAD_DIM, axis=1)
    r_end = cs_full.reshape(T // tqs, tqs, D)[:, tqs - 1:, :]
    r_exp = jnp.broadcast_to(r_end, (T // tqs, tqs, D)).reshape(T, D)
    X, XK = _ew(lambda xv, dv, cv, rv: (xv * dv, xv * dv * jnp.exp(rv - cv)),
                [(xbc, 0), (dtf, 0), (cs_full, 0), (r_exp, 0)], [], [BF16, BF16], [], name=f"{tag}_xdt",
                cb=512, ncb=D // 512)
    Y = _ssd2_fwd(xbc, X, XK, cs_rep, cs_row, cs_full, r_end, f"{tag}_ssd")
    dfull = jnp.repeat(p["d"], SSM_HEAD_DIM)[None]
    gw = D // SSM_GROUPS
    y_ssm = _ew(_ssm_out, [(Y, 0), (xbc, 0), (proj, cols["z"] // gw)], [(dfull, 0), (p["norm_g"], 0)], [BF16], [],
                name=f"{tag}_ssm_out", cb=gw, ncb=SSM_GROUPS)
    o, lse = _fox_fwd(proj, cols["q"], cols["k"], cols["v"], nh_f, cum_row, f"{tag}_fox")
    mix = _mm_parts([(y_ssm, p["w_out"][:D]), (o, p["w_out"][D:])], "nn", F32, f"{tag}_mm_out", tn=1024)
    x2 = _post_fwd(x, mix, g_post, 1.0, f"{tag}_post")
    return x2, (x, n, proj, bias, mult, dtf, cs_rep, cs_row, cum_row, xbc, X, Y, dfull, o, lse, y_ssm, mix,
                XK, cs_full, r_end, r_exp)


def _hyb_bwd(dy, saved, g_pre, g_post, p, tag):
    (x, n, proj, bias, mult, dtf, cs_rep, cs_row, cum_row, xbc, X, Y, dfull, o, lse, y_ssm, mix,
     XK, cs_full, r_end, r_exp) = saved
    T, D = x.shape
    cols = _hyb_cols(D)
    nh_s, nh_f = D // SSM_HEAD_DIM, D // FOX_HEAD_DIM
    gw = D // SSM_GROUPS
    dmix, dg_post = _post_bwd(dy, mix, g_post, 1.0, f"{tag}_post_b")
    dcat = _mm(dmix, p["w_out"], "nt", BF16, f"{tag}_mm_dcat", tn=1024, tk=1024)
    dw_out = jnp.concatenate([_mm(y_ssm, dmix, "tn", BF16, f"{tag}_mm_dwout_s", tm=1024, tn=1024, tk=1024),
                              _mm(o, dmix, "tn", BF16, f"{tag}_mm_dwout_f", tm=1024, tn=1024, tk=1024)], axis=0)

    def ssm_out_b(Yv, xv, zv, dv, dfv, ngv):
        _, vjp = jax.vjp(_ssm_out, Yv, xv, zv, dfv, ngv)
        return vjp(dv.astype(F32))
    dY, dxs_skip, dz, ddfull, dng = _ew(
        ssm_out_b, [(Y, 0), (xbc, 0), (proj, cols["z"] // gw), (dcat, 0)], [(dfull, 0), (p["norm_g"], 0)],
        [BF16, F32, BF16], [dfull.shape, p["norm_g"].shape], name=f"{tag}_ssm_out_b", cb=gw, ncb=SSM_GROUPS)
    dC, dB, dXd, dXK, dcs_q, dcs_k, dcs_f = _ssd2_bwd(xbc, X, XK, cs_rep, cs_row, cs_full, r_end, dY, f"{tag}_ssd_b")
    def xdt_b(dXdv, dXKv, skv, xv, dv, cv, rv, xkv):
        dX = dXdv + dXKv * jnp.exp(rv - cv)
        return dX * dv + skv, dX * xv, dXKv * xkv.astype(F32)
    dxs, ddtf, dcs_kf = _ew(
        xdt_b, [(dXd, 0), (dXK, 0), (dxs_skip, 0), (xbc, 0), (dtf, 0), (cs_full, 0), (r_exp, 0), (XK, 0)],
        [], [F32, F32, F32], [], name=f"{tag}_xdt_b", cb=512, ncb=D // 512)
    ddt = ddtf.reshape(T, nh_s, SSM_HEAD_DIM).sum(-1)
    dcs_s = (dcs_q[:, ::LANES] + dcs_k.reshape(nh_s, T).T
             + (dcs_f - dcs_kf).reshape(T, nh_s, SSM_HEAD_DIM).sum(-1))
    dq, dk, dv, dcum_k, dcum_q = _fox_bwd(proj, cols["q"], cols["k"], cols["v"], nh_f, cum_row, o, lse, dcat, D,
                                  f"{tag}_fox_b")
    dcum = dcum_q[:, ::LANES] + dcum_k.reshape(nh_f, T).T
    zpad = jnp.zeros((T, LANES - nh_s - nh_f), F32)
    dact = jnp.concatenate([ddt, jnp.zeros((T, nh_f), F32), zpad], axis=1)
    dcs = jnp.concatenate([dcs_s, dcum, zpad], axis=1)
    dsmall, dmult, dbias = _gates_bwd(proj, cols["small"], bias, mult, nh_s, dact, dcs, f"{tag}_gates_b")
    dxbc_act = jnp.concatenate([dxs, dB, dC], axis=1)
    dxbc, dconv_w, dconv_b = _conv_bwd(proj, cols["xbc"], cols["conv"], p["conv_w"], p["conv_b"], dxbc_act, True,
                                       f"{tag}_conv_b")
    pieces = [(dz, "z"), (dxbc, "xbc"), (dq, "q"), (dk, "k"), (dv, "v"), (dsmall, "small")]
    w_cols = lambda d, key: p["w_in"][:, cols[key]:cols[key] + d.shape[1]]
    dn = _mm_parts([(d, w_cols(d, key)) for d, key in pieces], "nt", F32, f"{tag}_mm_dn", tm=256, tn=512)
    dw_in = jnp.concatenate([_mm(n, d, "tn", BF16, f"{tag}_mm_dwin_{key}", tm=1024, tn=1024, tk=1024)
                             for d, key in pieces], axis=1)
    dy2, dg_pre = _rms_bwd_add(x, g_pre, dn, dy, f"{tag}_rms_b")
    grads = dict(w_in=dw_in, w_out=dw_out, conv_w=dconv_w, conv_b=dconv_b[0], dt_bias=dbias[0, :nh_s],
                 a_log=dmult[0, :nh_s] * mult[0, :nh_s], d=ddfull.reshape(nh_s, SSM_HEAD_DIM).sum(-1),
                 norm_g=dng[0], b_f=dbias[0, nh_s:nh_s + nh_f])
    return dy2, dg_pre, dg_post, grads


def _rec_fwd(x, g_pre, g_post, p, tag):
    T, D = x.shape
    n = _rms_fwd(x, g_pre, f"{tag}_rms")
    pr = _mm(n, p["w_in"], "nn", F32, f"{tag}_mm_in", tn=1024, tk=1024)
    xc = _conv_fwd(pr, D, D, p["conv_w"], p["conv_b"], False, f"{tag}_conv")
    a, u = _lru_gates_fwd(xc, p["w_a"], p["b_a"], p["w_x"], p["b_x"], p["lam"], f"{tag}_lru")
    hs = _scan_fwd(a, u, f"{tag}_scan")
    og = _ew(lambda hv, gv: hv * _gelu_tanh(gv), [(hs, 0), (pr, 0)], [], [BF16], [], name=f"{tag}_gate", cb=D)
    mix = _mm(og, p["w_out"], "nn", F32, f"{tag}_mm_out", tn=1024, tk=1024)
    x2 = _post_fwd(x, mix, g_post, 1.0, f"{tag}_post")
    return x2, (x, n, pr, xc, a, hs, og, mix)


def _rec_bwd(dy, saved, g_pre, g_post, p, tag):
    x, n, pr, xc, a, hs, og, mix = saved
    T, D = x.shape
    dmix, dg_post = _post_bwd(dy, mix, g_post, 1.0, f"{tag}_post_b")
    dog = _mm(dmix, p["w_out"], "nt", F32, f"{tag}_mm_dog", tn=1024, tk=1024)
    dw_out = _mm(og, dmix, "tn", BF16, f"{tag}_mm_dwout", tm=1024, tn=1024, tk=1024)

    def gate_b(hv, gv, dv):
        _, vjp = jax.vjp(lambda h_, g_: h_ * _gelu_tanh(g_), hv, gv)
        return vjp(dv)
    dhs, dgate = _ew(gate_b, [(hs, 0), (pr, 0), (dog, 0)], [], [F32, BF16], [], name=f"{tag}_gate_b", cb=D)
    du, da = _scan_bwd(a, dhs, hs, f"{tag}_scan_b")
    dxc, db_a, db_x, dlam, dw_a, dw_x = _lru_gates_bwd(xc, p["w_a"], p["b_a"], p["w_x"], p["b_x"], p["lam"], da, du,
                                                       f"{tag}_lru_b")
    dxr, dconv_w, dconv_b = _conv_bwd(pr, D, D, p["conv_w"], p["conv_b"], dxc, False, f"{tag}_conv_b")
    dn = _mm_parts([(dgate, p["w_in"][:, :D]), (dxr, p["w_in"][:, D:])], "nt", F32, f"{tag}_mm_dn", tn=1024)
    dw_in = jnp.concatenate([_mm(n, dgate, "tn", BF16, f"{tag}_mm_dwin_g", tm=1024, tn=1024, tk=1024),
                             _mm(n, dxr, "tn", BF16, f"{tag}_mm_dwin_x", tm=1024, tn=1024, tk=1024)], axis=1)
    dy2, dg_pre = _rms_bwd_add(x, g_pre, dn, dy, f"{tag}_rms_b")
    grads = dict(w_in=dw_in, w_out=dw_out, conv_w=dconv_w, conv_b=dconv_b[0], w_a=dw_a, b_a=db_a[0], w_x=dw_x,
                 b_x=db_x[0], lam=dlam[0])
    return dy2, dg_pre, dg_post, grads


def _hyb_params(W, i, D):
    return dict(w_in=_hyb_w_in_reorder(W["hyb_w_in"][i], D), w_out=W["hyb_w_out"][i], conv_w=W["ssm_conv_w"][i],
                conv_b=W["ssm_conv_b"][i][None], dt_bias=W["ssm_dt_bias"][i], a_log=W["ssm_a_log"][i],
                d=W["ssm_d"][i], norm_g=W["ssm_norm_g"][i][None], b_f=W["fox_b_f"][i])


def _rec_params(W, j):
    return dict(w_in=W["rec_w_in"][j], w_out=W["rec_w_out"][j], conv_w=W["rec_conv_w"][j],
                conv_b=W["rec_conv_b"][j][None], w_a=W["rec_w_a"][j], b_a=W["rec_b_a"][j][None],
                w_x=W["rec_w_x"][j], b_x=W["rec_b_x"][j][None], lam=W["rec_lambda"][j][None])


def _local_step(x, target, W):
    T, D = x.shape
    depth = W["norm_g"].shape[0]
    g = lambda l, k: W["norm_g"][l, k][None]
    saved = []
    for l in range(depth):
        x, s0 = _ffn_fwd(x, g(l, 0), g(l, 1), W["ffn_w_in"], W["ffn_w_out"], 2 * l, f"l{l}_ffn0")
        if l % 2 == 0:
            pm = _hyb_params(W, l // 2, D)
            x, s1 = _hyb_fwd(x, g(l, 2), g(l, 3), pm, f"l{l}_hyb")
        else:
            pm = _rec_params(W, l // 2)
            x, s1 = _rec_fwd(x, g(l, 2), g(l, 3), pm, f"l{l}_rec")
        x, s2 = _ffn_fwd(x, g(l, 4), g(l, 5), W["ffn_w_in"], W["ffn_w_out"], 2 * l + 1, f"l{l}_ffn1")
        saved.append((s0, s1, s2, pm))

    def loss_fn(yv, tv):
        err = yv - tv
        part = 0.5 * jnp.sum(jnp.sum(err * err, axis=1, keepdims=True), axis=0, keepdims=True) / D
        return err * (1.0 / D), jnp.broadcast_to(part, (1, D))
    dy, loss_row = _ew(loss_fn, [(x, 0), (target, 0)], [], [F32], [(1, D)], name="loss")
    loss = loss_row[0, 0]

    gn = [[None] * 6 for _ in range(depth)]
    g_ffn_in = [[None, None] for _ in range(depth)]
    g_ffn_out = [[None, None] for _ in range(depth)]
    g_hyb, g_rec = [], []
    for l in reversed(range(depth)):
        s0, s1, s2, pm = saved[l]
        dy, gn[l][4], gn[l][5], g_ffn_in[l][1], g_ffn_out[l][1] = _ffn_bwd(
            dy, s2, g(l, 4), g(l, 5), W["ffn_w_in"], W["ffn_w_out"], 2 * l + 1, f"l{l}_ffn1")
        if l % 2 == 0:
            dy, gn[l][2], gn[l][3], gm = _hyb_bwd(dy, s1, g(l, 2), g(l, 3), pm, f"l{l}_hyb")
            g_hyb.insert(0, gm)
        else:
            dy, gn[l][2], gn[l][3], gm = _rec_bwd(dy, s1, g(l, 2), g(l, 3), pm, f"l{l}_rec")
            g_rec.insert(0, gm)
        dy, gn[l][0], gn[l][1], g_ffn_in[l][0], g_ffn_out[l][0] = _ffn_bwd(
            dy, s0, g(l, 0), g(l, 1), W["ffn_w_in"], W["ffn_w_out"], 2 * l, f"l{l}_ffn0")

    st = lambda items: jnp.stack(items)
    grads = {
        "norm_g": st([st([r[0] for r in row]) for row in gn]),
        "ffn_w_in": [piece for row in g_ffn_in for piece in row],
        "ffn_w_out": [piece for row in g_ffn_out for piece in row],
        "hyb_w_in": st([_hyb_w_in_restore(m["w_in"], D) for m in g_hyb]),
        "ssm_conv_w": st([m["conv_w"] for m in g_hyb]), "ssm_conv_b": st([m["conv_b"] for m in g_hyb]),
        "ssm_dt_bias": st([m["dt_bias"] for m in g_hyb]), "ssm_a_log": st([m["a_log"] for m in g_hyb]),
        "ssm_d": st([m["d"] for m in g_hyb]), "ssm_norm_g": st([m["norm_g"] for m in g_hyb]),
        "fox_b_f": st([m["b_f"] for m in g_hyb]), "hyb_w_out": st([m["w_out"] for m in g_hyb]),
        "rec_w_in": st([m["w_in"] for m in g_rec]), "rec_conv_w": st([m["conv_w"] for m in g_rec]),
        "rec_conv_b": st([m["conv_b"] for m in g_rec]), "rec_w_a": st([m["w_a"] for m in g_rec]),
        "rec_b_a": st([m["b_a"] for m in g_rec]), "rec_w_x": st([m["w_x"] for m in g_rec]),
        "rec_b_x": st([m["b_x"] for m in g_rec]), "rec_lambda": st([m["lam"] for m in g_rec]),
        "rec_w_out": st([m["w_out"] for m in g_rec]),
    }
    return loss, dy, grads


MESH_AXES = ("x", "y", "c")
N_CHIPS = 4
N_DEV = 8
HBM = pl.BlockSpec(memory_space=pltpu.HBM)


def _mesh_pos():
    return tuple(lax.axis_index(n) for n in MESH_AXES)


def _other_chips(x, y):
    chips = [(1 - x, y), (x, 1 - y), (1 - x, 1 - y)]
    return chips, [2 * cx + cy for cx, cy in chips]


def _rcopy(src, dst, send_sem, recv_sem, dev):
    return pltpu.make_async_remote_copy(src_ref=src, dst_ref=dst, send_sem=send_sem, recv_sem=recv_sem,
                                        device_id=dev, device_id_type=pl.DeviceIdType.MESH)


def _comm_params():
    return pltpu.CompilerParams()


DMA_CHUNK_BYTES = 1 << 20
DMA_ROW_ALIGN = 16


def _nchunks(rows, row_bytes):
    n = max(1, min(rows // DMA_ROW_ALIGN, (rows * row_bytes) // DMA_CHUNK_BYTES))
    while n > 1 and (rows % n or (rows // n) % DMA_ROW_ALIGN):
        n -= 1
    return n


def _row_bytes(ref):
    return ref.shape[-1] * jnp.dtype(ref.dtype).itemsize


def _all_gather(arrs, name):
    n = len(arrs)

    def body(*refs):
        ins, outs = refs[:n], refs[n:2 * n]
        send, recv, fsend, frecv = refs[2 * n:]
        x, y, c = _mesh_pos()
        k = 2 * x + y
        sibling = (x, y, 1 - c)
        chips, chip_k = _other_chips(x, y)
        halves = [r.shape[0] // 2 for r in ins]
        nchs = [_nchunks(h, _row_bytes(r)) for h, r in zip(halves, ins)]
        for a in range(n):
            h, step = halves[a], halves[a] // nchs[a]
            for j, chip in enumerate(chips):
                for q in range(nchs[a]):
                    rows = pl.ds(c * h + q * step, step)
                    _rcopy(ins[a].at[rows], outs[a].at[k, rows], send.at[a, j], recv.at[a, j], (*chip, c)).start()
        for a in range(n):
            h, step = halves[a], halves[a] // nchs[a]
            for j, chip in enumerate(chips):
                blk = outs[a].at[chip_k[j], pl.ds(c * h, h)]
                _rcopy(blk, blk, send.at[a, j], recv.at[a, j], (*chip, c)).wait_recv()
                for q in range(nchs[a]):
                    part = outs[a].at[chip_k[j], pl.ds(c * h + q * step, step)]
                    _rcopy(part, part, fsend.at[a, j], frecv.at[a, j], sibling).start()
        for a in range(n):
            h = halves[a]
            for j, chip in enumerate(chips):
                blk = outs[a].at[chip_k[j], pl.ds((1 - c) * h, h)]
                _rcopy(blk, blk, fsend.at[a, j], frecv.at[a, j], sibling).wait_recv()
        for a in range(n):
            h = halves[a]
            for j, chip in enumerate(chips):
                _rcopy(ins[a].at[pl.ds(c * h, h)], outs[a].at[k, pl.ds(c * h, h)], send.at[a, j], recv.at[a, j],
                       (*chip, c)).wait_send()
                blk = outs[a].at[chip_k[j], pl.ds(c * h, h)]
                _rcopy(blk, blk, fsend.at[a, j], frecv.at[a, j], sibling).wait_send()

    outs = pl.pallas_call(
        body, name=name, in_specs=[HBM] * n, out_specs=[HBM] * n,
        out_shape=[jax.ShapeDtypeStruct((N_CHIPS,) + a.shape, a.dtype) for a in arrs],
        scratch_shapes=[pltpu.SemaphoreType.DMA((n, 3))] * 4, compiler_params=_comm_params(),
    )(*arrs)
    k = 2 * lax.axis_index("x") + lax.axis_index("y")
    return [lax.dynamic_update_index_in_dim(o, a, k, 0) for o, a in zip(outs, arrs)]


def _pair_exchange(gs, name):
    n = len(gs)

    def body(*refs):
        ins, theirs = refs[:n], refs[n:2 * n]
        send, recv = refs[2 * n:]
        x, y, c = _mesh_pos()
        sibling = (x, y, 1 - c)
        for a in range(n):
            h = ins[a].shape[1] // 2
            nch = _nchunks(h, _row_bytes(ins[a]))
            step = h // nch
            for kk in range(N_CHIPS):
                for q in range(nch):
                    _rcopy(ins[a].at[kk, pl.ds((1 - c) * h + q * step, step)], theirs[a].at[kk, pl.ds(q * step, step)],
                           send.at[a], recv.at[a], sibling).start()
        for a in range(n):
            h = ins[a].shape[1] // 2
            _rcopy(ins[a].at[:, pl.ds((1 - c) * h, h)], theirs[a], send.at[a], recv.at[a], sibling).wait()

    half = [jax.ShapeDtypeStruct((a.shape[0], a.shape[1] // 2) + a.shape[2:], a.dtype) for a in gs]
    theirs = pl.pallas_call(
        body, name=name, in_specs=[HBM] * n, out_specs=[HBM] * n, out_shape=half,
        scratch_shapes=[pltpu.SemaphoreType.DMA((n,))] * 2, compiler_params=_comm_params(),
    )(*gs)
    c = lax.axis_index("c")
    mine = [lax.dynamic_slice_in_dim(g, c * (g.shape[1] // 2), g.shape[1] // 2, axis=1) for g in gs]
    return mine, theirs


def _chip_exchange(ss, name):
    n = len(ss)

    def body(*refs):
        ins = refs[:n]
        got = [refs[n + 3 * a:n + 3 * a + 3] for a in range(n)]
        send, recv = refs[4 * n:]
        x, y, c = _mesh_pos()
        chips, chip_k = _other_chips(x, y)
        for a in range(n):
            h = ins[a].shape[1]
            nch = _nchunks(h, _row_bytes(ins[a]))
            step = h // nch
            for q in range(nch):
                rows = pl.ds(q * step, step)
                for j, chip in enumerate(chips):
                    _rcopy(ins[a].at[chip_k[j], rows], got[a][j].at[rows], send.at[a, j], recv.at[a, j],
                           (*chip, c)).start()
        for a in range(n):
            for j, chip in enumerate(chips):
                _rcopy(ins[a].at[chip_k[j]], got[a][j], send.at[a, j], recv.at[a, j], (*chip, c)).wait()

    got = [jax.ShapeDtypeStruct(a.shape[1:], a.dtype) for a in ss for _ in range(3)]
    res = pl.pallas_call(
        body, name=name, in_specs=[HBM] * n, out_specs=[HBM] * (3 * n), out_shape=got,
        scratch_shapes=[pltpu.SemaphoreType.DMA((n, 3))] * 2, compiler_params=_comm_params(),
    )(*ss)
    k = 2 * lax.axis_index("x") + lax.axis_index("y")
    own = [lax.dynamic_index_in_dim(s, k, 0, keepdims=False) for s in ss]
    return own, [res[3 * a:3 * a + 3] for a in range(n)]


def _pair_share(rs, name):
    n = len(rs)

    def body(*refs):
        ins, outs = refs[:n], refs[n:2 * n]
        send, recv = refs[2 * n:]
        x, y, c = _mesh_pos()
        sibling = (x, y, 1 - c)
        for a in range(n):
            h = ins[a].shape[0]
            nch = _nchunks(h, _row_bytes(ins[a]))
            step = h // nch
            for q in range(nch):
                rows = pl.ds(q * step, step)
                _rcopy(ins[a].at[rows], outs[a].at[rows], send.at[a], recv.at[a], sibling).start()
        for a in range(n):
            _rcopy(ins[a], outs[a], send.at[a], recv.at[a], sibling).wait()

    theirs = pl.pallas_call(
        body, name=name, in_specs=[HBM] * n, out_specs=[HBM] * n,
        out_shape=[jax.ShapeDtypeStruct(a.shape, a.dtype) for a in rs],
        scratch_shapes=[pltpu.SemaphoreType.DMA((n,))] * 2, compiler_params=_comm_params(),
    )(*rs)
    c = lax.axis_index("c")
    return [jnp.concatenate([jnp.where(c == 0, r, t), jnp.where(c == 0, t, r)], axis=0) for r, t in zip(rs, theirs)]


def _exchange_all(vec, name):
    def body(v_ref, out_ref, send, recv):
        x, y, c = _mesh_pos()
        flip = lambda p, f: 1 - p if f else p
        me = 4 * x + 2 * y + c
        rows_all = v_ref.shape[0]
        nch = _nchunks(rows_all, _row_bytes(v_ref))
        step = rows_all // nch
        for j in range(1, N_DEV):
            fx, fy, fc = (j >> 2) & 1, (j >> 1) & 1, j & 1
            for q in range(nch):
                rows = pl.ds(q * step, step)
                _rcopy(v_ref.at[rows], out_ref.at[me, rows], send.at[j - 1], recv.at[j - 1],
                       (flip(x, fx), flip(y, fy), flip(c, fc))).start()
        for j in range(1, N_DEV):
            fx, fy, fc = (j >> 2) & 1, (j >> 1) & 1, j & 1
            slot = out_ref.at[4 * flip(x, fx) + 2 * flip(y, fy) + flip(c, fc)]
            _rcopy(slot, slot, send.at[j - 1], recv.at[j - 1], (x, y, c)).wait_recv()
        for j in range(1, N_DEV):
            _rcopy(v_ref, out_ref.at[me], send.at[j - 1], recv.at[j - 1], (x, y, c)).wait_send()

    out = pl.pallas_call(
        body, name=name, in_specs=[HBM], out_specs=HBM,
        out_shape=jax.ShapeDtypeStruct((N_DEV,) + vec.shape, vec.dtype),
        scratch_shapes=[pltpu.SemaphoreType.DMA((N_DEV - 1,))] * 2, compiler_params=_comm_params(),
    )(vec)
    me = 4 * lax.axis_index("x") + 2 * lax.axis_index("y") + lax.axis_index("c")
    return lax.dynamic_update_index_in_dim(out, vec, me, 0)


def _rows(a):
    return a.reshape(-1, a.shape[-1])


def _sum_kernel(parts, out_dtype, name):
    def fn(*vals):
        acc = vals[0].astype(F32)
        for v in vals[1:]:
            acc = acc + v.astype(F32)
        return acc
    out = _ew(fn, [(_rows(p), 0) for p in parts], [], [out_dtype], [], name=name)
    return out.reshape(parts[0].shape)


def _reduce_scatter(gs, tag):
    mine, theirs = _pair_exchange(gs, f"{tag}_pair")
    pair = [_sum_kernel([m, t], BF16, f"{tag}_add_pair{a}") for a, (m, t) in enumerate(zip(mine, theirs))]
    own, got = _chip_exchange(pair, f"{tag}_chips")
    red = [_sum_kernel([o, g[0], g[1], g[2]], F32, f"{tag}_add_chips{a}") for a, (o, g) in enumerate(zip(own, got))]
    return _pair_share(red, f"{tag}_share")


def _pack(arrs, row_mult):
    flat = jnp.concatenate([a.reshape(-1).astype(F32) for a in arrs])
    unit = row_mult * LANES
    pad = (-flat.size) % unit
    return jnp.pad(flat, (0, pad)).reshape(-1, LANES)


def _unpack(mat, shapes):
    flat, out, pos = mat.reshape(-1), [], 0
    for s in shapes:
        size = 1
        for d in s:
            size *= d
        out.append(flat[pos:pos + size].reshape(s))
        pos += size
    return out


def _to_shards(a, axis):
    sh = a.shape
    a = a.reshape(sh[:axis] + (N_CHIPS, sh[axis] // N_CHIPS) + sh[axis + 1:])
    return jnp.moveaxis(a, axis, 0)


def _from_shards(g, axis):
    g = jnp.moveaxis(g, 0, axis)
    sh = g.shape
    return g.reshape(sh[:axis] + (sh[axis] * sh[axis + 1],) + sh[axis + 2:])


def _adamw_fn(w, g, m, v):
    m2 = ADAM_B1 * m + (1.0 - ADAM_B1) * g
    v2 = ADAM_B2 * v + (1.0 - ADAM_B2) * (g * g)
    m_hat = m2 / (1.0 - ADAM_B1 ** ADAM_STEP)
    v_hat = v2 / (1.0 - ADAM_B2 ** ADAM_STEP)
    return -ADAM_LR * (m_hat / (jnp.sqrt(v_hat) + ADAM_EPS) + ADAM_WD * w), m2, v2


def _adamw(w, g, m, v, name):
    res = _ew(_adamw_fn, [(_rows(a), 0) for a in (w, g, m, v)], [], [F32, F32, F32], [], name=name)
    return tuple(r.reshape(w.shape) for r in res)


def kernel(x, norm_g, ffn_w_in, ffn_w_out, hyb_w_in, ssm_conv_w, ssm_conv_b, ssm_dt_bias, ssm_a_log, ssm_d, ssm_norm_g, fox_b_f, hyb_w_out, rec_w_in, rec_conv_w, rec_conv_b, rec_w_a, rec_b_a, rec_w_x, rec_b_x, rec_lambda, rec_w_out, loss_target, m_norm_g, m_ffn_w_in, m_ffn_w_out, m_hyb_w_in, m_ssm_conv_w, m_ssm_conv_b, m_ssm_dt_bias, m_ssm_a_log, m_ssm_d, m_ssm_norm_g, m_fox_b_f, m_hyb_w_out, m_rec_w_in, m_rec_conv_w, m_rec_conv_b, m_rec_w_a, m_rec_b_a, m_rec_w_x, m_rec_b_x, m_rec_lambda, m_rec_w_out, v_norm_g, v_ffn_w_in, v_ffn_w_out, v_hyb_w_in, v_ssm_conv_w, v_ssm_conv_b, v_ssm_dt_bias, v_ssm_a_log, v_ssm_d, v_ssm_norm_g, v_fox_b_f, v_hyb_w_out, v_rec_w_in, v_rec_conv_w, v_rec_conv_b, v_rec_w_a, v_rec_b_a, v_rec_w_x, v_rec_b_x, v_rec_lambda, v_rec_w_out):
    given = dict(locals())
    w = {n: given[n] for n in WEIGHTS}
    m = {n: given["m_" + n] for n in WEIGHTS}
    v = {n: given["v_" + n] for n in WEIGHTS}
    k = 2 * lax.axis_index("x") + lax.axis_index("y")

    big_bf16 = [_ew(lambda t: t, [(_rows(w[n]), 0)], [], [BF16], [], name=f"cast_{n}") for n in BIG]
    small_shapes = [w[n].shape for n in SMALL_SHARDED]
    small_pack = _pack([w[n] for n in SMALL_SHARDED], 2 * SUBLANES)
    gathered = _all_gather(big_bf16 + [small_pack], "gather_weights")
    W = {n: w[n] for n in SMALL_REPL}
    for n, g in zip(BIG, gathered[:-1]):
        W[n] = g if n in FFN else _from_shards(g.reshape((N_CHIPS,) + w[n].shape), SHARD_AXIS[n])
    per_chip = [_unpack(gathered[-1][kk], small_shapes) for kk in range(N_CHIPS)]
    for idx, n in enumerate(SMALL_SHARDED):
        W[n] = jnp.concatenate([per_chip[kk][idx] for kk in range(N_CHIPS)], axis=SHARD_AXIS[n])

    loss_part, dy, grads = _local_step(x[0], loss_target[0], W)
    loss = lax.psum(loss_part, MESH_AXES)

    pieces = {n: grads[n] if n in FFN else [_to_shards(grads[n], SHARD_AXIS[n]).reshape(N_CHIPS, -1, w[n].shape[-1])]
              for n in BIG}
    red_big = _reduce_scatter([piece for n in BIG for piece in pieces[n]], "rs")
    g_out, pos = {}, 0
    for n in BIG:
        cnt = len(pieces[n])
        g_out[n] = jnp.concatenate(red_big[pos:pos + cnt], axis=0).reshape(w[n].shape)
        pos += cnt
    small_names = SMALL_SHARDED + SMALL_REPL
    slots = _exchange_all(_pack([grads[n] for n in small_names], DMA_ROW_ALIGN).astype(BF16), "gather_small_grads")
    small_sum = _sum_kernel([slots[d] for d in range(N_DEV)], F32, "add_small_grads")
    for n, g in zip(small_names, _unpack(small_sum, [grads[n].shape for n in small_names])):
        if n in SHARD_AXIS:
            loc = g.shape[SHARD_AXIS[n]] // N_CHIPS
            g = lax.dynamic_slice_in_dim(g, k * loc, loc, axis=SHARD_AXIS[n])
        g_out[n] = g

    delta, new_m, new_v = {}, {}, {}
    for n in BIG:
        delta[n], new_m[n], new_v[n] = _adamw(w[n], g_out[n], m[n], v[n], f"adamw_{n}")
    shapes = [w[n].shape for n in small_names]
    packed = [_pack([d[n] for n in small_names], SUBLANES) for d in (w, g_out, m, v)]
    for d, mat in zip((delta, new_m, new_v), _adamw(*packed, "adamw_small")):
        d.update(zip(small_names, _unpack(mat, shapes)))

    return (loss, dy[None], *[g_out[n] for n in WEIGHTS], *[delta[n] for n in WEIGHTS],
            *[new_m[n] for n in WEIGHTS], *[new_v[n] for n in WEIGHTS])
```
